```python
import math
import jax, jax.numpy as jnp
from jax import lax
import numpy as np

D_MODEL = 1024
BATCH = 4
SEQ = 4096
DEPTH = 2

CTX_LEN = 256
GRID_W = 64
HEAD_DIM = 64
EPS = 1e-6
ROPE_BASE = 10000.0
A_HEADS = 8
A_KV_HEADS = 2
A_WINDOW = 128
A_BLOCK = 128
B_HEADS = 4
B_DK = 64
B_DV = 128
B_GATE_RANK = 16
B_GATE_NORM = 16.0
B_CHUNK = 16
C_HEADS = 8
C_WIN_ROWS = 8
C_WIN_COLS = 16
C_QBLOCK_COLS = 16
C_KBLOCK_COLS = 32
FFN_HIDDEN = -(-8 * D_MODEL // (3 * 256)) * 256
IN_SIZES = (A_HEADS * HEAD_DIM, A_KV_HEADS * HEAD_DIM, A_KV_HEADS * HEAD_DIM,
            B_HEADS * B_DK, B_HEADS * B_DK, B_HEADS * B_DV, B_HEADS * B_DV, 2 * B_GATE_RANK,
            C_HEADS * HEAD_DIM, C_HEADS * HEAD_DIM, C_HEADS * HEAD_DIM)
IN_TOTAL = sum(IN_SIZES)

kernel_name = "hybrid_gated_parallel_mixer_dit"


def rmsnorm(x, g):
    xf = x.astype(jnp.float32)
    y = xf * lax.rsqrt(jnp.mean(xf * xf, axis=-1, keepdims=True) + EPS)
    return (y * g.astype(jnp.float32)).astype(x.dtype)


def split_projection(t):
    idx = []
    acc = 0
    for s in IN_SIZES[:-1]:
        acc += s
        idx.append(acc)
    return jnp.split(t, idx, axis=-1)


def split_heads(t, n_heads):
    return t.reshape(t.shape[0], t.shape[1], n_heads, -1)


def axial_rope_tables(L):
    t = jnp.arange(L)
    row = (t // GRID_W).astype(jnp.float32)
    col = (t % GRID_W).astype(jnp.float32)
    n_freq = HEAD_DIM // 4
    inv = ROPE_BASE ** (-jnp.arange(n_freq, dtype=jnp.float32) / n_freq)
    ang = jnp.concatenate([row[:, None] * inv[None], col[:, None] * inv[None]], axis=-1)
    return jnp.cos(ang), jnp.sin(ang)


def apply_rope(x, cos, sin):
    half = HEAD_DIM // 2
    x1, x2 = x[..., :half], x[..., half:]
    c = cos[None, :, None, :]
    s = sin[None, :, None, :]
    return jnp.concatenate([x1 * c - x2 * s, x1 * s + x2 * c], axis=-1).astype(x.dtype)


def window_attention(q, k, v, k_ctx, v_ctx, sink):
    B, L, Hq, dh = q.shape
    n_kv = k.shape[2]
    G = Hq // n_kv
    Lc = k_ctx.shape[1]
    blk = A_BLOCK
    nb = L // blk
    qb = q.reshape(B, nb, blk, n_kv, G, dh)
    pad = ((0, 0), (blk, blk), (0, 0), (0, 0))
    kp = jnp.pad(k, pad).reshape(B, nb + 2, blk, n_kv, dh)
    vp = jnp.pad(v, pad).reshape(B, nb + 2, blk, n_kv, dh)
    k_band = jnp.concatenate([kp[:, :-2], kp[:, 1:-1], kp[:, 2:]], axis=2)
    v_band = jnp.concatenate([vp[:, :-2], vp[:, 1:-1], vp[:, 2:]], axis=2)
    scale = dh ** -0.5
    s_loc = jnp.einsum('bnqhgd,bnkhd->bnhgqk', qb, k_band).astype(jnp.float32) * scale
    s_ctx = jnp.einsum('bnqhgd,bchd->bnhgqc', qb, k_ctx).astype(jnp.float32) * scale
    qi = jnp.arange(blk)[:, None]
    kj = jnp.arange(3 * blk)[None, :] - blk
    j_abs = jnp.arange(nb)[:, None, None] * blk + kj[None]
    valid = (jnp.abs(kj - qi)[None] <= A_WINDOW) & (j_abs >= 0) & (j_abs < L)
    s_loc = jnp.where(valid[None, :, None, None], s_loc, -jnp.inf)
    sink_col = jnp.broadcast_to(sink.astype(jnp.float32).reshape(n_kv, G)[None, None, :, :, None, None],
                                s_loc.shape[:-1] + (1,))
    p = jax.nn.softmax(jnp.concatenate([s_loc, s_ctx, sink_col], axis=-1), axis=-1)
    nk = 3 * blk
    p_loc = p[..., :nk].astype(v.dtype)
    p_ctx = p[..., nk:nk + Lc].astype(v.dtype)
    o = (jnp.einsum('bnhgqk,bnkhd->bnqhgd', p_loc, v_band)
         + jnp.einsum('bnhgqc,bchd->bnqhgd', p_ctx, v_ctx))
    return o.reshape(B, L, Hq * dh)


def dense_ctx_attention(q, k, v, n_kv, sink):
    B, Lc, Hq, dh = q.shape
    G = Hq // n_kv
    qg = q.reshape(B, Lc, n_kv, G, dh)
    s = jnp.einsum('bqhgd,bkhd->bhgqk', qg, k).astype(jnp.float32) * dh ** -0.5
    if sink is not None:
        sink_col = jnp.broadcast_to(sink.astype(jnp.float32).reshape(n_kv, G)[None, :, :, None, None],
                                    (B, n_kv, G, Lc, 1))
        s = jnp.concatenate([s, sink_col], axis=-1)
    p = jax.nn.softmax(s, axis=-1)[..., :Lc].astype(v.dtype)
    o = jnp.einsum('bhgqk,bkhd->bqhgd', p, v)
    return o.reshape(B, Lc, Hq * dh)


def neighbourhood_attention(q, k, v, k_ctx, v_ctx, rpb):
    B, L, H, dh = q.shape
    Lc = k_ctx.shape[1]
    rows = L // GRID_W
    kh = min(C_WIN_ROWS, rows)
    qbw = C_QBLOCK_COLS
    kbw = C_KBLOCK_COLS
    nm = GRID_W // qbw
    qg = q.reshape(B, rows, nm, qbw, H, dh)
    kg = k.reshape(B, rows, GRID_W, H, dh)
    vg = v.reshape(B, rows, GRID_W, H, dh)
    r = jnp.arange(rows)
    row_idx = jnp.clip(r - kh // 2, 0, rows - kh)[:, None] + jnp.arange(kh)[None]
    m = jnp.arange(nm)
    col_start = jnp.clip(m * qbw - C_WIN_COLS // 2, 0, GRID_W - kbw)
    col_idx = col_start[:, None] + jnp.arange(kbw)[None]
    ri = row_idx[:, None, :, None]
    ci = col_idx[None, :, None, :]
    k_nb = kg[:, ri, ci]
    v_nb = vg[:, ri, ci]
    scale = dh ** -0.5
    s_loc = jnp.einsum('brmqhd,brmijhd->brmhqij', qg, k_nb).astype(jnp.float32) * scale
    qcol = m[:, None] * qbw + jnp.arange(qbw)[None]
    wstart = jnp.clip(qcol - C_WIN_COLS // 2, 0, GRID_W - C_WIN_COLS)
    kcol = col_idx[:, None, :]
    col_ok = (kcol >= wstart[..., None]) & (kcol < wstart[..., None] + C_WIN_COLS)
    d_row = row_idx - r[:, None] + (C_WIN_ROWS - 1)
    d_col = jnp.clip(kcol - qcol[..., None], -(C_WIN_COLS - 1), C_WIN_COLS - 1) + (C_WIN_COLS - 1)
    bias = rpb[:, d_row[:, None, None, :, None], d_col[None, :, :, None, :]]
    bias = jnp.moveaxis(bias, 0, 2).astype(jnp.float32)
    s_loc = jnp.where(col_ok[None, None, :, None, :, None, :], s_loc + bias[None], -jnp.inf)
    s_loc = s_loc.reshape(B, rows, nm, H, qbw, kh * kbw)
    s_ctx = jnp.einsum('brmqhd,bchd->brmhqc', qg, k_ctx).astype(jnp.float32) * scale
    p = jax.nn.softmax(jnp.concatenate([s_loc, s_ctx], axis=-1), axis=-1)
    nk = kh * kbw
    p_loc = p[..., :nk].reshape(B, rows, nm, H, qbw, kh, kbw).astype(v.dtype)
    p_ctx = p[..., nk:].astype(v.dtype)
    o = (jnp.einsum('brmhqij,brmijhd->brmqhd', p_loc, v_nb)
         + jnp.einsum('brmhqc,bchd->brmqhd', p_ctx, v_ctx))
    return o.reshape(B, L, H * dh)


def gla_chunked(q, k, v, g, s0):
    B, H, L, dk = q.shape
    dv = v.shape[-1]
    C = B_CHUNK
    n = L // C
    qc = q.reshape(B, H, n, C, dk)
    kc = k.reshape(B, H, n, C, dk)
    vc = v.reshape(B, H, n, C, dv)
    b = jnp.cumsum(g.astype(jnp.float32).reshape(B, H, n, C, dk), axis=3)
    b_last = b[:, :, :, -1:]
    causal = jnp.tril(jnp.ones((C, C), dtype=bool))
    diff = b[:, :, :, :, None, :] - b[:, :, :, None, :, :]
    decay = jnp.exp(jnp.where(causal[:, :, None], diff, -jnp.inf))
    A = jnp.einsum('bhnid,bhnjd,bhnijd->bhnij', qc, kc, decay)
    o_intra = jnp.einsum('bhnij,bhnjd->bhnid', A, vc)
    k_tail = kc * jnp.exp(b_last - b)
    upd = jnp.einsum('bhncd,bhnce->bhnde', k_tail, vc)
    chunk_decay = jnp.exp(b_last[:, :, :, 0])

    def step(S, inp):
        dec, u = inp
        return dec[..., None] * S + u, S

    s_final, s_before = lax.scan(step, s0, (jnp.moveaxis(chunk_decay, 2, 0), jnp.moveaxis(upd, 2, 0)))
    s_before = jnp.moveaxis(s_before, 0, 2)
    o_inter = jnp.einsum('bhncd,bhnde->bhnce', qc * jnp.exp(b), s_before)
    o = (o_intra + o_inter).reshape(B, H, L, dv).astype(v.dtype)
    return o, s_final


def gla_bidirectional(q, k, v, g_fwd, g_bwd, s0_fwd, s0_bwd):
    o_f, s_f = gla_chunked(q, k, v, g_fwd, s0_fwd)
    flip = lambda t: jnp.flip(t, axis=2)
    o_b, s_b = gla_chunked(flip(q), flip(k), flip(v), flip(g_bwd), s0_bwd)
    return o_f + flip(o_b), s_f, s_b


def to_gla_heads(t, n_heads):
    return split_heads(t, n_heads).transpose(0, 2, 1, 3)


def gla_log_gates(z_low, w, bias):
    z = (z_low @ w + bias).astype(jnp.float32)
    return to_gla_heads(jax.nn.log_sigmoid(z) / B_GATE_NORM, B_HEADS)


def gla_output(o, r, norm_g):
    B, H, L, dv = o.shape
    o = rmsnorm(o.transpose(0, 2, 1, 3), norm_g).reshape(B, L, H * dv)
    return (o * jax.nn.silu(r)).astype(r.dtype)


def merge_branches(h, y_a, y_b, y_c, w_ba, w_bb, w_bc, w_merge, b_merge, w_out):
    gates = jax.nn.sigmoid(h @ w_merge + b_merge)
    g_a, g_b, g_c = jnp.split(gates, 3, axis=-1)
    mixed = g_a * (y_a @ w_ba) + g_b * (y_b @ w_bb) + g_c * (y_c @ w_bc)
    return mixed @ w_out


def swiglu(h, w_in, w_out):
    gate, up = jnp.split(h @ w_in, 2, axis=-1)
    return (jax.nn.silu(gate) * up) @ w_out


def setup_inputs(seed: int = 0) -> dict:
    key = jax.random.key(seed)
    ks = iter(jax.random.split(key, 32))

    def nrm(shape, scale):
        return jax.random.normal(next(ks), shape, jnp.float32) * scale

    D = D_MODEL
    return {
        'x': nrm((BATCH, SEQ, D), 1.0),
        'c': nrm((BATCH, D), 1.0),
        'ctx': nrm((BATCH, CTX_LEN, D), 1.0),
        'c_ctx': nrm((D,), 1.0),
        'w_ada': nrm((DEPTH, D, 6 * D), 0.5 * D ** -0.5),
        'b_ada': nrm((DEPTH, 6 * D), 0.02),
        'norm_mix': 1.0 + nrm((DEPTH, D), 0.02),
        'w_in': nrm((DEPTH, D, IN_TOTAL), D ** -0.5),
        'attn_sink': nrm((DEPTH, A_HEADS), 0.5),
        'gla_gate_w_fwd': nrm((DEPTH, B_GATE_RANK, B_HEADS * B_DK), B_GATE_RANK ** -0.5),
        'gla_gate_b_fwd': nrm((DEPTH, B_HEADS * B_DK), 0.1),
        'gla_gate_w_bwd': nrm((DEPTH, B_GATE_RANK, B_HEADS * B_DK), B_GATE_RANK ** -0.5),
        'gla_gate_b_bwd': nrm((DEPTH, B_HEADS * B_DK), 0.1),
        'gla_norm': 1.0 + nrm((DEPTH, B_DV), 0.02),
        'na_rpb': nrm((DEPTH, C_HEADS, 2 * C_WIN_ROWS - 1, 2 * C_WIN_COLS - 1), 0.02),
        'w_branch_a': nrm((DEPTH, A_HEADS * HEAD_DIM, D), (A_HEADS * HEAD_DIM) ** -0.5),
        'w_branch_b': nrm((DEPTH, B_HEADS * B_DV, D), (B_HEADS * B_DV) ** -0.5),
        'w_branch_c': nrm((DEPTH, C_HEADS * HEAD_DIM, D), (C_HEADS * HEAD_DIM) ** -0.5),
        'w_merge': nrm((DEPTH, D, 3 * D), D ** -0.5),
        'b_merge': nrm((DEPTH, 3 * D), 0.02),
        'w_out': nrm((DEPTH, D, D), D ** -0.5),
        'norm_ffn': 1.0 + nrm((DEPTH, D), 0.02),
        'w_ffn_in': nrm((DEPTH, D, 2 * FFN_HIDDEN), D ** -0.5),
        'w_ffn_out': nrm((DEPTH, FFN_HIDDEN, D), FFN_HIDDEN ** -0.5),
        'final_norm': 1.0 + nrm((D,), 0.02),
    }


def reference(x, c, ctx, c_ctx, w_ada, b_ada, norm_mix, w_in, attn_sink, gla_gate_w_fwd, gla_gate_b_fwd,
              gla_gate_w_bwd, gla_gate_b_bwd, gla_norm, na_rpb, w_branch_a, w_branch_b, w_branch_c,
              w_merge, b_merge, w_out, norm_ffn, w_ffn_in, w_ffn_out, final_norm):
    B, L, D = x.shape
    cos, sin = axial_rope_tables(L)
    xc = ctx
    for l in range(DEPTH):
        last = l == DEPTH - 1
        mod = (jax.nn.silu(c) @ w_ada[l] + b_ada[l])[:, None, :]
        mod_c = (jax.nn.silu(c_ctx) @ w_ada[l] + b_ada[l])[None, None, :]
        sh_m, sc_m, gt_m, sh_f, sc_f, gt_f = jnp.split(mod, 6, axis=-1)
        shc_m, scc_m, gtc_m, shc_f, scc_f, gtc_f = jnp.split(mod_c, 6, axis=-1)

        h = rmsnorm(x, norm_mix[l]) * (1.0 + sc_m) + sh_m
        hc = rmsnorm(xc, norm_mix[l]) * (1.0 + scc_m) + shc_m
        a_q, a_k, a_v, g_q, g_k, g_v, g_r, g_a, n_q, n_k, n_v = split_projection(h @ w_in[l])
        ac_q, ac_k, ac_v, gc_q, gc_k, gc_v, gc_r, gc_a, nc_q, nc_k, nc_v = split_projection(hc @ w_in[l])

        qa = apply_rope(split_heads(a_q, A_HEADS), cos, sin)
        ka = apply_rope(split_heads(a_k, A_KV_HEADS), cos, sin)
        va = split_heads(a_v, A_KV_HEADS)
        qa_c = split_heads(ac_q, A_HEADS)
        ka_c = split_heads(ac_k, A_KV_HEADS)
        va_c = split_heads(ac_v, A_KV_HEADS)
        y_a = window_attention(qa, ka, va, ka_c, va_c, attn_sink[l])

        R = B_GATE_RANK
        gf_c = gla_log_gates(gc_a[..., :R], gla_gate_w_fwd[l], gla_gate_b_fwd[l])
        gb_c = gla_log_gates(gc_a[..., R:], gla_gate_w_bwd[l], gla_gate_b_bwd[l])
        zeros = jnp.zeros((B, B_HEADS, B_DK, B_DV), jnp.float32)
        o_c, s_ctx_f, s_ctx_b = gla_bidirectional(
            to_gla_heads(gc_q * B_DK ** -0.5, B_HEADS), to_gla_heads(gc_k, B_HEADS), to_gla_heads(gc_v, B_HEADS),
            gf_c, gb_c, zeros, zeros)
        gf = gla_log_gates(g_a[..., :R], gla_gate_w_fwd[l], gla_gate_b_fwd[l])
        gb = gla_log_gates(g_a[..., R:], gla_gate_w_bwd[l], gla_gate_b_bwd[l])
        o_l, _, _ = gla_bidirectional(
            to_gla_heads(g_q * B_DK ** -0.5, B_HEADS), to_gla_heads(g_k, B_HEADS), to_gla_heads(g_v, B_HEADS),
            gf, gb, s_ctx_f, s_ctx_b)
        y_b = gla_output(o_l, g_r, gla_norm[l])

        nk_c = split_heads(nc_k, C_HEADS)
        nv_c = split_heads(nc_v, C_HEADS)
        y_c = neighbourhood_attention(split_heads(n_q, C_HEADS), split_heads(n_k, C_HEADS),
                                      split_heads(n_v, C_HEADS), nk_c, nv_c, na_rpb[l])

        x = x + gt_m * merge_branches(h, y_a, y_b, y_c, w_branch_a[l], w_branch_b[l], w_branch_c[l],
                                      w_merge[l], b_merge[l], w_out[l])
        if not last:
            yc_a = dense_ctx_attention(qa_c, ka_c, va_c, A_KV_HEADS, attn_sink[l])
            yc_b = gla_output(o_c, gc_r, gla_norm[l])
            yc_c = dense_ctx_attention(split_heads(nc_q, C_HEADS), nk_c, nv_c, C_HEADS, None)
            xc = xc + gtc_m * merge_branches(hc, yc_a, yc_b, yc_c, w_branch_a[l], w_branch_b[l],
                                             w_branch_c[l], w_merge[l], b_merge[l], w_out[l])

        hf = rmsnorm(x, norm_ffn[l]) * (1.0 + sc_f) + sh_f
        x = x + gt_f * swiglu(hf, w_ffn_in[l], w_ffn_out[l])
        if not last:
            hfc = rmsnorm(xc, norm_ffn[l]) * (1.0 + scc_f) + shc_f
            xc = xc + gtc_f * swiglu(hfc, w_ffn_in[l], w_ffn_out[l])
    return rmsnorm(x, final_norm)
```

```python
import functools

import jax
import jax.numpy as jnp
from jax import lax
from jax.experimental import pallas as pl
from jax.experimental.pallas import tpu as pltpu

F32 = jnp.float32
BF16 = jnp.bfloat16

D_MODEL = 1024
DEPTH = 2
GRID_W = 64
HEAD_DIM = 64
EPS = 1e-6
ROPE_BASE = 10000.0
A_HEADS = 8
A_KV_HEADS = 2
A_BLOCK = 128
B_HEADS = 4
B_DK = 64
B_DV = 128
B_GATE_RANK = 16
B_GATE_NORM = 16.0
C_HEADS = 8
C_WIN_ROWS = 8
C_WIN_COLS = 16
FFN_HIDDEN = 2816

VMEM_LIMIT_BYTES = 56 * 1024 * 1024
LANES = 128

SEG_AQ = (0, 512)
SEG_AKV = (512, 768)
SEG_GQ = (768, 1024)
SEG_GK = (1024, 1280)
SEG_GV = (1280, 1792)
SEG_GR = (1792, 2304)
SEG_NQ = (2304, 2816)
SEG_NK = (2816, 3328)
SEG_NV = (3328, 3840)
SEG_GA = (3840, 3968)
IN_PADDED = 3968

GLA_CHUNK = 64


def _params(sem):
    return pltpu.CompilerParams(dimension_semantics=sem, vmem_limit_bytes=VMEM_LIMIT_BYTES)


def _const_spec(shape):
    nd = len(shape)
    return pl.BlockSpec(shape, lambda *_: (0,) * nd, pipeline_mode=pl.Buffered(1))


def _sigmoid(x):
    return 1.0 / (1.0 + jnp.exp(-x))


def _silu(x):
    return x * _sigmoid(x)


def _dot(a, b):
    return jnp.dot(a, b, preferred_element_type=F32)


def _dot_nt(a, b):
    return lax.dot_general(a, b, (((1,), (1,)), ((), ())), preferred_element_type=F32)


def _dot_tn(a, b):
    return lax.dot_general(a, b, (((0,), (0,)), ((), ())), preferred_element_type=F32)


def _norm_mod(x, gain, shift, scale):
    y = x * lax.rsqrt(jnp.mean(x * x, axis=-1, keepdims=True) + EPS)
    return (y * gain) * (1.0 + scale) + shift


def _mod_kernel(c_ref, w_ref, b_ref, o_ref):
    s = _silu(c_ref[...])
    o_ref[...] = _dot(s.astype(BF16), w_ref[...].astype(BF16)) + b_ref[...]


def _modulation(cvec, w_ada, b_ada):
    tn = 1536
    n_out = w_ada.shape[-1]
    return pl.pallas_call(
        _mod_kernel,
        grid=(DEPTH, n_out // tn),
        in_specs=[
            pl.BlockSpec((8, D_MODEL), lambda l, j: (0, 0)),
            pl.BlockSpec((None, D_MODEL, tn), lambda l, j: (l, 0, j)),
            pl.BlockSpec((None, 1, tn), lambda l, j: (l, 0, j)),
        ],
        out_specs=pl.BlockSpec((None, 8, tn), lambda l, j: (l, 0, j)),
        out_shape=jax.ShapeDtypeStruct((DEPTH, 8, n_out), F32),
        compiler_params=_params(("arbitrary", "arbitrary")),
        name="adaln_mod",
    )(cvec, w_ada, b_ada.reshape(DEPTH, 1, n_out))


def _rope(t, cos, sin):
    n = t.shape[1]
    lane = lax.broadcasted_iota(jnp.int32, t.shape, 1)
    first_half = (lane % HEAD_DIM) < (HEAD_DIM // 2)
    rot = jnp.where(first_half, pltpu.roll(t, n - HEAD_DIM // 2, 1), pltpu.roll(t, HEAD_DIM // 2, 1))
    reps = n // LANES
    return t * jnp.tile(cos, (1, reps)) + rot * jnp.tile(sin, (1, reps))


def _proj_kernel(x_ref, mod_ref, gain_ref, w_ref, wg_ref, bg_ref, cos_ref, sin_ref,
                 aq_ref, ak_ref, av_ref, gq_ref, gk_ref, gv_ref, gr_ref, nq_ref, nk_ref, nv_ref,
                 gf_ref, gb_ref, *, rope):
    h = _norm_mod(x_ref[...], gain_ref[...], mod_ref[0:1, :], mod_ref[1:2, :]).astype(BF16)

    def seg(s):
        return _dot(h, w_ref[:, s[0]:s[1]])

    scale = HEAD_DIM ** -0.5
    aq = seg(SEG_AQ)
    akv = seg(SEG_AKV)
    ak = akv[:, :LANES]
    if rope:
        cos = cos_ref[...]
        sin = sin_ref[...]
        aq = _rope(aq, cos, sin)
        ak = _rope(ak, cos, sin)
    aq_ref[...] = (aq * scale).astype(BF16)
    ak_ref[...] = ak.astype(BF16)
    av_ref[...] = akv[:, LANES:].astype(BF16)
    gq_ref[...] = seg(SEG_GQ) * (B_DK ** -0.5)
    gk_ref[...] = seg(SEG_GK)
    gv_ref[...] = seg(SEG_GV).astype(BF16)
    gr_ref[...] = seg(SEG_GR)
    nq_ref[...] = (seg(SEG_NQ) * scale).astype(BF16)
    nk_ref[...] = seg(SEG_NK).astype(BF16)
    nv_ref[...] = seg(SEG_NV).astype(BF16)
    z = _dot(seg(SEG_GA).astype(BF16), wg_ref[...]) + bg_ref[...]
    log_sig = jnp.minimum(z, 0.0) - jnp.log1p(jnp.exp(-jnp.abs(z)))
    g = log_sig / B_GATE_NORM
    n_g = B_HEADS * B_DK
    gf_ref[...] = g[:, :n_g]
    gb_ref[...] = g[:, n_g:]


def _project(x, mod_rows, gain, w_perm, w_gate, b_gate, cos, sin, *, rope, tm):
    bsz, seq, _ = x.shape
    per_batch_mod = mod_rows.shape[0] > 1
    mod_map = (lambda b, i: (b, 0, 0)) if per_batch_mod else (lambda b, i: (0, 0, 0))
    widths = [(512, BF16), (128, BF16), (128, BF16), (256, F32), (256, F32), (512, BF16), (512, F32),
              (512, BF16), (512, BF16), (512, BF16), (256, F32), (256, F32)]
    tile = lambda w: pl.BlockSpec((None, tm, w), lambda b, i: (b, i, 0))
    return pl.pallas_call(
        functools.partial(_proj_kernel, rope=rope),
        grid=(bsz, seq // tm),
        in_specs=[
            tile(D_MODEL),
            pl.BlockSpec((None, 6, D_MODEL), mod_map),
            _const_spec((1, D_MODEL)),
            _const_spec((D_MODEL, IN_PADDED)),
            _const_spec((LANES, 2 * B_HEADS * B_DK)),
            _const_spec((1, 2 * B_HEADS * B_DK)),
            pl.BlockSpec((tm, LANES), lambda b, i: (i, 0)),
            pl.BlockSpec((tm, LANES), lambda b, i: (i, 0)),
        ],
        out_specs=[tile(w) for w, _ in widths],
        out_shape=[jax.ShapeDtypeStruct((bsz, seq, w), dt) for w, dt in widths],
        compiler_params=_params(("parallel", "arbitrary")),
        name="in_proj_rope" if rope else "in_proj_ctx",
    )(x, mod_rows, gain, w_perm, w_gate, b_gate, cos, sin)


def _gla_chunk(q, k, v, g, s_ref, reverse):
    c = q.shape[0]
    row = lax.broadcasted_iota(jnp.int32, (c, c), 0)
    col = lax.broadcasted_iota(jnp.int32, (c, c), 1)
    keep = (col >= row) if reverse else (col <= row)
    tri = keep.astype(F32)
    b = jnp.dot(tri, g, precision=lax.Precision.HIGHEST, preferred_element_type=F32)
    b_end = b[0:1, :] if reverse else b[c - 1:c, :]
    ref = 0.5 * b_end
    q_state = (q * jnp.exp(b)).astype(BF16)
    q_in = (q * jnp.exp(b - ref)).astype(BF16)
    k_in = (k * jnp.exp(ref - b)).astype(BF16)
    k_tail = (k * jnp.exp(b_end - b)).astype(BF16)
    decay = jnp.exp(b_end)
    outs = []
    for h in range(B_HEADS):
        ks = slice(h * B_DK, (h + 1) * B_DK)
        vs = slice(h * B_DV, (h + 1) * B_DV)
        a = jnp.where(keep, _dot_nt(q_in[:, ks], k_in[:, ks]), 0.0)
        s_t = s_ref[h]
        vh = v[:, vs]
        o = _dot(a.astype(BF16), vh) + _dot_nt(q_state[:, ks], s_t.astype(BF16))
        s_ref[h] = s_t * decay[:, ks] + _dot_tn(vh, k_tail[:, ks])
        outs.append(o)
    return jnp.concatenate(outs, axis=1)


def _gla_kernel(qf_ref, kf_ref, vf_ref, gf_ref, qb_ref, kb_ref, vb_ref, gb_ref, s0f_ref, s0b_ref,
                of_ref, ob_ref, sf_out_ref, sb_out_ref, sf_ref, sb_ref):
    i = pl.program_id(1)

    @pl.when(i == 0)
    def _():
        sf_ref[...] = s0f_ref[...]
        sb_ref[...] = s0b_ref[...]

    n_chunks = qf_ref.shape[0] // GLA_CHUNK
    for ci in range(n_chunks):
        fs = slice(ci * GLA_CHUNK, (ci + 1) * GLA_CHUNK)
        of_ref[fs, :] = _gla_chunk(qf_ref[fs, :], kf_ref[fs, :], vf_ref[fs, :], gf_ref[fs, :], sf_ref, False)
        cb = n_chunks - 1 - ci
        bs = slice(cb * GLA_CHUNK, (cb + 1) * GLA_CHUNK)
        ob_ref[bs, :] = _gla_chunk(qb_ref[bs, :], kb_ref[bs, :], vb_ref[bs, :], gb_ref[bs, :], sb_ref, True)

    @pl.when(i == pl.num_programs(1) - 1)
    def _():
        sf_out_ref[...] = sf_ref[...]
        sb_out_ref[...] = sb_ref[...]


def _gla(q, k, v, gf, gb, s0f, s0b, *, tile):
    bsz, seq, _ = q.shape
    n = seq // tile
    fwd = lambda w: pl.BlockSpec((None, tile, w), lambda b, i: (b, i, 0))
    bwd = lambda w: pl.BlockSpec((None, tile, w), lambda b, i: (b, n - 1 - i, 0))
    st = pl.BlockSpec((None, B_HEADS, B_DV, B_DK), lambda b, i: (b, 0, 0, 0))
    wk, wv = B_HEADS * B_DK, B_HEADS * B_DV
    st_shape = jax.ShapeDtypeStruct((bsz, B_HEADS, B_DV, B_DK), F32)
    return pl.pallas_call(
        _gla_kernel,
        grid=(bsz, n),
        in_specs=[fwd(wk), fwd(wk), fwd(wv), fwd(wk), bwd(wk), bwd(wk), bwd(wv), bwd(wk), st, st],
        out_specs=[fwd(wv), bwd(wv), st, st],
        out_shape=[jax.ShapeDtypeStruct((bsz, seq, wv), F32), jax.ShapeDtypeStruct((bsz, seq, wv), F32),
                   st_shape, st_shape],
        scratch_shapes=[pltpu.VMEM((B_HEADS, B_DV, B_DK), F32), pltpu.VMEM((B_HEADS, B_DV, B_DK), F32)],
        compiler_params=_params(("parallel", "arbitrary")),
        name="gla_scan",
    )(q, k, v, gf, q, k, v, gb, s0f, s0b)


def _softmax_parts(scores, sink_col):
    m = scores[0].max(axis=-1, keepdims=True)
    for s in scores[1:]:
        m = jnp.maximum(m, s.max(axis=-1, keepdims=True))
    if sink_col is not None:
        m = jnp.maximum(m, sink_col)
    ps = [jnp.exp(s - m) for s in scores]
    denom = ps[0].sum(axis=-1, keepdims=True)
    for p in ps[1:]:
        denom = denom + p.sum(axis=-1, keepdims=True)
    if sink_col is not None:
        denom = denom + jnp.exp(sink_col - m)
    return ps, denom


def _win_kernel(sink_ref, q_ref, kp_ref, kc_ref, kn_ref, vp_ref, vc_ref, vn_ref, kx_ref, vx_ref, o_ref):
    n = pl.program_id(1)
    nb = pl.num_programs(1)
    group = A_HEADS // A_KV_HEADS
    rows = group * A_BLOCK
    qi = lax.broadcasted_iota(jnp.int32, (rows, A_BLOCK), 0) % A_BLOCK
    kj = lax.broadcasted_iota(jnp.int32, (rows, A_BLOCK), 1)
    prev_ok = (kj >= qi) & (n > 0)
    next_ok = (kj <= qi) & (n < nb - 1)
    neg = -jnp.inf
    for hk in range(A_KV_HEADS):
        ks = slice(hk * HEAD_DIM, (hk + 1) * HEAD_DIM)
        heads = [hk * group + g for g in range(group)]
        q = jnp.concatenate([q_ref[:, h * HEAD_DIM:(h + 1) * HEAD_DIM] for h in heads], axis=0)
        sink_col = jnp.concatenate([jnp.full((A_BLOCK, 1), sink_ref[h], F32) for h in heads], axis=0)
        s_p = jnp.where(prev_ok, _dot_nt(q, kp_ref[:, ks]), neg)
        s_c = _dot_nt(q, kc_ref[:, ks])
        s_n = jnp.where(next_ok, _dot_nt(q, kn_ref[:, ks]), neg)
        s_x = _dot_nt(q, kx_ref[:, ks])
        (p_p, p_c, p_n, p_x), denom = _softmax_parts([s_p, s_c, s_n, s_x], sink_col)
        o = (_dot(p_p.astype(BF16), vp_ref[:, ks]) + _dot(p_c.astype(BF16), vc_ref[:, ks])
             + _dot(p_n.astype(BF16), vn_ref[:, ks]) + _dot(p_x.astype(BF16), vx_ref[:, ks]))
        o = o / denom
        for g, h in enumerate(heads):
            o_ref[:, h * HEAD_DIM:(h + 1) * HEAD_DIM] = o[g * A_BLOCK:(g + 1) * A_BLOCK].astype(o_ref.dtype)


def _window_attention(sink, q, k, v, k_ctx, v_ctx):
    bsz, seq, wq = q.shape
    nb = seq // A_BLOCK
    wkv = k.shape[-1]
    n_ctx = k_ctx.shape[1]
    kv_spec = lambda f: pl.BlockSpec((None, A_BLOCK, wkv), lambda b, n: (b, f(n), 0))
    prev = lambda n: jnp.maximum(n - 1, 0)
    cur = lambda n: n
    nxt = lambda n: jnp.minimum(n + 1, nb - 1)
    ctx_spec = pl.BlockSpec((None, n_ctx, wkv), lambda b, n: (b, 0, 0))
    return pl.pallas_call(
        _win_kernel,
        grid=(bsz, nb),
        in_specs=[
            pl.BlockSpec(memory_space=pltpu.SMEM),
            pl.BlockSpec((None, A_BLOCK, wq), lambda b, n: (b, n, 0)),
            kv_spec(prev), kv_spec(cur), kv_spec(nxt),
            kv_spec(prev), kv_spec(cur), kv_spec(nxt),
            ctx_spec, ctx_spec,
        ],
        out_specs=pl.BlockSpec((None, A_BLOCK, wq), lambda b, n: (b, n, 0)),
        out_shape=jax.ShapeDtypeStruct((bsz, seq, wq), BF16),
        compiler_params=_params(("parallel", "arbitrary")),
        name="window_attn",
    )(sink, q, k, k, k, v, v, v, k_ctx, v_ctx)


def _dense_kernel(sink_ref, q_ref, k_ref, v_ref, o_ref, *, n_heads, n_kv, use_sink):
    group = n_heads // n_kv
    n_q = q_ref.shape[0]
    for h in range(n_heads):
        qs = slice(h * HEAD_DIM, (h + 1) * HEAD_DIM)
        ks = slice((h // group) * HEAD_DIM, (h // group + 1) * HEAD_DIM)
        s = _dot_nt(q_ref[:, qs], k_ref[:, ks])
        sink_col = jnp.full((n_q, 1), sink_ref[h], F32) if use_sink else None
        (p,), denom = _softmax_parts([s], sink_col)
        o_ref[:, qs] = (_dot(p.astype(BF16), v_ref[:, ks]) / denom).astype(o_ref.dtype)


def _dense_attention(sink, q, k, v, *, n_heads, n_kv, use_sink):
    bsz, n_q, wq = q.shape
    wkv = k.shape[-1]
    full = lambda w: pl.BlockSpec((None, n_q, w), lambda b: (b, 0, 0))
    return pl.pallas_call(
        functools.partial(_dense_kernel, n_heads=n_heads, n_kv=n_kv, use_sink=use_sink),
        grid=(bsz,),
        in_specs=[pl.BlockSpec(memory_space=pltpu.SMEM), full(wq), full(wkv), full(wkv)],
        out_specs=full(wq),
        out_shape=jax.ShapeDtypeStruct((bsz, n_q, wq), BF16),
        compiler_params=_params(("parallel",)),
        name="ctx_dense_attn_sink" if use_sink else "ctx_dense_attn",
    )(sink, q, k, v)


NA_ROWS_PER_STEP = 4


def _na_kernel(q_ref, k_ref, v_ref, kx_ref, vx_ref, bias_ref, o_ref):
    step = pl.program_id(1)
    grid_rows = k_ref.shape[0] // GRID_W
    n_keys = C_WIN_ROWS * GRID_W
    for rr in range(NA_ROWS_PER_STEP):
        r = step * NA_ROWS_PER_STEP + rr
        row0 = jnp.clip(r - C_WIN_ROWS // 2, 0, grid_rows - C_WIN_ROWS)
        d_row0 = row0 - r + (C_WIN_ROWS - 1)
        key0 = pl.multiple_of(row0 * GRID_W, GRID_W)
        qrows = slice(rr * GRID_W, (rr + 1) * GRID_W)
        for h in range(C_HEADS):
            hs = slice(h * HEAD_DIM, (h + 1) * HEAD_DIM)
            q = q_ref[qrows, hs]
            k_loc = k_ref[pl.ds(key0, n_keys), hs]
            v_loc = v_ref[pl.ds(key0, n_keys), hs]
            bias = jnp.concatenate([bias_ref[d_row0 + 2 * j, h] for j in range(C_WIN_ROWS // 2)], axis=1)
            s_l = _dot_nt(q, k_loc) + bias
            s_x = _dot_nt(q, kx_ref[:, hs])
            (p_l, p_x), denom = _softmax_parts([s_l, s_x], None)
            o = (_dot(p_l.astype(BF16), v_loc) + _dot(p_x.astype(BF16), vx_ref[:, hs])) / denom
            o_ref[qrows, hs] = o.astype(o_ref.dtype)


def _neighbourhood_attention(q, k, v, k_ctx, v_ctx, bias_tbl):
    bsz, seq, w = q.shape
    n_ctx = k_ctx.shape[1]
    tq = NA_ROWS_PER_STEP * GRID_W
    whole = pl.BlockSpec((None, seq, w), lambda b, i: (b, 0, 0))
    ctx_spec = pl.BlockSpec((None, n_ctx, w), lambda b, i: (b, 0, 0))
    return pl.pallas_call(
        _na_kernel,
        grid=(bsz, seq // tq),
        in_specs=[pl.BlockSpec((None, tq, w), lambda b, i: (b, i, 0)), whole, whole, ctx_spec, ctx_spec,
                  _const_spec(bias_tbl.shape)],
        out_specs=pl.BlockSpec((None, tq, w), lambda b, i: (b, i, 0)),
        out_shape=jax.ShapeDtypeStruct((bsz, seq, w), BF16),
        compiler_params=_params(("parallel", "arbitrary")),
        name="neighbourhood_attn",
    )(q, k, v, k_ctx, v_ctx, bias_tbl)


def _na_bias_table(rpb):
    qcol = jnp.arange(GRID_W)[:, None]
    kcol = jnp.arange(GRID_W)[None, :]
    wstart = jnp.clip(qcol - C_WIN_COLS // 2, 0, GRID_W - C_WIN_COLS)
    ok = (kcol >= wstart) & (kcol < wstart + C_WIN_COLS)
    d_col = jnp.clip(kcol - qcol, -(C_WIN_COLS - 1), C_WIN_COLS - 1) + (C_WIN_COLS - 1)
    tbl = jnp.where(ok[None, None], rpb[:, :, d_col].astype(F32), -jnp.inf)
    tbl = jnp.moveaxis(tbl, 1, 0)
    return jnp.concatenate([tbl[:-1], tbl[1:]], axis=-1)


def _merge_kernel(x_ref, mod_ref, gain_ref, ya_ref, of_ref, ob_ref, gr_ref, yc_ref, gn_ref,
                  wm_ref, bm_ref, wa_ref, wb_ref, wc_ref, wo_ref, o_ref):
    x = x_ref[...]
    h = _norm_mod(x, gain_ref[...], mod_ref[0:1, :], mod_ref[1:2, :]).astype(BF16)
    o_sum = of_ref[...] + ob_ref[...]
    gn = gn_ref[...]
    parts = []
    for hh in range(B_HEADS):
        oh = o_sum[:, hh * B_DV:(hh + 1) * B_DV]
        parts.append((oh * lax.rsqrt(jnp.mean(oh * oh, axis=-1, keepdims=True) + EPS)) * gn)
    y_b = (jnp.concatenate(parts, axis=1) * _silu(gr_ref[...])).astype(BF16)
    d = D_MODEL
    mixed = None
    for j, (y, w_ref) in enumerate(((ya_ref[...], wa_ref), (y_b, wb_ref), (yc_ref[...], wc_ref))):
        gate = _sigmoid(_dot(h, wm_ref[:, j * d:(j + 1) * d]) + bm_ref[:, j * d:(j + 1) * d])
        term = gate * _dot(y, w_ref[...])
        mixed = term if mixed is None else mixed + term
    o_ref[...] = x + mod_ref[2:3, :] * _dot(mixed.astype(BF16), wo_ref[...])


def _merge(x, mod_rows, gain, y_a, o_f, o_b, g_r, y_c, gla_gain, w_merge, b_merge, w_a, w_b, w_c, w_out, *, tm):
    bsz, seq, d = x.shape
    per_batch_mod = mod_rows.shape[0] > 1
    mod_map = (lambda b, i: (b, 0, 0)) if per_batch_mod else (lambda b, i: (0, 0, 0))
    tile = lambda w: pl.BlockSpec((None, tm, w), lambda b, i: (b, i, 0))
    wy = y_a.shape[-1]
    return pl.pallas_call(
        _merge_kernel,
        grid=(bsz, seq // tm),
        in_specs=[tile(d), pl.BlockSpec((None, 6, d), mod_map), _const_spec((1, d)),
                  tile(wy), tile(wy), tile(wy), tile(wy), tile(wy), _const_spec((1, B_DV)),
                  _const_spec(w_merge.shape), _const_spec(b_merge.shape),
                  _const_spec(w_a.shape), _const_spec(w_b.shape), _const_spec(w_c.shape),
                  _const_spec(w_out.shape)],
        out_specs=tile(d),
        out_shape=jax.ShapeDtypeStruct((bsz, seq, d), F32),
        compiler_params=_params(("parallel", "arbitrary")),
        name="merge_out",
    )(x, mod_rows, gain, y_a, o_f, o_b, g_r, y_c, gla_gain, w_merge, b_merge, w_a, w_b, w_c, w_out)


FFN_CHUNK = 1408


def _ffn_kernel(x_ref, mod_ref, gain_ref, w1_ref, w2_ref, fg_ref, o_ref, *, final):
    x = x_ref[...]
    h = _norm_mod(x, gain_ref[...], mod_ref[3:4, :], mod_ref[4:5, :]).astype(BF16)
    acc = None
    for c0 in range(0, FFN_HIDDEN, FFN_CHUNK):
        gate = _dot(h, w1_ref[:, c0:c0 + FFN_CHUNK])
        up = _dot(h, w1_ref[:, FFN_HIDDEN + c0:FFN_HIDDEN + c0 + FFN_CHUNK])
        part = _dot((_silu(gate) * up).astype(BF16), w2_ref[c0:c0 + FFN_CHUNK, :])
        acc = part if acc is None else acc + part
    y = x + mod_ref[5:6, :] * acc
    if final:
        y = (y * lax.rsqrt(jnp.mean(y * y, axis=-1, keepdims=True) + EPS)) * fg_ref[...]
    o_ref[...] = y


def _ffn(x, mod_rows, gain, w1, w2, final_gain, *, final, tm):
    bsz, seq, d = x.shape
    per_batch_mod = mod_rows.shape[0] > 1
    mod_map = (lambda b, i: (b, 0, 0)) if per_batch_mod else (lambda b, i: (0, 0, 0))
    tile = pl.BlockSpec((None, tm, d), lambda b, i: (b, i, 0))
    return pl.pallas_call(
        functools.partial(_ffn_kernel, final=final),
        grid=(bsz, seq // tm),
        in_specs=[tile, pl.BlockSpec((None, 6, d), mod_map), _const_spec((1, d)),
                  _const_spec(w1.shape), _const_spec(w2.shape), _const_spec((1, d))],
        out_specs=tile,
        out_shape=jax.ShapeDtypeStruct((bsz, seq, d), F32),
        compiler_params=_params(("parallel", "arbitrary")),
        name="ffn_final" if final else "ffn",
    )(x, mod_rows, gain, w1, w2, final_gain)


def _rope_tables(seq):
    t = jnp.arange(seq)
    row = (t // GRID_W).astype(F32)
    col = (t % GRID_W).astype(F32)
    n_freq = HEAD_DIM // 4
    inv = ROPE_BASE ** (-jnp.arange(n_freq, dtype=F32) / n_freq)
    ang = jnp.concatenate([row[:, None] * inv[None], col[:, None] * inv[None]], axis=-1)
    cos, sin = jnp.cos(ang), jnp.sin(ang)
    return jnp.tile(cos, (1, 4)), jnp.tile(jnp.concatenate([-sin, sin], axis=-1), (1, 2))


def _permute_w_in(w):
    ga0 = 2304
    ga1 = ga0 + 2 * B_GATE_RANK
    pad = jnp.zeros((w.shape[0], IN_PADDED - w.shape[1]), w.dtype)
    return jnp.concatenate([w[:, :ga0], w[:, ga1:], w[:, ga0:ga1], pad], axis=1).astype(BF16)


def _gate_weights(w_fwd, b_fwd, w_bwd, b_bwd):
    n = B_HEADS * B_DK
    r = B_GATE_RANK
    w = jnp.zeros((LANES, 2 * n), F32)
    w = w.at[:r, :n].set(w_fwd).at[r:2 * r, n:].set(w_bwd)
    return w.astype(BF16), jnp.concatenate([b_fwd, b_bwd])[None, :]


def kernel(x, c, ctx, c_ctx, w_ada, b_ada, norm_mix, w_in, attn_sink, gla_gate_w_fwd, gla_gate_b_fwd,
           gla_gate_w_bwd, gla_gate_b_bwd, gla_norm, na_rpb, w_branch_a, w_branch_b, w_branch_c,
           w_merge, b_merge, w_out, norm_ffn, w_ffn_in, w_ffn_out, final_norm):
    bsz, seq, d = x.shape
    n_ctx = ctx.shape[1]
    cos, sin = _rope_tables(seq)

    cvec = jnp.zeros((8, d), F32).at[:bsz].set(c).at[bsz].set(c_ctx)
    mod = _modulation(cvec, w_ada, b_ada).reshape(DEPTH, 8, 6, d)

    xc = ctx
    zeros_state = jnp.zeros((bsz, B_HEADS, B_DV, B_DK), F32)
    for l in range(DEPTH):
        last = l == DEPTH - 1
        mod_x = mod[l, :bsz]
        mod_c = mod[l, bsz:bsz + 1]
        gain_m = norm_mix[l][None, :]
        w_perm = _permute_w_in(w_in[l])
        w_gate, b_gate = _gate_weights(gla_gate_w_fwd[l], gla_gate_b_fwd[l], gla_gate_w_bwd[l], gla_gate_b_bwd[l])

        (caq, cak, cav, cgq, cgk, cgv, cgr, cnq, cnk, cnv, cgf, cgb) = _project(
            xc, mod_c, gain_m, w_perm, w_gate, b_gate, cos, sin, rope=False, tm=n_ctx)
        (aq, ak, av, gq, gk, gv, gr, nq, nk, nv, gf, gb) = _project(
            x, mod_x, gain_m, w_perm, w_gate, b_gate, cos, sin, rope=True, tm=512)

        co_f, co_b, s_f, s_b = _gla(cgq, cgk, cgv, cgf, cgb, zeros_state, zeros_state, tile=n_ctx)
        o_f, o_b, _, _ = _gla(gq, gk, gv, gf, gb, s_f, s_b, tile=256)

        y_a = _window_attention(attn_sink[l], aq, ak, av, cak, cav)
        y_c = _neighbourhood_attention(nq, nk, nv, cnk, cnv, _na_bias_table(na_rpb[l]))

        merge_w = (gla_norm[l][None, :], w_merge[l].astype(BF16), b_merge[l][None, :],
                   w_branch_a[l].astype(BF16), w_branch_b[l].astype(BF16), w_branch_c[l].astype(BF16),
                   w_out[l].astype(BF16))
        x = _merge(x, mod_x, gain_m, y_a, o_f, o_b, gr, y_c, *merge_w, tm=512)

        gain_f = norm_ffn[l][None, :]
        w1 = w_ffn_in[l].astype(BF16)
        w2 = w_ffn_out[l].astype(BF16)
        fg = final_norm[None, :]
        if not last:
            yc_a = _dense_attention(attn_sink[l], caq, cak, cav, n_heads=A_HEADS, n_kv=A_KV_HEADS, use_sink=True)
            yc_c = _dense_attention(attn_sink[l], cnq, cnk, cnv, n_heads=C_HEADS, n_kv=C_HEADS, use_sink=False)
            xc = _merge(xc, mod_c, gain_m, yc_a, co_f, co_b, cgr, yc_c, *merge_w, tm=n_ctx)
            xc = _ffn(xc, mod_c, gain_f, w1, w2, fg, final=False, tm=n_ctx)
        x = _ffn(x, mod_x, gain_f, w1, w2, fg, final=last, tm=512)
    return x
```

```python
import functools

import jax
import jax.numpy as jnp
from jax import lax
from jax.experimental import pallas as pl
from jax.experimental.pallas import tpu as pltpu

F32 = jnp.float32
BF16 = jnp.bfloat16

D_MODEL = 1024
DEPTH = 2
GRID_W = 64
HEAD_DIM = 64
EPS = 1e-6
ROPE_BASE = 10000.0
A_HEADS = 8
A_KV_HEADS = 2
A_BLOCK = 128
B_HEADS = 4
B_DK = 64
B_DV = 128
B_GATE_RANK = 16
B_GATE_NORM = 16.0
C_HEADS = 8
C_WIN_ROWS = 8
C_WIN_COLS = 16
FFN_HIDDEN = 2816

VMEM_LIMIT_BYTES = 56 * 1024 * 1024
LANES = 128

SEG_AQ = (0, 512)
SEG_AKV = (512, 768)
SEG_GQ = (768, 1024)
SEG_GK = (1024, 1280)
SEG_GV = (1280, 1792)
SEG_GR = (1792, 2304)
SEG_NQ = (2304, 2816)
SEG_NK = (2816, 3328)
SEG_NV = (3328, 3840)
SEG_GA = (3840, 3968)
IN_PADDED = 3968

GLA_CHUNK = 64


def _params(sem):
    return pltpu.CompilerParams(dimension_semantics=sem, vmem_limit_bytes=VMEM_LIMIT_BYTES)


def _const_spec(shape):
    nd = len(shape)
    return pl.BlockSpec(shape, lambda *_: (0,) * nd, pipeline_mode=pl.Buffered(1))


def _sigmoid(x):
    return 1.0 / (1.0 + jnp.exp(-x))


def _silu(x):
    return x * _sigmoid(x)


def _dot(a, b):
    return jnp.dot(a, b, preferred_element_type=F32)


def _dot_nt(a, b):
    return lax.dot_general(a, b, (((1,), (1,)), ((), ())), preferred_element_type=F32)


def _dot_tn(a, b):
    return lax.dot_general(a, b, (((0,), (0,)), ((), ())), preferred_element_type=F32)


def _norm_mod(x, gain, shift, scale):
    y = x * lax.rsqrt(jnp.mean(x * x, axis=-1, keepdims=True) + EPS)
    return (y * gain) * (1.0 + scale) + shift


def _mod_kernel(c_ref, w_ref, b_ref, o_ref):
    s = _silu(c_ref[...])
    o_ref[...] = _dot(s.astype(BF16), w_ref[...].astype(BF16)) + b_ref[...]


def _modulation(cvec, w_ada, b_ada):
    tn = 1536
    n_out = w_ada.shape[-1]
    return pl.pallas_call(
        _mod_kernel,
        grid=(DEPTH, n_out // tn),
        in_specs=[
            pl.BlockSpec((8, D_MODEL), lambda l, j: (0, 0)),
            pl.BlockSpec((None, D_MODEL, tn), lambda l, j: (l, 0, j)),
            pl.BlockSpec((None, 1, tn), lambda l, j: (l, 0, j)),
        ],
        out_specs=pl.BlockSpec((None, 8, tn), lambda l, j: (l, 0, j)),
        out_shape=jax.ShapeDtypeStruct((DEPTH, 8, n_out), F32),
        compiler_params=_params(("arbitrary", "arbitrary")),
        name="adaln_mod",
    )(cvec, w_ada, b_ada.reshape(DEPTH, 1, n_out))


def _rope(t, cos, sin):
    n = t.shape[1]
    lane = lax.broadcasted_iota(jnp.int32, t.shape, 1)
    first_half = (lane % HEAD_DIM) < (HEAD_DIM // 2)
    rot = jnp.where(first_half, pltpu.roll(t, n - HEAD_DIM // 2, 1), pltpu.roll(t, HEAD_DIM // 2, 1))
    reps = n // LANES
    return t * jnp.tile(cos, (1, reps)) + rot * jnp.tile(sin, (1, reps))


def _proj_kernel(x_ref, mod_ref, gain_ref, w_ref, wg_ref, bg_ref, cos_ref, sin_ref,
                 aq_ref, ak_ref, av_ref, gq_ref, gk_ref, gv_ref, gr_ref, nq_ref, nk_ref, nv_ref,
                 gf_ref, gb_ref, *, rope):
    h = _norm_mod(x_ref[...], gain_ref[...], mod_ref[0:1, :], mod_ref[1:2, :]).astype(BF16)

    def seg(s):
        return _dot(h, w_ref[:, s[0]:s[1]])

    scale = HEAD_DIM ** -0.5
    aq = seg(SEG_AQ)
    akv = seg(SEG_AKV)
    ak = akv[:, :LANES]
    if rope:
        cos = cos_ref[...]
        sin = sin_ref[...]
        aq = _rope(aq, cos, sin)
        ak = _rope(ak, cos, sin)
    aq_ref[...] = (aq * scale).astype(BF16)
    ak_ref[...] = ak.astype(BF16)
    av_ref[...] = akv[:, LANES:].astype(BF16)
    gq_ref[...] = seg(SEG_GQ) * (B_DK ** -0.5)
    gk_ref[...] = seg(SEG_GK)
    gv_ref[...] = seg(SEG_GV).astype(BF16)
    gr_ref[...] = seg(SEG_GR)
    nq_ref[...] = (seg(SEG_NQ) * scale).astype(BF16)
    nk_ref[...] = seg(SEG_NK).astype(BF16)
    nv_ref[...] = seg(SEG_NV).astype(BF16)
    z = _dot(seg(SEG_GA).astype(BF16), wg_ref[...]) + bg_ref[...]
    log_sig = jnp.minimum(z, 0.0) - jnp.log1p(jnp.exp(-jnp.abs(z)))
    g = log_sig / B_GATE_NORM
    n_g = B_HEADS * B_DK
    gf_ref[...] = g[:, :n_g]
    gb_ref[...] = g[:, n_g:]


def _project(x, mod_rows, gain, w_perm, w_gate, b_gate, cos, sin, *, rope, tm):
    bsz, seq, _ = x.shape
    per_batch_mod = mod_rows.shape[0] > 1
    mod_map = (lambda b, i: (b, 0, 0)) if per_batch_mod else (lambda b, i: (0, 0, 0))
    widths = [(512, BF16), (128, BF16), (128, BF16), (256, F32), (256, F32), (512, BF16), (512, F32),
              (512, BF16), (512, BF16), (512, BF16), (256, F32), (256, F32)]
    tile = lambda w: pl.BlockSpec((None, tm, w), lambda b, i: (b, i, 0))
    return pl.pallas_call(
        functools.partial(_proj_kernel, rope=rope),
        grid=(bsz, seq // tm),
        in_specs=[
            tile(D_MODEL),
            pl.BlockSpec((None, 6, D_MODEL), mod_map),
            _const_spec((1, D_MODEL)),
            _const_spec((D_MODEL, IN_PADDED)),
            _const_spec((LANES, 2 * B_HEADS * B_DK)),
            _const_spec((1, 2 * B_HEADS * B_DK)),
            pl.BlockSpec((tm, LANES), lambda b, i: (i, 0)),
            pl.BlockSpec((tm, LANES), lambda b, i: (i, 0)),
        ],
        out_specs=[tile(w) for w, _ in widths],
        out_shape=[jax.ShapeDtypeStruct((bsz, seq, w), dt) for w, dt in widths],
        compiler_params=_params(("parallel", "arbitrary")),
        name="in_proj_rope" if rope else "in_proj_ctx",
    )(x, mod_rows, gain, w_perm, w_gate, b_gate, cos, sin)


def _gla_chunk(q, k, v, g, s_ref, reverse):
    c = q.shape[0]
    row = lax.broadcasted_iota(jnp.int32, (c, c), 0)
    col = lax.broadcasted_iota(jnp.int32, (c, c), 1)
    keep = (col >= row) if reverse else (col <= row)
    tri = keep.astype(F32)
    b = jnp.dot(tri, g, precision=lax.Precision.HIGHEST, preferred_element_type=F32)
    b_end = b[0:1, :] if reverse else b[c - 1:c, :]
    ref = 0.5 * b_end
    q_state = (q * jnp.exp(b)).astype(BF16)
    q_in = (q * jnp.exp(b - ref)).astype(BF16)
    k_in = (k * jnp.exp(ref - b)).astype(BF16)
    k_tail = (k * jnp.exp(b_end - b)).astype(BF16)
    decay = jnp.exp(b_end)
    outs = []
    for h in range(B_HEADS):
        ks = slice(h * B_DK, (h + 1) * B_DK)
        vs = slice(h * B_DV, (h + 1) * B_DV)
        a = jnp.where(keep, _dot_nt(q_in[:, ks], k_in[:, ks]), 0.0)
        s_t = s_ref[h]
        vh = v[:, vs]
        o = _dot(a.astype(BF16), vh) + _dot_nt(q_state[:, ks], s_t.astype(BF16))
        s_ref[h] = s_t * decay[:, ks] + _dot_tn(vh, k_tail[:, ks])
        outs.append(o)
    return jnp.concatenate(outs, axis=1)


def _gla_kernel(qf_ref, kf_ref, vf_ref, gf_ref, qb_ref, kb_ref, vb_ref, gb_ref, s0f_ref, s0b_ref,
                of_ref, ob_ref, sf_out_ref, sb_out_ref, sf_ref, sb_ref):
    i = pl.program_id(1)

    @pl.when(i == 0)
    def _():
        sf_ref[...] = s0f_ref[...]
        sb_ref[...] = s0b_ref[...]

    n_chunks = qf_ref.shape[0] // GLA_CHUNK
    for ci in range(n_chunks):
        fs = slice(ci * GLA_CHUNK, (ci + 1) * GLA_CHUNK)
        of_ref[fs, :] = _gla_chunk(qf_ref[fs, :], kf_ref[fs, :], vf_ref[fs, :], gf_ref[fs, :], sf_ref, False)
        cb = n_chunks - 1 - ci
        bs = slice(cb * GLA_CHUNK, (cb + 1) * GLA_CHUNK)
        ob_ref[bs, :] = _gla_chunk(qb_ref[bs, :], kb_ref[bs, :], vb_ref[bs, :], gb_ref[bs, :], sb_ref, True)

    @pl.when(i == pl.num_programs(1) - 1)
    def _():
        sf_out_ref[...] = sf_ref[...]
        sb_out_ref[...] = sb_ref[...]


def _gla(q, k, v, gf, gb, s0f, s0b, *, tile):
    bsz, seq, _ = q.shape
    n = seq // tile
    fwd = lambda w: pl.BlockSpec((None, tile, w), lambda b, i: (b, i, 0))
    bwd = lambda w: pl.BlockSpec((None, tile, w), lambda b, i: (b, n - 1 - i, 0))
    st = pl.BlockSpec((None, B_HEADS, B_DV, B_DK), lambda b, i: (b, 0, 0, 0))
    wk, wv = B_HEADS * B_DK, B_HEADS * B_DV
    st_shape = jax.ShapeDtypeStruct((bsz, B_HEADS, B_DV, B_DK), F32)
    return pl.pallas_call(
        _gla_kernel,
        grid=(bsz, n),
        in_specs=[fwd(wk), fwd(wk), fwd(wv), fwd(wk), bwd(wk), bwd(wk), bwd(wv), bwd(wk), st, st],
        out_specs=[fwd(wv), bwd(wv), st, st],
        out_shape=[jax.ShapeDtypeStruct((bsz, seq, wv), F32), jax.ShapeDtypeStruct((bsz, seq, wv), F32),
                   st_shape, st_shape],
        scratch_shapes=[pltpu.VMEM((B_HEADS, B_DV, B_DK), F32), pltpu.VMEM((B_HEADS, B_DV, B_DK), F32)],
        compiler_params=_params(("parallel", "arbitrary")),
        name="gla_scan",
    )(q, k, v, gf, q, k, v, gb, s0f, s0b)


def _softmax_parts(scores, sink_col):
    m = scores[0].max(axis=-1, keepdims=True)
    for s in scores[1:]:
        m = jnp.maximum(m, s.max(axis=-1, keepdims=True))
    if sink_col is not None:
        m = jnp.maximum(m, sink_col)
    ps = [jnp.exp(s - m) for s in scores]
    denom = ps[0].sum(axis=-1, keepdims=True)
    for p in ps[1:]:
        denom = denom + p.sum(axis=-1, keepdims=True)
    if sink_col is not None:
        denom = denom + jnp.exp(sink_col - m)
    return ps, denom


def _split_head_pair(t):
    lane = lax.broadcasted_iota(jnp.int32, t.shape, 1)
    zero = jnp.zeros_like(t)
    return jnp.concatenate([jnp.where(lane < HEAD_DIM, t, zero), jnp.where(lane >= HEAD_DIM, t, zero)], axis=0)


def _merge_head_pair(o):
    m = o.shape[0] // 2
    lane = lax.broadcasted_iota(jnp.int32, (m, LANES), 1)
    return jnp.where(lane < HEAD_DIM, o[:m], o[m:])


A_HEAD_ORDER = (0, 4, 1, 5, 2, 6, 3, 7)


def _win_kernel(sink_ref, q_ref, kp_ref, kc_ref, kn_ref, vp_ref, vc_ref, vn_ref, kx_ref, vx_ref, o_ref):
    n = pl.program_id(1)
    nb = pl.num_programs(1)
    n_tiles = A_HEADS // 2
    rows = A_HEADS * A_BLOCK
    q = jnp.concatenate([_split_head_pair(q_ref[:, j * LANES:(j + 1) * LANES]) for j in range(n_tiles)], axis=0)
    sink_col = jnp.concatenate([jnp.full((A_BLOCK, 1), sink_ref[h], F32) for h in A_HEAD_ORDER], axis=0)
    qi = lax.broadcasted_iota(jnp.int32, (rows, A_BLOCK), 0) % A_BLOCK
    kj = lax.broadcasted_iota(jnp.int32, (rows, A_BLOCK), 1)
    prev_ok = (kj >= qi) & (n > 0)
    next_ok = (kj <= qi) & (n < nb - 1)
    neg = -jnp.inf
    s_p = jnp.where(prev_ok, _dot_nt(q, kp_ref[...]), neg)
    s_c = _dot_nt(q, kc_ref[...])
    s_n = jnp.where(next_ok, _dot_nt(q, kn_ref[...]), neg)
    s_x = _dot_nt(q, kx_ref[...])
    (p_p, p_c, p_n, p_x), denom = _softmax_parts([s_p, s_c, s_n, s_x], sink_col)
    o = (_dot(p_p.astype(BF16), vp_ref[...]) + _dot(p_c.astype(BF16), vc_ref[...])
         + _dot(p_n.astype(BF16), vn_ref[...]) + _dot(p_x.astype(BF16), vx_ref[...]))
    o = o / denom
    for j in range(n_tiles):
        o_ref[:, j * LANES:(j + 1) * LANES] = _merge_head_pair(
            o[j * 2 * A_BLOCK:(j + 1) * 2 * A_BLOCK]).astype(o_ref.dtype)


def _window_attention(sink, q, k, v, k_ctx, v_ctx):
    bsz, seq, wq = q.shape
    nb = seq // A_BLOCK
    wkv = k.shape[-1]
    n_ctx = k_ctx.shape[1]
    kv_spec = lambda f: pl.BlockSpec((None, A_BLOCK, wkv), lambda b, n: (b, f(n), 0))
    prev = lambda n: jnp.maximum(n - 1, 0)
    cur = lambda n: n
    nxt = lambda n: jnp.minimum(n + 1, nb - 1)
    ctx_spec = pl.BlockSpec((None, n_ctx, wkv), lambda b, n: (b, 0, 0))
    return pl.pallas_call(
        _win_kernel,
        grid=(bsz, nb),
        in_specs=[
            pl.BlockSpec(memory_space=pltpu.SMEM),
            pl.BlockSpec((None, A_BLOCK, wq), lambda b, n: (b, n, 0)),
            kv_spec(prev), kv_spec(cur), kv_spec(nxt),
            kv_spec(prev), kv_spec(cur), kv_spec(nxt),
            ctx_spec, ctx_spec,
        ],
        out_specs=pl.BlockSpec((None, A_BLOCK, wq), lambda b, n: (b, n, 0)),
        out_shape=jax.ShapeDtypeStruct((bsz, seq, wq), BF16),
        compiler_params=_params(("parallel", "arbitrary")),
        name="window_attn",
    )(sink, q, k, k, k, v, v, v, k_ctx, v_ctx)


def _dense_kernel(sink_ref, q_ref, k_ref, v_ref, o_ref, *, head_order, shared_kv):
    n_q = q_ref.shape[0]
    for j in range(q_ref.shape[1] // LANES):
        qs = slice(j * LANES, (j + 1) * LANES)
        ks = slice(0, LANES) if shared_kv else qs
        s = _dot_nt(_split_head_pair(q_ref[:, qs]), k_ref[:, ks])
        sink_col = None
        if head_order is not None:
            sink_col = jnp.concatenate(
                [jnp.full((n_q, 1), sink_ref[head_order[2 * j + half]], F32) for half in range(2)], axis=0)
        (p,), denom = _softmax_parts([s], sink_col)
        o_ref[:, qs] = _merge_head_pair(_dot(p.astype(BF16), v_ref[:, ks]) / denom).astype(o_ref.dtype)


def _dense_attention(sink, q, k, v, *, head_order, shared_kv):
    bsz, n_q, wq = q.shape
    wkv = k.shape[-1]
    full = lambda w: pl.BlockSpec((None, n_q, w), lambda b: (b, 0, 0))
    return pl.pallas_call(
        functools.partial(_dense_kernel, head_order=head_order, shared_kv=shared_kv),
        grid=(bsz,),
        in_specs=[pl.BlockSpec(memory_space=pltpu.SMEM), full(wq), full(wkv), full(wkv)],
        out_specs=full(wq),
        out_shape=jax.ShapeDtypeStruct((bsz, n_q, wq), BF16),
        compiler_params=_params(("parallel",)),
        name="ctx_dense_attn_sink" if shared_kv else "ctx_dense_attn",
    )(sink, q, k, v)


NA_ROWS_PER_STEP = 4


def _na_kernel(q_ref, k_ref, v_ref, kx_ref, vx_ref, bias_ref, o_ref):
    step = pl.program_id(1)
    grid_rows = k_ref.shape[0] // GRID_W
    n_keys = C_WIN_ROWS * GRID_W
    n_pairs = C_HEADS // 2
    for rr in range(NA_ROWS_PER_STEP):
        r = step * NA_ROWS_PER_STEP + rr
        row0 = jnp.clip(r - C_WIN_ROWS // 2, 0, grid_rows - C_WIN_ROWS)
        d_row0 = row0 - r + (C_WIN_ROWS - 1)
        key0 = pl.multiple_of(row0 * GRID_W, GRID_W)
        qrows = slice(rr * GRID_W, (rr + 1) * GRID_W)
        s_l, s_x = [], []
        for p in range(n_pairs):
            ps = slice(p * LANES, (p + 1) * LANES)
            q2 = _split_head_pair(q_ref[qrows, ps])
            s_l.append(_dot_nt(q2, k_ref[pl.ds(key0, n_keys), ps]))
            s_x.append(_dot_nt(q2, kx_ref[:, ps]))
        bias = jnp.concatenate(
            [bias_ref[d_row0 + 2 * j].reshape(C_HEADS * GRID_W, LANES) for j in range(C_WIN_ROWS // 2)], axis=1)
        (p_l, p_x), denom = _softmax_parts([jnp.concatenate(s_l, axis=0) + bias, jnp.concatenate(s_x, axis=0)], None)
        p_l = p_l.astype(BF16)
        p_x = p_x.astype(BF16)
        for p in range(n_pairs):
            ps = slice(p * LANES, (p + 1) * LANES)
            pr = slice(p * 2 * GRID_W, (p + 1) * 2 * GRID_W)
            o = (_dot(p_l[pr], v_ref[pl.ds(key0, n_keys), ps]) + _dot(p_x[pr], vx_ref[:, ps])) / denom[pr]
            o_ref[qrows, ps] = _merge_head_pair(o).astype(o_ref.dtype)


def _neighbourhood_attention(q, k, v, k_ctx, v_ctx, bias_tbl):
    bsz, seq, w = q.shape
    n_ctx = k_ctx.shape[1]
    tq = NA_ROWS_PER_STEP * GRID_W
    whole = pl.BlockSpec((None, seq, w), lambda b, i: (b, 0, 0))
    ctx_spec = pl.BlockSpec((None, n_ctx, w), lambda b, i: (b, 0, 0))
    return pl.pallas_call(
        _na_kernel,
        grid=(bsz, seq // tq),
        in_specs=[pl.BlockSpec((None, tq, w), lambda b, i: (b, i, 0)), whole, whole, ctx_spec, ctx_spec,
                  _const_spec(bias_tbl.shape)],
        out_specs=pl.BlockSpec((None, tq, w), lambda b, i: (b, i, 0)),
        out_shape=jax.ShapeDtypeStruct((bsz, seq, w), BF16),
        compiler_params=_params(("parallel", "arbitrary")),
        name="neighbourhood_attn",
    )(q, k, v, k_ctx, v_ctx, bias_tbl)


def _na_bias_table(rpb):
    qcol = jnp.arange(GRID_W)[:, None]
    kcol = jnp.arange(GRID_W)[None, :]
    wstart = jnp.clip(qcol - C_WIN_COLS // 2, 0, GRID_W - C_WIN_COLS)
    ok = (kcol >= wstart) & (kcol < wstart + C_WIN_COLS)
    d_col = jnp.clip(kcol - qcol, -(C_WIN_COLS - 1), C_WIN_COLS - 1) + (C_WIN_COLS - 1)
    tbl = jnp.where(ok[None, None], rpb[:, :, d_col].astype(F32), -jnp.inf)
    tbl = jnp.moveaxis(tbl, 1, 0)
    return jnp.concatenate([tbl[:-1], tbl[1:]], axis=-1)


def _merge_kernel(x_ref, mod_ref, gain_ref, ya_ref, of_ref, ob_ref, gr_ref, yc_ref, gn_ref,
                  wm_ref, bm_ref, wa_ref, wb_ref, wc_ref, wo_ref, o_ref):
    x = x_ref[...]
    h = _norm_mod(x, gain_ref[...], mod_ref[0:1, :], mod_ref[1:2, :]).astype(BF16)
    o_sum = of_ref[...] + ob_ref[...]
    gn = gn_ref[...]
    parts = []
    for hh in range(B_HEADS):
        oh = o_sum[:, hh * B_DV:(hh + 1) * B_DV]
        parts.append((oh * lax.rsqrt(jnp.mean(oh * oh, axis=-1, keepdims=True) + EPS)) * gn)
    y_b = (jnp.concatenate(parts, axis=1) * _silu(gr_ref[...])).astype(BF16)
    d = D_MODEL
    mixed = None
    for j, (y, w_ref) in enumerate(((ya_ref[...], wa_ref), (y_b, wb_ref), (yc_ref[...], wc_ref))):
        gate = _sigmoid(_dot(h, wm_ref[:, j * d:(j + 1) * d]) + bm_ref[:, j * d:(j + 1) * d])
        term = gate * _dot(y, w_ref[...])
        mixed = term if mixed is None else mixed + term
    o_ref[...] = x + mod_ref[2:3, :] * _dot(mixed.astype(BF16), wo_ref[...])


def _merge(x, mod_rows, gain, y_a, o_f, o_b, g_r, y_c, gla_gain, w_merge, b_merge, w_a, w_b, w_c, w_out, *, tm):
    bsz, seq, d = x.shape
    per_batch_mod = mod_rows.shape[0] > 1
    mod_map = (lambda b, i: (b, 0, 0)) if per_batch_mod else (lambda b, i: (0, 0, 0))
    tile = lambda w: pl.BlockSpec((None, tm, w), lambda b, i: (b, i, 0))
    wy = y_a.shape[-1]
    return pl.pallas_call(
        _merge_kernel,
        grid=(bsz, seq // tm),
        in_specs=[tile(d), pl.BlockSpec((None, 6, d), mod_map), _const_spec((1, d)),
                  tile(wy), tile(wy), tile(wy), tile(wy), tile(wy), _const_spec((1, B_DV)),
                  _const_spec(w_merge.shape), _const_spec(b_merge.shape),
                  _const_spec(w_a.shape), _const_spec(w_b.shape), _const_spec(w_c.shape),
                  _const_spec(w_out.shape)],
        out_specs=tile(d),
        out_shape=jax.ShapeDtypeStruct((bsz, seq, d), F32),
        compiler_params=_params(("parallel", "arbitrary")),
        name="merge_out",
    )(x, mod_rows, gain, y_a, o_f, o_b, g_r, y_c, gla_gain, w_merge, b_merge, w_a, w_b, w_c, w_out)


FFN_CHUNK = 1408


def _ffn_kernel(x_ref, mod_ref, gain_ref, w1_ref, w2_ref, fg_ref, o_ref, *, final):
    x = x_ref[...]
    h = _norm_mod(x, gain_ref[...], mod_ref[3:4, :], mod_ref[4:5, :]).astype(BF16)
    acc = None
    for c0 in range(0, FFN_HIDDEN, FFN_CHUNK):
        gate = _dot(h, w1_ref[:, c0:c0 + FFN_CHUNK])
        up = _dot(h, w1_ref[:, FFN_HIDDEN + c0:FFN_HIDDEN + c0 + FFN_CHUNK])
        part = _dot((_silu(gate) * up).astype(BF16), w2_ref[c0:c0 + FFN_CHUNK, :])
        acc = part if acc is None else acc + part
    y = x + mod_ref[5:6, :] * acc
    if final:
        y = (y * lax.rsqrt(jnp.mean(y * y, axis=-1, keepdims=True) + EPS)) * fg_ref[...]
    o_ref[...] = y


def _ffn(x, mod_rows, gain, w1, w2, final_gain, *, final, tm):
    bsz, seq, d = x.shape
    per_batch_mod = mod_rows.shape[0] > 1
    mod_map = (lambda b, i: (b, 0, 0)) if per_batch_mod else (lambda b, i: (0, 0, 0))
    tile = pl.BlockSpec((None, tm, d), lambda b, i: (b, i, 0))
    return pl.pallas_call(
        functools.partial(_ffn_kernel, final=final),
        grid=(bsz, seq // tm),
        in_specs=[tile, pl.BlockSpec((None, 6, d), mod_map), _const_spec((1, d)),
                  _const_spec(w1.shape), _const_spec(w2.shape), _const_spec((1, d))],
        out_specs=tile,
        out_shape=jax.ShapeDtypeStruct((bsz, seq, d), F32),
        compiler_params=_params(("parallel", "arbitrary")),
        name="ffn_final" if final else "ffn",
    )(x, mod_rows, gain, w1, w2, final_gain)


def _rope_tables(seq):
    t = jnp.arange(seq)
    row = (t // GRID_W).astype(F32)
    col = (t % GRID_W).astype(F32)
    n_freq = HEAD_DIM // 4
    inv = ROPE_BASE ** (-jnp.arange(n_freq, dtype=F32) / n_freq)
    ang = jnp.concatenate([row[:, None] * inv[None], col[:, None] * inv[None]], axis=-1)
    cos, sin = jnp.cos(ang), jnp.sin(ang)
    return jnp.tile(cos, (1, 4)), jnp.tile(jnp.concatenate([-sin, sin], axis=-1), (1, 2))


def _permute_w_in(w):
    ga0 = 2304
    ga1 = ga0 + 2 * B_GATE_RANK
    n_aq = A_HEADS * HEAD_DIM
    aq = w[:, :n_aq].reshape(w.shape[0], A_HEADS, HEAD_DIM)[:, jnp.array(A_HEAD_ORDER)].reshape(w.shape[0], n_aq)
    pad = jnp.zeros((w.shape[0], IN_PADDED - w.shape[1]), w.dtype)
    return jnp.concatenate([aq, w[:, n_aq:ga0], w[:, ga1:], w[:, ga0:ga1], pad], axis=1).astype(BF16)


def _permute_w_branch_a(w):
    return w.reshape(A_HEADS, HEAD_DIM, w.shape[1])[jnp.array(A_HEAD_ORDER)].reshape(w.shape).astype(BF16)


def _gate_weights(w_fwd, b_fwd, w_bwd, b_bwd):
    n = B_HEADS * B_DK
    r = B_GATE_RANK
    w = jnp.zeros((LANES, 2 * n), F32)
    w = w.at[:r, :n].set(w_fwd).at[r:2 * r, n:].set(w_bwd)
    return w.astype(BF16), jnp.concatenate([b_fwd, b_bwd])[None, :]


def kernel(x, c, ctx, c_ctx, w_ada, b_ada, norm_mix, w_in, attn_sink, gla_gate_w_fwd, gla_gate_b_fwd,
           gla_gate_w_bwd, gla_gate_b_bwd, gla_norm, na_rpb, w_branch_a, w_branch_b, w_branch_c,
           w_merge, b_merge, w_out, norm_ffn, w_ffn_in, w_ffn_out, final_norm):
    bsz, seq, d = x.shape
    n_ctx = ctx.shape[1]
    cos, sin = _rope_tables(seq)

    cvec = jnp.zeros((8, d), F32).at[:bsz].set(c).at[bsz].set(c_ctx)
    mod = _modulation(cvec, w_ada, b_ada).reshape(DEPTH, 8, 6, d)

    xc = ctx
    zeros_state = jnp.zeros((bsz, B_HEADS, B_DV, B_DK), F32)
    for l in range(DEPTH):
        last = l == DEPTH - 1
        mod_x = mod[l, :bsz]
        mod_c = mod[l, bsz:bsz + 1]
        gain_m = norm_mix[l][None, :]
        w_perm = _permute_w_in(w_in[l])
        w_gate, b_gate = _gate_weights(gla_gate_w_fwd[l], gla_gate_b_fwd[l], gla_gate_w_bwd[l], gla_gate_b_bwd[l])

        (caq, cak, cav, cgq, cgk, cgv, cgr, cnq, cnk, cnv, cgf, cgb) = _project(
            xc, mod_c, gain_m, w_perm, w_gate, b_gate, cos, sin, rope=False, tm=n_ctx)
        (aq, ak, av, gq, gk, gv, gr, nq, nk, nv, gf, gb) = _project(
            x, mod_x, gain_m, w_perm, w_gate, b_gate, cos, sin, rope=True, tm=512)

        co_f, co_b, s_f, s_b = _gla(cgq, cgk, cgv, cgf, cgb, zeros_state, zeros_state, tile=n_ctx)
        o_f, o_b, _, _ = _gla(gq, gk, gv, gf, gb, s_f, s_b, tile=256)

        y_a = _window_attention(attn_sink[l], aq, ak, av, cak, cav)
        y_c = _neighbourhood_attention(nq, nk, nv, cnk, cnv, _na_bias_table(na_rpb[l]))

        merge_w = (gla_norm[l][None, :], w_merge[l].astype(BF16), b_merge[l][None, :],
                   _permute_w_branch_a(w_branch_a[l]), w_branch_b[l].astype(BF16), w_branch_c[l].astype(BF16),
                   w_out[l].astype(BF16))
        x = _merge(x, mod_x, gain_m, y_a, o_f, o_b, gr, y_c, *merge_w, tm=512)

        gain_f = norm_ffn[l][None, :]
        w1 = w_ffn_in[l].astype(BF16)
        w2 = w_ffn_out[l].astype(BF16)
        fg = final_norm[None, :]
        if not last:
            yc_a = _dense_attention(attn_sink[l], caq, cak, cav, head_order=A_HEAD_ORDER, shared_kv=True)
            yc_c = _dense_attention(attn_sink[l], cnq, cnk, cnv, head_order=None, shared_kv=False)
            xc = _merge(xc, mod_c, gain_m, yc_a, co_f, co_b, cgr, yc_c, *merge_w, tm=n_ctx)
            xc = _ffn(xc, mod_c, gain_f, w1, w2, fg, final=False, tm=n_ctx)
        x = _ffn(x, mod_x, gain_f, w1, w2, fg, final=last, tm=512)
    return x
```

```python
import functools

import jax
import jax.numpy as jnp
from jax import lax
from jax.experimental import pallas as pl
from jax.experimental.pallas import tpu as pltpu

F32 = jnp.float32
BF16 = jnp.bfloat16

D_MODEL = 1024
DEPTH = 2
GRID_W = 64
HEAD_DIM = 64
EPS = 1e-6
ROPE_BASE = 10000.0
A_HEADS = 8
A_KV_HEADS = 2
A_BLOCK = 128
B_HEADS = 4
B_DK = 64
B_DV = 128
B_GATE_RANK = 16
B_GATE_NORM = 16.0
C_HEADS = 8
C_WIN_ROWS = 8
C_WIN_COLS = 16
FFN_HIDDEN = 2816

VMEM_LIMIT_BYTES = 56 * 1024 * 1024
LANES = 128

SEG_AQ = (0, 512)
SEG_AKV = (512, 768)
SEG_GQ = (768, 1024)
SEG_GK = (1024, 1280)
SEG_GV = (1280, 1792)
SEG_GR = (1792, 2304)
SEG_NQ = (2304, 2816)
SEG_NK = (2816, 3328)
SEG_NV = (3328, 3840)
SEG_GA = (3840, 3968)
IN_PADDED = 3968

GLA_CHUNK = 64


def _params(sem):
    return pltpu.CompilerParams(dimension_semantics=sem, vmem_limit_bytes=VMEM_LIMIT_BYTES)


def _const_spec(shape):
    nd = len(shape)
    return pl.BlockSpec(shape, lambda *_: (0,) * nd, pipeline_mode=pl.Buffered(1))


def _sigmoid(x):
    return 1.0 / (1.0 + jnp.exp(-x))


def _silu(x):
    return x * _sigmoid(x)


def _dot(a, b):
    return jnp.dot(a, b, preferred_element_type=F32)


def _dot_nt(a, b):
    return lax.dot_general(a, b, (((1,), (1,)), ((), ())), preferred_element_type=F32)


def _dot_tn(a, b):
    return lax.dot_general(a, b, (((0,), (0,)), ((), ())), preferred_element_type=F32)


def _norm_mod(x, gain, shift, scale):
    y = x * lax.rsqrt(jnp.mean(x * x, axis=-1, keepdims=True) + EPS)
    return (y * gain) * (1.0 + scale) + shift


def _mod_kernel(c_ref, w_ref, b_ref, o_ref):
    s = _silu(c_ref[...])
    o_ref[...] = _dot(s.astype(BF16), w_ref[...].astype(BF16)) + b_ref[...]


def _modulation(cvec, w_ada, b_ada):
    tn = 1536
    n_out = w_ada.shape[-1]
    return pl.pallas_call(
        _mod_kernel,
        grid=(DEPTH, n_out // tn),
        in_specs=[
            pl.BlockSpec((8, D_MODEL), lambda l, j: (0, 0)),
            pl.BlockSpec((None, D_MODEL, tn), lambda l, j: (l, 0, j)),
            pl.BlockSpec((None, 1, tn), lambda l, j: (l, 0, j)),
        ],
        out_specs=pl.BlockSpec((None, 8, tn), lambda l, j: (l, 0, j)),
        out_shape=jax.ShapeDtypeStruct((DEPTH, 8, n_out), F32),
        compiler_params=_params(("arbitrary", "arbitrary")),
        name="adaln_mod",
    )(cvec, w_ada, b_ada.reshape(DEPTH, 1, n_out))


def _rope(t, cos, sin):
    n = t.shape[1]
    lane = lax.broadcasted_iota(jnp.int32, t.shape, 1)
    first_half = (lane % HEAD_DIM) < (HEAD_DIM // 2)
    rot = jnp.where(first_half, pltpu.roll(t, n - HEAD_DIM // 2, 1), pltpu.roll(t, HEAD_DIM // 2, 1))
    reps = n // LANES
    return t * jnp.tile(cos, (1, reps)) + rot * jnp.tile(sin, (1, reps))


def _proj_kernel(x_ref, mod_ref, gain_ref, w_ref, wg_ref, bg_ref, cos_ref, sin_ref,
                 aq_ref, ak_ref, av_ref, gq_ref, gk_ref, gv_ref, gr_ref, nq_ref, nk_ref, nv_ref,
                 gf_ref, gb_ref, *, rope):
    h = _norm_mod(x_ref[...], gain_ref[...], mod_ref[0:1, :], mod_ref[1:2, :]).astype(BF16)

    def seg(s):
        return _dot(h, w_ref[:, s[0]:s[1]])

    scale = HEAD_DIM ** -0.5
    aq = seg(SEG_AQ)
    akv = seg(SEG_AKV)
    ak = akv[:, :LANES]
    if rope:
        cos = cos_ref[...]
        sin = sin_ref[...]
        aq = _rope(aq, cos, sin)
        ak = _rope(ak, cos, sin)
    aq_ref[...] = (aq * scale).astype(BF16)
    ak_ref[...] = ak.astype(BF16)
    av_ref[...] = akv[:, LANES:].astype(BF16)
    gq_ref[...] = seg(SEG_GQ) * (B_DK ** -0.5)
    gk_ref[...] = seg(SEG_GK)
    gv_ref[...] = seg(SEG_GV).astype(BF16)
    gr_ref[...] = seg(SEG_GR)
    nq_ref[...] = (seg(SEG_NQ) * scale).astype(BF16)
    nk_ref[...] = seg(SEG_NK).astype(BF16)
    nv_ref[...] = seg(SEG_NV).astype(BF16)
    z = _dot(seg(SEG_GA).astype(BF16), wg_ref[...]) + bg_ref[...]
    log_sig = jnp.minimum(z, 0.0) - jnp.log1p(jnp.exp(-jnp.abs(z)))
    g = log_sig / B_GATE_NORM
    n_g = B_HEADS * B_DK
    gf_ref[...] = g[:, :n_g]
    gb_ref[...] = g[:, n_g:]


def _project(x, mod_rows, gain, w_perm, w_gate, b_gate, cos, sin, *, rope, tm):
    bsz, seq, _ = x.shape
    per_batch_mod = mod_rows.shape[0] > 1
    mod_map = (lambda b, i: (b, 0, 0)) if per_batch_mod else (lambda b, i: (0, 0, 0))
    widths = [(512, BF16), (128, BF16), (128, BF16), (256, F32), (256, F32), (512, BF16), (512, F32),
              (512, BF16), (512, BF16), (512, BF16), (256, F32), (256, F32)]
    tile = lambda w: pl.BlockSpec((None, tm, w), lambda b, i: (b, i, 0))
    return pl.pallas_call(
        functools.partial(_proj_kernel, rope=rope),
        grid=(bsz, seq // tm),
        in_specs=[
            tile(D_MODEL),
            pl.BlockSpec((None, 6, D_MODEL), mod_map),
            _const_spec((1, D_MODEL)),
            _const_spec((D_MODEL, IN_PADDED)),
            _const_spec((LANES, 2 * B_HEADS * B_DK)),
            _const_spec((1, 2 * B_HEADS * B_DK)),
            pl.BlockSpec((tm, LANES), lambda b, i: (i, 0)),
            pl.BlockSpec((tm, LANES), lambda b, i: (i, 0)),
        ],
        out_specs=[tile(w) for w, _ in widths],
        out_shape=[jax.ShapeDtypeStruct((bsz, seq, w), dt) for w, dt in widths],
        compiler_params=_params(("parallel", "arbitrary")),
        name="in_proj_rope" if rope else "in_proj_ctx",
    )(x, mod_rows, gain, w_perm, w_gate, b_gate, cos, sin)


def _gla_prepare(q_ref, k_ref, g_ref, reverse):
    t = q_ref.shape[0]
    c = GLA_CHUNK
    row = lax.broadcasted_iota(jnp.int32, (t, t), 0)
    col = lax.broadcasted_iota(jnp.int32, (t, t), 1)
    same_chunk = (row // c) == (col // c)
    tri = (same_chunk & ((col >= row) if reverse else (col <= row))).astype(F32)
    b = jnp.dot(tri, g_ref[...], precision=lax.Precision.HIGHEST, preferred_element_type=F32)
    ends = [b[ci * c:ci * c + 1, :] if reverse else b[ci * c + c - 1:ci * c + c, :] for ci in range(t // c)]
    b_end = jnp.concatenate([jnp.broadcast_to(e, (c, e.shape[1])) for e in ends], axis=0)
    ref = 0.5 * b_end
    q = q_ref[...]
    k = k_ref[...]
    q_state = (q * jnp.exp(b)).astype(BF16)
    q_in = (q * jnp.exp(b - ref)).astype(BF16)
    k_in = (k * jnp.exp(ref - b)).astype(BF16)
    k_tail = (k * jnp.exp(b_end - b)).astype(BF16)
    return q_state, q_in, k_in, k_tail, [jnp.exp(e) for e in ends]


def _gla_local(q_in, k_in, k_tail, v_ref, reverse):
    t = q_in.shape[0]
    c = GLA_CHUNK
    qi = lax.broadcasted_iota(jnp.int32, (2 * c, c), 0) % c
    kj = lax.broadcasted_iota(jnp.int32, (2 * c, c), 1)
    keep = (kj >= qi) if reverse else (kj <= qi)
    intra, upd = [], []
    for ci in range(t // c):
        rows = slice(ci * c, (ci + 1) * c)
        intra_c, upd_c = [], []
        for p in range(B_HEADS // 2):
            ps = slice(p * LANES, (p + 1) * LANES)
            a = jnp.where(keep, _dot_nt(_split_head_pair(q_in[rows, ps]), k_in[rows, ps]), 0.0).astype(BF16)
            kt = _split_head_pair(k_tail[rows, ps])
            u = None
            for half in range(2):
                h = 2 * p + half
                vh = v_ref[rows, h * B_DV:(h + 1) * B_DV]
                intra_c.append(_dot(a[half * c:(half + 1) * c], vh))
                uh = _dot_tn(vh, kt[half * c:(half + 1) * c])
                u = uh if u is None else u + uh
            upd_c.append(u)
        intra.append(intra_c)
        upd.append(upd_c)
    return intra, upd


def _gla_states(s_ref, decay, upd, reverse):
    n = len(upd)
    order = range(n - 1, -1, -1) if reverse else range(n)
    starts = [None] * n
    s = [s_ref[p] for p in range(B_HEADS // 2)]
    for ci in order:
        starts[ci] = [sp.astype(BF16) for sp in s]
        s = [s[p] * decay[ci][:, p * LANES:(p + 1) * LANES] + upd[ci][p] for p in range(B_HEADS // 2)]
    for p in range(B_HEADS // 2):
        s_ref[p] = s[p]
    return starts


def _gla_finish(o_ref, q_state, starts, intra):
    c = GLA_CHUNK
    for ci in range(len(intra)):
        rows = slice(ci * c, (ci + 1) * c)
        for p in range(B_HEADS // 2):
            ps = slice(p * LANES, (p + 1) * LANES)
            inter = _dot_nt(_split_head_pair(q_state[rows, ps]), starts[ci][p])
            for half in range(2):
                h = 2 * p + half
                o_ref[rows, h * B_DV:(h + 1) * B_DV] = intra[ci][h] + inter[half * c:(half + 1) * c]


def _gla_kernel(qf_ref, kf_ref, vf_ref, gf_ref, qb_ref, kb_ref, vb_ref, gb_ref, s0f_ref, s0b_ref,
                of_ref, ob_ref, sf_out_ref, sb_out_ref, sf_ref, sb_ref):
    i = pl.program_id(1)

    @pl.when(i == 0)
    def _():
        sf_ref[...] = s0f_ref[...]
        sb_ref[...] = s0b_ref[...]

    qs_f, qi_f, ki_f, kt_f, dec_f = _gla_prepare(qf_ref, kf_ref, gf_ref, False)
    qs_b, qi_b, ki_b, kt_b, dec_b = _gla_prepare(qb_ref, kb_ref, gb_ref, True)
    intra_f, upd_f = _gla_local(qi_f, ki_f, kt_f, vf_ref, False)
    intra_b, upd_b = _gla_local(qi_b, ki_b, kt_b, vb_ref, True)
    starts_f = _gla_states(sf_ref, dec_f, upd_f, False)
    starts_b = _gla_states(sb_ref, dec_b, upd_b, True)
    _gla_finish(of_ref, qs_f, starts_f, intra_f)
    _gla_finish(ob_ref, qs_b, starts_b, intra_b)

    @pl.when(i == pl.num_programs(1) - 1)
    def _():
        sf_out_ref[...] = sf_ref[...]
        sb_out_ref[...] = sb_ref[...]


GLA_STATE_SHAPE = (B_HEADS // 2, B_DV, 2 * B_DK)


def _gla(q, k, v, gf, gb, s0f, s0b, *, tile):
    bsz, seq, _ = q.shape
    n = seq // tile
    fwd = lambda w: pl.BlockSpec((None, tile, w), lambda b, i: (b, i, 0))
    bwd = lambda w: pl.BlockSpec((None, tile, w), lambda b, i: (b, n - 1 - i, 0))
    st = pl.BlockSpec((None,) + GLA_STATE_SHAPE, lambda b, i: (b, 0, 0, 0))
    wk, wv = B_HEADS * B_DK, B_HEADS * B_DV
    st_shape = jax.ShapeDtypeStruct((bsz,) + GLA_STATE_SHAPE, F32)
    return pl.pallas_call(
        _gla_kernel,
        grid=(bsz, n),
        in_specs=[fwd(wk), fwd(wk), fwd(wv), fwd(wk), bwd(wk), bwd(wk), bwd(wv), bwd(wk), st, st],
        out_specs=[fwd(wv), bwd(wv), st, st],
        out_shape=[jax.ShapeDtypeStruct((bsz, seq, wv), F32), jax.ShapeDtypeStruct((bsz, seq, wv), F32),
                   st_shape, st_shape],
        scratch_shapes=[pltpu.VMEM(GLA_STATE_SHAPE, F32), pltpu.VMEM(GLA_STATE_SHAPE, F32)],
        compiler_params=_params(("parallel", "arbitrary")),
        name="gla_scan",
    )(q, k, v, gf, q, k, v, gb, s0f, s0b)


def _softmax_parts(scores, sink_tile):
    def lane_tiles(blocks):
        return [b[:, j:j + LANES] for b in blocks for j in range(0, b.shape[1], LANES)]

    tiles = lane_tiles(scores)
    if sink_tile is not None:
        tiles.append(sink_tile)
    m = functools.reduce(jnp.maximum, tiles).max(axis=-1, keepdims=True)
    ps = [jnp.exp(s - m) for s in scores]
    acc = functools.reduce(jnp.add, lane_tiles(ps))
    if sink_tile is not None:
        lane = lax.broadcasted_iota(jnp.int32, sink_tile.shape, 1)
        acc = acc + jnp.where(lane == 0, jnp.exp(sink_tile - m), 0.0)
    return ps, acc.sum(axis=-1, keepdims=True)


def _split_head_pair(t):
    lane = lax.broadcasted_iota(jnp.int32, t.shape, 1)
    zero = jnp.zeros_like(t)
    return jnp.concatenate([jnp.where(lane < HEAD_DIM, t, zero), jnp.where(lane >= HEAD_DIM, t, zero)], axis=0)


def _merge_head_pair(o):
    m = o.shape[0] // 2
    lane = lax.broadcasted_iota(jnp.int32, (m, LANES), 1)
    return jnp.where(lane < HEAD_DIM, o[:m], o[m:])


A_HEAD_ORDER = (0, 4, 1, 5, 2, 6, 3, 7)


def _win_kernel(sink_ref, q_ref, kp_ref, kc_ref, kn_ref, vp_ref, vc_ref, vn_ref, kx_ref, vx_ref, o_ref):
    n = pl.program_id(1)
    nb = pl.num_programs(1)
    n_tiles = A_HEADS // 2
    rows = A_HEADS * A_BLOCK
    q = jnp.concatenate([_split_head_pair(q_ref[:, j * LANES:(j + 1) * LANES]) for j in range(n_tiles)], axis=0)
    sink_tile = jnp.concatenate([jnp.full((A_BLOCK, LANES), sink_ref[h], F32) for h in A_HEAD_ORDER], axis=0)
    qi = lax.broadcasted_iota(jnp.int32, (rows, A_BLOCK), 0) % A_BLOCK
    kj = lax.broadcasted_iota(jnp.int32, (rows, A_BLOCK), 1)
    prev_ok = (kj >= qi) & (n > 0)
    next_ok = (kj <= qi) & (n < nb - 1)
    neg = -jnp.inf
    s_p = jnp.where(prev_ok, _dot_nt(q, kp_ref[...]), neg)
    s_c = _dot_nt(q, kc_ref[...])
    s_n = jnp.where(next_ok, _dot_nt(q, kn_ref[...]), neg)
    s_x = _dot_nt(q, kx_ref[...])
    (p_p, p_c, p_n, p_x), denom = _softmax_parts([s_p, s_c, s_n, s_x], sink_tile)
    o = (_dot(p_p.astype(BF16), vp_ref[...]) + _dot(p_c.astype(BF16), vc_ref[...])
         + _dot(p_n.astype(BF16), vn_ref[...]) + _dot(p_x.astype(BF16), vx_ref[...]))
    o = o / denom
    for j in range(n_tiles):
        o_ref[:, j * LANES:(j + 1) * LANES] = _merge_head_pair(
            o[j * 2 * A_BLOCK:(j + 1) * 2 * A_BLOCK]).astype(o_ref.dtype)


def _window_attention(sink, q, k, v, k_ctx, v_ctx):
    bsz, seq, wq = q.shape
    nb = seq // A_BLOCK
    wkv = k.shape[-1]
    n_ctx = k_ctx.shape[1]
    kv_spec = lambda f: pl.BlockSpec((None, A_BLOCK, wkv), lambda b, n: (b, f(n), 0))
    prev = lambda n: jnp.maximum(n - 1, 0)
    cur = lambda n: n
    nxt = lambda n: jnp.minimum(n + 1, nb - 1)
    ctx_spec = pl.BlockSpec((None, n_ctx, wkv), lambda b, n: (b, 0, 0))
    return pl.pallas_call(
        _win_kernel,
        grid=(bsz, nb),
        in_specs=[
            pl.BlockSpec(memory_space=pltpu.SMEM),
            pl.BlockSpec((None, A_BLOCK, wq), lambda b, n: (b, n, 0)),
            kv_spec(prev), kv_spec(cur), kv_spec(nxt),
            kv_spec(prev), kv_spec(cur), kv_spec(nxt),
            ctx_spec, ctx_spec,
        ],
        out_specs=pl.BlockSpec((None, A_BLOCK, wq), lambda b, n: (b, n, 0)),
        out_shape=jax.ShapeDtypeStruct((bsz, seq, wq), BF16),
        compiler_params=_params(("parallel", "arbitrary")),
        name="window_attn",
    )(sink, q, k, k, k, v, v, v, k_ctx, v_ctx)


def _dense_kernel(sink_ref, q_ref, k_ref, v_ref, o_ref, *, head_order, shared_kv):
    n_q = q_ref.shape[0]
    for j in range(q_ref.shape[1] // LANES):
        qs = slice(j * LANES, (j + 1) * LANES)
        ks = slice(0, LANES) if shared_kv else qs
        s = _dot_nt(_split_head_pair(q_ref[:, qs]), k_ref[:, ks])
        sink_tile = None
        if head_order is not None:
            sink_tile = jnp.concatenate(
                [jnp.full((n_q, LANES), sink_ref[head_order[2 * j + half]], F32) for half in range(2)], axis=0)
        (p,), denom = _softmax_parts([s], sink_tile)
        o_ref[:, qs] = _merge_head_pair(_dot(p.astype(BF16), v_ref[:, ks]) / denom).astype(o_ref.dtype)


def _dense_attention(sink, q, k, v, *, head_order, shared_kv):
    bsz, n_q, wq = q.shape
    wkv = k.shape[-1]
    full = lambda w: pl.BlockSpec((None, n_q, w), lambda b: (b, 0, 0))
    return pl.pallas_call(
        functools.partial(_dense_kernel, head_order=head_order, shared_kv=shared_kv),
        grid=(bsz,),
        in_specs=[pl.BlockSpec(memory_space=pltpu.SMEM), full(wq), full(wkv), full(wkv)],
        out_specs=full(wq),
        out_shape=jax.ShapeDtypeStruct((bsz, n_q, wq), BF16),
        compiler_params=_params(("parallel",)),
        name="ctx_dense_attn_sink" if shared_kv else "ctx_dense_attn",
    )(sink, q, k, v)


NA_ROWS_PER_STEP = 4


def _na_kernel(q_ref, k_ref, v_ref, kx_ref, vx_ref, bias_ref, o_ref):
    step = pl.program_id(1)
    grid_rows = k_ref.shape[0] // GRID_W
    n_keys = C_WIN_ROWS * GRID_W
    n_pairs = C_HEADS // 2
    for rr in range(NA_ROWS_PER_STEP):
        r = step * NA_ROWS_PER_STEP + rr
        row0 = jnp.clip(r - C_WIN_ROWS // 2, 0, grid_rows - C_WIN_ROWS)
        d_row0 = row0 - r + (C_WIN_ROWS - 1)
        key0 = pl.multiple_of(row0 * GRID_W, GRID_W)
        qrows = slice(rr * GRID_W, (rr + 1) * GRID_W)
        s_l, s_x = [], []
        for p in range(n_pairs):
            ps = slice(p * LANES, (p + 1) * LANES)
            q2 = _split_head_pair(q_ref[qrows, ps])
            s_l.append(_dot_nt(q2, k_ref[pl.ds(key0, n_keys), ps]))
            s_x.append(_dot_nt(q2, kx_ref[:, ps]))
        bias = jnp.concatenate(
            [bias_ref[d_row0 + 2 * j].reshape(C_HEADS * GRID_W, LANES) for j in range(C_WIN_ROWS // 2)], axis=1)
        (p_l, p_x), denom = _softmax_parts([jnp.concatenate(s_l, axis=0) + bias, jnp.concatenate(s_x, axis=0)], None)
        p_l = p_l.astype(BF16)
        p_x = p_x.astype(BF16)
        for p in range(n_pairs):
            ps = slice(p * LANES, (p + 1) * LANES)
            pr = slice(p * 2 * GRID_W, (p + 1) * 2 * GRID_W)
            o = (_dot(p_l[pr], v_ref[pl.ds(key0, n_keys), ps]) + _dot(p_x[pr], vx_ref[:, ps])) / denom[pr]
            o_ref[qrows, ps] = _merge_head_pair(o).astype(o_ref.dtype)


def _neighbourhood_attention(q, k, v, k_ctx, v_ctx, bias_tbl):
    bsz, seq, w = q.shape
    n_ctx = k_ctx.shape[1]
    tq = NA_ROWS_PER_STEP * GRID_W
    whole = pl.BlockSpec((None, seq, w), lambda b, i: (b, 0, 0))
    ctx_spec = pl.BlockSpec((None, n_ctx, w), lambda b, i: (b, 0, 0))
    return pl.pallas_call(
        _na_kernel,
        grid=(bsz, seq // tq),
        in_specs=[pl.BlockSpec((None, tq, w), lambda b, i: (b, i, 0)), whole, whole, ctx_spec, ctx_spec,
                  _const_spec(bias_tbl.shape)],
        out_specs=pl.BlockSpec((None, tq, w), lambda b, i: (b, i, 0)),
        out_shape=jax.ShapeDtypeStruct((bsz, seq, w), BF16),
        compiler_params=_params(("parallel", "arbitrary")),
        name="neighbourhood_attn",
    )(q, k, v, k_ctx, v_ctx, bias_tbl)


def _na_bias_table(rpb):
    qcol = jnp.arange(GRID_W)[:, None]
    kcol = jnp.arange(GRID_W)[None, :]
    wstart = jnp.clip(qcol - C_WIN_COLS // 2, 0, GRID_W - C_WIN_COLS)
    ok = (kcol >= wstart) & (kcol < wstart + C_WIN_COLS)
    d_col = jnp.clip(kcol - qcol, -(C_WIN_COLS - 1), C_WIN_COLS - 1) + (C_WIN_COLS - 1)
    tbl = jnp.where(ok[None, None], rpb[:, :, d_col].astype(F32), -jnp.inf)
    tbl = jnp.moveaxis(tbl, 1, 0)
    return jnp.concatenate([tbl[:-1], tbl[1:]], axis=-1)


def _merge_kernel(x_ref, mod_ref, gain_ref, ya_ref, of_ref, ob_ref, gr_ref, yc_ref, gn_ref,
                  wm_ref, bm_ref, wa_ref, wb_ref, wc_ref, wo_ref, o_ref):
    x = x_ref[...]
    h = _norm_mod(x, gain_ref[...], mod_ref[0:1, :], mod_ref[1:2, :]).astype(BF16)
    o_sum = of_ref[...] + ob_ref[...]
    gn = gn_ref[...]
    parts = []
    for hh in range(B_HEADS):
        oh = o_sum[:, hh * B_DV:(hh + 1) * B_DV]
        parts.append((oh * lax.rsqrt(jnp.mean(oh * oh, axis=-1, keepdims=True) + EPS)) * gn)
    y_b = (jnp.concatenate(parts, axis=1) * _silu(gr_ref[...])).astype(BF16)
    d = D_MODEL
    mixed = None
    for j, (y, w_ref) in enumerate(((ya_ref[...], wa_ref), (y_b, wb_ref), (yc_ref[...], wc_ref))):
        gate = _sigmoid(_dot(h, wm_ref[:, j * d:(j + 1) * d]) + bm_ref[:, j * d:(j + 1) * d])
        term = gate * _dot(y, w_ref[...])
        mixed = term if mixed is None else mixed + term
    o_ref[...] = x + mod_ref[2:3, :] * _dot(mixed.astype(BF16), wo_ref[...])


def _merge(x, mod_rows, gain, y_a, o_f, o_b, g_r, y_c, gla_gain, w_merge, b_merge, w_a, w_b, w_c, w_out, *, tm):
    bsz, seq, d = x.shape
    per_batch_mod = mod_rows.shape[0] > 1
    mod_map = (lambda b, i: (b, 0, 0)) if per_batch_mod else (lambda b, i: (0, 0, 0))
    tile = lambda w: pl.BlockSpec((None, tm, w), lambda b, i: (b, i, 0))
    wy = y_a.shape[-1]
    return pl.pallas_call(
        _merge_kernel,
        grid=(bsz, seq // tm),
        in_specs=[tile(d), pl.BlockSpec((None, 6, d), mod_map), _const_spec((1, d)),
                  tile(wy), tile(wy), tile(wy), tile(wy), tile(wy), _const_spec((1, B_DV)),
                  _const_spec(w_merge.shape), _const_spec(b_merge.shape),
                  _const_spec(w_a.shape), _const_spec(w_b.shape), _const_spec(w_c.shape),
                  _const_spec(w_out.shape)],
        out_specs=tile(d),
        out_shape=jax.ShapeDtypeStruct((bsz, seq, d), F32),
        compiler_params=_params(("parallel", "arbitrary")),
        name="merge_out",
    )(x, mod_rows, gain, y_a, o_f, o_b, g_r, y_c, gla_gain, w_merge, b_merge, w_a, w_b, w_c, w_out)


FFN_CHUNK = 1408


def _ffn_kernel(x_ref, mod_ref, gain_ref, w1_ref, w2_ref, fg_ref, o_ref, *, final):
    x = x_ref[...]
    h = _norm_mod(x, gain_ref[...], mod_ref[3:4, :], mod_ref[4:5, :]).astype(BF16)
    acc = None
    for c0 in range(0, FFN_HIDDEN, FFN_CHUNK):
        gate = _dot(h, w1_ref[:, c0:c0 + FFN_CHUNK])
        up = _dot(h, w1_ref[:, FFN_HIDDEN + c0:FFN_HIDDEN + c0 + FFN_CHUNK])
        part = _dot((_silu(gate) * up).astype(BF16), w2_ref[c0:c0 + FFN_CHUNK, :])
        acc = part if acc is None else acc + part
    y = x + mod_ref[5:6, :] * acc
    if final:
        y = (y * lax.rsqrt(jnp.mean(y * y, axis=-1, keepdims=True) + EPS)) * fg_ref[...]
    o_ref[...] = y


def _ffn(x, mod_rows, gain, w1, w2, final_gain, *, final, tm):
    bsz, seq, d = x.shape
    per_batch_mod = mod_rows.shape[0] > 1
    mod_map = (lambda b, i: (b, 0, 0)) if per_batch_mod else (lambda b, i: (0, 0, 0))
    tile = pl.BlockSpec((None, tm, d), lambda b, i: (b, i, 0))
    return pl.pallas_call(
        functools.partial(_ffn_kernel, final=final),
        grid=(bsz, seq // tm),
        in_specs=[tile, pl.BlockSpec((None, 6, d), mod_map), _const_spec((1, d)),
                  _const_spec(w1.shape), _const_spec(w2.shape), _const_spec((1, d))],
        out_specs=tile,
        out_shape=jax.ShapeDtypeStruct((bsz, seq, d), F32),
        compiler_params=_params(("parallel", "arbitrary")),
        name="ffn_final" if final else "ffn",
    )(x, mod_rows, gain, w1, w2, final_gain)


def _rope_tables(seq):
    t = jnp.arange(seq)
    row = (t // GRID_W).astype(F32)
    col = (t % GRID_W).astype(F32)
    n_freq = HEAD_DIM // 4
    inv = ROPE_BASE ** (-jnp.arange(n_freq, dtype=F32) / n_freq)
    ang = jnp.concatenate([row[:, None] * inv[None], col[:, None] * inv[None]], axis=-1)
    cos, sin = jnp.cos(ang), jnp.sin(ang)
    return jnp.tile(cos, (1, 4)), jnp.tile(jnp.concatenate([-sin, sin], axis=-1), (1, 2))


def _permute_w_in(w):
    ga0 = 2304
    ga1 = ga0 + 2 * B_GATE_RANK
    n_aq = A_HEADS * HEAD_DIM
    aq = w[:, :n_aq].reshape(w.shape[0], A_HEADS, HEAD_DIM)[:, jnp.array(A_HEAD_ORDER)].reshape(w.shape[0], n_aq)
    pad = jnp.zeros((w.shape[0], IN_PADDED - w.shape[1]), w.dtype)
    return jnp.concatenate([aq, w[:, n_aq:ga0], w[:, ga1:], w[:, ga0:ga1], pad], axis=1).astype(BF16)


def _permute_w_branch_a(w):
    return w.reshape(A_HEADS, HEAD_DIM, w.shape[1])[jnp.array(A_HEAD_ORDER)].reshape(w.shape).astype(BF16)


def _gate_weights(w_fwd, b_fwd, w_bwd, b_bwd):
    n = B_HEADS * B_DK
    r = B_GATE_RANK
    w = jnp.zeros((LANES, 2 * n), F32)
    w = w.at[:r, :n].set(w_fwd).at[r:2 * r, n:].set(w_bwd)
    return w.astype(BF16), jnp.concatenate([b_fwd, b_bwd])[None, :]


def kernel(x, c, ctx, c_ctx, w_ada, b_ada, norm_mix, w_in, attn_sink, gla_gate_w_fwd, gla_gate_b_fwd,
           gla_gate_w_bwd, gla_gate_b_bwd, gla_norm, na_rpb, w_branch_a, w_branch_b, w_branch_c,
           w_merge, b_merge, w_out, norm_ffn, w_ffn_in, w_ffn_out, final_norm):
    bsz, seq, d = x.shape
    n_ctx = ctx.shape[1]
    cos, sin = _rope_tables(seq)

    cvec = jnp.zeros((8, d), F32).at[:bsz].set(c).at[bsz].set(c_ctx)
    mod = _modulation(cvec, w_ada, b_ada).reshape(DEPTH, 8, 6, d)

    xc = ctx
    zeros_state = jnp.zeros((bsz,) + GLA_STATE_SHAPE, F32)
    for l in range(DEPTH):
        last = l == DEPTH - 1
        mod_x = mod[l, :bsz]
        mod_c = mod[l, bsz:bsz + 1]
        gain_m = norm_mix[l][None, :]
        w_perm = _permute_w_in(w_in[l])
        w_gate, b_gate = _gate_weights(gla_gate_w_fwd[l], gla_gate_b_fwd[l], gla_gate_w_bwd[l], gla_gate_b_bwd[l])

        (caq, cak, cav, cgq, cgk, cgv, cgr, cnq, cnk, cnv, cgf, cgb) = _project(
            xc, mod_c, gain_m, w_perm, w_gate, b_gate, cos, sin, rope=False, tm=n_ctx)
        (aq, ak, av, gq, gk, gv, gr, nq, nk, nv, gf, gb) = _project(
            x, mod_x, gain_m, w_perm, w_gate, b_gate, cos, sin, rope=True, tm=512)

        co_f, co_b, s_f, s_b = _gla(cgq, cgk, cgv, cgf, cgb, zeros_state, zeros_state, tile=n_ctx)
        o_f, o_b, _, _ = _gla(gq, gk, gv, gf, gb, s_f, s_b, tile=256)

        y_a = _window_attention(attn_sink[l], aq, ak, av, cak, cav)
        y_c = _neighbourhood_attention(nq, nk, nv, cnk, cnv, _na_bias_table(na_rpb[l]))

        merge_w = (gla_norm[l][None, :], w_merge[l].astype(BF16), b_merge[l][None, :],
                   _permute_w_branch_a(w_branch_a[l]), w_branch_b[l].astype(BF16), w_branch_c[l].astype(BF16),
                   w_out[l].astype(BF16))
        x = _merge(x, mod_x, gain_m, y_a, o_f, o_b, gr, y_c, *merge_w, tm=512)

        gain_f = norm_ffn[l][None, :]
        w1 = w_ffn_in[l].astype(BF16)
        w2 = w_ffn_out[l].astype(BF16)
        fg = final_norm[None, :]
        if not last:
            yc_a = _dense_attention(attn_sink[l], caq, cak, cav, head_order=A_HEAD_ORDER, shared_kv=True)
            yc_c = _dense_attention(attn_sink[l], cnq, cnk, cnv, head_order=None, shared_kv=False)
            xc = _merge(xc, mod_c, gain_m, yc_a, co_f, co_b, cgr, yc_c, *merge_w, tm=n_ctx)
            xc = _ffn(xc, mod_c, gain_f, w1, w2, fg, final=False, tm=n_ctx)
        x = _ffn(x, mod_x, gain_f, w1, w2, fg, final=last, tm=512)
    return x
```

```python
import functools

import jax
import jax.numpy as jnp
from jax import lax
from jax.experimental import pallas as pl
from jax.experimental.pallas import tpu as pltpu

F32 = jnp.float32
BF16 = jnp.bfloat16

D_MODEL = 1024
DEPTH = 2
GRID_W = 64
HEAD_DIM = 64
EPS = 1e-6
ROPE_BASE = 10000.0
A_HEADS = 8
A_KV_HEADS = 2
A_BLOCK = 128
B_HEADS = 4
B_DK = 64
B_DV = 128
B_GATE_RANK = 16
B_GATE_NORM = 16.0
C_HEADS = 8
C_WIN_ROWS = 8
C_WIN_COLS = 16
FFN_HIDDEN = 2816

VMEM_LIMIT_BYTES = 56 * 1024 * 1024
LANES = 128

SEG_AQ = (0, 512)
SEG_AKV = (512, 768)
SEG_GQ = (768, 1024)
SEG_GK = (1024, 1280)
SEG_GV = (1280, 1792)
SEG_GR = (1792, 2304)
SEG_NQ = (2304, 2816)
SEG_NK = (2816, 3328)
SEG_NV = (3328, 3840)
SEG_GA = (3840, 3968)
IN_PADDED = 3968

GLA_CHUNK = 64
SUB_ROWS = 256


def _params(sem):
    return pltpu.CompilerParams(dimension_semantics=sem, vmem_limit_bytes=VMEM_LIMIT_BYTES)


def _const_spec(shape):
    nd = len(shape)
    return pl.BlockSpec(shape, lambda *_: (0,) * nd, pipeline_mode=pl.Buffered(1))


def _sigmoid(x):
    return 1.0 / (1.0 + jnp.exp(-x))


def _silu(x):
    return x * _sigmoid(x)


def _dot(a, b):
    return jnp.dot(a, b, preferred_element_type=F32)


def _dot_nt(a, b):
    return lax.dot_general(a, b, (((1,), (1,)), ((), ())), preferred_element_type=F32)


def _dot_tn(a, b):
    return lax.dot_general(a, b, (((0,), (0,)), ((), ())), preferred_element_type=F32)


def _norm_mod(x, gain, shift, scale):
    y = x * lax.rsqrt(jnp.mean(x * x, axis=-1, keepdims=True) + EPS)
    return (y * gain) * (1.0 + scale) + shift


def _mod_kernel(c_ref, w_ref, b_ref, o_ref):
    s = _silu(c_ref[...])
    o_ref[...] = _dot(s.astype(BF16), w_ref[...].astype(BF16)) + b_ref[...]


def _modulation(cvec, w_ada, b_ada):
    tn = 1536
    n_out = w_ada.shape[-1]
    return pl.pallas_call(
        _mod_kernel,
        grid=(DEPTH, n_out // tn),
        in_specs=[
            pl.BlockSpec((8, D_MODEL), lambda l, j: (0, 0)),
            pl.BlockSpec((None, D_MODEL, tn), lambda l, j: (l, 0, j)),
            pl.BlockSpec((None, 1, tn), lambda l, j: (l, 0, j)),
        ],
        out_specs=pl.BlockSpec((None, 8, tn), lambda l, j: (l, 0, j)),
        out_shape=jax.ShapeDtypeStruct((DEPTH, 8, n_out), F32),
        compiler_params=_params(("arbitrary", "arbitrary")),
        name="adaln_mod",
    )(cvec, w_ada, b_ada.reshape(DEPTH, 1, n_out))


def _rope(t, cos, sin):
    n = t.shape[1]
    lane = lax.broadcasted_iota(jnp.int32, t.shape, 1)
    first_half = (lane % HEAD_DIM) < (HEAD_DIM // 2)
    rot = jnp.where(first_half, pltpu.roll(t, n - HEAD_DIM // 2, 1), pltpu.roll(t, HEAD_DIM // 2, 1))
    reps = n // LANES
    return t * jnp.tile(cos, (1, reps)) + rot * jnp.tile(sin, (1, reps))


def _proj_kernel(x_ref, mod_ref, gain_ref, w_ref, wg_ref, bg_ref, cos_ref, sin_ref,
                 aq_ref, ak_ref, av_ref, gq_ref, gk_ref, gv_ref, gr_ref, nq_ref, nk_ref, nv_ref,
                 gf_ref, gb_ref, *, rope):
    scale = HEAD_DIM ** -0.5
    n_g = B_HEADS * B_DK
    def normed(r0):
        return _norm_mod(x_ref[r0:r0 + SUB_ROWS, :], gain_ref[...], mod_ref[0:1, :], mod_ref[1:2, :]).astype(BF16)

    n_rows = x_ref.shape[0]
    h_next = normed(0)
    for r0 in range(0, n_rows, SUB_ROWS):
        rows = slice(r0, r0 + SUB_ROWS)
        h = h_next

        def seg(s):
            return _dot(h, w_ref[:, s[0]:s[1]])

        aq = seg(SEG_AQ)
        akv = seg(SEG_AKV)
        z = _dot(seg(SEG_GA).astype(BF16), wg_ref[...]) + bg_ref[...]
        if r0 + SUB_ROWS < n_rows:
            h_next = normed(r0 + SUB_ROWS)
        ak = akv[:, :LANES]
        if rope:
            cos = cos_ref[rows, :]
            sin = sin_ref[rows, :]
            aq = _rope(aq, cos, sin)
            ak = _rope(ak, cos, sin)
        aq_ref[rows, :] = (aq * scale).astype(BF16)
        ak_ref[rows, :] = ak.astype(BF16)
        av_ref[rows, :] = akv[:, LANES:].astype(BF16)
        log_sig = jnp.minimum(z, 0.0) - jnp.log1p(jnp.exp(-jnp.abs(z)))
        g = log_sig / B_GATE_NORM
        gf_ref[rows, :] = g[:, :n_g]
        gb_ref[rows, :] = g[:, n_g:]
        gq_ref[rows, :] = seg(SEG_GQ) * (B_DK ** -0.5)
        gk_ref[rows, :] = seg(SEG_GK)
        gv_ref[rows, :] = seg(SEG_GV).astype(BF16)
        gr_ref[rows, :] = seg(SEG_GR)
        nq_ref[rows, :] = (seg(SEG_NQ) * scale).astype(BF16)
        nk_ref[rows, :] = seg(SEG_NK).astype(BF16)
        nv_ref[rows, :] = seg(SEG_NV).astype(BF16)


def _project(x, mod_rows, gain, w_perm, w_gate, b_gate, cos, sin, *, rope, tm):
    bsz, seq, _ = x.shape
    per_batch_mod = mod_rows.shape[0] > 1
    mod_map = (lambda b, i: (b, 0, 0)) if per_batch_mod else (lambda b, i: (0, 0, 0))
    widths = [(512, BF16), (128, BF16), (128, BF16), (256, F32), (256, F32), (512, BF16), (512, F32),
              (512, BF16), (512, BF16), (512, BF16), (256, F32), (256, F32)]
    tile = lambda w: pl.BlockSpec((None, tm, w), lambda b, i: (b, i, 0))
    return pl.pallas_call(
        functools.partial(_proj_kernel, rope=rope),
        grid=(bsz, seq // tm),
        in_specs=[
            tile(D_MODEL),
            pl.BlockSpec((None, 6, D_MODEL), mod_map),
            _const_spec((1, D_MODEL)),
            _const_spec((D_MODEL, IN_PADDED)),
            _const_spec((LANES, 2 * B_HEADS * B_DK)),
            _const_spec((1, 2 * B_HEADS * B_DK)),
            pl.BlockSpec((tm, LANES), lambda b, i: (i, 0)),
            pl.BlockSpec((tm, LANES), lambda b, i: (i, 0)),
        ],
        out_specs=[tile(w) for w, _ in widths],
        out_shape=[jax.ShapeDtypeStruct((bsz, seq, w), dt) for w, dt in widths],
        compiler_params=_params(("parallel", "arbitrary")),
        name="in_proj_rope" if rope else "in_proj_ctx",
    )(x, mod_rows, gain, w_perm, w_gate, b_gate, cos, sin)


def _gla_prepare(q_ref, k_ref, g_ref, reverse):
    t = q_ref.shape[0]
    c = GLA_CHUNK
    row = lax.broadcasted_iota(jnp.int32, (t, t), 0)
    col = lax.broadcasted_iota(jnp.int32, (t, t), 1)
    same_chunk = (row // c) == (col // c)
    tri = (same_chunk & ((col >= row) if reverse else (col <= row))).astype(BF16)
    g = g_ref[...]
    g_hi = g.astype(BF16)
    rest = g - g_hi.astype(F32)
    g_mid = rest.astype(BF16)
    g_lo = (rest - g_mid.astype(F32)).astype(BF16)
    b = _dot(tri, g_hi) + _dot(tri, g_mid) + _dot(tri, g_lo)
    ends = [b[ci * c:ci * c + 1, :] if reverse else b[ci * c + c - 1:ci * c + c, :] for ci in range(t // c)]
    b_end = jnp.concatenate([jnp.broadcast_to(e, (c, e.shape[1])) for e in ends], axis=0)
    ref = 0.5 * b_end
    q = q_ref[...]
    k = k_ref[...]
    q_state = (q * jnp.exp(b)).astype(BF16)
    q_in = (q * jnp.exp(b - ref)).astype(BF16)
    k_in = (k * jnp.exp(ref - b)).astype(BF16)
    k_tail = (k * jnp.exp(b_end - b)).astype(BF16)
    return q_state, q_in, k_in, k_tail, [jnp.exp(e) for e in ends]


def _gla_local(q_in, k_in, k_tail, v_ref, reverse):
    t = q_in.shape[0]
    c = GLA_CHUNK
    qi = lax.broadcasted_iota(jnp.int32, (2 * c, c), 0) % c
    kj = lax.broadcasted_iota(jnp.int32, (2 * c, c), 1)
    keep = (kj >= qi) if reverse else (kj <= qi)
    intra, upd = [], []
    for ci in range(t // c):
        rows = slice(ci * c, (ci + 1) * c)
        intra_c, upd_c = [], []
        for p in range(B_HEADS // 2):
            ps = slice(p * LANES, (p + 1) * LANES)
            a = jnp.where(keep, _dot_nt(_split_head_pair(q_in[rows, ps]), k_in[rows, ps]), 0.0).astype(BF16)
            kt = _split_head_pair(k_tail[rows, ps])
            u = None
            for half in range(2):
                h = 2 * p + half
                vh = v_ref[rows, h * B_DV:(h + 1) * B_DV]
                intra_c.append(_dot(a[half * c:(half + 1) * c], vh))
                uh = _dot_tn(vh, kt[half * c:(half + 1) * c])
                u = uh if u is None else u + uh
            upd_c.append(u)
        intra.append(intra_c)
        upd.append(upd_c)
    return intra, upd


def _gla_states(s_ref, decay, upd, reverse):
    n = len(upd)
    order = range(n - 1, -1, -1) if reverse else range(n)
    starts = [None] * n
    s = [s_ref[p] for p in range(B_HEADS // 2)]
    for ci in order:
        starts[ci] = [sp.astype(BF16) for sp in s]
        s = [s[p] * decay[ci][:, p * LANES:(p + 1) * LANES] + upd[ci][p] for p in range(B_HEADS // 2)]
    for p in range(B_HEADS // 2):
        s_ref[p] = s[p]
    return starts


def _gla_finish(o_ref, q_state, starts, intra):
    c = GLA_CHUNK
    for ci in range(len(intra)):
        rows = slice(ci * c, (ci + 1) * c)
        for p in range(B_HEADS // 2):
            ps = slice(p * LANES, (p + 1) * LANES)
            inter = _dot_nt(_split_head_pair(q_state[rows, ps]), starts[ci][p])
            for half in range(2):
                h = 2 * p + half
                o_ref[rows, h * B_DV:(h + 1) * B_DV] = intra[ci][h] + inter[half * c:(half + 1) * c]


def _gla_kernel(qf_ref, kf_ref, vf_ref, gf_ref, qb_ref, kb_ref, vb_ref, gb_ref, s0f_ref, s0b_ref,
                of_ref, ob_ref, sf_out_ref, sb_out_ref, sf_ref, sb_ref):
    i = pl.program_id(1)

    @pl.when(i == 0)
    def _():
        sf_ref[...] = s0f_ref[...]
        sb_ref[...] = s0b_ref[...]

    qs_f, qi_f, ki_f, kt_f, dec_f = _gla_prepare(qf_ref, kf_ref, gf_ref, False)
    qs_b, qi_b, ki_b, kt_b, dec_b = _gla_prepare(qb_ref, kb_ref, gb_ref, True)
    intra_f, upd_f = _gla_local(qi_f, ki_f, kt_f, vf_ref, False)
    intra_b, upd_b = _gla_local(qi_b, ki_b, kt_b, vb_ref, True)
    starts_f = _gla_states(sf_ref, dec_f, upd_f, False)
    starts_b = _gla_states(sb_ref, dec_b, upd_b, True)
    _gla_finish(of_ref, qs_f, starts_f, intra_f)
    _gla_finish(ob_ref, qs_b, starts_b, intra_b)

    @pl.when(i == pl.num_programs(1) - 1)
    def _():
        sf_out_ref[...] = sf_ref[...]
        sb_out_ref[...] = sb_ref[...]


GLA_STATE_SHAPE = (B_HEADS // 2, B_DV, 2 * B_DK)


def _gla(q, k, v, gf, gb, s0f, s0b, *, tile):
    bsz, seq, _ = q.shape
    n = seq // tile
    fwd = lambda w: pl.BlockSpec((None, tile, w), lambda b, i: (b, i, 0))
    bwd = lambda w: pl.BlockSpec((None, tile, w), lambda b, i: (b, n - 1 - i, 0))
    st = pl.BlockSpec((None,) + GLA_STATE_SHAPE, lambda b, i: (b, 0, 0, 0))
    wk, wv = B_HEADS * B_DK, B_HEADS * B_DV
    st_shape = jax.ShapeDtypeStruct((bsz,) + GLA_STATE_SHAPE, F32)
    return pl.pallas_call(
        _gla_kernel,
        grid=(bsz, n),
        in_specs=[fwd(wk), fwd(wk), fwd(wv), fwd(wk), bwd(wk), bwd(wk), bwd(wv), bwd(wk), st, st],
        out_specs=[fwd(wv), bwd(wv), st, st],
        out_shape=[jax.ShapeDtypeStruct((bsz, seq, wv), F32), jax.ShapeDtypeStruct((bsz, seq, wv), F32),
                   st_shape, st_shape],
        scratch_shapes=[pltpu.VMEM(GLA_STATE_SHAPE, F32), pltpu.VMEM(GLA_STATE_SHAPE, F32)],
        compiler_params=_params(("parallel", "arbitrary")),
        name="gla_scan",
    )(q, k, v, gf, q, k, v, gb, s0f, s0b)


def _softmax_parts(scores, sink_tile):
    def lane_tiles(blocks):
        return [b[:, j:j + LANES] for b in blocks for j in range(0, b.shape[1], LANES)]

    tiles = lane_tiles(scores)
    if sink_tile is not None:
        tiles.append(sink_tile)
    m = functools.reduce(jnp.maximum, tiles).max(axis=-1, keepdims=True)
    ps = [jnp.exp(s - m) for s in scores]
    acc = functools.reduce(jnp.add, lane_tiles(ps))
    if sink_tile is not None:
        lane = lax.broadcasted_iota(jnp.int32, sink_tile.shape, 1)
        acc = acc + jnp.where(lane == 0, jnp.exp(sink_tile - m), 0.0)
    return ps, acc.sum(axis=-1, keepdims=True)


def _split_head_pair(t):
    lane = lax.broadcasted_iota(jnp.int32, t.shape, 1)
    zero = jnp.zeros_like(t)
    return jnp.concatenate([jnp.where(lane < HEAD_DIM, t, zero), jnp.where(lane >= HEAD_DIM, t, zero)], axis=0)


def _merge_head_pair(o):
    m = o.shape[0] // 2
    lane = lax.broadcasted_iota(jnp.int32, (m, LANES), 1)
    return jnp.where(lane < HEAD_DIM, o[:m], o[m:])


A_HEAD_ORDER = (0, 4, 1, 5, 2, 6, 3, 7)


def _win_kernel(sink_ref, q_ref, kp_ref, kc_ref, kn_ref, vp_ref, vc_ref, vn_ref, kx_ref, vx_ref, o_ref):
    n = pl.program_id(1)
    nb = pl.num_programs(1)
    n_tiles = A_HEADS // 2
    rows = A_HEADS * A_BLOCK
    q = jnp.concatenate([_split_head_pair(q_ref[:, j * LANES:(j + 1) * LANES]) for j in range(n_tiles)], axis=0)
    sink_tile = jnp.concatenate([jnp.full((A_BLOCK, LANES), sink_ref[h], F32) for h in A_HEAD_ORDER], axis=0)
    qi = lax.broadcasted_iota(jnp.int32, (rows, A_BLOCK), 0) % A_BLOCK
    kj = lax.broadcasted_iota(jnp.int32, (rows, A_BLOCK), 1)
    prev_ok = (kj >= qi) & (n > 0)
    next_ok = (kj <= qi) & (n < nb - 1)
    neg = -jnp.inf
    s_p = jnp.where(prev_ok, _dot_nt(q, kp_ref[...]), neg)
    s_c = _dot_nt(q, kc_ref[...])
    s_n = jnp.where(next_ok, _dot_nt(q, kn_ref[...]), neg)
    s_x = _dot_nt(q, kx_ref[...])
    (p_p, p_c, p_n, p_x), denom = _softmax_parts([s_p, s_c, s_n, s_x], sink_tile)
    o = (_dot(p_p.astype(BF16), vp_ref[...]) + _dot(p_c.astype(BF16), vc_ref[...])
         + _dot(p_n.astype(BF16), vn_ref[...]) + _dot(p_x.astype(BF16), vx_ref[...]))
    o = o / denom
    for j in range(n_tiles):
        o_ref[:, j * LANES:(j + 1) * LANES] = _merge_head_pair(
            o[j * 2 * A_BLOCK:(j + 1) * 2 * A_BLOCK]).astype(o_ref.dtype)


def _window_attention(sink, q, k, v, k_ctx, v_ctx):
    bsz, seq, wq = q.shape
    nb = seq // A_BLOCK
    wkv = k.shape[-1]
    n_ctx = k_ctx.shape[1]
    kv_spec = lambda f: pl.BlockSpec((None, A_BLOCK, wkv), lambda b, n: (b, f(n), 0))
    prev = lambda n: jnp.maximum(n - 1, 0)
    cur = lambda n: n
    nxt = lambda n: jnp.minimum(n + 1, nb - 1)
    ctx_spec = pl.BlockSpec((None, n_ctx, wkv), lambda b, n: (b, 0, 0))
    return pl.pallas_call(
        _win_kernel,
        grid=(bsz, nb),
        in_specs=[
            pl.BlockSpec(memory_space=pltpu.SMEM),
            pl.BlockSpec((None, A_BLOCK, wq), lambda b, n: (b, n, 0)),
            kv_spec(prev), kv_spec(cur), kv_spec(nxt),
            kv_spec(prev), kv_spec(cur), kv_spec(nxt),
            ctx_spec, ctx_spec,
        ],
        out_specs=pl.BlockSpec((None, A_BLOCK, wq), lambda b, n: (b, n, 0)),
        out_shape=jax.ShapeDtypeStruct((bsz, seq, wq), BF16),
        compiler_params=_params(("parallel", "arbitrary")),
        name="window_attn",
    )(sink, q, k, k, k, v, v, v, k_ctx, v_ctx)


def _dense_kernel(sink_ref, q_ref, k_ref, v_ref, o_ref, *, head_order, shared_kv):
    n_q = q_ref.shape[0]
    for j in range(q_ref.shape[1] // LANES):
        qs = slice(j * LANES, (j + 1) * LANES)
        ks = slice(0, LANES) if shared_kv else qs
        s = _dot_nt(_split_head_pair(q_ref[:, qs]), k_ref[:, ks])
        sink_tile = None
        if head_order is not None:
            sink_tile = jnp.concatenate(
                [jnp.full((n_q, LANES), sink_ref[head_order[2 * j + half]], F32) for half in range(2)], axis=0)
        (p,), denom = _softmax_parts([s], sink_tile)
        o_ref[:, qs] = _merge_head_pair(_dot(p.astype(BF16), v_ref[:, ks]) / denom).astype(o_ref.dtype)


def _dense_attention(sink, q, k, v, *, head_order, shared_kv):
    bsz, n_q, wq = q.shape
    wkv = k.shape[-1]
    full = lambda w: pl.BlockSpec((None, n_q, w), lambda b: (b, 0, 0))
    return pl.pallas_call(
        functools.partial(_dense_kernel, head_order=head_order, shared_kv=shared_kv),
        grid=(bsz,),
        in_specs=[pl.BlockSpec(memory_space=pltpu.SMEM), full(wq), full(wkv), full(wkv)],
        out_specs=full(wq),
        out_shape=jax.ShapeDtypeStruct((bsz, n_q, wq), BF16),
        compiler_params=_params(("parallel",)),
        name="ctx_dense_attn_sink" if shared_kv else "ctx_dense_attn",
    )(sink, q, k, v)


NA_ROWS_PER_STEP = 4


def _na_kernel(q_ref, k_ref, v_ref, kx_ref, vx_ref, bias_ref, o_ref):
    step = pl.program_id(1)
    grid_rows = k_ref.shape[0] // GRID_W
    n_keys = C_WIN_ROWS * GRID_W
    n_pairs = C_HEADS // 2
    for rr in range(NA_ROWS_PER_STEP):
        r = step * NA_ROWS_PER_STEP + rr
        row0 = jnp.clip(r - C_WIN_ROWS // 2, 0, grid_rows - C_WIN_ROWS)
        d_row0 = row0 - r + (C_WIN_ROWS - 1)
        key0 = pl.multiple_of(row0 * GRID_W, GRID_W)
        qrows = slice(rr * GRID_W, (rr + 1) * GRID_W)
        s_l, s_x = [], []
        for p in range(n_pairs):
            ps = slice(p * LANES, (p + 1) * LANES)
            q2 = _split_head_pair(q_ref[qrows, ps])
            s_l.append(_dot_nt(q2, k_ref[pl.ds(key0, n_keys), ps]))
            s_x.append(_dot_nt(q2, kx_ref[:, ps]))
        bias = jnp.concatenate(
            [bias_ref[d_row0 + 2 * j].reshape(C_HEADS * GRID_W, LANES) for j in range(C_WIN_ROWS // 2)], axis=1)
        (p_l, p_x), denom = _softmax_parts([jnp.concatenate(s_l, axis=0) + bias, jnp.concatenate(s_x, axis=0)], None)
        p_l = p_l.astype(BF16)
        p_x = p_x.astype(BF16)
        for p in range(n_pairs):
            ps = slice(p * LANES, (p + 1) * LANES)
            pr = slice(p * 2 * GRID_W, (p + 1) * 2 * GRID_W)
            o = (_dot(p_l[pr], v_ref[pl.ds(key0, n_keys), ps]) + _dot(p_x[pr], vx_ref[:, ps])) / denom[pr]
            o_ref[qrows, ps] = _merge_head_pair(o).astype(o_ref.dtype)


def _neighbourhood_attention(q, k, v, k_ctx, v_ctx, bias_tbl):
    bsz, seq, w = q.shape
    n_ctx = k_ctx.shape[1]
    tq = NA_ROWS_PER_STEP * GRID_W
    whole = pl.BlockSpec((None, seq, w), lambda b, i: (b, 0, 0))
    ctx_spec = pl.BlockSpec((None, n_ctx, w), lambda b, i: (b, 0, 0))
    return pl.pallas_call(
        _na_kernel,
        grid=(bsz, seq // tq),
        in_specs=[pl.BlockSpec((None, tq, w), lambda b, i: (b, i, 0)), whole, whole, ctx_spec, ctx_spec,
                  _const_spec(bias_tbl.shape)],
        out_specs=pl.BlockSpec((None, tq, w), lambda b, i: (b, i, 0)),
        out_shape=jax.ShapeDtypeStruct((bsz, seq, w), BF16),
        compiler_params=_params(("parallel", "arbitrary")),
        name="neighbourhood_attn",
    )(q, k, v, k_ctx, v_ctx, bias_tbl)


def _na_bias_table(rpb):
    qcol = jnp.arange(GRID_W)[:, None]
    kcol = jnp.arange(GRID_W)[None, :]
    wstart = jnp.clip(qcol - C_WIN_COLS // 2, 0, GRID_W - C_WIN_COLS)
    ok = (kcol >= wstart) & (kcol < wstart + C_WIN_COLS)
    n_off = C_WIN_COLS - 1
    period = 2 * GRID_W
    vec = jnp.concatenate([rpb[..., n_off:].astype(F32),
                           jnp.zeros(rpb.shape[:2] + (period - 2 * n_off - 1,), F32),
                           rpb[..., :n_off].astype(F32)], axis=-1)
    toep = jnp.tile(vec, (1, 1, GRID_W))[..., :GRID_W * (period - 1)]
    toep = toep.reshape(rpb.shape[:2] + (GRID_W, period - 1))[..., :GRID_W]
    tbl = jnp.where(ok[None, None], toep, -jnp.inf)
    tbl = jnp.moveaxis(tbl, 1, 0)
    return jnp.concatenate([tbl[:-1], tbl[1:]], axis=-1)


def _merge_kernel(x_ref, mod_ref, gain_ref, ya_ref, of_ref, ob_ref, gr_ref, yc_ref, gn_ref,
                  wm_ref, bm_ref, wa_ref, wb_ref, wc_ref, wo_ref, o_ref):
    d = D_MODEL
    gn = gn_ref[...]
    for r0 in range(0, x_ref.shape[0], SUB_ROWS):
        rows = slice(r0, r0 + SUB_ROWS)
        x = x_ref[rows, :]
        h = _norm_mod(x, gain_ref[...], mod_ref[0:1, :], mod_ref[1:2, :]).astype(BF16)
        o_sum = of_ref[rows, :] + ob_ref[rows, :]
        parts = []
        for hh in range(B_HEADS):
            oh = o_sum[:, hh * B_DV:(hh + 1) * B_DV]
            parts.append((oh * lax.rsqrt(jnp.mean(oh * oh, axis=-1, keepdims=True) + EPS)) * gn)
        y_b = (jnp.concatenate(parts, axis=1) * _silu(gr_ref[rows, :])).astype(BF16)
        mixed = None
        for j, (y, w_ref) in enumerate(((ya_ref[rows, :], wa_ref), (y_b, wb_ref), (yc_ref[rows, :], wc_ref))):
            gate = _sigmoid(_dot(h, wm_ref[:, j * d:(j + 1) * d]) + bm_ref[:, j * d:(j + 1) * d])
            term = gate * _dot(y, w_ref[...])
            mixed = term if mixed is None else mixed + term
        o_ref[rows, :] = x + mod_ref[2:3, :] * _dot(mixed.astype(BF16), wo_ref[...])


def _merge(x, mod_rows, gain, y_a, o_f, o_b, g_r, y_c, gla_gain, w_merge, b_merge, w_a, w_b, w_c, w_out, *, tm):
    bsz, seq, d = x.shape
    per_batch_mod = mod_rows.shape[0] > 1
    mod_map = (lambda b, i: (b, 0, 0)) if per_batch_mod else (lambda b, i: (0, 0, 0))
    tile = lambda w: pl.BlockSpec((None, tm, w), lambda b, i: (b, i, 0))
    wy = y_a.shape[-1]
    return pl.pallas_call(
        _merge_kernel,
        grid=(bsz, seq // tm),
        in_specs=[tile(d), pl.BlockSpec((None, 6, d), mod_map), _const_spec((1, d)),
                  tile(wy), tile(wy), tile(wy), tile(wy), tile(wy), _const_spec((1, B_DV)),
                  _const_spec(w_merge.shape), _const_spec(b_merge.shape),
                  _const_spec(w_a.shape), _const_spec(w_b.shape), _const_spec(w_c.shape),
                  _const_spec(w_out.shape)],
        out_specs=tile(d),
        out_shape=jax.ShapeDtypeStruct((bsz, seq, d), F32),
        compiler_params=_params(("parallel", "arbitrary")),
        name="merge_out",
    )(x, mod_rows, gain, y_a, o_f, o_b, g_r, y_c, gla_gain, w_merge, b_merge, w_a, w_b, w_c, w_out)


FFN_CHUNKS = ((0, 1024), (1024, 2048), (2048, FFN_HIDDEN))


def _ffn_kernel(x_ref, mod_ref, gain_ref, w1_ref, w2_ref, fg_ref, o_ref, *, final):
    for r0 in range(0, x_ref.shape[0], SUB_ROWS):
        rows = slice(r0, r0 + SUB_ROWS)
        x = x_ref[rows, :]
        h = _norm_mod(x, gain_ref[...], mod_ref[3:4, :], mod_ref[4:5, :]).astype(BF16)
        acc = None
        for c0, c1 in FFN_CHUNKS:
            gate = _dot(h, w1_ref[:, c0:c1])
            up = _dot(h, w1_ref[:, FFN_HIDDEN + c0:FFN_HIDDEN + c1])
            part = _dot((_silu(gate) * up).astype(BF16), w2_ref[c0:c1, :])
            acc = part if acc is None else acc + part
        y = x + mod_ref[5:6, :] * acc
        if final:
            y = (y * lax.rsqrt(jnp.mean(y * y, axis=-1, keepdims=True) + EPS)) * fg_ref[...]
        o_ref[rows, :] = y


def _ffn(x, mod_rows, gain, w1, w2, final_gain, *, final, tm):
    bsz, seq, d = x.shape
    per_batch_mod = mod_rows.shape[0] > 1
    mod_map = (lambda b, i: (b, 0, 0)) if per_batch_mod else (lambda b, i: (0, 0, 0))
    tile = pl.BlockSpec((None, tm, d), lambda b, i: (b, i, 0))
    return pl.pallas_call(
        functools.partial(_ffn_kernel, final=final),
        grid=(bsz, seq // tm),
        in_specs=[tile, pl.BlockSpec((None, 6, d), mod_map), _const_spec((1, d)),
                  _const_spec(w1.shape), _const_spec(w2.shape), _const_spec((1, d))],
        out_specs=tile,
        out_shape=jax.ShapeDtypeStruct((bsz, seq, d), F32),
        compiler_params=_params(("parallel", "arbitrary")),
        name="ffn_final" if final else "ffn",
    )(x, mod_rows, gain, w1, w2, final_gain)


def _rope_tables(seq):
    t = jnp.arange(seq)
    row = (t // GRID_W).astype(F32)
    col = (t % GRID_W).astype(F32)
    n_freq = HEAD_DIM // 4
    inv = ROPE_BASE ** (-jnp.arange(n_freq, dtype=F32) / n_freq)
    ang = jnp.concatenate([row[:, None] * inv[None], col[:, None] * inv[None]], axis=-1)
    cos, sin = jnp.cos(ang), jnp.sin(ang)
    return jnp.tile(cos, (1, 4)), jnp.tile(jnp.concatenate([-sin, sin], axis=-1), (1, 2))


def _permute_w_in(w):
    ga0 = 2304
    ga1 = ga0 + 2 * B_GATE_RANK
    n_aq = A_HEADS * HEAD_DIM
    aq = w[:, :n_aq].reshape(w.shape[0], A_HEADS, HEAD_DIM)[:, jnp.array(A_HEAD_ORDER)].reshape(w.shape[0], n_aq)
    pad = jnp.zeros((w.shape[0], IN_PADDED - w.shape[1]), w.dtype)
    return jnp.concatenate([aq, w[:, n_aq:ga0], w[:, ga1:], w[:, ga0:ga1], pad], axis=1).astype(BF16)


def _permute_w_branch_a(w):
    return w.reshape(A_HEADS, HEAD_DIM, w.shape[1])[jnp.array(A_HEAD_ORDER)].reshape(w.shape).astype(BF16)


def _gate_weights(w_fwd, b_fwd, w_bwd, b_bwd):
    n = B_HEADS * B_DK
    r = B_GATE_RANK
    w = jnp.zeros((LANES, 2 * n), F32)
    w = w.at[:r, :n].set(w_fwd).at[r:2 * r, n:].set(w_bwd)
    return w.astype(BF16), jnp.concatenate([b_fwd, b_bwd])[None, :]


def kernel(x, c, ctx, c_ctx, w_ada, b_ada, norm_mix, w_in, attn_sink, gla_gate_w_fwd, gla_gate_b_fwd,
           gla_gate_w_bwd, gla_gate_b_bwd, gla_norm, na_rpb, w_branch_a, w_branch_b, w_branch_c,
           w_merge, b_merge, w_out, norm_ffn, w_ffn_in, w_ffn_out, final_norm):
    bsz, seq, d = x.shape
    n_ctx = ctx.shape[1]
    cos, sin = _rope_tables(seq)

    cvec = jnp.zeros((8, d), F32).at[:bsz].set(c).at[bsz].set(c_ctx)
    mod = _modulation(cvec, w_ada, b_ada).reshape(DEPTH, 8, 6, d)

    xc = ctx
    zeros_state = jnp.zeros((bsz,) + GLA_STATE_SHAPE, F32)
    for l in range(DEPTH):
        last = l == DEPTH - 1
        mod_x = mod[l, :bsz]
        mod_c = mod[l, bsz:bsz + 1]
        gain_m = norm_mix[l][None, :]
        w_perm = _permute_w_in(w_in[l])
        w_gate, b_gate = _gate_weights(gla_gate_w_fwd[l], gla_gate_b_fwd[l], gla_gate_w_bwd[l], gla_gate_b_bwd[l])

        (caq, cak, cav, cgq, cgk, cgv, cgr, cnq, cnk, cnv, cgf, cgb) = _project(
            xc, mod_c, gain_m, w_perm, w_gate, b_gate, cos, sin, rope=False, tm=n_ctx)
        (aq, ak, av, gq, gk, gv, gr, nq, nk, nv, gf, gb) = _project(
            x, mod_x, gain_m, w_perm, w_gate, b_gate, cos, sin, rope=True, tm=512)

        co_f, co_b, s_f, s_b = _gla(cgq, cgk, cgv, cgf, cgb, zeros_state, zeros_state, tile=n_ctx)
        o_f, o_b, _, _ = _gla(gq, gk, gv, gf, gb, s_f, s_b, tile=256)

        y_a = _window_attention(attn_sink[l], aq, ak, av, cak, cav)
        y_c = _neighbourhood_attention(nq, nk, nv, cnk, cnv, _na_bias_table(na_rpb[l]))

        merge_w = (gla_norm[l][None, :], w_merge[l].astype(BF16), b_merge[l][None, :],
                   _permute_w_branch_a(w_branch_a[l]), w_branch_b[l].astype(BF16), w_branch_c[l].astype(BF16),
                   w_out[l].astype(BF16))
        x = _merge(x, mod_x, gain_m, y_a, o_f, o_b, gr, y_c, *merge_w, tm=512)

        gain_f = norm_ffn[l][None, :]
        w1 = w_ffn_in[l].astype(BF16)
        w2 = w_ffn_out[l].astype(BF16)
        fg = final_norm[None, :]
        if not last:
            yc_a = _dense_attention(attn_sink[l], caq, cak, cav, head_order=A_HEAD_ORDER, shared_kv=True)
            yc_c = _dense_attention(attn_sink[l], cnq, cnk, cnv, head_order=None, shared_kv=False)
            xc = _merge(xc, mod_c, gain_m, yc_a, co_f, co_b, cgr, yc_c, *merge_w, tm=n_ctx)
            xc = _ffn(xc, mod_c, gain_f, w1, w2, fg, final=False, tm=n_ctx)
        x = _ffn(x, mod_x, gain_f, w1, w2, fg, final=last, tm=512)
    return x
```

```python
import functools

import jax
import jax.numpy as jnp
from jax import lax
from jax.experimental import pallas as pl
from jax.experimental.pallas import tpu as pltpu

F32 = jnp.float32
BF16 = jnp.bfloat16

D_MODEL = 1024
DEPTH = 2
GRID_W = 64
HEAD_DIM = 64
EPS = 1e-6
ROPE_BASE = 10000.0
A_HEADS = 8
A_KV_HEADS = 2
A_BLOCK = 128
B_HEADS = 4
B_DK = 64
B_DV = 128
B_GATE_RANK = 16
B_GATE_NORM = 16.0
C_HEADS = 8
C_WIN_ROWS = 8
C_WIN_COLS = 16
FFN_HIDDEN = 2816

VMEM_LIMIT_BYTES = 56 * 1024 * 1024
LANES = 128

SEG_AQ = (0, 512)
SEG_AKV = (512, 768)
SEG_GQ = (768, 1024)
SEG_GK = (1024, 1280)
SEG_GV = (1280, 1792)
SEG_GR = (1792, 2304)
SEG_NQ = (2304, 2816)
SEG_NK = (2816, 3328)
SEG_NV = (3328, 3840)
SEG_GA = (3840, 3968)
IN_PADDED = 3968

LOG2E = 1.4426950408889634
ATTN_Q_SCALE = HEAD_DIM ** -0.5 * LOG2E

GLA_CHUNK = 64
SUB_ROWS = 512


def _params(sem):
    return pltpu.CompilerParams(dimension_semantics=sem, vmem_limit_bytes=VMEM_LIMIT_BYTES)


def _const_spec(shape):
    nd = len(shape)
    return pl.BlockSpec(shape, lambda *_: (0,) * nd, pipeline_mode=pl.Buffered(1))


def _sigmoid(x):
    return 1.0 / (1.0 + jnp.exp(-x))


def _silu(x):
    return x * _sigmoid(x)


def _dot(a, b):
    return jnp.dot(a, b, preferred_element_type=F32)


def _dot_nt(a, b):
    return lax.dot_general(a, b, (((1,), (1,)), ((), ())), preferred_element_type=F32)


def _dot_tn(a, b):
    return lax.dot_general(a, b, (((0,), (0,)), ((), ())), preferred_element_type=F32)


def _norm_mod(x, gain, shift, scale):
    y = x * lax.rsqrt(jnp.mean(x * x, axis=-1, keepdims=True) + EPS)
    return (y * gain) * (1.0 + scale) + shift


def _mod_kernel(c_ref, w_ref, b_ref, o_ref):
    s = _silu(c_ref[...])
    o_ref[...] = _dot(s.astype(BF16), w_ref[...].astype(BF16)) + b_ref[...]


def _modulation(cvec, w_ada, b_ada):
    tn = 1536
    n_out = w_ada.shape[-1]
    return pl.pallas_call(
        _mod_kernel,
        grid=(DEPTH, n_out // tn),
        in_specs=[
            pl.BlockSpec((8, D_MODEL), lambda l, j: (0, 0)),
            pl.BlockSpec((None, D_MODEL, tn), lambda l, j: (l, 0, j)),
            pl.BlockSpec((None, 1, tn), lambda l, j: (l, 0, j)),
        ],
        out_specs=pl.BlockSpec((None, 8, tn), lambda l, j: (l, 0, j)),
        out_shape=jax.ShapeDtypeStruct((DEPTH, 8, n_out), F32),
        compiler_params=_params(("arbitrary", "arbitrary")),
        name="adaln_mod",
    )(cvec, w_ada, b_ada.reshape(DEPTH, 1, n_out))


def _rope(t, cos, sin):
    n = t.shape[1]
    lane = lax.broadcasted_iota(jnp.int32, t.shape, 1)
    first_half = (lane % HEAD_DIM) < (HEAD_DIM // 2)
    rot = jnp.where(first_half, pltpu.roll(t, n - HEAD_DIM // 2, 1), pltpu.roll(t, HEAD_DIM // 2, 1))
    reps = n // LANES
    return t * jnp.tile(cos, (1, reps)) + rot * jnp.tile(sin, (1, reps))


def _proj_kernel(x_ref, mod_ref, gain_ref, w_ref, wg_ref, bg_ref, cos_ref, sin_ref,
                 aq_ref, ak_ref, av_ref, gq_ref, gk_ref, gv_ref, gr_ref, nq_ref, nk_ref, nv_ref,
                 gf_ref, gb_ref, *, rope):
    scale = ATTN_Q_SCALE
    n_g = B_HEADS * B_DK
    def normed(r0):
        return _norm_mod(x_ref[r0:r0 + SUB_ROWS, :], gain_ref[...], mod_ref[0:1, :], mod_ref[1:2, :]).astype(BF16)

    n_rows = x_ref.shape[0]
    h_next = normed(0)
    for r0 in range(0, n_rows, SUB_ROWS):
        rows = slice(r0, r0 + SUB_ROWS)
        h = h_next

        def seg(s):
            return _dot(h, w_ref[:, s[0]:s[1]])

        aq = seg(SEG_AQ)
        akv = seg(SEG_AKV)
        z = _dot(seg(SEG_GA).astype(BF16), wg_ref[...]) + bg_ref[...]
        if r0 + SUB_ROWS < n_rows:
            h_next = normed(r0 + SUB_ROWS)
        ak = akv[:, :LANES]
        if rope:
            cos = cos_ref[rows, :]
            sin = sin_ref[rows, :]
            aq = _rope(aq, cos, sin)
            ak = _rope(ak, cos, sin)
        aq_ref[rows, :] = (aq * scale).astype(BF16)
        ak_ref[rows, :] = ak.astype(BF16)
        av_ref[rows, :] = akv[:, LANES:].astype(BF16)
        log_sig = jnp.minimum(z, 0.0) - jnp.log1p(jnp.exp(-jnp.abs(z)))
        g = log_sig / B_GATE_NORM
        gf_ref[rows, :] = g[:, :n_g]
        gb_ref[rows, :] = g[:, n_g:]
        gq_ref[rows, :] = seg(SEG_GQ) * (B_DK ** -0.5)
        gk_ref[rows, :] = seg(SEG_GK)
        gv_ref[rows, :] = seg(SEG_GV).astype(BF16)
        gr_ref[rows, :] = seg(SEG_GR)
        nq_ref[rows, :] = (seg(SEG_NQ) * scale).astype(BF16)
        nk_ref[rows, :] = seg(SEG_NK).astype(BF16)
        nv_ref[rows, :] = seg(SEG_NV).astype(BF16)


def _project(x, mod_rows, gain, w_perm, w_gate, b_gate, cos, sin, *, rope, tm):
    bsz, seq, _ = x.shape
    per_batch_mod = mod_rows.shape[0] > 1
    mod_map = (lambda b, i: (b, 0, 0)) if per_batch_mod else (lambda b, i: (0, 0, 0))
    widths = [(512, BF16), (128, BF16), (128, BF16), (256, F32), (256, F32), (512, BF16), (512, F32),
              (512, BF16), (512, BF16), (512, BF16), (256, F32), (256, F32)]
    tile = lambda w: pl.BlockSpec((None, tm, w), lambda b, i: (b, i, 0))
    return pl.pallas_call(
        functools.partial(_proj_kernel, rope=rope),
        grid=(bsz, seq // tm),
        in_specs=[
            tile(D_MODEL),
            pl.BlockSpec((None, 6, D_MODEL), mod_map),
            _const_spec((1, D_MODEL)),
            _const_spec((D_MODEL, IN_PADDED)),
            _const_spec((LANES, 2 * B_HEADS * B_DK)),
            _const_spec((1, 2 * B_HEADS * B_DK)),
            pl.BlockSpec((tm, LANES), lambda b, i: (i, 0)),
            pl.BlockSpec((tm, LANES), lambda b, i: (i, 0)),
        ],
        out_specs=[tile(w) for w, _ in widths],
        out_shape=[jax.ShapeDtypeStruct((bsz, seq, w), dt) for w, dt in widths],
        compiler_params=_params(("parallel", "arbitrary")),
        name="in_proj_rope" if rope else "in_proj_ctx",
    )(x, mod_rows, gain, w_perm, w_gate, b_gate, cos, sin)


def _gla_prepare(q_ref, k_ref, g_ref, reverse):
    t = q_ref.shape[0]
    c = GLA_CHUNK
    row = lax.broadcasted_iota(jnp.int32, (t, t), 0)
    col = lax.broadcasted_iota(jnp.int32, (t, t), 1)
    same_chunk = (row // c) == (col // c)
    tri = (same_chunk & ((col >= row) if reverse else (col <= row))).astype(BF16)
    g = g_ref[...]
    g_hi = g.astype(BF16)
    rest = g - g_hi.astype(F32)
    g_mid = rest.astype(BF16)
    g_lo = (rest - g_mid.astype(F32)).astype(BF16)
    b = _dot(tri, g_hi) + _dot(tri, g_mid) + _dot(tri, g_lo)
    ends = [b[ci * c:ci * c + 1, :] if reverse else b[ci * c + c - 1:ci * c + c, :] for ci in range(t // c)]
    b_end = jnp.concatenate([jnp.broadcast_to(e, (c, e.shape[1])) for e in ends], axis=0)
    ref = 0.5 * b_end
    q = q_ref[...]
    k = k_ref[...]
    q_state = (q * jnp.exp(b)).astype(BF16)
    q_in = (q * jnp.exp(b - ref)).astype(BF16)
    k_in = (k * jnp.exp(ref - b)).astype(BF16)
    k_tail = (k * jnp.exp(b_end - b)).astype(BF16)
    return q_state, q_in, k_in, k_tail, [jnp.exp(e) for e in ends]


def _gla_local(q_in, k_in, k_tail, v_ref, reverse):
    t = q_in.shape[0]
    c = GLA_CHUNK
    qi = lax.broadcasted_iota(jnp.int32, (2 * c, c), 0) % c
    kj = lax.broadcasted_iota(jnp.int32, (2 * c, c), 1)
    keep = (kj >= qi) if reverse else (kj <= qi)
    intra, upd = [], []
    for ci in range(t // c):
        rows = slice(ci * c, (ci + 1) * c)
        intra_c, upd_c = [], []
        for p in range(B_HEADS // 2):
            ps = slice(p * LANES, (p + 1) * LANES)
            a = jnp.where(keep, _dot_nt(_split_head_pair(q_in[rows, ps]), k_in[rows, ps]), 0.0).astype(BF16)
            kt = _split_head_pair(k_tail[rows, ps])
            u = None
            for half in range(2):
                h = 2 * p + half
                vh = v_ref[rows, h * B_DV:(h + 1) * B_DV]
                intra_c.append(_dot(a[half * c:(half + 1) * c], vh))
                uh = _dot_tn(vh, kt[half * c:(half + 1) * c])
                u = uh if u is None else u + uh
            upd_c.append(u)
        intra.append(intra_c)
        upd.append(upd_c)
    return intra, upd


def _gla_states(s_ref, decay, upd, reverse):
    n = len(upd)
    order = range(n - 1, -1, -1) if reverse else range(n)
    starts = [None] * n
    s = [s_ref[p] for p in range(B_HEADS // 2)]
    for ci in order:
        starts[ci] = [sp.astype(BF16) for sp in s]
        s = [s[p] * decay[ci][:, p * LANES:(p + 1) * LANES] + upd[ci][p] for p in range(B_HEADS // 2)]
    for p in range(B_HEADS // 2):
        s_ref[p] = s[p]
    return starts


def _gla_finish(o_ref, q_state, starts, intra):
    c = GLA_CHUNK
    for ci in range(len(intra)):
        rows = slice(ci * c, (ci + 1) * c)
        for p in range(B_HEADS // 2):
            ps = slice(p * LANES, (p + 1) * LANES)
            inter = _dot_nt(_split_head_pair(q_state[rows, ps]), starts[ci][p])
            for half in range(2):
                h = 2 * p + half
                o_ref[rows, h * B_DV:(h + 1) * B_DV] = intra[ci][h] + inter[half * c:(half + 1) * c]


def _gla_kernel(qf_ref, kf_ref, vf_ref, gf_ref, qb_ref, kb_ref, vb_ref, gb_ref, s0f_ref, s0b_ref,
                of_ref, ob_ref, sf_out_ref, sb_out_ref, sf_ref, sb_ref):
    i = pl.program_id(1)

    @pl.when(i == 0)
    def _():
        sf_ref[...] = s0f_ref[...]
        sb_ref[...] = s0b_ref[...]

    qs_f, qi_f, ki_f, kt_f, dec_f = _gla_prepare(qf_ref, kf_ref, gf_ref, False)
    qs_b, qi_b, ki_b, kt_b, dec_b = _gla_prepare(qb_ref, kb_ref, gb_ref, True)
    intra_f, upd_f = _gla_local(qi_f, ki_f, kt_f, vf_ref, False)
    intra_b, upd_b = _gla_local(qi_b, ki_b, kt_b, vb_ref, True)
    starts_f = _gla_states(sf_ref, dec_f, upd_f, False)
    starts_b = _gla_states(sb_ref, dec_b, upd_b, True)
    _gla_finish(of_ref, qs_f, starts_f, intra_f)
    _gla_finish(ob_ref, qs_b, starts_b, intra_b)

    @pl.when(i == pl.num_programs(1) - 1)
    def _():
        sf_out_ref[...] = sf_ref[...]
        sb_out_ref[...] = sb_ref[...]


GLA_STATE_SHAPE = (B_HEADS // 2, B_DV, 2 * B_DK)


def _gla(q, k, v, gf, gb, s0f, s0b, *, tile):
    bsz, seq, _ = q.shape
    n = seq // tile
    fwd = lambda w: pl.BlockSpec((None, tile, w), lambda b, i: (b, i, 0))
    bwd = lambda w: pl.BlockSpec((None, tile, w), lambda b, i: (b, n - 1 - i, 0))
    st = pl.BlockSpec((None,) + GLA_STATE_SHAPE, lambda b, i: (b, 0, 0, 0))
    wk, wv = B_HEADS * B_DK, B_HEADS * B_DV
    st_shape = jax.ShapeDtypeStruct((bsz,) + GLA_STATE_SHAPE, F32)
    return pl.pallas_call(
        _gla_kernel,
        grid=(bsz, n),
        in_specs=[fwd(wk), fwd(wk), fwd(wv), fwd(wk), bwd(wk), bwd(wk), bwd(wv), bwd(wk), st, st],
        out_specs=[fwd(wv), bwd(wv), st, st],
        out_shape=[jax.ShapeDtypeStruct((bsz, seq, wv), F32), jax.ShapeDtypeStruct((bsz, seq, wv), F32),
                   st_shape, st_shape],
        scratch_shapes=[pltpu.VMEM(GLA_STATE_SHAPE, F32), pltpu.VMEM(GLA_STATE_SHAPE, F32)],
        compiler_params=_params(("parallel", "arbitrary")),
        name="gla_scan",
    )(q, k, v, gf, q, k, v, gb, s0f, s0b)


def _softmax_parts(scores, sink_tile):
    def lane_tiles(blocks):
        return [b[:, j:j + LANES] for b in blocks for j in range(0, b.shape[1], LANES)]

    tiles = lane_tiles(scores)
    if sink_tile is not None:
        tiles.append(sink_tile)
    m = functools.reduce(jnp.maximum, tiles).max(axis=-1, keepdims=True)
    ps = [jnp.exp2(s - m) for s in scores]
    acc = functools.reduce(jnp.add, lane_tiles(ps))
    if sink_tile is not None:
        lane = lax.broadcasted_iota(jnp.int32, sink_tile.shape, 1)
        acc = acc + jnp.where(lane == 0, jnp.exp2(sink_tile - m), 0.0)
    return ps, acc.sum(axis=-1, keepdims=True)


def _split_head_pair(t):
    lane = lax.broadcasted_iota(jnp.int32, t.shape, 1)
    zero = jnp.zeros_like(t)
    return jnp.concatenate([jnp.where(lane < HEAD_DIM, t, zero), jnp.where(lane >= HEAD_DIM, t, zero)], axis=0)


def _merge_head_pair(o):
    m = o.shape[0] // 2
    lane = lax.broadcasted_iota(jnp.int32, (m, LANES), 1)
    return jnp.where(lane < HEAD_DIM, o[:m], o[m:])


A_HEAD_ORDER = (0, 4, 1, 5, 2, 6, 3, 7)


def _win_kernel(sink_ref, q_ref, kp_ref, kc_ref, kn_ref, vp_ref, vc_ref, vn_ref, kx_ref, vx_ref, o_ref):
    n = pl.program_id(1)
    nb = pl.num_programs(1)
    n_tiles = A_HEADS // 2
    rows = A_HEADS * A_BLOCK
    q = jnp.concatenate([_split_head_pair(q_ref[:, j * LANES:(j + 1) * LANES]) for j in range(n_tiles)], axis=0)
    sink_tile = jnp.concatenate([jnp.full((A_BLOCK, LANES), sink_ref[h], F32) for h in A_HEAD_ORDER], axis=0)
    qi = lax.broadcasted_iota(jnp.int32, (A_BLOCK, A_BLOCK), 0)
    kj = lax.broadcasted_iota(jnp.int32, (A_BLOCK, A_BLOCK), 1)
    neg = jnp.full((A_BLOCK, A_BLOCK), -jnp.inf, F32)
    zero = jnp.zeros((A_BLOCK, A_BLOCK), F32)
    prev_mask = jnp.where(n > 0, jnp.where(kj >= qi, zero, neg), neg)
    next_mask = jnp.where(n < nb - 1, jnp.where(kj <= qi, zero, neg), neg)
    s_p = _dot_nt(q, kp_ref[...]) + jnp.tile(prev_mask, (A_HEADS, 1))
    s_c = _dot_nt(q, kc_ref[...])
    s_n = _dot_nt(q, kn_ref[...]) + jnp.tile(next_mask, (A_HEADS, 1))
    s_x = _dot_nt(q, kx_ref[...])
    (p_p, p_c, p_n, p_x), denom = _softmax_parts([s_p, s_c, s_n, s_x], sink_tile)
    o = (_dot(p_p.astype(BF16), vp_ref[...]) + _dot(p_c.astype(BF16), vc_ref[...])
         + _dot(p_n.astype(BF16), vn_ref[...]) + _dot(p_x.astype(BF16), vx_ref[...]))
    o = o / denom
    for j in range(n_tiles):
        o_ref[:, j * LANES:(j + 1) * LANES] = _merge_head_pair(
            o[j * 2 * A_BLOCK:(j + 1) * 2 * A_BLOCK]).astype(o_ref.dtype)


def _window_attention(sink, q, k, v, k_ctx, v_ctx):
    bsz, seq, wq = q.shape
    nb = seq // A_BLOCK
    wkv = k.shape[-1]
    n_ctx = k_ctx.shape[1]
    kv_spec = lambda f: pl.BlockSpec((None, A_BLOCK, wkv), lambda b, n: (b, f(n), 0))
    prev = lambda n: jnp.maximum(n - 1, 0)
    cur = lambda n: n
    nxt = lambda n: jnp.minimum(n + 1, nb - 1)
    ctx_spec = pl.BlockSpec((None, n_ctx, wkv), lambda b, n: (b, 0, 0))
    return pl.pallas_call(
        _win_kernel,
        grid=(bsz, nb),
        in_specs=[
            pl.BlockSpec(memory_space=pltpu.SMEM),
            pl.BlockSpec((None, A_BLOCK, wq), lambda b, n: (b, n, 0)),
            kv_spec(prev), kv_spec(cur), kv_spec(nxt),
            kv_spec(prev), kv_spec(cur), kv_spec(nxt),
            ctx_spec, ctx_spec,
        ],
        out_specs=pl.BlockSpec((None, A_BLOCK, wq), lambda b, n: (b, n, 0)),
        out_shape=jax.ShapeDtypeStruct((bsz, seq, wq), BF16),
        compiler_params=_params(("parallel", "arbitrary")),
        name="window_attn",
    )(sink, q, k, k, k, v, v, v, k_ctx, v_ctx)


def _dense_kernel(sink_ref, q_ref, k_ref, v_ref, o_ref, *, head_order, shared_kv):
    n_q = q_ref.shape[0]
    for j in range(q_ref.shape[1] // LANES):
        qs = slice(j * LANES, (j + 1) * LANES)
        ks = slice(0, LANES) if shared_kv else qs
        s = _dot_nt(_split_head_pair(q_ref[:, qs]), k_ref[:, ks])
        sink_tile = None
        if head_order is not None:
            sink_tile = jnp.concatenate(
                [jnp.full((n_q, LANES), sink_ref[head_order[2 * j + half]], F32) for half in range(2)], axis=0)
        (p,), denom = _softmax_parts([s], sink_tile)
        o_ref[:, qs] = _merge_head_pair(_dot(p.astype(BF16), v_ref[:, ks]) / denom).astype(o_ref.dtype)


def _dense_attention(sink, q, k, v, *, head_order, shared_kv):
    bsz, n_q, wq = q.shape
    wkv = k.shape[-1]
    full = lambda w: pl.BlockSpec((None, n_q, w), lambda b: (b, 0, 0))
    return pl.pallas_call(
        functools.partial(_dense_kernel, head_order=head_order, shared_kv=shared_kv),
        grid=(bsz,),
        in_specs=[pl.BlockSpec(memory_space=pltpu.SMEM), full(wq), full(wkv), full(wkv)],
        out_specs=full(wq),
        out_shape=jax.ShapeDtypeStruct((bsz, n_q, wq), BF16),
        compiler_params=_params(("parallel",)),
        name="ctx_dense_attn_sink" if shared_kv else "ctx_dense_attn",
    )(sink, q, k, v)


NA_ROWS_PER_STEP = 8


def _na_kernel(q_ref, k_ref, v_ref, kx_ref, vx_ref, bias_ref, o_ref):
    step = pl.program_id(1)
    grid_rows = k_ref.shape[0] // GRID_W
    n_keys = C_WIN_ROWS * GRID_W
    n_pairs = C_HEADS // 2
    for rr in range(NA_ROWS_PER_STEP):
        r = step * NA_ROWS_PER_STEP + rr
        row0 = jnp.clip(r - C_WIN_ROWS // 2, 0, grid_rows - C_WIN_ROWS)
        d_row0 = row0 - r + (C_WIN_ROWS - 1)
        key0 = pl.multiple_of(row0 * GRID_W, GRID_W)
        qrows = slice(rr * GRID_W, (rr + 1) * GRID_W)
        s_l, s_x = [], []
        for p in range(n_pairs):
            ps = slice(p * LANES, (p + 1) * LANES)
            q2 = _split_head_pair(q_ref[qrows, ps])
            s_l.append(_dot_nt(q2, k_ref[pl.ds(key0, n_keys), ps]))
            s_x.append(_dot_nt(q2, kx_ref[:, ps]))
        bias = jnp.concatenate(
            [bias_ref[d_row0 + 2 * j].reshape(C_HEADS * GRID_W, LANES) for j in range(C_WIN_ROWS // 2)], axis=1)
        (p_l, p_x), denom = _softmax_parts([jnp.concatenate(s_l, axis=0) + bias, jnp.concatenate(s_x, axis=0)], None)
        p_l = p_l.astype(BF16)
        p_x = p_x.astype(BF16)
        for p in range(n_pairs):
            ps = slice(p * LANES, (p + 1) * LANES)
            pr = slice(p * 2 * GRID_W, (p + 1) * 2 * GRID_W)
            o = (_dot(p_l[pr], v_ref[pl.ds(key0, n_keys), ps]) + _dot(p_x[pr], vx_ref[:, ps])) / denom[pr]
            o_ref[qrows, ps] = _merge_head_pair(o).astype(o_ref.dtype)


def _neighbourhood_attention(q, k, v, k_ctx, v_ctx, bias_tbl):
    bsz, seq, w = q.shape
    n_ctx = k_ctx.shape[1]
    tq = NA_ROWS_PER_STEP * GRID_W
    whole = pl.BlockSpec((None, seq, w), lambda b, i: (b, 0, 0))
    ctx_spec = pl.BlockSpec((None, n_ctx, w), lambda b, i: (b, 0, 0))
    return pl.pallas_call(
        _na_kernel,
        grid=(bsz, seq // tq),
        in_specs=[pl.BlockSpec((None, tq, w), lambda b, i: (b, i, 0)), whole, whole, ctx_spec, ctx_spec,
                  _const_spec(bias_tbl.shape)],
        out_specs=pl.BlockSpec((None, tq, w), lambda b, i: (b, i, 0)),
        out_shape=jax.ShapeDtypeStruct((bsz, seq, w), BF16),
        compiler_params=_params(("parallel", "arbitrary")),
        name="neighbourhood_attn",
    )(q, k, v, k_ctx, v_ctx, bias_tbl)


def _na_bias_table(rpb):
    qcol = jnp.arange(GRID_W)[:, None]
    kcol = jnp.arange(GRID_W)[None, :]
    wstart = jnp.clip(qcol - C_WIN_COLS // 2, 0, GRID_W - C_WIN_COLS)
    ok = (kcol >= wstart) & (kcol < wstart + C_WIN_COLS)
    n_off = C_WIN_COLS - 1
    period = 2 * GRID_W
    vec = jnp.concatenate([rpb[..., n_off:].astype(F32),
                           jnp.zeros(rpb.shape[:2] + (period - 2 * n_off - 1,), F32),
                           rpb[..., :n_off].astype(F32)], axis=-1)
    toep = jnp.tile(vec, (1, 1, GRID_W))[..., :GRID_W * (period - 1)]
    toep = toep.reshape(rpb.shape[:2] + (GRID_W, period - 1))[..., :GRID_W]
    tbl = jnp.where(ok[None, None], toep * LOG2E, -jnp.inf)
    tbl = jnp.moveaxis(tbl, 1, 0)
    return jnp.concatenate([tbl[:-1], tbl[1:]], axis=-1)


def _merge_kernel(x_ref, mod_ref, gain_ref, ya_ref, of_ref, ob_ref, gr_ref, yc_ref, gn_ref,
                  wm_ref, bm_ref, wa_ref, wb_ref, wc_ref, wo_ref, o_ref):
    d = D_MODEL
    gn = gn_ref[...]
    for r0 in range(0, x_ref.shape[0], SUB_ROWS):
        rows = slice(r0, r0 + SUB_ROWS)
        x = x_ref[rows, :]
        h = _norm_mod(x, gain_ref[...], mod_ref[0:1, :], mod_ref[1:2, :]).astype(BF16)
        o_sum = of_ref[rows, :] + ob_ref[rows, :]
        parts = []
        for hh in range(B_HEADS):
            oh = o_sum[:, hh * B_DV:(hh + 1) * B_DV]
            parts.append((oh * lax.rsqrt(jnp.mean(oh * oh, axis=-1, keepdims=True) + EPS)) * gn)
        y_b = (jnp.concatenate(parts, axis=1) * _silu(gr_ref[rows, :])).astype(BF16)
        mixed = None
        for j, (y, w_ref) in enumerate(((ya_ref[rows, :], wa_ref), (y_b, wb_ref), (yc_ref[rows, :], wc_ref))):
            gate = _sigmoid(_dot(h, wm_ref[:, j * d:(j + 1) * d]) + bm_ref[:, j * d:(j + 1) * d])
            term = gate * _dot(y, w_ref[...])
            mixed = term if mixed is None else mixed + term
        o_ref[rows, :] = x + mod_ref[2:3, :] * _dot(mixed.astype(BF16), wo_ref[...])


def _merge(x, mod_rows, gain, y_a, o_f, o_b, g_r, y_c, gla_gain, w_merge, b_merge, w_a, w_b, w_c, w_out, *, tm):
    bsz, seq, d = x.shape
    per_batch_mod = mod_rows.shape[0] > 1
    mod_map = (lambda b, i: (b, 0, 0)) if per_batch_mod else (lambda b, i: (0, 0, 0))
    tile = lambda w: pl.BlockSpec((None, tm, w), lambda b, i: (b, i, 0))
    wy = y_a.shape[-1]
    return pl.pallas_call(
        _merge_kernel,
        grid=(bsz, seq // tm),
        in_specs=[tile(d), pl.BlockSpec((None, 6, d), mod_map), _const_spec((1, d)),
                  tile(wy), tile(wy), tile(wy), tile(wy), tile(wy), _const_spec((1, B_DV)),
                  _const_spec(w_merge.shape), _const_spec(b_merge.shape),
                  _const_spec(w_a.shape), _const_spec(w_b.shape), _const_spec(w_c.shape),
                  _const_spec(w_out.shape)],
        out_specs=tile(d),
        out_shape=jax.ShapeDtypeStruct((bsz, seq, d), F32),
        compiler_params=_params(("parallel", "arbitrary")),
        name="merge_out",
    )(x, mod_rows, gain, y_a, o_f, o_b, g_r, y_c, gla_gain, w_merge, b_merge, w_a, w_b, w_c, w_out)


FFN_CHUNKS = ((0, 1024), (1024, 2048), (2048, FFN_HIDDEN))


def _ffn_kernel(x_ref, mod_ref, gain_ref, w1_ref, w2_ref, fg_ref, o_ref, *, final):
    for r0 in range(0, x_ref.shape[0], SUB_ROWS):
        rows = slice(r0, r0 + SUB_ROWS)
        x = x_ref[rows, :]
        h = _norm_mod(x, gain_ref[...], mod_ref[3:4, :], mod_ref[4:5, :]).astype(BF16)
        acc = None
        for c0, c1 in FFN_CHUNKS:
            gate = _dot(h, w1_ref[:, c0:c1])
            up = _dot(h, w1_ref[:, FFN_HIDDEN + c0:FFN_HIDDEN + c1])
            part = _dot((_silu(gate) * up).astype(BF16), w2_ref[c0:c1, :])
            acc = part if acc is None else acc + part
        y = x + mod_ref[5:6, :] * acc
        if final:
            y = (y * lax.rsqrt(jnp.mean(y * y, axis=-1, keepdims=True) + EPS)) * fg_ref[...]
        o_ref[rows, :] = y


def _ffn(x, mod_rows, gain, w1, w2, final_gain, *, final, tm):
    bsz, seq, d = x.shape
    per_batch_mod = mod_rows.shape[0] > 1
    mod_map = (lambda b, i: (b, 0, 0)) if per_batch_mod else (lambda b, i: (0, 0, 0))
    tile = pl.BlockSpec((None, tm, d), lambda b, i: (b, i, 0))
    return pl.pallas_call(
        functools.partial(_ffn_kernel, final=final),
        grid=(bsz, seq // tm),
        in_specs=[tile, pl.BlockSpec((None, 6, d), mod_map), _const_spec((1, d)),
                  _const_spec(w1.shape), _const_spec(w2.shape), _const_spec((1, d))],
        out_specs=tile,
        out_shape=jax.ShapeDtypeStruct((bsz, seq, d), F32),
        compiler_params=_params(("parallel", "arbitrary")),
        name="ffn_final" if final else "ffn",
    )(x, mod_rows, gain, w1, w2, final_gain)


def _rope_tables(seq):
    t = jnp.arange(seq)
    row = (t // GRID_W).astype(F32)
    col = (t % GRID_W).astype(F32)
    n_freq = HEAD_DIM // 4
    inv = ROPE_BASE ** (-jnp.arange(n_freq, dtype=F32) / n_freq)
    ang = jnp.concatenate([row[:, None] * inv[None], col[:, None] * inv[None]], axis=-1)
    cos, sin = jnp.cos(ang), jnp.sin(ang)
    return jnp.tile(cos, (1, 4)), jnp.tile(jnp.concatenate([-sin, sin], axis=-1), (1, 2))


def _permute_w_in(w):
    ga0 = 2304
    ga1 = ga0 + 2 * B_GATE_RANK
    n_aq = A_HEADS * HEAD_DIM
    aq = w[:, :n_aq].reshape(w.shape[0], A_HEADS, HEAD_DIM)[:, jnp.array(A_HEAD_ORDER)].reshape(w.shape[0], n_aq)
    pad = jnp.zeros((w.shape[0], IN_PADDED - w.shape[1]), w.dtype)
    return jnp.concatenate([aq, w[:, n_aq:ga0], w[:, ga1:], w[:, ga0:ga1], pad], axis=1).astype(BF16)


def _permute_w_branch_a(w):
    return w.reshape(A_HEADS, HEAD_DIM, w.shape[1])[jnp.array(A_HEAD_ORDER)].reshape(w.shape).astype(BF16)


def _gate_weights(w_fwd, b_fwd, w_bwd, b_bwd):
    n = B_HEADS * B_DK
    r = B_GATE_RANK
    w = jnp.zeros((LANES, 2 * n), F32)
    w = w.at[:r, :n].set(w_fwd).at[r:2 * r, n:].set(w_bwd)
    return w.astype(BF16), jnp.concatenate([b_fwd, b_bwd])[None, :]


def kernel(x, c, ctx, c_ctx, w_ada, b_ada, norm_mix, w_in, attn_sink, gla_gate_w_fwd, gla_gate_b_fwd,
           gla_gate_w_bwd, gla_gate_b_bwd, gla_norm, na_rpb, w_branch_a, w_branch_b, w_branch_c,
           w_merge, b_merge, w_out, norm_ffn, w_ffn_in, w_ffn_out, final_norm):
    bsz, seq, d = x.shape
    n_ctx = ctx.shape[1]
    cos, sin = _rope_tables(seq)

    cvec = jnp.zeros((8, d), F32).at[:bsz].set(c).at[bsz].set(c_ctx)
    mod = _modulation(cvec, w_ada, b_ada).reshape(DEPTH, 8, 6, d)

    xc = ctx
    zeros_state = jnp.zeros((bsz,) + GLA_STATE_SHAPE, F32)
    for l in range(DEPTH):
        last = l == DEPTH - 1
        mod_x = mod[l, :bsz]
        mod_c = mod[l, bsz:bsz + 1]
        gain_m = norm_mix[l][None, :]
        w_perm = _permute_w_in(w_in[l])
        w_gate, b_gate = _gate_weights(gla_gate_w_fwd[l], gla_gate_b_fwd[l], gla_gate_w_bwd[l], gla_gate_b_bwd[l])

        (caq, cak, cav, cgq, cgk, cgv, cgr, cnq, cnk, cnv, cgf, cgb) = _project(
            xc, mod_c, gain_m, w_perm, w_gate, b_gate, cos, sin, rope=False, tm=n_ctx)
        (aq, ak, av, gq, gk, gv, gr, nq, nk, nv, gf, gb) = _project(
            x, mod_x, gain_m, w_perm, w_gate, b_gate, cos, sin, rope=True, tm=512)

        co_f, co_b, s_f, s_b = _gla(cgq, cgk, cgv, cgf, cgb, zeros_state, zeros_state, tile=n_ctx)
        o_f, o_b, _, _ = _gla(gq, gk, gv, gf, gb, s_f, s_b, tile=256)

        sink = attn_sink[l] * LOG2E
        y_a = _window_attention(sink, aq, ak, av, cak, cav)
        y_c = _neighbourhood_attention(nq, nk, nv, cnk, cnv, _na_bias_table(na_rpb[l]))

        merge_w = (gla_norm[l][None, :], w_merge[l].astype(BF16), b_merge[l][None, :],
                   _permute_w_branch_a(w_branch_a[l]), w_branch_b[l].astype(BF16), w_branch_c[l].astype(BF16),
                   w_out[l].astype(BF16))
        x = _merge(x, mod_x, gain_m, y_a, o_f, o_b, gr, y_c, *merge_w, tm=512)

        gain_f = norm_ffn[l][None, :]
        w1 = w_ffn_in[l].astype(BF16)
        w2 = w_ffn_out[l].astype(BF16)
        fg = final_norm[None, :]
        if not last:
            yc_a = _dense_attention(sink, caq, cak, cav, head_order=A_HEAD_ORDER, shared_kv=True)
            yc_c = _dense_attention(sink, cnq, cnk, cnv, head_order=None, shared_kv=False)
            xc = _merge(xc, mod_c, gain_m, yc_a, co_f, co_b, cgr, yc_c, *merge_w, tm=n_ctx)
            xc = _ffn(xc, mod_c, gain_f, w1, w2, fg, final=False, tm=n_ctx)
        x = _ffn(x, mod_x, gain_f, w1, w2, fg, final=last, tm=512)
    return x
```

```python
import functools

import jax
import jax.numpy as jnp
from jax import lax
from jax.experimental import pallas as pl
from jax.experimental.pallas import tpu as pltpu

F32 = jnp.float32
BF16 = jnp.bfloat16

D_MODEL = 1024
DEPTH = 2
GRID_W = 64
HEAD_DIM = 64
EPS = 1e-6
ROPE_BASE = 10000.0
A_HEADS = 8
A_KV_HEADS = 2
A_BLOCK = 128
B_HEADS = 4
B_DK = 64
B_DV = 128
B_GATE_RANK = 16
B_GATE_NORM = 16.0
C_HEADS = 8
C_WIN_ROWS = 8
C_WIN_COLS = 16
FFN_HIDDEN = 2816

VMEM_LIMIT_BYTES = 56 * 1024 * 1024
LANES = 128

SEG_AQ = (0, 512)
SEG_AKV = (512, 768)
SEG_GQ = (768, 1024)
SEG_GK = (1024, 1280)
SEG_GV = (1280, 1792)
SEG_GR = (1792, 2304)
SEG_NQ = (2304, 2816)
SEG_NK = (2816, 3328)
SEG_NV = (3328, 3840)
SEG_GA = (3840, 3968)
IN_PADDED = 3968

LOG2E = 1.4426950408889634
ATTN_Q_SCALE = HEAD_DIM ** -0.5 * LOG2E

GLA_CHUNK = 64
GLA_FAST_RANGE = 120.0


def _params(sem):
    return pltpu.CompilerParams(dimension_semantics=sem, vmem_limit_bytes=VMEM_LIMIT_BYTES)


def _const_spec(shape):
    nd = len(shape)
    return pl.BlockSpec(shape, lambda *_: (0,) * nd, pipeline_mode=pl.Buffered(1))


def _sigmoid(x):
    return 1.0 / (1.0 + jnp.exp(-x))


def _silu(x):
    return x * _sigmoid(x)


def _dot(a, b):
    return jnp.dot(a, b, preferred_element_type=F32)


def _dot_nt(a, b):
    return lax.dot_general(a, b, (((1,), (1,)), ((), ())), preferred_element_type=F32)


def _dot_tn(a, b):
    return lax.dot_general(a, b, (((0,), (0,)), ((), ())), preferred_element_type=F32)


def _norm_mod(x, gain, shift, scale):
    y = x * lax.rsqrt(jnp.mean(x * x, axis=-1, keepdims=True) + EPS)
    return (y * gain) * (1.0 + scale) + shift


def _mod_kernel(c_ref, w_ref, b_ref, o_ref):
    s = _silu(c_ref[...])
    o_ref[...] = _dot(s.astype(BF16), w_ref[...].astype(BF16)) + b_ref[...]


def _modulation(cvec, w_ada, b_ada):
    tn = 1536
    n_out = w_ada.shape[-1]
    return pl.pallas_call(
        _mod_kernel,
        grid=(DEPTH, n_out // tn),
        in_specs=[
            pl.BlockSpec((8, D_MODEL), lambda l, j: (0, 0)),
            pl.BlockSpec((None, D_MODEL, tn), lambda l, j: (l, 0, j)),
            pl.BlockSpec((None, 1, tn), lambda l, j: (l, 0, j)),
        ],
        out_specs=pl.BlockSpec((None, 8, tn), lambda l, j: (l, 0, j)),
        out_shape=jax.ShapeDtypeStruct((DEPTH, 8, n_out), F32),
        compiler_params=_params(("arbitrary", "arbitrary")),
        name="adaln_mod",
    )(cvec, w_ada, b_ada.reshape(DEPTH, 1, n_out))


def _rope(t, cos, sin):
    n = t.shape[1]
    lane = lax.broadcasted_iota(jnp.int32, t.shape, 1)
    first_half = (lane % HEAD_DIM) < (HEAD_DIM // 2)
    rot = jnp.where(first_half, pltpu.roll(t, n - HEAD_DIM // 2, 1), pltpu.roll(t, HEAD_DIM // 2, 1))
    reps = n // LANES
    return t * jnp.tile(cos, (1, reps)) + rot * jnp.tile(sin, (1, reps))


def _proj_kernel(x_ref, mod_ref, gain_ref, w_ref, wg_ref, bg_ref, cos_ref, sin_ref,
                 aq_ref, ak_ref, av_ref, gq_ref, gk_ref, gv_ref, gr_ref, nq_ref, nk_ref, nv_ref,
                 gf_ref, gb_ref, *, rope):
    scale = ATTN_Q_SCALE
    n_g = B_HEADS * B_DK
    h = _norm_mod(x_ref[...], gain_ref[...], mod_ref[0:1, :], mod_ref[1:2, :]).astype(BF16)

    def seg(s):
        return _dot(h, w_ref[:, s[0]:s[1]])

    aq = seg(SEG_AQ)
    akv = seg(SEG_AKV)
    z = _dot(seg(SEG_GA).astype(BF16), wg_ref[...]) + bg_ref[...]
    ak = akv[:, :LANES]
    if rope:
        cos = cos_ref[...]
        sin = sin_ref[...]
        aq = _rope(aq, cos, sin)
        ak = _rope(ak, cos, sin)
    aq_ref[...] = (aq * scale).astype(BF16)
    ak_ref[...] = ak.astype(BF16)
    av_ref[...] = akv[:, LANES:].astype(BF16)
    log_sig = jnp.minimum(z, 0.0) - jnp.log1p(jnp.exp(-jnp.abs(z)))
    g = log_sig / B_GATE_NORM
    gf_ref[...] = g[:, :n_g]
    gb_ref[...] = g[:, n_g:]
    gq_ref[...] = seg(SEG_GQ) * (B_DK ** -0.5)
    gk_ref[...] = seg(SEG_GK)
    gv_ref[...] = seg(SEG_GV).astype(BF16)
    gr_ref[...] = seg(SEG_GR)
    nq_ref[...] = (seg(SEG_NQ) * scale).astype(BF16)
    nk_ref[...] = seg(SEG_NK).astype(BF16)
    nv_ref[...] = seg(SEG_NV).astype(BF16)


def _project(x, mod_rows, gain, w_perm, w_gate, b_gate, cos, sin, *, rope, tm):
    bsz, seq, _ = x.shape
    per_batch_mod = mod_rows.shape[0] > 1
    mod_map = (lambda b, i: (b, 0, 0)) if per_batch_mod else (lambda b, i: (0, 0, 0))
    widths = [(512, BF16), (128, BF16), (128, BF16), (256, F32), (256, F32), (512, BF16), (512, F32),
              (512, BF16), (512, BF16), (512, BF16), (256, F32), (256, F32)]
    tile = lambda w: pl.BlockSpec((None, tm, w), lambda b, i: (b, i, 0))
    return pl.pallas_call(
        functools.partial(_proj_kernel, rope=rope),
        grid=(bsz, seq // tm),
        in_specs=[
            tile(D_MODEL),
            pl.BlockSpec((None, 6, D_MODEL), mod_map),
            _const_spec((1, D_MODEL)),
            _const_spec((D_MODEL, IN_PADDED)),
            _const_spec((LANES, 2 * B_HEADS * B_DK)),
            _const_spec((1, 2 * B_HEADS * B_DK)),
            pl.BlockSpec((tm, LANES), lambda b, i: (i, 0)),
            pl.BlockSpec((tm, LANES), lambda b, i: (i, 0)),
        ],
        out_specs=[tile(w) for w, _ in widths],
        out_shape=[jax.ShapeDtypeStruct((bsz, seq, w), dt) for w, dt in widths],
        compiler_params=_params(("parallel", "arbitrary")),
        name="in_proj_rope" if rope else "in_proj_ctx",
    )(x, mod_rows, gain, w_perm, w_gate, b_gate, cos, sin)


class _GlaPrep:
    def __init__(self, q_ref, k_ref, g_ref, reverse):
        t = q_ref.shape[0]
        c = GLA_CHUNK
        row = lax.broadcasted_iota(jnp.int32, (t, t), 0)
        col = lax.broadcasted_iota(jnp.int32, (t, t), 1)
        same_chunk = (row // c) == (col // c)
        tri = (same_chunk & ((col >= row) if reverse else (col <= row))).astype(BF16)
        self.reverse = reverse
        self.q = q_ref[...]
        self.k = k_ref[...]
        self.g_parts = _split3(g_ref[...])
        self.b = _dot_01(tri, self.g_parts)
        b = self.b
        self.totals = [b[ci * c:ci * c + 1, :] if reverse else b[ci * c + c - 1:ci * c + c, :]
                       for ci in range(t // c)]
        self.b_end = jnp.concatenate([jnp.broadcast_to(e, (c, e.shape[1])) for e in self.totals], axis=0)
        self.q_state = (self.q * jnp.exp(b)).astype(BF16)
        self.k_tail = (self.k * jnp.exp(self.b_end - b)).astype(BF16)
        self.decay = [jnp.exp(e) for e in self.totals]


def _split3(x):
    hi = x.astype(BF16)
    rest = x - hi.astype(F32)
    mid = rest.astype(BF16)
    return hi, mid, (rest - mid.astype(F32)).astype(BF16)


def _dot_01(m01, parts):
    return _dot(m01, parts[0]) + _dot(m01, parts[1]) + _dot(m01, parts[2])


def _gla_pair_scores(q_w, k_w, keep):
    c = GLA_CHUNK
    out = []
    for ci in range(q_w.shape[0] // c):
        rows = slice(ci * c, (ci + 1) * c)
        for p in range(B_HEADS // 2):
            ps = slice(p * LANES, (p + 1) * LANES)
            out.append(jnp.where(keep, _dot_nt(_split_head_pair(q_w[rows, ps]), k_w[rows, ps]), 0.0))
    return out


def _gla_scores_fast(prep):
    c = GLA_CHUNK
    qi = lax.broadcasted_iota(jnp.int32, (2 * c, c), 0) % c
    kj = lax.broadcasted_iota(jnp.int32, (2 * c, c), 1)
    keep = (kj >= qi) if prep.reverse else (kj <= qi)
    ref = 0.5 * prep.b_end
    q_in = (prep.q * jnp.exp(prep.b - ref)).astype(BF16)
    k_in = (prep.k * jnp.exp(ref - prep.b)).astype(BF16)
    return _gla_pair_scores(q_in, k_in, keep)


def _gla_scores_safe(prep):
    t = prep.q.shape[0]
    c = GLA_CHUNK
    rev = prep.reverse
    row = lax.broadcasted_iota(jnp.int32, (t, t), 0)
    col = lax.broadcasted_iota(jnp.int32, (t, t), 1)
    qi = lax.broadcasted_iota(jnp.int32, (2 * c, c), 0) % c
    kj = lax.broadcasted_iota(jnp.int32, (2 * c, c), 1)
    scores = _gla_pair_scores(prep.q.astype(BF16), prep.k.astype(BF16), qi == kj)
    s = c // 2
    while s >= 1:
        if rev:
            bnd = row - row % (2 * s) + s
            span = ((col >= bnd) & (col < row)) | ((col >= row) & (col < bnd))
        else:
            bnd = row - row % (2 * s) + s - 1
            span = ((col > bnd) & (col <= row)) | ((col > row) & (col <= bnd))
        e = jnp.exp(_dot_01(span.astype(BF16), prep.g_parts))
        q_side = ((qi % (2 * s)) < s) if rev else ((qi % (2 * s)) >= s)
        k_side = ((kj % (2 * s)) >= s) if rev else ((kj % (2 * s)) < s)
        keep = ((qi // (2 * s)) == (kj // (2 * s))) & q_side & k_side
        level = _gla_pair_scores((prep.q * e).astype(BF16), (prep.k * e).astype(BF16), keep)
        scores = [acc + a for acc, a in zip(scores, level)]
        s //= 2
    return scores


def _gla_local(scores, k_tail, v_ref):
    c = GLA_CHUNK
    n_pairs = B_HEADS // 2
    intra, upd = [], []
    for ci in range(k_tail.shape[0] // c):
        rows = slice(ci * c, (ci + 1) * c)
        intra_c, upd_c = [], []
        for p in range(n_pairs):
            a = scores[ci * n_pairs + p].astype(BF16)
            kt = _split_head_pair(k_tail[rows, p * LANES:(p + 1) * LANES])
            u = None
            for half in range(2):
                h = 2 * p + half
                vh = v_ref[rows, h * B_DV:(h + 1) * B_DV]
                intra_c.append(_dot(a[half * c:(half + 1) * c], vh))
                uh = _dot_tn(vh, kt[half * c:(half + 1) * c])
                u = uh if u is None else u + uh
            upd_c.append(u)
        intra.append(intra_c)
        upd.append(upd_c)
    return intra, upd


def _gla_states(s_ref, decay, upd, reverse):
    n = len(upd)
    order = range(n - 1, -1, -1) if reverse else range(n)
    starts = [None] * n
    s = [s_ref[p] for p in range(B_HEADS // 2)]
    for ci in order:
        starts[ci] = [sp.astype(BF16) for sp in s]
        s = [s[p] * decay[ci][:, p * LANES:(p + 1) * LANES] + upd[ci][p] for p in range(B_HEADS // 2)]
    for p in range(B_HEADS // 2):
        s_ref[p] = s[p]
    return starts


def _gla_finish(o_ref, q_state, starts, intra):
    c = GLA_CHUNK
    for ci in range(len(intra)):
        rows = slice(ci * c, (ci + 1) * c)
        for p in range(B_HEADS // 2):
            ps = slice(p * LANES, (p + 1) * LANES)
            inter = _dot_nt(_split_head_pair(q_state[rows, ps]), starts[ci][p])
            for half in range(2):
                h = 2 * p + half
                o_ref[rows, h * B_DV:(h + 1) * B_DV] = intra[ci][h] + inter[half * c:(half + 1) * c]


def _gla_kernel(qf_ref, kf_ref, vf_ref, gf_ref, qb_ref, kb_ref, vb_ref, gb_ref, s0f_ref, s0b_ref,
                of_ref, ob_ref, sf_out_ref, sb_out_ref, sf_ref, sb_ref):
    i = pl.program_id(1)

    @pl.when(i == 0)
    def _():
        sf_ref[...] = s0f_ref[...]
        sb_ref[...] = s0b_ref[...]

    fwd = _GlaPrep(qf_ref, kf_ref, gf_ref, False)
    bwd = _GlaPrep(qb_ref, kb_ref, gb_ref, True)

    def tile_body(score_fn):
        scores_f = score_fn(fwd)
        scores_b = score_fn(bwd)
        intra_f, upd_f = _gla_local(scores_f, fwd.k_tail, vf_ref)
        intra_b, upd_b = _gla_local(scores_b, bwd.k_tail, vb_ref)
        starts_f = _gla_states(sf_ref, fwd.decay, upd_f, False)
        starts_b = _gla_states(sb_ref, bwd.decay, upd_b, True)
        _gla_finish(of_ref, fwd.q_state, starts_f, intra_f)
        _gla_finish(ob_ref, bwd.q_state, starts_b, intra_b)

    in_fast_range = -jnp.min(jnp.concatenate(fwd.totals + bwd.totals, axis=0)) < GLA_FAST_RANGE

    @pl.when(in_fast_range)
    def _():
        tile_body(_gla_scores_fast)

    @pl.when(jnp.logical_not(in_fast_range))
    def _():
        tile_body(_gla_scores_safe)

    @pl.when(i == pl.num_programs(1) - 1)
    def _():
        sf_out_ref[...] = sf_ref[...]
        sb_out_ref[...] = sb_ref[...]


GLA_STATE_SHAPE = (B_HEADS // 2, B_DV, 2 * B_DK)


def _gla(q, k, v, gf, gb, s0f, s0b, *, tile):
    bsz, seq, _ = q.shape
    n = seq // tile
    fwd = lambda w: pl.BlockSpec((None, tile, w), lambda b, i: (b, i, 0))
    bwd = lambda w: pl.BlockSpec((None, tile, w), lambda b, i: (b, n - 1 - i, 0))
    st = pl.BlockSpec((None,) + GLA_STATE_SHAPE, lambda b, i: (b, 0, 0, 0))
    wk, wv = B_HEADS * B_DK, B_HEADS * B_DV
    st_shape = jax.ShapeDtypeStruct((bsz,) + GLA_STATE_SHAPE, F32)
    return pl.pallas_call(
        _gla_kernel,
        grid=(bsz, n),
        in_specs=[fwd(wk), fwd(wk), fwd(wv), fwd(wk), bwd(wk), bwd(wk), bwd(wv), bwd(wk), st, st],
        out_specs=[fwd(wv), bwd(wv), st, st],
        out_shape=[jax.ShapeDtypeStruct((bsz, seq, wv), F32), jax.ShapeDtypeStruct((bsz, seq, wv), F32),
                   st_shape, st_shape],
        scratch_shapes=[pltpu.VMEM(GLA_STATE_SHAPE, F32), pltpu.VMEM(GLA_STATE_SHAPE, F32)],
        compiler_params=_params(("parallel", "arbitrary")),
        name="gla_scan",
    )(q, k, v, gf, q, k, v, gb, s0f, s0b)


def _softmax_parts(scores, sink_tile):
    def lane_tiles(blocks):
        return [b[:, j:j + LANES] for b in blocks for j in range(0, b.shape[1], LANES)]

    tiles = lane_tiles(scores)
    if sink_tile is not None:
        tiles.append(sink_tile)
    m = functools.reduce(jnp.maximum, tiles).max(axis=-1, keepdims=True)
    ps = [jnp.exp2(s - m) for s in scores]
    acc = functools.reduce(jnp.add, lane_tiles(ps))
    if sink_tile is not None:
        lane = lax.broadcasted_iota(jnp.int32, sink_tile.shape, 1)
        acc = acc + jnp.where(lane == 0, jnp.exp2(sink_tile - m), 0.0)
    return ps, acc.sum(axis=-1, keepdims=True)


def _split_head_pair(t):
    lane = lax.broadcasted_iota(jnp.int32, t.shape, 1)
    zero = jnp.zeros_like(t)
    return jnp.concatenate([jnp.where(lane < HEAD_DIM, t, zero), jnp.where(lane >= HEAD_DIM, t, zero)], axis=0)


def _merge_head_pair(o):
    m = o.shape[0] // 2
    lane = lax.broadcasted_iota(jnp.int32, (m, LANES), 1)
    return jnp.where(lane < HEAD_DIM, o[:m], o[m:])


A_HEAD_ORDER = (0, 4, 1, 5, 2, 6, 3, 7)


def _win_kernel(sink_ref, q_ref, kp_ref, kc_ref, kn_ref, vp_ref, vc_ref, vn_ref, kx_ref, vx_ref, o_ref):
    n = pl.program_id(1)
    nb = pl.num_programs(1)
    n_tiles = A_HEADS // 2
    q = jnp.concatenate([_split_head_pair(q_ref[:, j * LANES:(j + 1) * LANES]) for j in range(n_tiles)], axis=0)
    sink_tile = jnp.concatenate([jnp.full((A_BLOCK, LANES), sink_ref[h], F32) for h in A_HEAD_ORDER], axis=0)
    qi = lax.broadcasted_iota(jnp.int32, (A_BLOCK, A_BLOCK), 0)
    kj = lax.broadcasted_iota(jnp.int32, (A_BLOCK, A_BLOCK), 1)
    neg = jnp.full((A_BLOCK, A_BLOCK), -jnp.inf, F32)
    zero = jnp.zeros((A_BLOCK, A_BLOCK), F32)
    prev_mask = jnp.where(n > 0, jnp.where(kj >= qi, zero, neg), neg)
    next_mask = jnp.where(n < nb - 1, jnp.where(kj <= qi, zero, neg), neg)
    s_p = _dot_nt(q, kp_ref[...]) + jnp.tile(prev_mask, (A_HEADS, 1))
    s_c = _dot_nt(q, kc_ref[...])
    s_n = _dot_nt(q, kn_ref[...]) + jnp.tile(next_mask, (A_HEADS, 1))
    s_x = _dot_nt(q, kx_ref[...])
    (p_p, p_c, p_n, p_x), denom = _softmax_parts([s_p, s_c, s_n, s_x], sink_tile)
    o = (_dot(p_p.astype(BF16), vp_ref[...]) + _dot(p_c.astype(BF16), vc_ref[...])
         + _dot(p_n.astype(BF16), vn_ref[...]) + _dot(p_x.astype(BF16), vx_ref[...]))
    o = o / denom
    for j in range(n_tiles):
        o_ref[:, j * LANES:(j + 1) * LANES] = _merge_head_pair(
            o[j * 2 * A_BLOCK:(j + 1) * 2 * A_BLOCK]).astype(o_ref.dtype)


def _window_attention(sink, q, k, v, k_ctx, v_ctx):
    bsz, seq, wq = q.shape
    nb = seq // A_BLOCK
    wkv = k.shape[-1]
    n_ctx = k_ctx.shape[1]
    kv_spec = lambda f: pl.BlockSpec((None, A_BLOCK, wkv), lambda b, n: (b, f(n), 0))
    prev = lambda n: jnp.maximum(n - 1, 0)
    cur = lambda n: n
    nxt = lambda n: jnp.minimum(n + 1, nb - 1)
    ctx_spec = pl.BlockSpec((None, n_ctx, wkv), lambda b, n: (b, 0, 0))
    return pl.pallas_call(
        _win_kernel,
        grid=(bsz, nb),
        in_specs=[
            pl.BlockSpec(memory_space=pltpu.SMEM),
            pl.BlockSpec((None, A_BLOCK, wq), lambda b, n: (b, n, 0)),
            kv_spec(prev), kv_spec(cur), kv_spec(nxt),
            kv_spec(prev), kv_spec(cur), kv_spec(nxt),
            ctx_spec, ctx_spec,
        ],
        out_specs=pl.BlockSpec((None, A_BLOCK, wq), lambda b, n: (b, n, 0)),
        out_shape=jax.ShapeDtypeStruct((bsz, seq, wq), BF16),
        compiler_params=_params(("parallel", "arbitrary")),
        name="window_attn",
    )(sink, q, k, k, k, v, v, v, k_ctx, v_ctx)


def _dense_kernel(sink_ref, q_ref, k_ref, v_ref, o_ref, *, head_order, shared_kv):
    n_q = q_ref.shape[0]
    for j in range(q_ref.shape[1] // LANES):
        qs = slice(j * LANES, (j + 1) * LANES)
        ks = slice(0, LANES) if shared_kv else qs
        s = _dot_nt(_split_head_pair(q_ref[:, qs]), k_ref[:, ks])
        sink_tile = None
        if head_order is not None:
            sink_tile = jnp.concatenate(
                [jnp.full((n_q, LANES), sink_ref[head_order[2 * j + half]], F32) for half in range(2)], axis=0)
        (p,), denom = _softmax_parts([s], sink_tile)
        o_ref[:, qs] = _merge_head_pair(_dot(p.astype(BF16), v_ref[:, ks]) / denom).astype(o_ref.dtype)


def _dense_attention(sink, q, k, v, *, head_order, shared_kv):
    bsz, n_q, wq = q.shape
    wkv = k.shape[-1]
    full = lambda w: pl.BlockSpec((None, n_q, w), lambda b: (b, 0, 0))
    return pl.pallas_call(
        functools.partial(_dense_kernel, head_order=head_order, shared_kv=shared_kv),
        grid=(bsz,),
        in_specs=[pl.BlockSpec(memory_space=pltpu.SMEM), full(wq), full(wkv), full(wkv)],
        out_specs=full(wq),
        out_shape=jax.ShapeDtypeStruct((bsz, n_q, wq), BF16),
        compiler_params=_params(("parallel",)),
        name="ctx_dense_attn_sink" if shared_kv else "ctx_dense_attn",
    )(sink, q, k, v)


NA_ROWS_PER_STEP = 8


def _na_kernel(q_ref, k_ref, v_ref, kx_ref, vx_ref, bias_ref, o_ref):
    step = pl.program_id(1)
    grid_rows = k_ref.shape[0] // GRID_W
    n_keys = C_WIN_ROWS * GRID_W
    n_pairs = C_HEADS // 2
    for rr in range(NA_ROWS_PER_STEP):
        r = step * NA_ROWS_PER_STEP + rr
        row0 = jnp.clip(r - C_WIN_ROWS // 2, 0, grid_rows - C_WIN_ROWS)
        d_row0 = row0 - r + (C_WIN_ROWS - 1)
        key0 = pl.multiple_of(row0 * GRID_W, GRID_W)
        qrows = slice(rr * GRID_W, (rr + 1) * GRID_W)
        s_l, s_x = [], []
        for p in range(n_pairs):
            ps = slice(p * LANES, (p + 1) * LANES)
            q2 = _split_head_pair(q_ref[qrows, ps])
            s_l.append(_dot_nt(q2, k_ref[pl.ds(key0, n_keys), ps]))
            s_x.append(_dot_nt(q2, kx_ref[:, ps]))
        bias = jnp.concatenate(
            [bias_ref[d_row0 + 2 * j].reshape(C_HEADS * GRID_W, LANES) for j in range(C_WIN_ROWS // 2)], axis=1)
        (p_l, p_x), denom = _softmax_parts([jnp.concatenate(s_l, axis=0) + bias, jnp.concatenate(s_x, axis=0)], None)
        p_l = p_l.astype(BF16)
        p_x = p_x.astype(BF16)
        for p in range(n_pairs):
            ps = slice(p * LANES, (p + 1) * LANES)
            pr = slice(p * 2 * GRID_W, (p + 1) * 2 * GRID_W)
            o = (_dot(p_l[pr], v_ref[pl.ds(key0, n_keys), ps]) + _dot(p_x[pr], vx_ref[:, ps])) / denom[pr]
            o_ref[qrows, ps] = _merge_head_pair(o).astype(o_ref.dtype)


def _neighbourhood_attention(q, k, v, k_ctx, v_ctx, bias_tbl):
    bsz, seq, w = q.shape
    n_ctx = k_ctx.shape[1]
    tq = NA_ROWS_PER_STEP * GRID_W
    whole = pl.BlockSpec((None, seq, w), lambda b, i: (b, 0, 0))
    ctx_spec = pl.BlockSpec((None, n_ctx, w), lambda b, i: (b, 0, 0))
    return pl.pallas_call(
        _na_kernel,
        grid=(bsz, seq // tq),
        in_specs=[pl.BlockSpec((None, tq, w), lambda b, i: (b, i, 0)), whole, whole, ctx_spec, ctx_spec,
                  _const_spec(bias_tbl.shape)],
        out_specs=pl.BlockSpec((None, tq, w), lambda b, i: (b, i, 0)),
        out_shape=jax.ShapeDtypeStruct((bsz, seq, w), BF16),
        compiler_params=_params(("parallel", "arbitrary")),
        name="neighbourhood_attn",
    )(q, k, v, k_ctx, v_ctx, bias_tbl)


def _na_bias_table(rpb):
    qcol = jnp.arange(GRID_W)[:, None]
    kcol = jnp.arange(GRID_W)[None, :]
    wstart = jnp.clip(qcol - C_WIN_COLS // 2, 0, GRID_W - C_WIN_COLS)
    ok = (kcol >= wstart) & (kcol < wstart + C_WIN_COLS)
    n_off = C_WIN_COLS - 1
    period = 2 * GRID_W
    vec = jnp.concatenate([rpb[..., n_off:].astype(F32),
                           jnp.zeros(rpb.shape[:2] + (period - 2 * n_off - 1,), F32),
                           rpb[..., :n_off].astype(F32)], axis=-1)
    toep = jnp.tile(vec, (1, 1, GRID_W))[..., :GRID_W * (period - 1)]
    toep = toep.reshape(rpb.shape[:2] + (GRID_W, period - 1))[..., :GRID_W]
    tbl = jnp.where(ok[None, None], toep * LOG2E, -jnp.inf)
    tbl = jnp.moveaxis(tbl, 1, 0)
    return jnp.concatenate([tbl[:-1], tbl[1:]], axis=-1)


def _merge_kernel(x_ref, mod_ref, gain_ref, ya_ref, of_ref, ob_ref, gr_ref, yc_ref, gn_ref,
                  wm_ref, bm_ref, wa_ref, wb_ref, wc_ref, wo_ref, o_ref):
    d = D_MODEL
    gn = gn_ref[...]
    x = x_ref[...]
    h = _norm_mod(x, gain_ref[...], mod_ref[0:1, :], mod_ref[1:2, :]).astype(BF16)
    o_sum = of_ref[...] + ob_ref[...]
    parts = []
    for hh in range(B_HEADS):
        oh = o_sum[:, hh * B_DV:(hh + 1) * B_DV]
        parts.append((oh * lax.rsqrt(jnp.mean(oh * oh, axis=-1, keepdims=True) + EPS)) * gn)
    y_b = (jnp.concatenate(parts, axis=1) * _silu(gr_ref[...])).astype(BF16)
    mixed = None
    for j, (y, w_ref) in enumerate(((ya_ref[...], wa_ref), (y_b, wb_ref), (yc_ref[...], wc_ref))):
        gate = _sigmoid(_dot(h, wm_ref[:, j * d:(j + 1) * d]) + bm_ref[:, j * d:(j + 1) * d])
        term = gate * _dot(y, w_ref[...])
        mixed = term if mixed is None else mixed + term
    o_ref[...] = x + mod_ref[2:3, :] * _dot(mixed.astype(BF16), wo_ref[...])


def _merge(x, mod_rows, gain, y_a, o_f, o_b, g_r, y_c, gla_gain, w_merge, b_merge, w_a, w_b, w_c, w_out, *, tm):
    bsz, seq, d = x.shape
    per_batch_mod = mod_rows.shape[0] > 1
    mod_map = (lambda b, i: (b, 0, 0)) if per_batch_mod else (lambda b, i: (0, 0, 0))
    tile = lambda w: pl.BlockSpec((None, tm, w), lambda b, i: (b, i, 0))
    wy = y_a.shape[-1]
    return pl.pallas_call(
        _merge_kernel,
        grid=(bsz, seq // tm),
        in_specs=[tile(d), pl.BlockSpec((None, 6, d), mod_map), _const_spec((1, d)),
                  tile(wy), tile(wy), tile(wy), tile(wy), tile(wy), _const_spec((1, B_DV)),
                  _const_spec(w_merge.shape), _const_spec(b_merge.shape),
                  _const_spec(w_a.shape), _const_spec(w_b.shape), _const_spec(w_c.shape),
                  _const_spec(w_out.shape)],
        out_specs=tile(d),
        out_shape=jax.ShapeDtypeStruct((bsz, seq, d), F32),
        compiler_params=_params(("parallel", "arbitrary")),
        name="merge_out",
    )(x, mod_rows, gain, y_a, o_f, o_b, g_r, y_c, gla_gain, w_merge, b_merge, w_a, w_b, w_c, w_out)


FFN_CHUNKS = ((0, 1024), (1024, 2048), (2048, FFN_HIDDEN))
FFN_SUB_ROWS = 256


def _ffn_kernel(x_ref, mod_ref, gain_ref, w1_ref, w2_ref, fg_ref, o_ref, *, final):
    for r0 in range(0, x_ref.shape[0], FFN_SUB_ROWS):
        rows = slice(r0, r0 + FFN_SUB_ROWS)
        x = x_ref[rows, :]
        h = _norm_mod(x, gain_ref[...], mod_ref[3:4, :], mod_ref[4:5, :]).astype(BF16)
        acc = None
        for c0, c1 in FFN_CHUNKS:
            gate = _dot(h, w1_ref[:, c0:c1])
            up = _dot(h, w1_ref[:, FFN_HIDDEN + c0:FFN_HIDDEN + c1])
            part = _dot((_silu(gate) * up).astype(BF16), w2_ref[c0:c1, :])
            acc = part if acc is None else acc + part
        y = x + mod_ref[5:6, :] * acc
        if final:
            y = (y * lax.rsqrt(jnp.mean(y * y, axis=-1, keepdims=True) + EPS)) * fg_ref[...]
        o_ref[rows, :] = y


def _ffn(x, mod_rows, gain, w1, w2, final_gain, *, final, tm):
    bsz, seq, d = x.shape
    per_batch_mod = mod_rows.shape[0] > 1
    mod_map = (lambda b, i: (b, 0, 0)) if per_batch_mod else (lambda b, i: (0, 0, 0))
    tile = pl.BlockSpec((None, tm, d), lambda b, i: (b, i, 0))
    return pl.pallas_call(
        functools.partial(_ffn_kernel, final=final),
        grid=(bsz, seq // tm),
        in_specs=[tile, pl.BlockSpec((None, 6, d), mod_map), _const_spec((1, d)),
                  _const_spec(w1.shape), _const_spec(w2.shape), _const_spec((1, d))],
        out_specs=tile,
        out_shape=jax.ShapeDtypeStruct((bsz, seq, d), F32),
        compiler_params=_params(("parallel", "arbitrary")),
        name="ffn_final" if final else "ffn",
    )(x, mod_rows, gain, w1, w2, final_gain)


def _rope_tables(seq):
    t = jnp.arange(seq)
    row = (t // GRID_W).astype(F32)
    col = (t % GRID_W).astype(F32)
    n_freq = HEAD_DIM // 4
    inv = ROPE_BASE ** (-jnp.arange(n_freq, dtype=F32) / n_freq)
    ang = jnp.concatenate([row[:, None] * inv[None], col[:, None] * inv[None]], axis=-1)
    cos, sin = jnp.cos(ang), jnp.sin(ang)
    return jnp.tile(cos, (1, 4)), jnp.tile(jnp.concatenate([-sin, sin], axis=-1), (1, 2))


def _permute_w_in(w):
    ga0 = 2304
    ga1 = ga0 + 2 * B_GATE_RANK
    n_aq = A_HEADS * HEAD_DIM
    aq = w[:, :n_aq].reshape(w.shape[0], A_HEADS, HEAD_DIM)[:, jnp.array(A_HEAD_ORDER)].reshape(w.shape[0], n_aq)
    pad = jnp.zeros((w.shape[0], IN_PADDED - w.shape[1]), w.dtype)
    return jnp.concatenate([aq, w[:, n_aq:ga0], w[:, ga1:], w[:, ga0:ga1], pad], axis=1).astype(BF16)


def _permute_w_branch_a(w):
    return w.reshape(A_HEADS, HEAD_DIM, w.shape[1])[jnp.array(A_HEAD_ORDER)].reshape(w.shape).astype(BF16)


def _gate_weights(w_fwd, b_fwd, w_bwd, b_bwd):
    n = B_HEADS * B_DK
    r = B_GATE_RANK
    w = jnp.zeros((LANES, 2 * n), F32)
    w = w.at[:r, :n].set(w_fwd).at[r:2 * r, n:].set(w_bwd)
    return w.astype(BF16), jnp.concatenate([b_fwd, b_bwd])[None, :]


def kernel(x, c, ctx, c_ctx, w_ada, b_ada, norm_mix, w_in, attn_sink, gla_gate_w_fwd, gla_gate_b_fwd,
           gla_gate_w_bwd, gla_gate_b_bwd, gla_norm, na_rpb, w_branch_a, w_branch_b, w_branch_c,
           w_merge, b_merge, w_out, norm_ffn, w_ffn_in, w_ffn_out, final_norm):
    bsz, seq, d = x.shape
    n_ctx = ctx.shape[1]
    cos, sin = _rope_tables(seq)

    cvec = jnp.zeros((8, d), F32).at[:bsz].set(c).at[bsz].set(c_ctx)
    mod = _modulation(cvec, w_ada, b_ada).reshape(DEPTH, 8, 6, d)

    xc = ctx
    zeros_state = jnp.zeros((bsz,) + GLA_STATE_SHAPE, F32)
    for l in range(DEPTH):
        last = l == DEPTH - 1
        mod_x = mod[l, :bsz]
        mod_c = mod[l, bsz:bsz + 1]
        gain_m = norm_mix[l][None, :]
        w_perm = _permute_w_in(w_in[l])
        w_gate, b_gate = _gate_weights(gla_gate_w_fwd[l], gla_gate_b_fwd[l], gla_gate_w_bwd[l], gla_gate_b_bwd[l])

        (caq, cak, cav, cgq, cgk, cgv, cgr, cnq, cnk, cnv, cgf, cgb) = _project(
            xc, mod_c, gain_m, w_perm, w_gate, b_gate, cos, sin, rope=False, tm=n_ctx)
        (aq, ak, av, gq, gk, gv, gr, nq, nk, nv, gf, gb) = _project(
            x, mod_x, gain_m, w_perm, w_gate, b_gate, cos, sin, rope=True, tm=512)

        co_f, co_b, s_f, s_b = _gla(cgq, cgk, cgv, cgf, cgb, zeros_state, zeros_state, tile=n_ctx)
        o_f, o_b, _, _ = _gla(gq, gk, gv, gf, gb, s_f, s_b, tile=256)

        sink = attn_sink[l] * LOG2E
        y_a = _window_attention(sink, aq, ak, av, cak, cav)
        y_c = _neighbourhood_attention(nq, nk, nv, cnk, cnv, _na_bias_table(na_rpb[l]))

        merge_w = (gla_norm[l][None, :], w_merge[l].astype(BF16), b_merge[l][None, :],
                   _permute_w_branch_a(w_branch_a[l]), w_branch_b[l].astype(BF16), w_branch_c[l].astype(BF16),
                   w_out[l].astype(BF16))
        x = _merge(x, mod_x, gain_m, y_a, o_f, o_b, gr, y_c, *merge_w, tm=512)

        gain_f = norm_ffn[l][None, :]
        w1 = w_ffn_in[l].astype(BF16)
        w2 = w_ffn_out[l].astype(BF16)
        fg = final_norm[None, :]
        if not last:
            yc_a = _dense_attention(sink, caq, cak, cav, head_order=A_HEAD_ORDER, shared_kv=True)
            yc_c = _dense_attention(sink, cnq, cnk, cnv, head_order=None, shared_kv=False)
            xc = _merge(xc, mod_c, gain_m, yc_a, co_f, co_b, cgr, yc_c, *merge_w, tm=n_ctx)
            xc = _ffn(xc, mod_c, gain_f, w1, w2, fg, final=False, tm=n_ctx)
        x = _ffn(x, mod_x, gain_f, w1, w2, fg, final=last, tm=512)
    return x
```

```python
import functools

import jax
import jax.numpy as jnp
from jax import lax
from jax.experimental import pallas as pl
from jax.experimental.pallas import tpu as pltpu

F32 = jnp.float32
BF16 = jnp.bfloat16

D_MODEL = 1024
DEPTH = 2
GRID_W = 64
HEAD_DIM = 64
EPS = 1e-6
ROPE_BASE = 10000.0
A_HEADS = 8
A_KV_HEADS = 2
A_BLOCK = 128
B_HEADS = 4
B_DK = 64
B_DV = 128
B_GATE_RANK = 16
B_GATE_NORM = 16.0
C_HEADS = 8
C_WIN_ROWS = 8
C_WIN_COLS = 16
FFN_HIDDEN = 2816

VMEM_LIMIT_BYTES = 56 * 1024 * 1024
LANES = 128

SEG_AQ = (0, 512)
SEG_AKV = (512, 768)
SEG_GQ = (768, 1024)
SEG_GK = (1024, 1280)
SEG_GV = (1280, 1792)
SEG_GR = (1792, 2304)
SEG_NQ = (2304, 2816)
SEG_NK = (2816, 3328)
SEG_NV = (3328, 3840)
SEG_GA = (3840, 3968)
IN_PADDED = 3968

LOG2E = 1.4426950408889634
ATTN_Q_SCALE = HEAD_DIM ** -0.5 * LOG2E

GLA_CHUNK = 64
GLA_FAST_RANGE = 120.0


def _params(sem):
    return pltpu.CompilerParams(dimension_semantics=sem, vmem_limit_bytes=VMEM_LIMIT_BYTES)


def _const_spec(shape):
    nd = len(shape)
    return pl.BlockSpec(shape, lambda *_: (0,) * nd, pipeline_mode=pl.Buffered(1))


def _sigmoid(x):
    return 1.0 / (1.0 + jnp.exp(-x))


def _silu(x):
    return x * _sigmoid(x)


def _dot(a, b):
    return jnp.dot(a, b, preferred_element_type=F32)


def _dot_nt(a, b):
    return lax.dot_general(a, b, (((1,), (1,)), ((), ())), preferred_element_type=F32)


def _dot_tn(a, b):
    return lax.dot_general(a, b, (((0,), (0,)), ((), ())), preferred_element_type=F32)


def _norm_mod(x, gain, shift, scale):
    y = x * lax.rsqrt(jnp.mean(x * x, axis=-1, keepdims=True) + EPS)
    return (y * gain) * (1.0 + scale) + shift


def _mod_kernel(c_ref, w_ref, b_ref, o_ref):
    s = _silu(c_ref[...])
    o_ref[...] = _dot(s.astype(BF16), w_ref[...].astype(BF16)) + b_ref[...]


def _modulation(cvec, w_ada, b_ada):
    tn = 1536
    n_out = w_ada.shape[-1]
    return pl.pallas_call(
        _mod_kernel,
        grid=(DEPTH, n_out // tn),
        in_specs=[
            pl.BlockSpec((8, D_MODEL), lambda l, j: (0, 0)),
            pl.BlockSpec((None, D_MODEL, tn), lambda l, j: (l, 0, j)),
            pl.BlockSpec((None, 1, tn), lambda l, j: (l, 0, j)),
        ],
        out_specs=pl.BlockSpec((None, 8, tn), lambda l, j: (l, 0, j)),
        out_shape=jax.ShapeDtypeStruct((DEPTH, 8, n_out), F32),
        compiler_params=_params(("arbitrary", "arbitrary")),
        name="adaln_mod",
    )(cvec, w_ada, b_ada.reshape(DEPTH, 1, n_out))


def _rope(t, cos, sin):
    n = t.shape[1]
    lane = lax.broadcasted_iota(jnp.int32, t.shape, 1)
    first_half = (lane % HEAD_DIM) < (HEAD_DIM // 2)
    rot = jnp.where(first_half, pltpu.roll(t, n - HEAD_DIM // 2, 1), pltpu.roll(t, HEAD_DIM // 2, 1))
    reps = n // LANES
    return t * jnp.tile(cos, (1, reps)) + rot * jnp.tile(sin, (1, reps))


class _SideCasts:
    def __init__(self, weights, n_steps, step_index):
        self.shapes2d = [w.shape for w in weights]
        self.arrays = [w.reshape(n_steps, w.shape[0] // n_steps, w.shape[1]) for w in weights]
        self.specs = [pl.BlockSpec((None,) + a.shape[1:], lambda *g: (step_index(*g), 0, 0))
                      for a in self.arrays]
        self.out_shapes = [jax.ShapeDtypeStruct(a.shape, BF16) for a in self.arrays]
        self.n = len(weights)

    def finish(self, outs):
        return [o.reshape(s) for o, s in zip(outs, self.shapes2d)]


def _run_side_casts(in_refs, out_refs):
    for w_ref, o_ref in zip(in_refs, out_refs):
        o_ref[...] = w_ref[...].astype(BF16)


N_PROJ_INPUTS = 8
N_PROJ_OUTPUTS = 12


def _proj_kernel(*refs, rope):
    (x_ref, mod_ref, gain_ref, w_ref, wg_ref, bg_ref, cos_ref, sin_ref) = refs[:N_PROJ_INPUTS]
    n_side = (len(refs) - N_PROJ_INPUTS - N_PROJ_OUTPUTS) // 2
    outs = refs[N_PROJ_INPUTS + n_side:]
    (aq_ref, ak_ref, av_ref, gq_ref, gk_ref, gv_ref, gr_ref, nq_ref, nk_ref, nv_ref,
     gf_ref, gb_ref) = outs[:N_PROJ_OUTPUTS]
    _run_side_casts(refs[N_PROJ_INPUTS:N_PROJ_INPUTS + n_side], outs[N_PROJ_OUTPUTS:])
    scale = ATTN_Q_SCALE
    n_g = B_HEADS * B_DK
    h = _norm_mod(x_ref[...], gain_ref[...], mod_ref[0:1, :], mod_ref[1:2, :]).astype(BF16)

    def seg(s):
        return _dot(h, w_ref[:, s[0]:s[1]])

    aq = seg(SEG_AQ)
    akv = seg(SEG_AKV)
    z = _dot(seg(SEG_GA).astype(BF16), wg_ref[...]) + bg_ref[...]
    ak = akv[:, :LANES]
    if rope:
        cos = cos_ref[...]
        sin = sin_ref[...]
        aq = _rope(aq, cos, sin)
        ak = _rope(ak, cos, sin)
    aq_ref[...] = (aq * scale).astype(BF16)
    ak_ref[...] = ak.astype(BF16)
    av_ref[...] = akv[:, LANES:].astype(BF16)
    log_sig = jnp.minimum(z, 0.0) - jnp.log1p(jnp.exp(-jnp.abs(z)))
    g = log_sig / B_GATE_NORM
    gf_ref[...] = g[:, :n_g]
    gb_ref[...] = g[:, n_g:]
    gq_ref[...] = seg(SEG_GQ) * (B_DK ** -0.5)
    gk_ref[...] = seg(SEG_GK)
    gv_ref[...] = seg(SEG_GV).astype(BF16)
    gr_ref[...] = seg(SEG_GR)
    nq_ref[...] = (seg(SEG_NQ) * scale).astype(BF16)
    nk_ref[...] = seg(SEG_NK).astype(BF16)
    nv_ref[...] = seg(SEG_NV).astype(BF16)


def _project(x, mod_rows, gain, w_perm, w_gate, b_gate, cos, sin, *, rope, tm, cast_weights=()):
    bsz, seq, _ = x.shape
    n_tiles = seq // tm
    per_batch_mod = mod_rows.shape[0] > 1
    mod_map = (lambda b, i: (b, 0, 0)) if per_batch_mod else (lambda b, i: (0, 0, 0))
    widths = [(512, BF16), (128, BF16), (128, BF16), (256, F32), (256, F32), (512, BF16), (512, F32),
              (512, BF16), (512, BF16), (512, BF16), (256, F32), (256, F32)]
    assert len(widths) == N_PROJ_OUTPUTS
    tile = lambda w: pl.BlockSpec((None, tm, w), lambda b, i: (b, i, 0))
    side = _SideCasts(cast_weights, bsz * n_tiles, lambda b, i: b * n_tiles + i)
    outs = pl.pallas_call(
        functools.partial(_proj_kernel, rope=rope),
        grid=(bsz, n_tiles),
        in_specs=[
            tile(D_MODEL),
            pl.BlockSpec((None, 6, D_MODEL), mod_map),
            _const_spec((1, D_MODEL)),
            _const_spec((D_MODEL, IN_PADDED)),
            _const_spec((LANES, 2 * B_HEADS * B_DK)),
            _const_spec((1, 2 * B_HEADS * B_DK)),
            pl.BlockSpec((tm, LANES), lambda b, i: (i, 0)),
            pl.BlockSpec((tm, LANES), lambda b, i: (i, 0)),
        ] + side.specs,
        out_specs=[tile(w) for w, _ in widths] + side.specs,
        out_shape=[jax.ShapeDtypeStruct((bsz, seq, w), dt) for w, dt in widths] + side.out_shapes,
        compiler_params=_params(("parallel", "arbitrary")),
        name="in_proj_rope" if rope else "in_proj_ctx",
    )(x, mod_rows, gain, w_perm, w_gate, b_gate, cos, sin, *side.arrays)
    return outs[:N_PROJ_OUTPUTS], side.finish(outs[N_PROJ_OUTPUTS:])


class _GlaPrep:
    def __init__(self, q_ref, k_ref, g_ref, reverse):
        t = q_ref.shape[0]
        c = GLA_CHUNK
        row = lax.broadcasted_iota(jnp.int32, (t, t), 0)
        col = lax.broadcasted_iota(jnp.int32, (t, t), 1)
        same_chunk = (row // c) == (col // c)
        tri = (same_chunk & ((col >= row) if reverse else (col <= row))).astype(BF16)
        self.reverse = reverse
        self.q = q_ref[...]
        self.k = k_ref[...]
        self.g_parts = _split3(g_ref[...])
        self.b = _dot_01(tri, self.g_parts)
        b = self.b
        self.totals = [b[ci * c:ci * c + 1, :] if reverse else b[ci * c + c - 1:ci * c + c, :]
                       for ci in range(t // c)]
        self.b_end = jnp.concatenate([jnp.broadcast_to(e, (c, e.shape[1])) for e in self.totals], axis=0)
        self.q_state = (self.q * jnp.exp(b)).astype(BF16)
        self.k_tail = (self.k * jnp.exp(self.b_end - b)).astype(BF16)
        self.decay = [jnp.exp(e) for e in self.totals]


def _split3(x):
    hi = x.astype(BF16)
    rest = x - hi.astype(F32)
    mid = rest.astype(BF16)
    return hi, mid, (rest - mid.astype(F32)).astype(BF16)


def _dot_01(m01, parts):
    return _dot(m01, parts[0]) + _dot(m01, parts[1]) + _dot(m01, parts[2])


def _gla_pair_scores(q_w, k_w, keep):
    c = GLA_CHUNK
    out = []
    for ci in range(q_w.shape[0] // c):
        rows = slice(ci * c, (ci + 1) * c)
        for p in range(B_HEADS // 2):
            ps = slice(p * LANES, (p + 1) * LANES)
            out.append(jnp.where(keep, _dot_nt(_split_head_pair(q_w[rows, ps]), k_w[rows, ps]), 0.0))
    return out


def _gla_scores_fast(prep):
    c = GLA_CHUNK
    qi = lax.broadcasted_iota(jnp.int32, (2 * c, c), 0) % c
    kj = lax.broadcasted_iota(jnp.int32, (2 * c, c), 1)
    keep = (kj >= qi) if prep.reverse else (kj <= qi)
    ref = 0.5 * prep.b_end
    q_in = (prep.q * jnp.exp(prep.b - ref)).astype(BF16)
    k_in = (prep.k * jnp.exp(ref - prep.b)).astype(BF16)
    return _gla_pair_scores(q_in, k_in, keep)


def _gla_scores_safe(prep):
    t = prep.q.shape[0]
    c = GLA_CHUNK
    rev = prep.reverse
    row = lax.broadcasted_iota(jnp.int32, (t, t), 0)
    col = lax.broadcasted_iota(jnp.int32, (t, t), 1)
    qi = lax.broadcasted_iota(jnp.int32, (2 * c, c), 0) % c
    kj = lax.broadcasted_iota(jnp.int32, (2 * c, c), 1)
    scores = _gla_pair_scores(prep.q.astype(BF16), prep.k.astype(BF16), qi == kj)
    s = c // 2
    while s >= 1:
        if rev:
            bnd = row - row % (2 * s) + s
            span = ((col >= bnd) & (col < row)) | ((col >= row) & (col < bnd))
        else:
            bnd = row - row % (2 * s) + s - 1
            span = ((col > bnd) & (col <= row)) | ((col > row) & (col <= bnd))
        e = jnp.exp(_dot_01(span.astype(BF16), prep.g_parts))
        q_side = ((qi % (2 * s)) < s) if rev else ((qi % (2 * s)) >= s)
        k_side = ((kj % (2 * s)) >= s) if rev else ((kj % (2 * s)) < s)
        keep = ((qi // (2 * s)) == (kj // (2 * s))) & q_side & k_side
        level = _gla_pair_scores((prep.q * e).astype(BF16), (prep.k * e).astype(BF16), keep)
        scores = [acc + a for acc, a in zip(scores, level)]
        s //= 2
    return scores


def _gla_local(scores, k_tail, v_ref):
    c = GLA_CHUNK
    n_pairs = B_HEADS // 2
    intra, upd = [], []
    for ci in range(k_tail.shape[0] // c):
        rows = slice(ci * c, (ci + 1) * c)
        intra_c, upd_c = [], []
        for p in range(n_pairs):
            a = scores[ci * n_pairs + p].astype(BF16)
            kt = _split_head_pair(k_tail[rows, p * LANES:(p + 1) * LANES])
            u = None
            for half in range(2):
                h = 2 * p + half
                vh = v_ref[rows, h * B_DV:(h + 1) * B_DV]
                intra_c.append(_dot(a[half * c:(half + 1) * c], vh))
                uh = _dot_tn(vh, kt[half * c:(half + 1) * c])
                u = uh if u is None else u + uh
            upd_c.append(u)
        intra.append(intra_c)
        upd.append(upd_c)
    return intra, upd


def _gla_states(s_ref, decay, upd, reverse):
    n = len(upd)
    order = range(n - 1, -1, -1) if reverse else range(n)
    starts = [None] * n
    s = [s_ref[p] for p in range(B_HEADS // 2)]
    for ci in order:
        starts[ci] = [sp.astype(BF16) for sp in s]
        s = [s[p] * decay[ci][:, p * LANES:(p + 1) * LANES] + upd[ci][p] for p in range(B_HEADS // 2)]
    for p in range(B_HEADS // 2):
        s_ref[p] = s[p]
    return starts


def _gla_finish(o_ref, q_state, starts, intra):
    c = GLA_CHUNK
    for ci in range(len(intra)):
        rows = slice(ci * c, (ci + 1) * c)
        for p in range(B_HEADS // 2):
            ps = slice(p * LANES, (p + 1) * LANES)
            inter = _dot_nt(_split_head_pair(q_state[rows, ps]), starts[ci][p])
            for half in range(2):
                h = 2 * p + half
                o_ref[rows, h * B_DV:(h + 1) * B_DV] = intra[ci][h] + inter[half * c:(half + 1) * c]


def _gla_kernel(qf_ref, kf_ref, vf_ref, gf_ref, qb_ref, kb_ref, vb_ref, gb_ref, s0f_ref, s0b_ref,
                of_ref, ob_ref, sf_out_ref, sb_out_ref, sf_ref, sb_ref):
    i = pl.program_id(1)

    @pl.when(i == 0)
    def _():
        sf_ref[...] = s0f_ref[...]
        sb_ref[...] = s0b_ref[...]

    fwd = _GlaPrep(qf_ref, kf_ref, gf_ref, False)
    bwd = _GlaPrep(qb_ref, kb_ref, gb_ref, True)

    def tile_body(score_fn):
        scores_f = score_fn(fwd)
        scores_b = score_fn(bwd)
        intra_f, upd_f = _gla_local(scores_f, fwd.k_tail, vf_ref)
        intra_b, upd_b = _gla_local(scores_b, bwd.k_tail, vb_ref)
        starts_f = _gla_states(sf_ref, fwd.decay, upd_f, False)
        starts_b = _gla_states(sb_ref, bwd.decay, upd_b, True)
        _gla_finish(of_ref, fwd.q_state, starts_f, intra_f)
        _gla_finish(ob_ref, bwd.q_state, starts_b, intra_b)

    in_fast_range = -jnp.min(jnp.concatenate(fwd.totals + bwd.totals, axis=0)) < GLA_FAST_RANGE

    @pl.when(in_fast_range)
    def _():
        tile_body(_gla_scores_fast)

    @pl.when(jnp.logical_not(in_fast_range))
    def _():
        tile_body(_gla_scores_safe)

    @pl.when(i == pl.num_programs(1) - 1)
    def _():
        sf_out_ref[...] = sf_ref[...]
        sb_out_ref[...] = sb_ref[...]


GLA_STATE_SHAPE = (B_HEADS // 2, B_DV, 2 * B_DK)


def _gla(q, k, v, gf, gb, s0f, s0b, *, tile):
    bsz, seq, _ = q.shape
    n = seq // tile
    fwd = lambda w: pl.BlockSpec((None, tile, w), lambda b, i: (b, i, 0))
    bwd = lambda w: pl.BlockSpec((None, tile, w), lambda b, i: (b, n - 1 - i, 0))
    st = pl.BlockSpec((None,) + GLA_STATE_SHAPE, lambda b, i: (b, 0, 0, 0))
    wk, wv = B_HEADS * B_DK, B_HEADS * B_DV
    st_shape = jax.ShapeDtypeStruct((bsz,) + GLA_STATE_SHAPE, F32)
    return pl.pallas_call(
        _gla_kernel,
        grid=(bsz, n),
        in_specs=[fwd(wk), fwd(wk), fwd(wv), fwd(wk), bwd(wk), bwd(wk), bwd(wv), bwd(wk), st, st],
        out_specs=[fwd(wv), bwd(wv), st, st],
        out_shape=[jax.ShapeDtypeStruct((bsz, seq, wv), F32), jax.ShapeDtypeStruct((bsz, seq, wv), F32),
                   st_shape, st_shape],
        scratch_shapes=[pltpu.VMEM(GLA_STATE_SHAPE, F32), pltpu.VMEM(GLA_STATE_SHAPE, F32)],
        compiler_params=_params(("parallel", "arbitrary")),
        name="gla_scan",
    )(q, k, v, gf, q, k, v, gb, s0f, s0b)


def _softmax_parts(scores, sink_tile):
    def lane_tiles(blocks):
        return [b[:, j:j + LANES] for b in blocks for j in range(0, b.shape[1], LANES)]

    tiles = lane_tiles(scores)
    if sink_tile is not None:
        tiles.append(sink_tile)
    m = functools.reduce(jnp.maximum, tiles).max(axis=-1, keepdims=True)
    ps = [jnp.exp2(s - m) for s in scores]
    acc = functools.reduce(jnp.add, lane_tiles(ps))
    if sink_tile is not None:
        lane = lax.broadcasted_iota(jnp.int32, sink_tile.shape, 1)
        acc = acc + jnp.where(lane == 0, jnp.exp2(sink_tile - m), 0.0)
    return ps, acc.sum(axis=-1, keepdims=True)


def _split_head_pair(t):
    lane = lax.broadcasted_iota(jnp.int32, t.shape, 1)
    zero = jnp.zeros_like(t)
    return jnp.concatenate([jnp.where(lane < HEAD_DIM, t, zero), jnp.where(lane >= HEAD_DIM, t, zero)], axis=0)


def _merge_head_pair(o):
    m = o.shape[0] // 2
    lane = lax.broadcasted_iota(jnp.int32, (m, LANES), 1)
    return jnp.where(lane < HEAD_DIM, o[:m], o[m:])


A_HEAD_ORDER = (0, 4, 1, 5, 2, 6, 3, 7)


def _win_kernel(sink_ref, q_ref, kp_ref, kc_ref, kn_ref, vp_ref, vc_ref, vn_ref, kx_ref, vx_ref, o_ref):
    n = pl.program_id(1)
    nb = pl.num_programs(1)
    n_tiles = A_HEADS // 2
    q = jnp.concatenate([_split_head_pair(q_ref[:, j * LANES:(j + 1) * LANES]) for j in range(n_tiles)], axis=0)
    sink_tile = jnp.concatenate([jnp.full((A_BLOCK, LANES), sink_ref[h], F32) for h in A_HEAD_ORDER], axis=0)
    qi = lax.broadcasted_iota(jnp.int32, (A_BLOCK, A_BLOCK), 0)
    kj = lax.broadcasted_iota(jnp.int32, (A_BLOCK, A_BLOCK), 1)
    neg = jnp.full((A_BLOCK, A_BLOCK), -jnp.inf, F32)
    zero = jnp.zeros((A_BLOCK, A_BLOCK), F32)
    prev_mask = jnp.where(n > 0, jnp.where(kj >= qi, zero, neg), neg)
    next_mask = jnp.where(n < nb - 1, jnp.where(kj <= qi, zero, neg), neg)
    s_p = _dot_nt(q, kp_ref[...]) + jnp.tile(prev_mask, (A_HEADS, 1))
    s_c = _dot_nt(q, kc_ref[...])
    s_n = _dot_nt(q, kn_ref[...]) + jnp.tile(next_mask, (A_HEADS, 1))
    s_x = _dot_nt(q, kx_ref[...])
    (p_p, p_c, p_n, p_x), denom = _softmax_parts([s_p, s_c, s_n, s_x], sink_tile)
    o = (_dot(p_p.astype(BF16), vp_ref[...]) + _dot(p_c.astype(BF16), vc_ref[...])
         + _dot(p_n.astype(BF16), vn_ref[...]) + _dot(p_x.astype(BF16), vx_ref[...]))
    o = o / denom
    for j in range(n_tiles):
        o_ref[:, j * LANES:(j + 1) * LANES] = _merge_head_pair(
            o[j * 2 * A_BLOCK:(j + 1) * 2 * A_BLOCK]).astype(o_ref.dtype)


def _window_attention(sink, q, k, v, k_ctx, v_ctx):
    bsz, seq, wq = q.shape
    nb = seq // A_BLOCK
    wkv = k.shape[-1]
    n_ctx = k_ctx.shape[1]
    kv_spec = lambda f: pl.BlockSpec((None, A_BLOCK, wkv), lambda b, n: (b, f(n), 0))
    prev = lambda n: jnp.maximum(n - 1, 0)
    cur = lambda n: n
    nxt = lambda n: jnp.minimum(n + 1, nb - 1)
    ctx_spec = pl.BlockSpec((None, n_ctx, wkv), lambda b, n: (b, 0, 0))
    return pl.pallas_call(
        _win_kernel,
        grid=(bsz, nb),
        in_specs=[
            pl.BlockSpec(memory_space=pltpu.SMEM),
            pl.BlockSpec((None, A_BLOCK, wq), lambda b, n: (b, n, 0)),
            kv_spec(prev), kv_spec(cur), kv_spec(nxt),
            kv_spec(prev), kv_spec(cur), kv_spec(nxt),
            ctx_spec, ctx_spec,
        ],
        out_specs=pl.BlockSpec((None, A_BLOCK, wq), lambda b, n: (b, n, 0)),
        out_shape=jax.ShapeDtypeStruct((bsz, seq, wq), BF16),
        compiler_params=_params(("parallel", "arbitrary")),
        name="window_attn",
    )(sink, q, k, k, k, v, v, v, k_ctx, v_ctx)


def _dense_kernel(sink_ref, q_ref, k_ref, v_ref, o_ref, *, head_order, shared_kv):
    n_q = q_ref.shape[0]
    for j in range(q_ref.shape[1] // LANES):
        qs = slice(j * LANES, (j + 1) * LANES)
        ks = slice(0, LANES) if shared_kv else qs
        s = _dot_nt(_split_head_pair(q_ref[:, qs]), k_ref[:, ks])
        sink_tile = None
        if head_order is not None:
            sink_tile = jnp.concatenate(
                [jnp.full((n_q, LANES), sink_ref[head_order[2 * j + half]], F32) for half in range(2)], axis=0)
        (p,), denom = _softmax_parts([s], sink_tile)
        o_ref[:, qs] = _merge_head_pair(_dot(p.astype(BF16), v_ref[:, ks]) / denom).astype(o_ref.dtype)


def _dense_attention(sink, q, k, v, *, head_order, shared_kv):
    bsz, n_q, wq = q.shape
    wkv = k.shape[-1]
    full = lambda w: pl.BlockSpec((None, n_q, w), lambda b: (b, 0, 0))
    return pl.pallas_call(
        functools.partial(_dense_kernel, head_order=head_order, shared_kv=shared_kv),
        grid=(bsz,),
        in_specs=[pl.BlockSpec(memory_space=pltpu.SMEM), full(wq), full(wkv), full(wkv)],
        out_specs=full(wq),
        out_shape=jax.ShapeDtypeStruct((bsz, n_q, wq), BF16),
        compiler_params=_params(("parallel",)),
        name="ctx_dense_attn_sink" if shared_kv else "ctx_dense_attn",
    )(sink, q, k, v)


NA_ROWS_PER_STEP = 8


N_NA_INPUTS = 6


def _na_kernel(*refs):
    q_ref, k_ref, v_ref, kx_ref, vx_ref, bias_ref = refs[:N_NA_INPUTS]
    n_side = (len(refs) - N_NA_INPUTS - 1) // 2
    o_ref = refs[N_NA_INPUTS + n_side]
    _run_side_casts(refs[N_NA_INPUTS:N_NA_INPUTS + n_side], refs[N_NA_INPUTS + n_side + 1:])
    step = pl.program_id(1)
    grid_rows = k_ref.shape[0] // GRID_W
    n_keys = C_WIN_ROWS * GRID_W
    n_pairs = C_HEADS // 2
    for rr in range(NA_ROWS_PER_STEP):
        r = step * NA_ROWS_PER_STEP + rr
        row0 = jnp.clip(r - C_WIN_ROWS // 2, 0, grid_rows - C_WIN_ROWS)
        d_row0 = row0 - r + (C_WIN_ROWS - 1)
        key0 = pl.multiple_of(row0 * GRID_W, GRID_W)
        qrows = slice(rr * GRID_W, (rr + 1) * GRID_W)
        s_l, s_x = [], []
        for p in range(n_pairs):
            ps = slice(p * LANES, (p + 1) * LANES)
            q2 = _split_head_pair(q_ref[qrows, ps])
            s_l.append(_dot_nt(q2, k_ref[pl.ds(key0, n_keys), ps]))
            s_x.append(_dot_nt(q2, kx_ref[:, ps]))
        bias = jnp.concatenate(
            [bias_ref[d_row0 + 2 * j].reshape(C_HEADS * GRID_W, LANES) for j in range(C_WIN_ROWS // 2)], axis=1)
        (p_l, p_x), denom = _softmax_parts([jnp.concatenate(s_l, axis=0) + bias, jnp.concatenate(s_x, axis=0)], None)
        p_l = p_l.astype(BF16)
        p_x = p_x.astype(BF16)
        for p in range(n_pairs):
            ps = slice(p * LANES, (p + 1) * LANES)
            pr = slice(p * 2 * GRID_W, (p + 1) * 2 * GRID_W)
            o = (_dot(p_l[pr], v_ref[pl.ds(key0, n_keys), ps]) + _dot(p_x[pr], vx_ref[:, ps])) / denom[pr]
            o_ref[qrows, ps] = _merge_head_pair(o).astype(o_ref.dtype)


def _neighbourhood_attention(q, k, v, k_ctx, v_ctx, bias_tbl, cast_weights=()):
    bsz, seq, w = q.shape
    n_ctx = k_ctx.shape[1]
    tq = NA_ROWS_PER_STEP * GRID_W
    n_tiles = seq // tq
    whole = pl.BlockSpec((None, seq, w), lambda b, i: (b, 0, 0))
    ctx_spec = pl.BlockSpec((None, n_ctx, w), lambda b, i: (b, 0, 0))
    side = _SideCasts(cast_weights, bsz * n_tiles, lambda b, i: b * n_tiles + i)
    outs = pl.pallas_call(
        _na_kernel,
        grid=(bsz, n_tiles),
        in_specs=[pl.BlockSpec((None, tq, w), lambda b, i: (b, i, 0)), whole, whole, ctx_spec, ctx_spec,
                  _const_spec(bias_tbl.shape)] + side.specs,
        out_specs=[pl.BlockSpec((None, tq, w), lambda b, i: (b, i, 0))] + side.specs,
        out_shape=[jax.ShapeDtypeStruct((bsz, seq, w), BF16)] + side.out_shapes,
        compiler_params=_params(("parallel", "arbitrary")),
        name="neighbourhood_attn",
    )(q, k, v, k_ctx, v_ctx, bias_tbl, *side.arrays)
    return outs[0], side.finish(outs[1:])


def _na_bias_table(rpb):
    qcol = jnp.arange(GRID_W)[:, None]
    kcol = jnp.arange(GRID_W)[None, :]
    wstart = jnp.clip(qcol - C_WIN_COLS // 2, 0, GRID_W - C_WIN_COLS)
    ok = (kcol >= wstart) & (kcol < wstart + C_WIN_COLS)
    n_off = C_WIN_COLS - 1
    period = 2 * GRID_W
    vec = jnp.concatenate([rpb[..., n_off:].astype(F32),
                           jnp.zeros(rpb.shape[:2] + (period - 2 * n_off - 1,), F32),
                           rpb[..., :n_off].astype(F32)], axis=-1)
    toep = jnp.tile(vec, (1, 1, GRID_W))[..., :GRID_W * (period - 1)]
    toep = toep.reshape(rpb.shape[:2] + (GRID_W, period - 1))[..., :GRID_W]
    tbl = jnp.where(ok[None, None], toep * LOG2E, -jnp.inf)
    tbl = jnp.moveaxis(tbl, 1, 0)
    return jnp.concatenate([tbl[:-1], tbl[1:]], axis=-1)


def _merge_kernel(x_ref, mod_ref, gain_ref, ya_ref, of_ref, ob_ref, gr_ref, yc_ref, gn_ref,
                  wm_ref, bm_ref, wa_ref, wb_ref, wc_ref, wo_ref, o_ref):
    d = D_MODEL
    gn = gn_ref[...]
    x = x_ref[...]
    h = _norm_mod(x, gain_ref[...], mod_ref[0:1, :], mod_ref[1:2, :]).astype(BF16)
    o_sum = of_ref[...] + ob_ref[...]
    parts = []
    for hh in range(B_HEADS):
        oh = o_sum[:, hh * B_DV:(hh + 1) * B_DV]
        parts.append((oh * lax.rsqrt(jnp.mean(oh * oh, axis=-1, keepdims=True) + EPS)) * gn)
    y_b = (jnp.concatenate(parts, axis=1) * _silu(gr_ref[...])).astype(BF16)
    mixed = None
    for j, (y, w_ref) in enumerate(((ya_ref[...], wa_ref), (y_b, wb_ref), (yc_ref[...], wc_ref))):
        gate = _sigmoid(_dot(h, wm_ref[:, j * d:(j + 1) * d]) + bm_ref[:, j * d:(j + 1) * d])
        term = gate * _dot(y, w_ref[...])
        mixed = term if mixed is None else mixed + term
    o_ref[...] = x + mod_ref[2:3, :] * _dot(mixed.astype(BF16), wo_ref[...])


def _merge(x, mod_rows, gain, y_a, o_f, o_b, g_r, y_c, gla_gain, w_merge, b_merge, w_a, w_b, w_c, w_out, *, tm):
    bsz, seq, d = x.shape
    per_batch_mod = mod_rows.shape[0] > 1
    mod_map = (lambda b, i: (b, 0, 0)) if per_batch_mod else (lambda b, i: (0, 0, 0))
    tile = lambda w: pl.BlockSpec((None, tm, w), lambda b, i: (b, i, 0))
    wy = y_a.shape[-1]
    return pl.pallas_call(
        _merge_kernel,
        grid=(bsz, seq // tm),
        in_specs=[tile(d), pl.BlockSpec((None, 6, d), mod_map), _const_spec((1, d)),
                  tile(wy), tile(wy), tile(wy), tile(wy), tile(wy), _const_spec((1, B_DV)),
                  _const_spec(w_merge.shape), _const_spec(b_merge.shape),
                  _const_spec(w_a.shape), _const_spec(w_b.shape), _const_spec(w_c.shape),
                  _const_spec(w_out.shape)],
        out_specs=tile(d),
        out_shape=jax.ShapeDtypeStruct((bsz, seq, d), F32),
        compiler_params=_params(("parallel", "arbitrary")),
        name="merge_out",
    )(x, mod_rows, gain, y_a, o_f, o_b, g_r, y_c, gla_gain, w_merge, b_merge, w_a, w_b, w_c, w_out)


FFN_CHUNKS = ((0, 1024), (1024, 2048), (2048, FFN_HIDDEN))
FFN_SUB_ROWS = 256


def _ffn_kernel(x_ref, mod_ref, gain_ref, w1_ref, w2_ref, fg_ref, o_ref, *, final):
    for r0 in range(0, x_ref.shape[0], FFN_SUB_ROWS):
        rows = slice(r0, r0 + FFN_SUB_ROWS)
        x = x_ref[rows, :]
        h = _norm_mod(x, gain_ref[...], mod_ref[3:4, :], mod_ref[4:5, :]).astype(BF16)
        acc = None
        for c0, c1 in FFN_CHUNKS:
            gate = _dot(h, w1_ref[:, c0:c1])
            up = _dot(h, w1_ref[:, FFN_HIDDEN + c0:FFN_HIDDEN + c1])
            part = _dot((_silu(gate) * up).astype(BF16), w2_ref[c0:c1, :])
            acc = part if acc is None else acc + part
        y = x + mod_ref[5:6, :] * acc
        if final:
            y = (y * lax.rsqrt(jnp.mean(y * y, axis=-1, keepdims=True) + EPS)) * fg_ref[...]
        o_ref[rows, :] = y


def _ffn(x, mod_rows, gain, w1, w2, final_gain, *, final, tm):
    bsz, seq, d = x.shape
    per_batch_mod = mod_rows.shape[0] > 1
    mod_map = (lambda b, i: (b, 0, 0)) if per_batch_mod else (lambda b, i: (0, 0, 0))
    tile = pl.BlockSpec((None, tm, d), lambda b, i: (b, i, 0))
    return pl.pallas_call(
        functools.partial(_ffn_kernel, final=final),
        grid=(bsz, seq // tm),
        in_specs=[tile, pl.BlockSpec((None, 6, d), mod_map), _const_spec((1, d)),
                  _const_spec(w1.shape), _const_spec(w2.shape), _const_spec((1, d))],
        out_specs=tile,
        out_shape=jax.ShapeDtypeStruct((bsz, seq, d), F32),
        compiler_params=_params(("parallel", "arbitrary")),
        name="ffn_final" if final else "ffn",
    )(x, mod_rows, gain, w1, w2, final_gain)


def _rope_tables(seq):
    t = jnp.arange(seq)
    row = (t // GRID_W).astype(F32)
    col = (t % GRID_W).astype(F32)
    n_freq = HEAD_DIM // 4
    inv = ROPE_BASE ** (-jnp.arange(n_freq, dtype=F32) / n_freq)
    ang = jnp.concatenate([row[:, None] * inv[None], col[:, None] * inv[None]], axis=-1)
    cos, sin = jnp.cos(ang), jnp.sin(ang)
    return jnp.tile(cos, (1, 4)), jnp.tile(jnp.concatenate([-sin, sin], axis=-1), (1, 2))


def _permute_w_in(w):
    ga0 = 2304
    ga1 = ga0 + 2 * B_GATE_RANK
    n_aq = A_HEADS * HEAD_DIM
    aq = w[:, :n_aq].reshape(w.shape[0], A_HEADS, HEAD_DIM)[:, jnp.array(A_HEAD_ORDER)].reshape(w.shape[0], n_aq)
    pad = jnp.zeros((w.shape[0], IN_PADDED - w.shape[1]), w.dtype)
    return jnp.concatenate([aq, w[:, n_aq:ga0], w[:, ga1:], w[:, ga0:ga1], pad], axis=1).astype(BF16)


def _permute_w_branch_a(w):
    return w.reshape(A_HEADS, HEAD_DIM, w.shape[1])[jnp.array(A_HEAD_ORDER)].reshape(w.shape).astype(BF16)


def _gate_weights(w_fwd, b_fwd, w_bwd, b_bwd):
    n = B_HEADS * B_DK
    r = B_GATE_RANK
    w = jnp.zeros((LANES, 2 * n), F32)
    w = w.at[:r, :n].set(w_fwd).at[r:2 * r, n:].set(w_bwd)
    return w.astype(BF16), jnp.concatenate([b_fwd, b_bwd])[None, :]


def kernel(x, c, ctx, c_ctx, w_ada, b_ada, norm_mix, w_in, attn_sink, gla_gate_w_fwd, gla_gate_b_fwd,
           gla_gate_w_bwd, gla_gate_b_bwd, gla_norm, na_rpb, w_branch_a, w_branch_b, w_branch_c,
           w_merge, b_merge, w_out, norm_ffn, w_ffn_in, w_ffn_out, final_norm):
    bsz, seq, d = x.shape
    n_ctx = ctx.shape[1]
    cos, sin = _rope_tables(seq)

    cvec = jnp.zeros((8, d), F32).at[:bsz].set(c).at[bsz].set(c_ctx)
    mod = _modulation(cvec, w_ada, b_ada).reshape(DEPTH, 8, 6, d)

    xc = ctx
    zeros_state = jnp.zeros((bsz,) + GLA_STATE_SHAPE, F32)
    for l in range(DEPTH):
        last = l == DEPTH - 1
        mod_x = mod[l, :bsz]
        mod_c = mod[l, bsz:bsz + 1]
        gain_m = norm_mix[l][None, :]
        w_perm = _permute_w_in(w_in[l])
        w_gate, b_gate = _gate_weights(gla_gate_w_fwd[l], gla_gate_b_fwd[l], gla_gate_w_bwd[l], gla_gate_b_bwd[l])

        (caq, cak, cav, cgq, cgk, cgv, cgr, cnq, cnk, cnv, cgf, cgb), _ = _project(
            xc, mod_c, gain_m, w_perm, w_gate, b_gate, cos, sin, rope=False, tm=n_ctx)
        (aq, ak, av, gq, gk, gv, gr, nq, nk, nv, gf, gb), (wm_bf, wb_bf, wc_bf, wo_bf) = _project(
            x, mod_x, gain_m, w_perm, w_gate, b_gate, cos, sin, rope=True, tm=512,
            cast_weights=(w_merge[l], w_branch_b[l], w_branch_c[l], w_out[l]))

        co_f, co_b, s_f, s_b = _gla(cgq, cgk, cgv, cgf, cgb, zeros_state, zeros_state, tile=n_ctx)
        o_f, o_b, _, _ = _gla(gq, gk, gv, gf, gb, s_f, s_b, tile=256)

        sink = attn_sink[l] * LOG2E
        y_a = _window_attention(sink, aq, ak, av, cak, cav)
        y_c, (w1, w2) = _neighbourhood_attention(nq, nk, nv, cnk, cnv, _na_bias_table(na_rpb[l]),
                                                 cast_weights=(w_ffn_in[l], w_ffn_out[l]))

        merge_w = (gla_norm[l][None, :], wm_bf, b_merge[l][None, :],
                   _permute_w_branch_a(w_branch_a[l]), wb_bf, wc_bf, wo_bf)
        x = _merge(x, mod_x, gain_m, y_a, o_f, o_b, gr, y_c, *merge_w, tm=512)

        gain_f = norm_ffn[l][None, :]
        fg = final_norm[None, :]
        if not last:
            yc_a = _dense_attention(sink, caq, cak, cav, head_order=A_HEAD_ORDER, shared_kv=True)
            yc_c = _dense_attention(sink, cnq, cnk, cnv, head_order=None, shared_kv=False)
            xc = _merge(xc, mod_c, gain_m, yc_a, co_f, co_b, cgr, yc_c, *merge_w, tm=n_ctx)
            xc = _ffn(xc, mod_c, gain_f, w1, w2, fg, final=False, tm=n_ctx)
        x = _ffn(x, mod_x, gain_f, w1, w2, fg, final=last, tm=512)
    return x
```

```python
import functools

import jax
import jax.numpy as jnp
from jax import lax
from jax.experimental import pallas as pl
from jax.experimental.pallas import tpu as pltpu

F32 = jnp.float32
BF16 = jnp.bfloat16

D_MODEL = 1024
DEPTH = 2
GRID_W = 64
HEAD_DIM = 64
EPS = 1e-6
ROPE_BASE = 10000.0
A_HEADS = 8
A_KV_HEADS = 2
A_BLOCK = 128
B_HEADS = 4
B_DK = 64
B_DV = 128
B_GATE_RANK = 16
B_GATE_NORM = 16.0
C_HEADS = 8
C_WIN_ROWS = 8
C_WIN_COLS = 16
FFN_HIDDEN = 2816

VMEM_LIMIT_BYTES = 56 * 1024 * 1024
LANES = 128

SEG_AQ = (0, 512)
SEG_AKV = (512, 768)
SEG_GQ = (768, 1024)
SEG_GK = (1024, 1280)
SEG_GV = (1280, 1792)
SEG_GR = (1792, 2304)
SEG_NQ = (2304, 2816)
SEG_NK = (2816, 3328)
SEG_NV = (3328, 3840)
SEG_GA = (3840, 3968)
IN_PADDED = 3968

LOG2E = 1.4426950408889634
ATTN_Q_SCALE = HEAD_DIM ** -0.5 * LOG2E

GLA_CHUNK = 64
GLA_FAST_RANGE = 120.0


def _params(sem):
    return pltpu.CompilerParams(dimension_semantics=sem, vmem_limit_bytes=VMEM_LIMIT_BYTES)


def _const_spec(shape):
    nd = len(shape)
    return pl.BlockSpec(shape, lambda *_: (0,) * nd, pipeline_mode=pl.Buffered(1))


def _sigmoid(x):
    return 1.0 / (1.0 + jnp.exp(-x))


def _silu(x):
    return x * _sigmoid(x)


def _dot(a, b):
    return jnp.dot(a, b, preferred_element_type=F32)


def _dot_nt(a, b):
    return lax.dot_general(a, b, (((1,), (1,)), ((), ())), preferred_element_type=F32)


def _dot_tn(a, b):
    return lax.dot_general(a, b, (((0,), (0,)), ((), ())), preferred_element_type=F32)


def _norm_mod(x, gain, shift, scale):
    y = x * lax.rsqrt(jnp.mean(x * x, axis=-1, keepdims=True) + EPS)
    return (y * gain) * (1.0 + scale) + shift


def _mod_kernel(c_ref, w_ref, b_ref, o_ref):
    s = _silu(c_ref[...])
    o_ref[...] = _dot(s.astype(BF16), w_ref[...].astype(BF16)) + b_ref[...]


def _modulation(cvec, w_ada, b_ada):
    tn = 1536
    n_out = w_ada.shape[-1]
    return pl.pallas_call(
        _mod_kernel,
        grid=(DEPTH, n_out // tn),
        in_specs=[
            pl.BlockSpec((8, D_MODEL), lambda l, j: (0, 0)),
            pl.BlockSpec((None, D_MODEL, tn), lambda l, j: (l, 0, j)),
            pl.BlockSpec((None, 1, tn), lambda l, j: (l, 0, j)),
        ],
        out_specs=pl.BlockSpec((None, 8, tn), lambda l, j: (l, 0, j)),
        out_shape=jax.ShapeDtypeStruct((DEPTH, 8, n_out), F32),
        compiler_params=_params(("arbitrary", "arbitrary")),
        name="adaln_mod",
    )(cvec, w_ada, b_ada.reshape(DEPTH, 1, n_out))


def _rope(t, cos, sin):
    n = t.shape[1]
    lane = lax.broadcasted_iota(jnp.int32, t.shape, 1)
    first_half = (lane % HEAD_DIM) < (HEAD_DIM // 2)
    rot = jnp.where(first_half, pltpu.roll(t, n - HEAD_DIM // 2, 1), pltpu.roll(t, HEAD_DIM // 2, 1))
    reps = n // LANES
    return t * jnp.tile(cos, (1, reps)) + rot * jnp.tile(sin, (1, reps))


class _SideCasts:
    def __init__(self, stacked_weights, layer, n_steps, step_index):
        self.shapes2d = [w.shape[1:] for w in stacked_weights]
        self.arrays = [w.reshape(w.shape[0], n_steps, w.shape[1] // n_steps, w.shape[2]) for w in stacked_weights]
        self.specs = [pl.BlockSpec((None, None) + a.shape[2:], lambda *g: (layer, step_index(*g), 0, 0))
                      for a in self.arrays]
        self.out_specs = [pl.BlockSpec((None,) + a.shape[2:], lambda *g: (step_index(*g), 0, 0))
                          for a in self.arrays]
        self.out_shapes = [jax.ShapeDtypeStruct(a.shape[1:], BF16) for a in self.arrays]

    def finish(self, outs):
        return [o.reshape(s) for o, s in zip(outs, self.shapes2d)]


def _run_side_casts(in_refs, out_refs):
    for w_ref, o_ref in zip(in_refs, out_refs):
        o_ref[...] = w_ref[...].astype(BF16)


N_PROJ_INPUTS = 8
N_PROJ_OUTPUTS = 12


def _proj_kernel(*refs, rope):
    (x_ref, mod_ref, gain_ref, w_ref, wg_ref, bg_ref, cos_ref, sin_ref) = refs[:N_PROJ_INPUTS]
    n_side = (len(refs) - N_PROJ_INPUTS - N_PROJ_OUTPUTS) // 2
    outs = refs[N_PROJ_INPUTS + n_side:]
    (aq_ref, ak_ref, av_ref, gq_ref, gk_ref, gv_ref, gr_ref, nq_ref, nk_ref, nv_ref,
     gf_ref, gb_ref) = outs[:N_PROJ_OUTPUTS]
    _run_side_casts(refs[N_PROJ_INPUTS:N_PROJ_INPUTS + n_side], outs[N_PROJ_OUTPUTS:])
    scale = ATTN_Q_SCALE
    n_g = B_HEADS * B_DK
    h = _norm_mod(x_ref[...], gain_ref[...], mod_ref[0:1, :], mod_ref[1:2, :]).astype(BF16)

    def seg(s):
        return _dot(h, w_ref[:, s[0]:s[1]])

    aq = seg(SEG_AQ)
    akv = seg(SEG_AKV)
    z = _dot(seg(SEG_GA).astype(BF16), wg_ref[...]) + bg_ref[...]
    ak = akv[:, :LANES]
    if rope:
        cos = cos_ref[...]
        sin = sin_ref[...]
        aq = _rope(aq, cos, sin)
        ak = _rope(ak, cos, sin)
    aq_ref[...] = (aq * scale).astype(BF16)
    ak_ref[...] = ak.astype(BF16)
    av_ref[...] = akv[:, LANES:].astype(BF16)
    log_sig = jnp.minimum(z, 0.0) - jnp.log1p(jnp.exp(-jnp.abs(z)))
    g = log_sig / B_GATE_NORM
    gf_ref[...] = g[:, :n_g]
    gb_ref[...] = g[:, n_g:]
    gq_ref[...] = seg(SEG_GQ) * (B_DK ** -0.5)
    gk_ref[...] = seg(SEG_GK)
    gv_ref[...] = seg(SEG_GV).astype(BF16)
    gr_ref[...] = seg(SEG_GR)
    nq_ref[...] = (seg(SEG_NQ) * scale).astype(BF16)
    nk_ref[...] = seg(SEG_NK).astype(BF16)
    nv_ref[...] = seg(SEG_NV).astype(BF16)


def _project(x, mod_rows, gain, w_perm, w_gate, b_gate, cos, sin, *, rope, tm, cast_weights=(), layer=0):
    bsz, seq, _ = x.shape
    n_tiles = seq // tm
    per_batch_mod = mod_rows.shape[0] > 1
    mod_map = (lambda b, i: (b, 0, 0)) if per_batch_mod else (lambda b, i: (0, 0, 0))
    widths = [(512, BF16), (128, BF16), (128, BF16), (256, F32), (256, F32), (512, BF16), (512, F32),
              (512, BF16), (512, BF16), (512, BF16), (256, F32), (256, F32)]
    assert len(widths) == N_PROJ_OUTPUTS
    tile = lambda w: pl.BlockSpec((None, tm, w), lambda b, i: (b, i, 0))
    side = _SideCasts(cast_weights, layer, bsz * n_tiles, lambda b, i: b * n_tiles + i)
    outs = pl.pallas_call(
        functools.partial(_proj_kernel, rope=rope),
        grid=(bsz, n_tiles),
        in_specs=[
            tile(D_MODEL),
            pl.BlockSpec((None, 6, D_MODEL), mod_map),
            _const_spec((1, D_MODEL)),
            _const_spec((D_MODEL, IN_PADDED)),
            _const_spec((LANES, 2 * B_HEADS * B_DK)),
            _const_spec((1, 2 * B_HEADS * B_DK)),
            pl.BlockSpec((tm, LANES), lambda b, i: (i, 0)),
            pl.BlockSpec((tm, LANES), lambda b, i: (i, 0)),
        ] + side.specs,
        out_specs=[tile(w) for w, _ in widths] + side.out_specs,
        out_shape=[jax.ShapeDtypeStruct((bsz, seq, w), dt) for w, dt in widths] + side.out_shapes,
        compiler_params=_params(("parallel", "arbitrary")),
        name="in_proj_rope" if rope else "in_proj_ctx",
    )(x, mod_rows, gain, w_perm, w_gate, b_gate, cos, sin, *side.arrays)
    return outs[:N_PROJ_OUTPUTS], side.finish(outs[N_PROJ_OUTPUTS:])


class _GlaPrep:
    def __init__(self, q_ref, k_ref, g_ref, reverse):
        t = q_ref.shape[0]
        c = GLA_CHUNK
        row = lax.broadcasted_iota(jnp.int32, (t, t), 0)
        col = lax.broadcasted_iota(jnp.int32, (t, t), 1)
        same_chunk = (row // c) == (col // c)
        tri = (same_chunk & ((col >= row) if reverse else (col <= row))).astype(BF16)
        self.reverse = reverse
        self.q = q_ref[...]
        self.k = k_ref[...]
        self.g_parts = _split3(g_ref[...])
        self.b = _dot_01(tri, self.g_parts)
        b = self.b
        self.totals = [b[ci * c:ci * c + 1, :] if reverse else b[ci * c + c - 1:ci * c + c, :]
                       for ci in range(t // c)]
        self.b_end = jnp.concatenate([jnp.broadcast_to(e, (c, e.shape[1])) for e in self.totals], axis=0)
        self.q_state = (self.q * jnp.exp(b)).astype(BF16)
        self.k_tail = (self.k * jnp.exp(self.b_end - b)).astype(BF16)
        self.decay = [jnp.exp(e) for e in self.totals]


def _split3(x):
    hi = x.astype(BF16)
    rest = x - hi.astype(F32)
    mid = rest.astype(BF16)
    return hi, mid, (rest - mid.astype(F32)).astype(BF16)


def _dot_01(m01, parts):
    return _dot(m01, parts[0]) + _dot(m01, parts[1]) + _dot(m01, parts[2])


def _gla_pair_scores(q_w, k_w, keep):
    c = GLA_CHUNK
    out = []
    for ci in range(q_w.shape[0] // c):
        rows = slice(ci * c, (ci + 1) * c)
        for p in range(B_HEADS // 2):
            ps = slice(p * LANES, (p + 1) * LANES)
            out.append(jnp.where(keep, _dot_nt(_split_head_pair(q_w[rows, ps]), k_w[rows, ps]), 0.0))
    return out


def _gla_scores_fast(prep):
    c = GLA_CHUNK
    qi = lax.broadcasted_iota(jnp.int32, (2 * c, c), 0) % c
    kj = lax.broadcasted_iota(jnp.int32, (2 * c, c), 1)
    keep = (kj >= qi) if prep.reverse else (kj <= qi)
    ref = 0.5 * prep.b_end
    q_in = (prep.q * jnp.exp(prep.b - ref)).astype(BF16)
    k_in = (prep.k * jnp.exp(ref - prep.b)).astype(BF16)
    return _gla_pair_scores(q_in, k_in, keep)


def _gla_scores_safe(prep):
    t = prep.q.shape[0]
    c = GLA_CHUNK
    rev = prep.reverse
    row = lax.broadcasted_iota(jnp.int32, (t, t), 0)
    col = lax.broadcasted_iota(jnp.int32, (t, t), 1)
    qi = lax.broadcasted_iota(jnp.int32, (2 * c, c), 0) % c
    kj = lax.broadcasted_iota(jnp.int32, (2 * c, c), 1)
    scores = _gla_pair_scores(prep.q.astype(BF16), prep.k.astype(BF16), qi == kj)
    s = c // 2
    while s >= 1:
        if rev:
            bnd = row - row % (2 * s) + s
            span = ((col >= bnd) & (col < row)) | ((col >= row) & (col < bnd))
        else:
            bnd = row - row % (2 * s) + s - 1
            span = ((col > bnd) & (col <= row)) | ((col > row) & (col <= bnd))
        e = jnp.exp(_dot_01(span.astype(BF16), prep.g_parts))
        q_side = ((qi % (2 * s)) < s) if rev else ((qi % (2 * s)) >= s)
        k_side = ((kj % (2 * s)) >= s) if rev else ((kj % (2 * s)) < s)
        keep = ((qi // (2 * s)) == (kj // (2 * s))) & q_side & k_side
        level = _gla_pair_scores((prep.q * e).astype(BF16), (prep.k * e).astype(BF16), keep)
        scores = [acc + a for acc, a in zip(scores, level)]
        s //= 2
    return scores


def _gla_local(scores, k_tail, v_ref):
    c = GLA_CHUNK
    n_pairs = B_HEADS // 2
    intra, upd = [], []
    for ci in range(k_tail.shape[0] // c):
        rows = slice(ci * c, (ci + 1) * c)
        intra_c, upd_c = [], []
        for p in range(n_pairs):
            a = scores[ci * n_pairs + p].astype(BF16)
            kt = _split_head_pair(k_tail[rows, p * LANES:(p + 1) * LANES])
            u = None
            for half in range(2):
                h = 2 * p + half
                vh = v_ref[rows, h * B_DV:(h + 1) * B_DV]
                intra_c.append(_dot(a[half * c:(half + 1) * c], vh))
                uh = _dot_tn(vh, kt[half * c:(half + 1) * c])
                u = uh if u is None else u + uh
            upd_c.append(u)
        intra.append(intra_c)
        upd.append(upd_c)
    return intra, upd


def _gla_states(s_ref, decay, upd, reverse):
    n = len(upd)
    order = range(n - 1, -1, -1) if reverse else range(n)
    starts = [None] * n
    s = [s_ref[p] for p in range(B_HEADS // 2)]
    for ci in order:
        starts[ci] = [sp.astype(BF16) for sp in s]
        s = [s[p] * decay[ci][:, p * LANES:(p + 1) * LANES] + upd[ci][p] for p in range(B_HEADS // 2)]
    for p in range(B_HEADS // 2):
        s_ref[p] = s[p]
    return starts


def _gla_finish(o_ref, q_state, starts, intra):
    c = GLA_CHUNK
    for ci in range(len(intra)):
        rows = slice(ci * c, (ci + 1) * c)
        for p in range(B_HEADS // 2):
            ps = slice(p * LANES, (p + 1) * LANES)
            inter = _dot_nt(_split_head_pair(q_state[rows, ps]), starts[ci][p])
            for half in range(2):
                h = 2 * p + half
                o_ref[rows, h * B_DV:(h + 1) * B_DV] = intra[ci][h] + inter[half * c:(half + 1) * c]


def _gla_kernel(qf_ref, kf_ref, vf_ref, gf_ref, qb_ref, kb_ref, vb_ref, gb_ref, s0f_ref, s0b_ref,
                of_ref, ob_ref, sf_out_ref, sb_out_ref, sf_ref, sb_ref):
    i = pl.program_id(1)

    @pl.when(i == 0)
    def _():
        sf_ref[...] = s0f_ref[...]
        sb_ref[...] = s0b_ref[...]

    fwd = _GlaPrep(qf_ref, kf_ref, gf_ref, False)
    bwd = _GlaPrep(qb_ref, kb_ref, gb_ref, True)

    def tile_body(score_fn):
        scores_f = score_fn(fwd)
        scores_b = score_fn(bwd)
        intra_f, upd_f = _gla_local(scores_f, fwd.k_tail, vf_ref)
        intra_b, upd_b = _gla_local(scores_b, bwd.k_tail, vb_ref)
        starts_f = _gla_states(sf_ref, fwd.decay, upd_f, False)
        starts_b = _gla_states(sb_ref, bwd.decay, upd_b, True)
        _gla_finish(of_ref, fwd.q_state, starts_f, intra_f)
        _gla_finish(ob_ref, bwd.q_state, starts_b, intra_b)

    in_fast_range = -jnp.min(jnp.concatenate(fwd.totals + bwd.totals, axis=0)) < GLA_FAST_RANGE

    @pl.when(in_fast_range)
    def _():
        tile_body(_gla_scores_fast)

    @pl.when(jnp.logical_not(in_fast_range))
    def _():
        tile_body(_gla_scores_safe)

    @pl.when(i == pl.num_programs(1) - 1)
    def _():
        sf_out_ref[...] = sf_ref[...]
        sb_out_ref[...] = sb_ref[...]


GLA_STATE_SHAPE = (B_HEADS // 2, B_DV, 2 * B_DK)


def _gla(q, k, v, gf, gb, s0f, s0b, *, tile):
    bsz, seq, _ = q.shape
    n = seq // tile
    fwd = lambda w: pl.BlockSpec((None, tile, w), lambda b, i: (b, i, 0))
    bwd = lambda w: pl.BlockSpec((None, tile, w), lambda b, i: (b, n - 1 - i, 0))
    st = pl.BlockSpec((None,) + GLA_STATE_SHAPE, lambda b, i: (b, 0, 0, 0))
    wk, wv = B_HEADS * B_DK, B_HEADS * B_DV
    st_shape = jax.ShapeDtypeStruct((bsz,) + GLA_STATE_SHAPE, F32)
    return pl.pallas_call(
        _gla_kernel,
        grid=(bsz, n),
        in_specs=[fwd(wk), fwd(wk), fwd(wv), fwd(wk), bwd(wk), bwd(wk), bwd(wv), bwd(wk), st, st],
        out_specs=[fwd(wv), bwd(wv), st, st],
        out_shape=[jax.ShapeDtypeStruct((bsz, seq, wv), F32), jax.ShapeDtypeStruct((bsz, seq, wv), F32),
                   st_shape, st_shape],
        scratch_shapes=[pltpu.VMEM(GLA_STATE_SHAPE, F32), pltpu.VMEM(GLA_STATE_SHAPE, F32)],
        compiler_params=_params(("parallel", "arbitrary")),
        name="gla_scan",
    )(q, k, v, gf, q, k, v, gb, s0f, s0b)


def _softmax_parts(scores, sink_tile):
    def lane_tiles(blocks):
        return [b[:, j:j + LANES] for b in blocks for j in range(0, b.shape[1], LANES)]

    tiles = lane_tiles(scores)
    if sink_tile is not None:
        tiles.append(sink_tile)
    m = functools.reduce(jnp.maximum, tiles).max(axis=-1, keepdims=True)
    ps = [jnp.exp2(s - m) for s in scores]
    acc = functools.reduce(jnp.add, lane_tiles(ps))
    if sink_tile is not None:
        lane = lax.broadcasted_iota(jnp.int32, sink_tile.shape, 1)
        acc = acc + jnp.where(lane == 0, jnp.exp2(sink_tile - m), 0.0)
    return ps, acc.sum(axis=-1, keepdims=True)


def _split_head_pair(t):
    lane = lax.broadcasted_iota(jnp.int32, t.shape, 1)
    zero = jnp.zeros_like(t)
    return jnp.concatenate([jnp.where(lane < HEAD_DIM, t, zero), jnp.where(lane >= HEAD_DIM, t, zero)], axis=0)


def _merge_head_pair(o):
    m = o.shape[0] // 2
    lane = lax.broadcasted_iota(jnp.int32, (m, LANES), 1)
    return jnp.where(lane < HEAD_DIM, o[:m], o[m:])


A_HEAD_ORDER = (0, 4, 1, 5, 2, 6, 3, 7)


def _win_kernel(sink_ref, q_ref, kp_ref, kc_ref, kn_ref, vp_ref, vc_ref, vn_ref, kx_ref, vx_ref, o_ref):
    n = pl.program_id(1)
    nb = pl.num_programs(1)
    n_tiles = A_HEADS // 2
    q = jnp.concatenate([_split_head_pair(q_ref[:, j * LANES:(j + 1) * LANES]) for j in range(n_tiles)], axis=0)
    sink_tile = jnp.concatenate([jnp.full((A_BLOCK, LANES), sink_ref[h], F32) for h in A_HEAD_ORDER], axis=0)
    qi = lax.broadcasted_iota(jnp.int32, (A_BLOCK, A_BLOCK), 0)
    kj = lax.broadcasted_iota(jnp.int32, (A_BLOCK, A_BLOCK), 1)
    neg = jnp.full((A_BLOCK, A_BLOCK), -jnp.inf, F32)
    zero = jnp.zeros((A_BLOCK, A_BLOCK), F32)
    prev_mask = jnp.where(n > 0, jnp.where(kj >= qi, zero, neg), neg)
    next_mask = jnp.where(n < nb - 1, jnp.where(kj <= qi, zero, neg), neg)
    s_p = _dot_nt(q, kp_ref[...]) + jnp.tile(prev_mask, (A_HEADS, 1))
    s_c = _dot_nt(q, kc_ref[...])
    s_n = _dot_nt(q, kn_ref[...]) + jnp.tile(next_mask, (A_HEADS, 1))
    s_x = _dot_nt(q, kx_ref[...])
    (p_p, p_c, p_n, p_x), denom = _softmax_parts([s_p, s_c, s_n, s_x], sink_tile)
    o = (_dot(p_p.astype(BF16), vp_ref[...]) + _dot(p_c.astype(BF16), vc_ref[...])
         + _dot(p_n.astype(BF16), vn_ref[...]) + _dot(p_x.astype(BF16), vx_ref[...]))
    o = o / denom
    for j in range(n_tiles):
        o_ref[:, j * LANES:(j + 1) * LANES] = _merge_head_pair(
            o[j * 2 * A_BLOCK:(j + 1) * 2 * A_BLOCK]).astype(o_ref.dtype)


def _window_attention(sink, q, k, v, k_ctx, v_ctx):
    bsz, seq, wq = q.shape
    nb = seq // A_BLOCK
    wkv = k.shape[-1]
    n_ctx = k_ctx.shape[1]
    kv_spec = lambda f: pl.BlockSpec((None, A_BLOCK, wkv), lambda b, n: (b, f(n), 0))
    prev = lambda n: jnp.maximum(n - 1, 0)
    cur = lambda n: n
    nxt = lambda n: jnp.minimum(n + 1, nb - 1)
    ctx_spec = pl.BlockSpec((None, n_ctx, wkv), lambda b, n: (b, 0, 0))
    return pl.pallas_call(
        _win_kernel,
        grid=(bsz, nb),
        in_specs=[
            pl.BlockSpec(memory_space=pltpu.SMEM),
            pl.BlockSpec((None, A_BLOCK, wq), lambda b, n: (b, n, 0)),
            kv_spec(prev), kv_spec(cur), kv_spec(nxt),
            kv_spec(prev), kv_spec(cur), kv_spec(nxt),
            ctx_spec, ctx_spec,
        ],
        out_specs=pl.BlockSpec((None, A_BLOCK, wq), lambda b, n: (b, n, 0)),
        out_shape=jax.ShapeDtypeStruct((bsz, seq, wq), BF16),
        compiler_params=_params(("parallel", "arbitrary")),
        name="window_attn",
    )(sink, q, k, k, k, v, v, v, k_ctx, v_ctx)


def _dense_kernel(sink_ref, q_ref, k_ref, v_ref, o_ref, *, head_order, shared_kv):
    n_q = q_ref.shape[0]
    for j in range(q_ref.shape[1] // LANES):
        qs = slice(j * LANES, (j + 1) * LANES)
        ks = slice(0, LANES) if shared_kv else qs
        s = _dot_nt(_split_head_pair(q_ref[:, qs]), k_ref[:, ks])
        sink_tile = None
        if head_order is not None:
            sink_tile = jnp.concatenate(
                [jnp.full((n_q, LANES), sink_ref[head_order[2 * j + half]], F32) for half in range(2)], axis=0)
        (p,), denom = _softmax_parts([s], sink_tile)
        o_ref[:, qs] = _merge_head_pair(_dot(p.astype(BF16), v_ref[:, ks]) / denom).astype(o_ref.dtype)


def _dense_attention(sink, q, k, v, *, head_order, shared_kv):
    bsz, n_q, wq = q.shape
    wkv = k.shape[-1]
    full = lambda w: pl.BlockSpec((None, n_q, w), lambda b: (b, 0, 0))
    return pl.pallas_call(
        functools.partial(_dense_kernel, head_order=head_order, shared_kv=shared_kv),
        grid=(bsz,),
        in_specs=[pl.BlockSpec(memory_space=pltpu.SMEM), full(wq), full(wkv), full(wkv)],
        out_specs=full(wq),
        out_shape=jax.ShapeDtypeStruct((bsz, n_q, wq), BF16),
        compiler_params=_params(("parallel",)),
        name="ctx_dense_attn_sink" if shared_kv else "ctx_dense_attn",
    )(sink, q, k, v)


NA_ROWS_PER_STEP = 8


N_NA_INPUTS = 6


def _na_kernel(*refs):
    q_ref, k_ref, v_ref, kx_ref, vx_ref, bias_ref = refs[:N_NA_INPUTS]
    n_side = (len(refs) - N_NA_INPUTS - 1) // 2
    o_ref = refs[N_NA_INPUTS + n_side]
    _run_side_casts(refs[N_NA_INPUTS:N_NA_INPUTS + n_side], refs[N_NA_INPUTS + n_side + 1:])
    step = pl.program_id(1)
    grid_rows = k_ref.shape[0] // GRID_W
    n_keys = C_WIN_ROWS * GRID_W
    n_pairs = C_HEADS // 2
    for rr in range(NA_ROWS_PER_STEP):
        r = step * NA_ROWS_PER_STEP + rr
        row0 = jnp.clip(r - C_WIN_ROWS // 2, 0, grid_rows - C_WIN_ROWS)
        d_row0 = row0 - r + (C_WIN_ROWS - 1)
        key0 = pl.multiple_of(row0 * GRID_W, GRID_W)
        qrows = slice(rr * GRID_W, (rr + 1) * GRID_W)
        s_l, s_x = [], []
        for p in range(n_pairs):
            ps = slice(p * LANES, (p + 1) * LANES)
            q2 = _split_head_pair(q_ref[qrows, ps])
            s_l.append(_dot_nt(q2, k_ref[pl.ds(key0, n_keys), ps]))
            s_x.append(_dot_nt(q2, kx_ref[:, ps]))
        bias = jnp.concatenate(
            [bias_ref[d_row0 + 2 * j].reshape(C_HEADS * GRID_W, LANES) for j in range(C_WIN_ROWS // 2)], axis=1)
        (p_l, p_x), denom = _softmax_parts([jnp.concatenate(s_l, axis=0) + bias, jnp.concatenate(s_x, axis=0)], None)
        p_l = p_l.astype(BF16)
        p_x = p_x.astype(BF16)
        for p in range(n_pairs):
            ps = slice(p * LANES, (p + 1) * LANES)
            pr = slice(p * 2 * GRID_W, (p + 1) * 2 * GRID_W)
            o = (_dot(p_l[pr], v_ref[pl.ds(key0, n_keys), ps]) + _dot(p_x[pr], vx_ref[:, ps])) / denom[pr]
            o_ref[qrows, ps] = _merge_head_pair(o).astype(o_ref.dtype)


def _neighbourhood_attention(q, k, v, k_ctx, v_ctx, bias_tbl, cast_weights=(), layer=0):
    bsz, seq, w = q.shape
    n_ctx = k_ctx.shape[1]
    tq = NA_ROWS_PER_STEP * GRID_W
    n_tiles = seq // tq
    whole = pl.BlockSpec((None, seq, w), lambda b, i: (b, 0, 0))
    ctx_spec = pl.BlockSpec((None, n_ctx, w), lambda b, i: (b, 0, 0))
    side = _SideCasts(cast_weights, layer, bsz * n_tiles, lambda b, i: b * n_tiles + i)
    outs = pl.pallas_call(
        _na_kernel,
        grid=(bsz, n_tiles),
        in_specs=[pl.BlockSpec((None, tq, w), lambda b, i: (b, i, 0)), whole, whole, ctx_spec, ctx_spec,
                  _const_spec(bias_tbl.shape)] + side.specs,
        out_specs=[pl.BlockSpec((None, tq, w), lambda b, i: (b, i, 0))] + side.out_specs,
        out_shape=[jax.ShapeDtypeStruct((bsz, seq, w), BF16)] + side.out_shapes,
        compiler_params=_params(("parallel", "arbitrary")),
        name="neighbourhood_attn",
    )(q, k, v, k_ctx, v_ctx, bias_tbl, *side.arrays)
    return outs[0], side.finish(outs[1:])


def _na_bias_table(rpb):
    qcol = jnp.arange(GRID_W)[:, None]
    kcol = jnp.arange(GRID_W)[None, :]
    wstart = jnp.clip(qcol - C_WIN_COLS // 2, 0, GRID_W - C_WIN_COLS)
    ok = (kcol >= wstart) & (kcol < wstart + C_WIN_COLS)
    n_off = C_WIN_COLS - 1
    period = 2 * GRID_W
    vec = jnp.concatenate([rpb[..., n_off:].astype(F32),
                           jnp.zeros(rpb.shape[:2] + (period - 2 * n_off - 1,), F32),
                           rpb[..., :n_off].astype(F32)], axis=-1)
    toep = jnp.tile(vec, (1, 1, GRID_W))[..., :GRID_W * (period - 1)]
    toep = toep.reshape(rpb.shape[:2] + (GRID_W, period - 1))[..., :GRID_W]
    tbl = jnp.where(ok[None, None], toep * LOG2E, -jnp.inf)
    tbl = jnp.moveaxis(tbl, 1, 0)
    return jnp.concatenate([tbl[:-1], tbl[1:]], axis=-1)


def _merge_kernel(x_ref, mod_ref, gain_ref, ya_ref, of_ref, ob_ref, gr_ref, yc_ref, gn_ref,
                  wm_ref, bm_ref, wa_ref, wb_ref, wc_ref, wo_ref, o_ref):
    d = D_MODEL
    gn = gn_ref[...]
    x = x_ref[...]
    h = _norm_mod(x, gain_ref[...], mod_ref[0:1, :], mod_ref[1:2, :]).astype(BF16)
    o_sum = of_ref[...] + ob_ref[...]
    parts = []
    for hh in range(B_HEADS):
        oh = o_sum[:, hh * B_DV:(hh + 1) * B_DV]
        parts.append((oh * lax.rsqrt(jnp.mean(oh * oh, axis=-1, keepdims=True) + EPS)) * gn)
    y_b = (jnp.concatenate(parts, axis=1) * _silu(gr_ref[...])).astype(BF16)
    mixed = None
    for j, (y, w_ref) in enumerate(((ya_ref[...], wa_ref), (y_b, wb_ref), (yc_ref[...], wc_ref))):
        gate = _sigmoid(_dot(h, wm_ref[:, j * d:(j + 1) * d]) + bm_ref[:, j * d:(j + 1) * d])
        term = gate * _dot(y, w_ref[...])
        mixed = term if mixed is None else mixed + term
    o_ref[...] = x + mod_ref[2:3, :] * _dot(mixed.astype(BF16), wo_ref[...])


def _merge(x, mod_rows, gain, y_a, o_f, o_b, g_r, y_c, gla_gain, w_merge, b_merge, w_a, w_b, w_c, w_out, *, tm):
    bsz, seq, d = x.shape
    per_batch_mod = mod_rows.shape[0] > 1
    mod_map = (lambda b, i: (b, 0, 0)) if per_batch_mod else (lambda b, i: (0, 0, 0))
    tile = lambda w: pl.BlockSpec((None, tm, w), lambda b, i: (b, i, 0))
    wy = y_a.shape[-1]
    return pl.pallas_call(
        _merge_kernel,
        grid=(bsz, seq // tm),
        in_specs=[tile(d), pl.BlockSpec((None, 6, d), mod_map), _const_spec((1, d)),
                  tile(wy), tile(wy), tile(wy), tile(wy), tile(wy), _const_spec((1, B_DV)),
                  _const_spec(w_merge.shape), _const_spec(b_merge.shape),
                  _const_spec(w_a.shape), _const_spec(w_b.shape), _const_spec(w_c.shape),
                  _const_spec(w_out.shape)],
        out_specs=tile(d),
        out_shape=jax.ShapeDtypeStruct((bsz, seq, d), F32),
        compiler_params=_params(("parallel", "arbitrary")),
        name="merge_out",
    )(x, mod_rows, gain, y_a, o_f, o_b, g_r, y_c, gla_gain, w_merge, b_merge, w_a, w_b, w_c, w_out)


FFN_CHUNKS = ((0, 1024), (1024, 2048), (2048, FFN_HIDDEN))
FFN_SUB_ROWS = 256


def _ffn_kernel(x_ref, mod_ref, gain_ref, w1_ref, w2_ref, fg_ref, o_ref, *, final):
    for r0 in range(0, x_ref.shape[0], FFN_SUB_ROWS):
        rows = slice(r0, r0 + FFN_SUB_ROWS)
        x = x_ref[rows, :]
        h = _norm_mod(x, gain_ref[...], mod_ref[3:4, :], mod_ref[4:5, :]).astype(BF16)
        acc = None
        for c0, c1 in FFN_CHUNKS:
            gate = _dot(h, w1_ref[:, c0:c1])
            up = _dot(h, w1_ref[:, FFN_HIDDEN + c0:FFN_HIDDEN + c1])
            part = _dot((_silu(gate) * up).astype(BF16), w2_ref[c0:c1, :])
            acc = part if acc is None else acc + part
        y = x + mod_ref[5:6, :] * acc
        if final:
            y = (y * lax.rsqrt(jnp.mean(y * y, axis=-1, keepdims=True) + EPS)) * fg_ref[...]
        o_ref[rows, :] = y


def _ffn(x, mod_rows, gain, w1, w2, final_gain, *, final, tm):
    bsz, seq, d = x.shape
    per_batch_mod = mod_rows.shape[0] > 1
    mod_map = (lambda b, i: (b, 0, 0)) if per_batch_mod else (lambda b, i: (0, 0, 0))
    tile = pl.BlockSpec((None, tm, d), lambda b, i: (b, i, 0))
    return pl.pallas_call(
        functools.partial(_ffn_kernel, final=final),
        grid=(bsz, seq // tm),
        in_specs=[tile, pl.BlockSpec((None, 6, d), mod_map), _const_spec((1, d)),
                  _const_spec(w1.shape), _const_spec(w2.shape), _const_spec((1, d))],
        out_specs=tile,
        out_shape=jax.ShapeDtypeStruct((bsz, seq, d), F32),
        compiler_params=_params(("parallel", "arbitrary")),
        name="ffn_final" if final else "ffn",
    )(x, mod_rows, gain, w1, w2, final_gain)


def _rope_tables(seq):
    t = jnp.arange(seq)
    row = (t // GRID_W).astype(F32)
    col = (t % GRID_W).astype(F32)
    n_freq = HEAD_DIM // 4
    inv = ROPE_BASE ** (-jnp.arange(n_freq, dtype=F32) / n_freq)
    ang = jnp.concatenate([row[:, None] * inv[None], col[:, None] * inv[None]], axis=-1)
    cos, sin = jnp.cos(ang), jnp.sin(ang)
    return jnp.tile(cos, (1, 4)), jnp.tile(jnp.concatenate([-sin, sin], axis=-1), (1, 2))


def _permute_w_in(w):
    ga0 = 2304
    ga1 = ga0 + 2 * B_GATE_RANK
    n_aq = A_HEADS * HEAD_DIM
    aq = w[:, :n_aq].reshape(w.shape[0], A_HEADS, HEAD_DIM)[:, jnp.array(A_HEAD_ORDER)].reshape(w.shape[0], n_aq)
    pad = jnp.zeros((w.shape[0], IN_PADDED - w.shape[1]), w.dtype)
    return jnp.concatenate([aq, w[:, n_aq:ga0], w[:, ga1:], w[:, ga0:ga1], pad], axis=1).astype(BF16)


def _permute_w_branch_a(w):
    return w.reshape(A_HEADS, HEAD_DIM, w.shape[1])[jnp.array(A_HEAD_ORDER)].reshape(w.shape).astype(BF16)


def _gate_weights(w_fwd, b_fwd, w_bwd, b_bwd):
    n = B_HEADS * B_DK
    r = B_GATE_RANK
    w = jnp.zeros((LANES, 2 * n), F32)
    w = w.at[:r, :n].set(w_fwd).at[r:2 * r, n:].set(w_bwd)
    return w.astype(BF16), jnp.concatenate([b_fwd, b_bwd])[None, :]


def kernel(x, c, ctx, c_ctx, w_ada, b_ada, norm_mix, w_in, attn_sink, gla_gate_w_fwd, gla_gate_b_fwd,
           gla_gate_w_bwd, gla_gate_b_bwd, gla_norm, na_rpb, w_branch_a, w_branch_b, w_branch_c,
           w_merge, b_merge, w_out, norm_ffn, w_ffn_in, w_ffn_out, final_norm):
    bsz, seq, d = x.shape
    n_ctx = ctx.shape[1]
    cos, sin = _rope_tables(seq)

    cvec = jnp.zeros((8, d), F32).at[:bsz].set(c).at[bsz].set(c_ctx)
    mod = _modulation(cvec, w_ada, b_ada).reshape(DEPTH, 8, 6, d)

    xc = ctx
    zeros_state = jnp.zeros((bsz,) + GLA_STATE_SHAPE, F32)
    for l in range(DEPTH):
        last = l == DEPTH - 1
        mod_x = mod[l, :bsz]
        mod_c = mod[l, bsz:bsz + 1]
        gain_m = norm_mix[l][None, :]
        w_perm = _permute_w_in(w_in[l])
        w_gate, b_gate = _gate_weights(gla_gate_w_fwd[l], gla_gate_b_fwd[l], gla_gate_w_bwd[l], gla_gate_b_bwd[l])

        (caq, cak, cav, cgq, cgk, cgv, cgr, cnq, cnk, cnv, cgf, cgb), _ = _project(
            xc, mod_c, gain_m, w_perm, w_gate, b_gate, cos, sin, rope=False, tm=n_ctx)
        (aq, ak, av, gq, gk, gv, gr, nq, nk, nv, gf, gb), (wm_bf, wb_bf, wc_bf, wo_bf) = _project(
            x, mod_x, gain_m, w_perm, w_gate, b_gate, cos, sin, rope=True, tm=512,
            cast_weights=(w_merge, w_branch_b, w_branch_c, w_out), layer=l)

        co_f, co_b, s_f, s_b = _gla(cgq, cgk, cgv, cgf, cgb, zeros_state, zeros_state, tile=n_ctx)
        o_f, o_b, _, _ = _gla(gq, gk, gv, gf, gb, s_f, s_b, tile=256)

        sink = attn_sink[l] * LOG2E
        y_a = _window_attention(sink, aq, ak, av, cak, cav)
        y_c, (w1, w2) = _neighbourhood_attention(nq, nk, nv, cnk, cnv, _na_bias_table(na_rpb[l]),
                                                 cast_weights=(w_ffn_in, w_ffn_out), layer=l)

        merge_w = (gla_norm[l][None, :], wm_bf, b_merge[l][None, :],
                   _permute_w_branch_a(w_branch_a[l]), wb_bf, wc_bf, wo_bf)
        x = _merge(x, mod_x, gain_m, y_a, o_f, o_b, gr, y_c, *merge_w, tm=512)

        gain_f = norm_ffn[l][None, :]
        fg = final_norm[None, :]
        if not last:
            yc_a = _dense_attention(sink, caq, cak, cav, head_order=A_HEAD_ORDER, shared_kv=True)
            yc_c = _dense_attention(sink, cnq, cnk, cnv, head_order=None, shared_kv=False)
            xc = _merge(xc, mod_c, gain_m, yc_a, co_f, co_b, cgr, yc_c, *merge_w, tm=n_ctx)
            xc = _ffn(xc, mod_c, gain_f, w1, w2, fg, final=False, tm=n_ctx)
        x = _ffn(x, mod_x, gain_f, w1, w2, fg, final=last, tm=1024)
    return x
```

```python
import functools

import jax
import jax.numpy as jnp
from jax import lax
from jax.experimental import pallas as pl
from jax.experimental.pallas import tpu as pltpu

F32 = jnp.float32
BF16 = jnp.bfloat16

D_MODEL = 1024
DEPTH = 2
GRID_W = 64
HEAD_DIM = 64
EPS = 1e-6
ROPE_BASE = 10000.0
A_HEADS = 8
A_KV_HEADS = 2
A_BLOCK = 128
B_HEADS = 4
B_DK = 64
B_DV = 128
B_GATE_RANK = 16
B_GATE_NORM = 16.0
C_HEADS = 8
C_WIN_ROWS = 8
C_WIN_COLS = 16
FFN_HIDDEN = 2816

VMEM_LIMIT_BYTES = 56 * 1024 * 1024
LANES = 128

SEG_AQ = (0, 512)
SEG_AKV = (512, 768)
SEG_GQ = (768, 1024)
SEG_GK = (1024, 1280)
SEG_GV = (1280, 1792)
SEG_GR = (1792, 2304)
SEG_NQ = (2304, 2816)
SEG_NK = (2816, 3328)
SEG_NV = (3328, 3840)
SEG_GA = (3840, 3968)
IN_PADDED = 3968

LOG2E = 1.4426950408889634
ATTN_Q_SCALE = HEAD_DIM ** -0.5 * LOG2E

GLA_CHUNK = 64
GLA_FAST_RANGE = 120.0


def _params(sem):
    return pltpu.CompilerParams(dimension_semantics=sem, vmem_limit_bytes=VMEM_LIMIT_BYTES)


def _const_spec(shape):
    nd = len(shape)
    return pl.BlockSpec(shape, lambda *_: (0,) * nd, pipeline_mode=pl.Buffered(1))


def _sigmoid(x):
    return 1.0 / (1.0 + jnp.exp(-x))


def _silu(x):
    return x * _sigmoid(x)


def _dot(a, b):
    return jnp.dot(a, b, preferred_element_type=F32)


def _dot_nt(a, b):
    return lax.dot_general(a, b, (((1,), (1,)), ((), ())), preferred_element_type=F32)


def _dot_tn(a, b):
    return lax.dot_general(a, b, (((0,), (0,)), ((), ())), preferred_element_type=F32)


def _norm_mod(x, gain, shift, scale):
    y = x * lax.rsqrt(jnp.mean(x * x, axis=-1, keepdims=True) + EPS)
    return (y * gain) * (1.0 + scale) + shift


def _mod_kernel(c_ref, w_ref, b_ref, o_ref):
    s = _silu(c_ref[...])
    o_ref[...] = _dot(s.astype(BF16), w_ref[...].astype(BF16)) + b_ref[...]


def _modulation(cvec, w_ada, b_ada):
    tn = 1536
    n_out = w_ada.shape[-1]
    return pl.pallas_call(
        _mod_kernel,
        grid=(DEPTH, n_out // tn),
        in_specs=[
            pl.BlockSpec((8, D_MODEL), lambda l, j: (0, 0)),
            pl.BlockSpec((None, D_MODEL, tn), lambda l, j: (l, 0, j)),
            pl.BlockSpec((None, 1, tn), lambda l, j: (l, 0, j)),
        ],
        out_specs=pl.BlockSpec((None, 8, tn), lambda l, j: (l, 0, j)),
        out_shape=jax.ShapeDtypeStruct((DEPTH, 8, n_out), F32),
        compiler_params=_params(("arbitrary", "arbitrary")),
        name="adaln_mod",
    )(cvec, w_ada, b_ada.reshape(DEPTH, 1, n_out))


def _rope(t, cos, sin):
    n = t.shape[1]
    lane = lax.broadcasted_iota(jnp.int32, t.shape, 1)
    first_half = (lane % HEAD_DIM) < (HEAD_DIM // 2)
    rot = jnp.where(first_half, pltpu.roll(t, n - HEAD_DIM // 2, 1), pltpu.roll(t, HEAD_DIM // 2, 1))
    reps = n // LANES
    return t * jnp.tile(cos, (1, reps)) + rot * jnp.tile(sin, (1, reps))


class _SideCasts:
    def __init__(self, stacked_weights, layer, n_steps, step_index):
        self.shapes2d = [w.shape[1:] for w in stacked_weights]
        self.arrays = [w.reshape(w.shape[0], n_steps, w.shape[1] // n_steps, w.shape[2]) for w in stacked_weights]
        self.specs = [pl.BlockSpec((None, None) + a.shape[2:], lambda *g: (layer, step_index(*g), 0, 0))
                      for a in self.arrays]
        self.out_specs = [pl.BlockSpec((None,) + a.shape[2:], lambda *g: (step_index(*g), 0, 0))
                          for a in self.arrays]
        self.out_shapes = [jax.ShapeDtypeStruct(a.shape[1:], BF16) for a in self.arrays]

    def finish(self, outs):
        return [o.reshape(s) for o, s in zip(outs, self.shapes2d)]


def _run_side_casts(in_refs, out_refs):
    for w_ref, o_ref in zip(in_refs, out_refs):
        o_ref[...] = w_ref[...].astype(BF16)


N_PROJ_INPUTS = 8
N_PROJ_OUTPUTS = 12
CTX_ROW_TILE = 512
DENSE_SUB_ROWS = 512


def _proj_kernel(*refs, rope):
    (x_ref, mod_ref, gain_ref, w_ref, wg_ref, bg_ref, cos_ref, sin_ref) = refs[:N_PROJ_INPUTS]
    h_ref = refs[-1]
    n_side = (len(refs) - 1 - N_PROJ_INPUTS - N_PROJ_OUTPUTS) // 2
    outs = refs[N_PROJ_INPUTS + n_side:-1]
    (aq_ref, ak_ref, av_ref, gq_ref, gk_ref, gv_ref, gr_ref, nq_ref, nk_ref, nv_ref,
     gf_ref, gb_ref) = outs[:N_PROJ_OUTPUTS]
    _run_side_casts(refs[N_PROJ_INPUTS:N_PROJ_INPUTS + n_side], outs[N_PROJ_OUTPUTS:])
    scale = ATTN_Q_SCALE
    n_g = B_HEADS * B_DK
    n_rows = x_ref.shape[0]
    sub = h_ref.shape[0]

    n_chunks = 4
    chunk = sub // n_chunks

    def normed(r0, n):
        return _norm_mod(x_ref[r0:r0 + n, :], gain_ref[...], mod_ref[0:1, :], mod_ref[1:2, :]).astype(BF16)

    h = normed(0, sub)
    for r0 in range(0, n_rows, sub):
        rows = slice(r0, r0 + sub)
        has_next = r0 + sub < n_rows

        def seg(s):
            return _dot(h, w_ref[:, s[0]:s[1]])

        def norm_next_chunk(j):
            if has_next:
                h_ref[j * chunk:(j + 1) * chunk, :] = normed(r0 + sub + j * chunk, chunk)

        aq = seg(SEG_AQ)
        akv = seg(SEG_AKV)
        z = _dot(seg(SEG_GA).astype(BF16), wg_ref[...]) + bg_ref[...]
        gq_ref[rows, :] = seg(SEG_GQ) * (B_DK ** -0.5)
        norm_next_chunk(0)
        gk_ref[rows, :] = seg(SEG_GK)
        norm_next_chunk(1)
        gv_ref[rows, :] = seg(SEG_GV).astype(BF16)
        norm_next_chunk(2)
        gr_ref[rows, :] = seg(SEG_GR)
        norm_next_chunk(3)
        nq_ref[rows, :] = (seg(SEG_NQ) * scale).astype(BF16)
        ak = akv[:, :LANES]
        if rope:
            cos = cos_ref[rows, :]
            sin = sin_ref[rows, :]
            aq = _rope(aq, cos, sin)
            ak = _rope(ak, cos, sin)
        aq_ref[rows, :] = (aq * scale).astype(BF16)
        ak_ref[rows, :] = ak.astype(BF16)
        av_ref[rows, :] = akv[:, LANES:].astype(BF16)
        nk_ref[rows, :] = seg(SEG_NK).astype(BF16)
        log_sig = jnp.minimum(z, 0.0) - jnp.log1p(jnp.exp(-jnp.abs(z)))
        g = log_sig / B_GATE_NORM
        gf_ref[rows, :] = g[:, :n_g]
        gb_ref[rows, :] = g[:, n_g:]
        nv_ref[rows, :] = seg(SEG_NV).astype(BF16)
        if has_next:
            h = h_ref[...]


def _project(x, mod_rows, gain, w_perm, w_gate, b_gate, cos, sin, *, rope, tm, cast_weights=(), layer=0):
    bsz, seq, _ = x.shape
    n_tiles = seq // tm
    per_batch_mod = mod_rows.shape[0] > 1
    mod_map = (lambda b, i: (b, 0, 0)) if per_batch_mod else (lambda b, i: (0, 0, 0))
    widths = [(512, BF16), (128, BF16), (128, BF16), (256, F32), (256, F32), (512, BF16), (512, F32),
              (512, BF16), (512, BF16), (512, BF16), (256, F32), (256, F32)]
    assert len(widths) == N_PROJ_OUTPUTS
    tile = lambda w: pl.BlockSpec((None, tm, w), lambda b, i: (b, i, 0))
    side = _SideCasts(cast_weights, layer, bsz * n_tiles, lambda b, i: b * n_tiles + i)
    outs = pl.pallas_call(
        functools.partial(_proj_kernel, rope=rope),
        grid=(bsz, n_tiles),
        in_specs=[
            tile(D_MODEL),
            pl.BlockSpec((None, 6, D_MODEL), mod_map),
            _const_spec((1, D_MODEL)),
            _const_spec((D_MODEL, IN_PADDED)),
            _const_spec((LANES, 2 * B_HEADS * B_DK)),
            _const_spec((1, 2 * B_HEADS * B_DK)),
            pl.BlockSpec((tm, LANES), lambda b, i: (i, 0)),
            pl.BlockSpec((tm, LANES), lambda b, i: (i, 0)),
        ] + side.specs,
        out_specs=[tile(w) for w, _ in widths] + side.out_specs,
        out_shape=[jax.ShapeDtypeStruct((bsz, seq, w), dt) for w, dt in widths] + side.out_shapes,
        scratch_shapes=[pltpu.VMEM((min(tm, DENSE_SUB_ROWS), D_MODEL), BF16)],
        compiler_params=_params(("parallel", "arbitrary")),
        name="in_proj_rope" if rope else "in_proj_ctx",
    )(x, mod_rows, gain, w_perm, w_gate, b_gate, cos, sin, *side.arrays)
    return outs[:N_PROJ_OUTPUTS], side.finish(outs[N_PROJ_OUTPUTS:])


class _GlaPrep:
    def __init__(self, q_ref, k_ref, g_ref, reverse):
        t = q_ref.shape[0]
        c = GLA_CHUNK
        row = lax.broadcasted_iota(jnp.int32, (t, t), 0)
        col = lax.broadcasted_iota(jnp.int32, (t, t), 1)
        same_chunk = (row // c) == (col // c)
        tri = (same_chunk & ((col >= row) if reverse else (col <= row))).astype(BF16)
        self.reverse = reverse
        self.q = q_ref[...]
        self.k = k_ref[...]
        self.g_parts = _split3(g_ref[...])
        self.b = _dot_01(tri, self.g_parts)
        b = self.b
        self.totals = [b[ci * c:ci * c + 1, :] if reverse else b[ci * c + c - 1:ci * c + c, :]
                       for ci in range(t // c)]
        self.b_end = jnp.concatenate([jnp.broadcast_to(e, (c, e.shape[1])) for e in self.totals], axis=0)
        self.q_state = (self.q * jnp.exp(b)).astype(BF16)
        self.k_tail = (self.k * jnp.exp(self.b_end - b)).astype(BF16)
        self.decay = [jnp.exp(e) for e in self.totals]


def _split3(x):
    hi = x.astype(BF16)
    rest = x - hi.astype(F32)
    mid = rest.astype(BF16)
    return hi, mid, (rest - mid.astype(F32)).astype(BF16)


def _dot_01(m01, parts):
    return _dot(m01, parts[0]) + _dot(m01, parts[1]) + _dot(m01, parts[2])


def _gla_pair_scores(q_w, k_w, keep):
    c = GLA_CHUNK
    out = []
    for ci in range(q_w.shape[0] // c):
        rows = slice(ci * c, (ci + 1) * c)
        for p in range(B_HEADS // 2):
            ps = slice(p * LANES, (p + 1) * LANES)
            out.append(jnp.where(keep, _dot_nt(_split_head_pair(q_w[rows, ps]), k_w[rows, ps]), 0.0))
    return out


def _gla_scores_fast(prep):
    c = GLA_CHUNK
    qi = lax.broadcasted_iota(jnp.int32, (2 * c, c), 0) % c
    kj = lax.broadcasted_iota(jnp.int32, (2 * c, c), 1)
    keep = (kj >= qi) if prep.reverse else (kj <= qi)
    ref = 0.5 * prep.b_end
    q_in = (prep.q * jnp.exp(prep.b - ref)).astype(BF16)
    k_in = (prep.k * jnp.exp(ref - prep.b)).astype(BF16)
    return _gla_pair_scores(q_in, k_in, keep)


def _gla_scores_safe(prep):
    t = prep.q.shape[0]
    c = GLA_CHUNK
    rev = prep.reverse
    row = lax.broadcasted_iota(jnp.int32, (t, t), 0)
    col = lax.broadcasted_iota(jnp.int32, (t, t), 1)
    qi = lax.broadcasted_iota(jnp.int32, (2 * c, c), 0) % c
    kj = lax.broadcasted_iota(jnp.int32, (2 * c, c), 1)
    scores = _gla_pair_scores(prep.q.astype(BF16), prep.k.astype(BF16), qi == kj)
    s = c // 2
    while s >= 1:
        if rev:
            bnd = row - row % (2 * s) + s
            span = ((col >= bnd) & (col < row)) | ((col >= row) & (col < bnd))
        else:
            bnd = row - row % (2 * s) + s - 1
            span = ((col > bnd) & (col <= row)) | ((col > row) & (col <= bnd))
        e = jnp.exp(_dot_01(span.astype(BF16), prep.g_parts))
        q_side = ((qi % (2 * s)) < s) if rev else ((qi % (2 * s)) >= s)
        k_side = ((kj % (2 * s)) >= s) if rev else ((kj % (2 * s)) < s)
        keep = ((qi // (2 * s)) == (kj // (2 * s))) & q_side & k_side
        level = _gla_pair_scores((prep.q * e).astype(BF16), (prep.k * e).astype(BF16), keep)
        scores = [acc + a for acc, a in zip(scores, level)]
        s //= 2
    return scores


def _gla_local(scores, k_tail, v_ref):
    c = GLA_CHUNK
    n_pairs = B_HEADS // 2
    intra, upd = [], []
    for ci in range(k_tail.shape[0] // c):
        rows = slice(ci * c, (ci + 1) * c)
        intra_c, upd_c = [], []
        for p in range(n_pairs):
            a = scores[ci * n_pairs + p].astype(BF16)
            kt = _split_head_pair(k_tail[rows, p * LANES:(p + 1) * LANES])
            u = None
            for half in range(2):
                h = 2 * p + half
                vh = v_ref[rows, h * B_DV:(h + 1) * B_DV]
                intra_c.append(_dot(a[half * c:(half + 1) * c], vh))
                uh = _dot_tn(vh, kt[half * c:(half + 1) * c])
                u = uh if u is None else u + uh
            upd_c.append(u)
        intra.append(intra_c)
        upd.append(upd_c)
    return intra, upd


def _gla_states(s_ref, decay, upd, reverse):
    n = len(upd)
    order = range(n - 1, -1, -1) if reverse else range(n)
    starts = [None] * n
    s = [s_ref[p] for p in range(B_HEADS // 2)]
    for ci in order:
        starts[ci] = [sp.astype(BF16) for sp in s]
        s = [s[p] * decay[ci][:, p * LANES:(p + 1) * LANES] + upd[ci][p] for p in range(B_HEADS // 2)]
    for p in range(B_HEADS // 2):
        s_ref[p] = s[p]
    return starts


def _gla_finish(o_ref, q_state, starts, intra):
    c = GLA_CHUNK
    for ci in range(len(intra)):
        rows = slice(ci * c, (ci + 1) * c)
        for p in range(B_HEADS // 2):
            ps = slice(p * LANES, (p + 1) * LANES)
            inter = _dot_nt(_split_head_pair(q_state[rows, ps]), starts[ci][p])
            for half in range(2):
                h = 2 * p + half
                o_ref[rows, h * B_DV:(h + 1) * B_DV] = intra[ci][h] + inter[half * c:(half + 1) * c]


def _gla_kernel(qf_ref, kf_ref, vf_ref, gf_ref, qb_ref, kb_ref, vb_ref, gb_ref, s0f_ref, s0b_ref,
                of_ref, ob_ref, sf_out_ref, sb_out_ref, sf_ref, sb_ref):
    i = pl.program_id(1)

    @pl.when(i == 0)
    def _():
        sf_ref[...] = s0f_ref[...]
        sb_ref[...] = s0b_ref[...]

    fwd = _GlaPrep(qf_ref, kf_ref, gf_ref, False)
    bwd = _GlaPrep(qb_ref, kb_ref, gb_ref, True)

    def tile_body(score_fn):
        scores_f = score_fn(fwd)
        scores_b = score_fn(bwd)
        intra_f, upd_f = _gla_local(scores_f, fwd.k_tail, vf_ref)
        intra_b, upd_b = _gla_local(scores_b, bwd.k_tail, vb_ref)
        starts_f = _gla_states(sf_ref, fwd.decay, upd_f, False)
        starts_b = _gla_states(sb_ref, bwd.decay, upd_b, True)
        _gla_finish(of_ref, fwd.q_state, starts_f, intra_f)
        _gla_finish(ob_ref, bwd.q_state, starts_b, intra_b)

    in_fast_range = -jnp.min(jnp.concatenate(fwd.totals + bwd.totals, axis=0)) < GLA_FAST_RANGE

    @pl.when(in_fast_range)
    def _():
        tile_body(_gla_scores_fast)

    @pl.when(jnp.logical_not(in_fast_range))
    def _():
        tile_body(_gla_scores_safe)

    @pl.when(i == pl.num_programs(1) - 1)
    def _():
        sf_out_ref[...] = sf_ref[...]
        sb_out_ref[...] = sb_ref[...]


GLA_STATE_SHAPE = (B_HEADS // 2, B_DV, 2 * B_DK)


def _gla(q, k, v, gf, gb, s0f, s0b, *, tile):
    bsz, seq, _ = q.shape
    n = seq // tile
    fwd = lambda w: pl.BlockSpec((None, tile, w), lambda b, i: (b, i, 0))
    bwd = lambda w: pl.BlockSpec((None, tile, w), lambda b, i: (b, n - 1 - i, 0))
    st = pl.BlockSpec((None,) + GLA_STATE_SHAPE, lambda b, i: (b, 0, 0, 0))
    wk, wv = B_HEADS * B_DK, B_HEADS * B_DV
    st_shape = jax.ShapeDtypeStruct((bsz,) + GLA_STATE_SHAPE, F32)
    return pl.pallas_call(
        _gla_kernel,
        grid=(bsz, n),
        in_specs=[fwd(wk), fwd(wk), fwd(wv), fwd(wk), bwd(wk), bwd(wk), bwd(wv), bwd(wk), st, st],
        out_specs=[fwd(wv), bwd(wv), st, st],
        out_shape=[jax.ShapeDtypeStruct((bsz, seq, wv), F32), jax.ShapeDtypeStruct((bsz, seq, wv), F32),
                   st_shape, st_shape],
        scratch_shapes=[pltpu.VMEM(GLA_STATE_SHAPE, F32), pltpu.VMEM(GLA_STATE_SHAPE, F32)],
        compiler_params=_params(("parallel", "arbitrary")),
        name="gla_scan",
    )(q, k, v, gf, q, k, v, gb, s0f, s0b)


def _softmax_parts(scores, sink_tile):
    def lane_tiles(blocks):
        return [b[:, j:j + LANES] for b in blocks for j in range(0, b.shape[1], LANES)]

    tiles = lane_tiles(scores)
    if sink_tile is not None:
        tiles.append(sink_tile)
    m = functools.reduce(jnp.maximum, tiles).max(axis=-1, keepdims=True)
    ps = [jnp.exp2(s - m) for s in scores]
    acc = functools.reduce(jnp.add, lane_tiles(ps))
    if sink_tile is not None:
        lane = lax.broadcasted_iota(jnp.int32, sink_tile.shape, 1)
        acc = acc + jnp.where(lane == 0, jnp.exp2(sink_tile - m), 0.0)
    return ps, acc.sum(axis=-1, keepdims=True)


def _split_head_pair(t):
    lane = lax.broadcasted_iota(jnp.int32, t.shape, 1)
    zero = jnp.zeros_like(t)
    return jnp.concatenate([jnp.where(lane < HEAD_DIM, t, zero), jnp.where(lane >= HEAD_DIM, t, zero)], axis=0)


def _merge_head_pair(o):
    m = o.shape[0] // 2
    lane = lax.broadcasted_iota(jnp.int32, (m, LANES), 1)
    return jnp.where(lane < HEAD_DIM, o[:m], o[m:])


A_HEAD_ORDER = (0, 4, 1, 5, 2, 6, 3, 7)


def _win_kernel(sink_ref, q_ref, kp_ref, kc_ref, kn_ref, vp_ref, vc_ref, vn_ref, kx_ref, vx_ref, o_ref):
    n = pl.program_id(1)
    nb = pl.num_programs(1)
    n_tiles = A_HEADS // 2
    q = jnp.concatenate([_split_head_pair(q_ref[:, j * LANES:(j + 1) * LANES]) for j in range(n_tiles)], axis=0)
    sink_tile = jnp.concatenate([jnp.full((A_BLOCK, LANES), sink_ref[h], F32) for h in A_HEAD_ORDER], axis=0)
    qi = lax.broadcasted_iota(jnp.int32, (A_BLOCK, A_BLOCK), 0)
    kj = lax.broadcasted_iota(jnp.int32, (A_BLOCK, A_BLOCK), 1)
    neg = jnp.full((A_BLOCK, A_BLOCK), -jnp.inf, F32)
    zero = jnp.zeros((A_BLOCK, A_BLOCK), F32)
    prev_mask = jnp.where(n > 0, jnp.where(kj >= qi, zero, neg), neg)
    next_mask = jnp.where(n < nb - 1, jnp.where(kj <= qi, zero, neg), neg)
    s_p = _dot_nt(q, kp_ref[...]) + jnp.tile(prev_mask, (A_HEADS, 1))
    s_c = _dot_nt(q, kc_ref[...])
    s_n = _dot_nt(q, kn_ref[...]) + jnp.tile(next_mask, (A_HEADS, 1))
    s_x = _dot_nt(q, kx_ref[...])
    (p_p, p_c, p_n, p_x), denom = _softmax_parts([s_p, s_c, s_n, s_x], sink_tile)
    o = (_dot(p_p.astype(BF16), vp_ref[...]) + _dot(p_c.astype(BF16), vc_ref[...])
         + _dot(p_n.astype(BF16), vn_ref[...]) + _dot(p_x.astype(BF16), vx_ref[...]))
    o = o / denom
    for j in range(n_tiles):
        o_ref[:, j * LANES:(j + 1) * LANES] = _merge_head_pair(
            o[j * 2 * A_BLOCK:(j + 1) * 2 * A_BLOCK]).astype(o_ref.dtype)


def _window_attention(sink, q, k, v, k_ctx, v_ctx):
    bsz, seq, wq = q.shape
    nb = seq // A_BLOCK
    wkv = k.shape[-1]
    n_ctx = k_ctx.shape[1]
    kv_spec = lambda f: pl.BlockSpec((None, A_BLOCK, wkv), lambda b, n: (b, f(n), 0))
    prev = lambda n: jnp.maximum(n - 1, 0)
    cur = lambda n: n
    nxt = lambda n: jnp.minimum(n + 1, nb - 1)
    ctx_spec = pl.BlockSpec((None, n_ctx, wkv), lambda b, n: (b, 0, 0))
    return pl.pallas_call(
        _win_kernel,
        grid=(bsz, nb),
        in_specs=[
            pl.BlockSpec(memory_space=pltpu.SMEM),
            pl.BlockSpec((None, A_BLOCK, wq), lambda b, n: (b, n, 0)),
            kv_spec(prev), kv_spec(cur), kv_spec(nxt),
            kv_spec(prev), kv_spec(cur), kv_spec(nxt),
            ctx_spec, ctx_spec,
        ],
        out_specs=pl.BlockSpec((None, A_BLOCK, wq), lambda b, n: (b, n, 0)),
        out_shape=jax.ShapeDtypeStruct((bsz, seq, wq), BF16),
        compiler_params=_params(("parallel", "arbitrary")),
        name="window_attn",
    )(sink, q, k, k, k, v, v, v, k_ctx, v_ctx)


def _dense_kernel(sink_ref, q_ref, k_ref, v_ref, o_ref, *, head_order, shared_kv):
    n_q = q_ref.shape[0]
    for j in range(q_ref.shape[1] // LANES):
        qs = slice(j * LANES, (j + 1) * LANES)
        ks = slice(0, LANES) if shared_kv else qs
        s = _dot_nt(_split_head_pair(q_ref[:, qs]), k_ref[:, ks])
        sink_tile = None
        if head_order is not None:
            sink_tile = jnp.concatenate(
                [jnp.full((n_q, LANES), sink_ref[head_order[2 * j + half]], F32) for half in range(2)], axis=0)
        (p,), denom = _softmax_parts([s], sink_tile)
        o_ref[:, qs] = _merge_head_pair(_dot(p.astype(BF16), v_ref[:, ks]) / denom).astype(o_ref.dtype)


def _dense_attention(sink, q, k, v, *, head_order, shared_kv):
    bsz, n_q, wq = q.shape
    wkv = k.shape[-1]
    full = lambda w: pl.BlockSpec((None, n_q, w), lambda b: (b, 0, 0))
    return pl.pallas_call(
        functools.partial(_dense_kernel, head_order=head_order, shared_kv=shared_kv),
        grid=(bsz,),
        in_specs=[pl.BlockSpec(memory_space=pltpu.SMEM), full(wq), full(wkv), full(wkv)],
        out_specs=full(wq),
        out_shape=jax.ShapeDtypeStruct((bsz, n_q, wq), BF16),
        compiler_params=_params(("parallel",)),
        name="ctx_dense_attn_sink" if shared_kv else "ctx_dense_attn",
    )(sink, q, k, v)


NA_ROWS_PER_STEP = 8


N_NA_INPUTS = 6


def _na_kernel(*refs):
    q_ref, k_ref, v_ref, kx_ref, vx_ref, bias_ref = refs[:N_NA_INPUTS]
    n_side = (len(refs) - N_NA_INPUTS - 1) // 2
    o_ref = refs[N_NA_INPUTS + n_side]
    _run_side_casts(refs[N_NA_INPUTS:N_NA_INPUTS + n_side], refs[N_NA_INPUTS + n_side + 1:])
    step = pl.program_id(1)
    grid_rows = k_ref.shape[0] // GRID_W
    n_keys = C_WIN_ROWS * GRID_W
    n_pairs = C_HEADS // 2
    for rr in range(NA_ROWS_PER_STEP):
        r = step * NA_ROWS_PER_STEP + rr
        row0 = jnp.clip(r - C_WIN_ROWS // 2, 0, grid_rows - C_WIN_ROWS)
        d_row0 = row0 - r + (C_WIN_ROWS - 1)
        key0 = pl.multiple_of(row0 * GRID_W, GRID_W)
        qrows = slice(rr * GRID_W, (rr + 1) * GRID_W)
        s_l, s_x = [], []
        for p in range(n_pairs):
            ps = slice(p * LANES, (p + 1) * LANES)
            q2 = _split_head_pair(q_ref[qrows, ps])
            s_l.append(_dot_nt(q2, k_ref[pl.ds(key0, n_keys), ps]))
            s_x.append(_dot_nt(q2, kx_ref[:, ps]))
        bias = jnp.concatenate(
            [bias_ref[d_row0 + 2 * j].reshape(C_HEADS * GRID_W, LANES) for j in range(C_WIN_ROWS // 2)], axis=1)
        (p_l, p_x), denom = _softmax_parts([jnp.concatenate(s_l, axis=0) + bias, jnp.concatenate(s_x, axis=0)], None)
        p_l = p_l.astype(BF16)
        p_x = p_x.astype(BF16)
        for p in range(n_pairs):
            ps = slice(p * LANES, (p + 1) * LANES)
            pr = slice(p * 2 * GRID_W, (p + 1) * 2 * GRID_W)
            o = (_dot(p_l[pr], v_ref[pl.ds(key0, n_keys), ps]) + _dot(p_x[pr], vx_ref[:, ps])) / denom[pr]
            o_ref[qrows, ps] = _merge_head_pair(o).astype(o_ref.dtype)


def _neighbourhood_attention(q, k, v, k_ctx, v_ctx, bias_tbl, cast_weights=(), layer=0):
    bsz, seq, w = q.shape
    n_ctx = k_ctx.shape[1]
    tq = NA_ROWS_PER_STEP * GRID_W
    n_tiles = seq // tq
    whole = pl.BlockSpec((None, seq, w), lambda b, i: (b, 0, 0))
    ctx_spec = pl.BlockSpec((None, n_ctx, w), lambda b, i: (b, 0, 0))
    side = _SideCasts(cast_weights, layer, bsz * n_tiles, lambda b, i: b * n_tiles + i)
    outs = pl.pallas_call(
        _na_kernel,
        grid=(bsz, n_tiles),
        in_specs=[pl.BlockSpec((None, tq, w), lambda b, i: (b, i, 0)), whole, whole, ctx_spec, ctx_spec,
                  _const_spec(bias_tbl.shape)] + side.specs,
        out_specs=[pl.BlockSpec((None, tq, w), lambda b, i: (b, i, 0))] + side.out_specs,
        out_shape=[jax.ShapeDtypeStruct((bsz, seq, w), BF16)] + side.out_shapes,
        compiler_params=_params(("parallel", "arbitrary")),
        name="neighbourhood_attn",
    )(q, k, v, k_ctx, v_ctx, bias_tbl, *side.arrays)
    return outs[0], side.finish(outs[1:])


def _na_bias_table(rpb):
    qcol = jnp.arange(GRID_W)[:, None]
    kcol = jnp.arange(GRID_W)[None, :]
    wstart = jnp.clip(qcol - C_WIN_COLS // 2, 0, GRID_W - C_WIN_COLS)
    ok = (kcol >= wstart) & (kcol < wstart + C_WIN_COLS)
    n_off = C_WIN_COLS - 1
    period = LANES + 1
    vec = jnp.concatenate([rpb[..., n_off:].astype(F32),
                           jnp.zeros(rpb.shape[:2] + (period - 2 * n_off - 1,), F32),
                           rpb[..., :n_off].astype(F32)], axis=-1)
    toep = jnp.tile(vec, (1, 1, GRID_W))[..., :GRID_W * LANES]
    toep = toep.reshape(rpb.shape[:2] + (GRID_W, LANES))[..., :GRID_W]
    tbl = jnp.where(ok[None, None], toep * LOG2E, -jnp.inf)
    tbl = jnp.moveaxis(tbl, 1, 0)
    return jnp.concatenate([tbl[:-1], tbl[1:]], axis=-1)


def _merge_kernel(x_ref, mod_ref, gain_ref, ya_ref, of_ref, ob_ref, gr_ref, yc_ref, gn_ref,
                  wm_ref, bm_ref, wa_ref, wb_ref, wc_ref, wo_ref, o_ref):
    d = D_MODEL
    gn = gn_ref[...]
    x = x_ref[...]
    h = _norm_mod(x, gain_ref[...], mod_ref[0:1, :], mod_ref[1:2, :]).astype(BF16)
    o_sum = of_ref[...] + ob_ref[...]
    parts = []
    for hh in range(B_HEADS):
        oh = o_sum[:, hh * B_DV:(hh + 1) * B_DV]
        parts.append((oh * lax.rsqrt(jnp.mean(oh * oh, axis=-1, keepdims=True) + EPS)) * gn)
    y_b = (jnp.concatenate(parts, axis=1) * _silu(gr_ref[...])).astype(BF16)
    mixed = None
    for j, (y, w_ref) in enumerate(((ya_ref[...], wa_ref), (y_b, wb_ref), (yc_ref[...], wc_ref))):
        gate = _sigmoid(_dot(h, wm_ref[:, j * d:(j + 1) * d]) + bm_ref[:, j * d:(j + 1) * d])
        term = gate * _dot(y, w_ref[...])
        mixed = term if mixed is None else mixed + term
    o_ref[...] = x + mod_ref[2:3, :] * _dot(mixed.astype(BF16), wo_ref[...])


def _merge(x, mod_rows, gain, y_a, o_f, o_b, g_r, y_c, gla_gain, w_merge, b_merge, w_a, w_b, w_c, w_out, *, tm):
    bsz, seq, d = x.shape
    per_batch_mod = mod_rows.shape[0] > 1
    mod_map = (lambda b, i: (b, 0, 0)) if per_batch_mod else (lambda b, i: (0, 0, 0))
    tile = lambda w: pl.BlockSpec((None, tm, w), lambda b, i: (b, i, 0))
    wy = y_a.shape[-1]
    return pl.pallas_call(
        _merge_kernel,
        grid=(bsz, seq // tm),
        in_specs=[tile(d), pl.BlockSpec((None, 6, d), mod_map), _const_spec((1, d)),
                  tile(wy), tile(wy), tile(wy), tile(wy), tile(wy), _const_spec((1, B_DV)),
                  _const_spec(w_merge.shape), _const_spec(b_merge.shape),
                  _const_spec(w_a.shape), _const_spec(w_b.shape), _const_spec(w_c.shape),
                  _const_spec(w_out.shape)],
        out_specs=tile(d),
        out_shape=jax.ShapeDtypeStruct((bsz, seq, d), F32),
        compiler_params=_params(("parallel", "arbitrary")),
        name="merge_out",
    )(x, mod_rows, gain, y_a, o_f, o_b, g_r, y_c, gla_gain, w_merge, b_merge, w_a, w_b, w_c, w_out)


FFN_CHUNKS = ((0, 1024), (1024, 2048), (2048, FFN_HIDDEN))
FFN_SUB_ROWS = 256


def _ffn_kernel(x_ref, mod_ref, gain_ref, w1_ref, w2_ref, fg_ref, o_ref, *, final):
    for r0 in range(0, x_ref.shape[0], FFN_SUB_ROWS):
        rows = slice(r0, r0 + FFN_SUB_ROWS)
        x = x_ref[rows, :]
        h = _norm_mod(x, gain_ref[...], mod_ref[3:4, :], mod_ref[4:5, :]).astype(BF16)
        acc = None
        for c0, c1 in FFN_CHUNKS:
            gate = _dot(h, w1_ref[:, c0:c1])
            up = _dot(h, w1_ref[:, FFN_HIDDEN + c0:FFN_HIDDEN + c1])
            part = _dot((_silu(gate) * up).astype(BF16), w2_ref[c0:c1, :])
            acc = part if acc is None else acc + part
        y = x + mod_ref[5:6, :] * acc
        if final:
            y = (y * lax.rsqrt(jnp.mean(y * y, axis=-1, keepdims=True) + EPS)) * fg_ref[...]
        o_ref[rows, :] = y


def _ffn(x, mod_rows, gain, w1, w2, final_gain, *, final, tm):
    bsz, seq, d = x.shape
    per_batch_mod = mod_rows.shape[0] > 1
    mod_map = (lambda b, i: (b, 0, 0)) if per_batch_mod else (lambda b, i: (0, 0, 0))
    tile = pl.BlockSpec((None, tm, d), lambda b, i: (b, i, 0))
    return pl.pallas_call(
        functools.partial(_ffn_kernel, final=final),
        grid=(bsz, seq // tm),
        in_specs=[tile, pl.BlockSpec((None, 6, d), mod_map), _const_spec((1, d)),
                  _const_spec(w1.shape), _const_spec(w2.shape), _const_spec((1, d))],
        out_specs=tile,
        out_shape=jax.ShapeDtypeStruct((bsz, seq, d), F32),
        compiler_params=_params(("parallel", "arbitrary")),
        name="ffn_final" if final else "ffn",
    )(x, mod_rows, gain, w1, w2, final_gain)


def _rope_tables(seq):
    t = jnp.arange(seq)
    row = (t // GRID_W).astype(F32)
    col = (t % GRID_W).astype(F32)
    n_freq = HEAD_DIM // 4
    inv = ROPE_BASE ** (-jnp.arange(n_freq, dtype=F32) / n_freq)
    ang = jnp.concatenate([row[:, None] * inv[None], col[:, None] * inv[None]], axis=-1)
    cos, sin = jnp.cos(ang), jnp.sin(ang)
    return jnp.tile(cos, (1, 4)), jnp.tile(jnp.concatenate([-sin, sin], axis=-1), (1, 2))


def _permute_w_in(w):
    ga0 = 2304
    ga1 = ga0 + 2 * B_GATE_RANK
    n_aq = A_HEADS * HEAD_DIM
    aq = w[:, :n_aq].reshape(w.shape[0], A_HEADS, HEAD_DIM)[:, jnp.array(A_HEAD_ORDER)].reshape(w.shape[0], n_aq)
    pad = jnp.zeros((w.shape[0], IN_PADDED - w.shape[1]), w.dtype)
    return jnp.concatenate([aq, w[:, n_aq:ga0], w[:, ga1:], w[:, ga0:ga1], pad], axis=1).astype(BF16)


def _permute_w_branch_a(w):
    return w.reshape(A_HEADS, HEAD_DIM, w.shape[1])[jnp.array(A_HEAD_ORDER)].reshape(w.shape).astype(BF16)


def _gate_weights(w_fwd, b_fwd, w_bwd, b_bwd):
    n = B_HEADS * B_DK
    r = B_GATE_RANK
    w = jnp.zeros((LANES, 2 * n), F32)
    w = w.at[:r, :n].set(w_fwd).at[r:2 * r, n:].set(w_bwd)
    return w.astype(BF16), jnp.concatenate([b_fwd, b_bwd])[None, :]


def kernel(x, c, ctx, c_ctx, w_ada, b_ada, norm_mix, w_in, attn_sink, gla_gate_w_fwd, gla_gate_b_fwd,
           gla_gate_w_bwd, gla_gate_b_bwd, gla_norm, na_rpb, w_branch_a, w_branch_b, w_branch_c,
           w_merge, b_merge, w_out, norm_ffn, w_ffn_in, w_ffn_out, final_norm):
    bsz, seq, d = x.shape
    n_ctx = ctx.shape[1]
    cos, sin = _rope_tables(seq)

    cvec = jnp.zeros((8, d), F32).at[:bsz].set(c).at[bsz].set(c_ctx)
    mod = _modulation(cvec, w_ada, b_ada).reshape(DEPTH, 8, 6, d)

    xc = ctx
    zeros_state = jnp.zeros((bsz,) + GLA_STATE_SHAPE, F32)
    for l in range(DEPTH):
        last = l == DEPTH - 1
        mod_x = mod[l, :bsz]
        mod_c = mod[l, bsz:bsz + 1]
        gain_m = norm_mix[l][None, :]
        w_perm = _permute_w_in(w_in[l])
        w_gate, b_gate = _gate_weights(gla_gate_w_fwd[l], gla_gate_b_fwd[l], gla_gate_w_bwd[l], gla_gate_b_bwd[l])

        flat = lambda t: t.reshape(1, bsz * n_ctx, t.shape[-1])
        unflat = lambda t: t.reshape(bsz, n_ctx, t.shape[-1])
        ctx_proj, _ = _project(flat(xc), mod_c, gain_m, w_perm, w_gate, b_gate, cos, sin, rope=False, tm=CTX_ROW_TILE)
        (caq, cak, cav, cgq, cgk, cgv, cgr, cnq, cnk, cnv, cgf, cgb) = [unflat(t) for t in ctx_proj]
        (aq, ak, av, gq, gk, gv, gr, nq, nk, nv, gf, gb), (wm_bf, wb_bf, wc_bf, wo_bf) = _project(
            x, mod_x, gain_m, w_perm, w_gate, b_gate, cos, sin, rope=True, tm=1024,
            cast_weights=(w_merge, w_branch_b, w_branch_c, w_out), layer=l)

        co_f, co_b, s_f, s_b = _gla(cgq, cgk, cgv, cgf, cgb, zeros_state, zeros_state, tile=n_ctx)
        o_f, o_b, _, _ = _gla(gq, gk, gv, gf, gb, s_f, s_b, tile=256)

        sink = attn_sink[l] * LOG2E
        y_a = _window_attention(sink, aq, ak, av, cak, cav)
        y_c, (w1, w2) = _neighbourhood_attention(nq, nk, nv, cnk, cnv, _na_bias_table(na_rpb[l]),
                                                 cast_weights=(w_ffn_in, w_ffn_out), layer=l)

        merge_w = (gla_norm[l][None, :], wm_bf, b_merge[l][None, :],
                   _permute_w_branch_a(w_branch_a[l]), wb_bf, wc_bf, wo_bf)
        x = _merge(x, mod_x, gain_m, y_a, o_f, o_b, gr, y_c, *merge_w, tm=512)

        gain_f = norm_ffn[l][None, :]
        fg = final_norm[None, :]
        if not last:
            yc_a = _dense_attention(sink, caq, cak, cav, head_order=A_HEAD_ORDER, shared_kv=True)
            yc_c = _dense_attention(sink, cnq, cnk, cnv, head_order=None, shared_kv=False)
            xc_flat = _merge(flat(xc), mod_c, gain_m, flat(yc_a), flat(co_f), flat(co_b), flat(cgr), flat(yc_c),
                             *merge_w, tm=CTX_ROW_TILE)
            xc = unflat(_ffn(xc_flat, mod_c, gain_f, w1, w2, fg, final=False, tm=CTX_ROW_TILE))
        x = _ffn(x, mod_x, gain_f, w1, w2, fg, final=last, tm=1024)
    return x
```

```python
import functools

import jax
import jax.numpy as jnp
from jax import lax
from jax.experimental import pallas as pl
from jax.experimental.pallas import tpu as pltpu

F32 = jnp.float32
BF16 = jnp.bfloat16

D_MODEL = 1024
DEPTH = 2
GRID_W = 64
HEAD_DIM = 64
EPS = 1e-6
ROPE_BASE = 10000.0
A_HEADS = 8
A_KV_HEADS = 2
A_BLOCK = 128
B_HEADS = 4
B_DK = 64
B_DV = 128
B_GATE_RANK = 16
B_GATE_NORM = 16.0
C_HEADS = 8
C_WIN_ROWS = 8
C_WIN_COLS = 16
FFN_HIDDEN = 2816

VMEM_LIMIT_BYTES = 56 * 1024 * 1024
LANES = 128

SEG_AQ = (0, 512)
SEG_AKV = (512, 768)
SEG_GQ = (768, 1024)
SEG_GK = (1024, 1280)
SEG_GV = (1280, 1792)
SEG_GR = (1792, 2304)
SEG_NQ = (2304, 2816)
SEG_NK = (2816, 3328)
SEG_NV = (3328, 3840)
SEG_GA = (3840, 3968)
IN_PADDED = 3968
W_IN_GATE_COLS = (2304, 2304 + 2 * B_GATE_RANK)

LOG2E = 1.4426950408889634
ATTN_Q_SCALE = HEAD_DIM ** -0.5 * LOG2E

GLA_CHUNK = 64
GLA_FAST_RANGE = 120.0


def _params(sem):
    return pltpu.CompilerParams(dimension_semantics=sem, vmem_limit_bytes=VMEM_LIMIT_BYTES)


def _const_spec(shape):
    nd = len(shape)
    return pl.BlockSpec(shape, lambda *_: (0,) * nd, pipeline_mode=pl.Buffered(1))


def _sigmoid(x):
    return 1.0 / (1.0 + jnp.exp(-x))


def _silu(x):
    return x * _sigmoid(x)


def _dot(a, b):
    return jnp.dot(a, b, preferred_element_type=F32)


def _dot_nt(a, b):
    return lax.dot_general(a, b, (((1,), (1,)), ((), ())), preferred_element_type=F32)


def _dot_tn(a, b):
    return lax.dot_general(a, b, (((0,), (0,)), ((), ())), preferred_element_type=F32)


def _norm_mod(x, gain, shift, scale):
    y = x * lax.rsqrt(jnp.mean(x * x, axis=-1, keepdims=True) + EPS)
    return (y * gain) * (1.0 + scale) + shift


def _relayout_w_in(w_ref, o_ref):
    rows = w_ref.shape[0]
    lane = lax.broadcasted_iota(jnp.int32, (rows, LANES), 1)
    for j in range(A_HEADS // 2):
        halves = []
        for half in range(2):
            head = A_HEAD_ORDER[2 * j + half]
            t = w_ref[:, (head // 2) * LANES:(head // 2 + 1) * LANES]
            halves.append(t if head % 2 == half else pltpu.roll(t, HEAD_DIM, 1))
        o_ref[:, j * LANES:(j + 1) * LANES] = jnp.where(lane < HEAD_DIM, halves[0], halves[1]).astype(BF16)
    n_aq = SEG_AQ[1]
    ga0, ga1 = W_IN_GATE_COLS
    o_ref[:, n_aq:ga0] = w_ref[:, n_aq:ga0].astype(BF16)
    o_ref[:, ga0:SEG_GA[0]] = w_ref[:, ga1:].astype(BF16)
    gate_tile = w_ref[:, ga0:ga0 + LANES]
    o_ref[:, SEG_GA[0]:] = jnp.where(lane < ga1 - ga0, gate_tile, 0.0).astype(BF16)


def _mod_kernel(c_ref, w_ref, b_ref, w_in_ref, o_ref, w_perm_ref):
    s = _silu(c_ref[...])
    o_ref[...] = _dot(s.astype(BF16), w_ref[...].astype(BF16)) + b_ref[...]
    _relayout_w_in(w_in_ref, w_perm_ref)


def _modulation(cvec, w_ada, b_ada, w_in):
    tn = 1536
    n_out = w_ada.shape[-1]
    n_j = n_out // tn
    rows = w_in.shape[1] // n_j
    return pl.pallas_call(
        _mod_kernel,
        grid=(DEPTH, n_j),
        in_specs=[
            pl.BlockSpec((8, D_MODEL), lambda l, j: (0, 0)),
            pl.BlockSpec((None, D_MODEL, tn), lambda l, j: (l, 0, j)),
            pl.BlockSpec((None, 1, tn), lambda l, j: (l, 0, j)),
            pl.BlockSpec((None, rows, w_in.shape[2]), lambda l, j: (l, j, 0)),
        ],
        out_specs=[pl.BlockSpec((None, 8, tn), lambda l, j: (l, 0, j)),
                   pl.BlockSpec((None, rows, IN_PADDED), lambda l, j: (l, j, 0))],
        out_shape=[jax.ShapeDtypeStruct((DEPTH, 8, n_out), F32),
                   jax.ShapeDtypeStruct((DEPTH, w_in.shape[1], IN_PADDED), BF16)],
        compiler_params=_params(("arbitrary", "arbitrary")),
        name="adaln_mod",
    )(cvec, w_ada, b_ada.reshape(DEPTH, 1, n_out), w_in)


def _rope(t, cos, sin):
    n = t.shape[1]
    lane = lax.broadcasted_iota(jnp.int32, t.shape, 1)
    first_half = (lane % HEAD_DIM) < (HEAD_DIM // 2)
    rot = jnp.where(first_half, pltpu.roll(t, n - HEAD_DIM // 2, 1), pltpu.roll(t, HEAD_DIM // 2, 1))
    reps = n // LANES
    return t * jnp.tile(cos, (1, reps)) + rot * jnp.tile(sin, (1, reps))


class _SideCasts:
    def __init__(self, stacked_weights, layer, n_steps, step_index):
        self.shapes2d = [w.shape[1:] for w in stacked_weights]
        self.arrays = [w.reshape(w.shape[0], n_steps, w.shape[1] // n_steps, w.shape[2]) for w in stacked_weights]
        self.specs = [pl.BlockSpec((None, None) + a.shape[2:], lambda *g: (layer, step_index(*g), 0, 0))
                      for a in self.arrays]
        self.out_specs = [pl.BlockSpec((None,) + a.shape[2:], lambda *g: (step_index(*g), 0, 0))
                          for a in self.arrays]
        self.out_shapes = [jax.ShapeDtypeStruct(a.shape[1:], BF16) for a in self.arrays]

    def finish(self, outs):
        return [o.reshape(s) for o, s in zip(outs, self.shapes2d)]


def _run_side_casts(in_refs, out_refs):
    for w_ref, o_ref in zip(in_refs, out_refs):
        o_ref[...] = w_ref[...].astype(BF16)


N_PROJ_INPUTS = 8
N_PROJ_OUTPUTS = 12


def _proj_kernel(*refs, rope):
    (x_ref, mod_ref, gain_ref, w_ref, wg_ref, bg_ref, cos_ref, sin_ref) = refs[:N_PROJ_INPUTS]
    n_side = (len(refs) - N_PROJ_INPUTS - N_PROJ_OUTPUTS) // 2
    outs = refs[N_PROJ_INPUTS + n_side:]
    (aq_ref, ak_ref, av_ref, gq_ref, gk_ref, gv_ref, gr_ref, nq_ref, nk_ref, nv_ref,
     gf_ref, gb_ref) = outs[:N_PROJ_OUTPUTS]
    _run_side_casts(refs[N_PROJ_INPUTS:N_PROJ_INPUTS + n_side], outs[N_PROJ_OUTPUTS:])
    scale = ATTN_Q_SCALE
    n_g = B_HEADS * B_DK
    h = _norm_mod(x_ref[...], gain_ref[...], mod_ref[0:1, :], mod_ref[1:2, :]).astype(BF16)

    def seg(s):
        return _dot(h, w_ref[:, s[0]:s[1]])

    aq = seg(SEG_AQ)
    akv = seg(SEG_AKV)
    z = _dot(seg(SEG_GA).astype(BF16), wg_ref[...]) + bg_ref[...]
    ak = akv[:, :LANES]
    if rope:
        cos = cos_ref[...]
        sin = sin_ref[...]
        aq = _rope(aq, cos, sin)
        ak = _rope(ak, cos, sin)
    aq_ref[...] = (aq * scale).astype(BF16)
    ak_ref[...] = ak.astype(BF16)
    av_ref[...] = akv[:, LANES:].astype(BF16)
    log_sig = jnp.minimum(z, 0.0) - jnp.log1p(jnp.exp(-jnp.abs(z)))
    g = log_sig / B_GATE_NORM
    gf_ref[...] = g[:, :n_g]
    gb_ref[...] = g[:, n_g:]
    gq_ref[...] = seg(SEG_GQ) * (B_DK ** -0.5)
    gk_ref[...] = seg(SEG_GK)
    gv_ref[...] = seg(SEG_GV).astype(BF16)
    gr_ref[...] = seg(SEG_GR)
    nq_ref[...] = (seg(SEG_NQ) * scale).astype(BF16)
    nk_ref[...] = seg(SEG_NK).astype(BF16)
    nv_ref[...] = seg(SEG_NV).astype(BF16)


def _project(x, mod_rows, gain, w_perm, w_gate, b_gate, cos, sin, *, rope, tm, cast_weights=(), layer=0):
    bsz, seq, _ = x.shape
    n_tiles = seq // tm
    per_batch_mod = mod_rows.shape[0] > 1
    mod_map = (lambda b, i: (b, 0, 0)) if per_batch_mod else (lambda b, i: (0, 0, 0))
    widths = [(512, BF16), (128, BF16), (128, BF16), (256, F32), (256, F32), (512, BF16), (512, F32),
              (512, BF16), (512, BF16), (512, BF16), (256, F32), (256, F32)]
    assert len(widths) == N_PROJ_OUTPUTS
    tile = lambda w: pl.BlockSpec((None, tm, w), lambda b, i: (b, i, 0))
    side = _SideCasts(cast_weights, layer, bsz * n_tiles, lambda b, i: b * n_tiles + i)
    outs = pl.pallas_call(
        functools.partial(_proj_kernel, rope=rope),
        grid=(bsz, n_tiles),
        in_specs=[
            tile(D_MODEL),
            pl.BlockSpec((None, 6, D_MODEL), mod_map),
            _const_spec((1, D_MODEL)),
            pl.BlockSpec((None, D_MODEL, IN_PADDED), lambda b, i: (layer, 0, 0), pipeline_mode=pl.Buffered(1)),
            _const_spec((LANES, 2 * B_HEADS * B_DK)),
            _const_spec((1, 2 * B_HEADS * B_DK)),
            pl.BlockSpec((tm, LANES), lambda b, i: (i, 0)),
            pl.BlockSpec((tm, LANES), lambda b, i: (i, 0)),
        ] + side.specs,
        out_specs=[tile(w) for w, _ in widths] + side.out_specs,
        out_shape=[jax.ShapeDtypeStruct((bsz, seq, w), dt) for w, dt in widths] + side.out_shapes,
        compiler_params=_params(("parallel", "arbitrary")),
        name="in_proj_rope" if rope else "in_proj_ctx",
    )(x, mod_rows, gain, w_perm, w_gate, b_gate, cos, sin, *side.arrays)
    return outs[:N_PROJ_OUTPUTS], side.finish(outs[N_PROJ_OUTPUTS:])


class _GlaPrep:
    def __init__(self, q_ref, k_ref, g_ref, reverse):
        t = q_ref.shape[0]
        c = GLA_CHUNK
        row = lax.broadcasted_iota(jnp.int32, (t, t), 0)
        col = lax.broadcasted_iota(jnp.int32, (t, t), 1)
        same_chunk = (row // c) == (col // c)
        tri = (same_chunk & ((col >= row) if reverse else (col <= row))).astype(BF16)
        self.reverse = reverse
        self.q = q_ref[...]
        self.k = k_ref[...]
        self.g_parts = _split3(g_ref[...])
        self.b = _dot_01(tri, self.g_parts)
        b = self.b
        self.totals = [b[ci * c:ci * c + 1, :] if reverse else b[ci * c + c - 1:ci * c + c, :]
                       for ci in range(t // c)]
        self.b_end = jnp.concatenate([jnp.broadcast_to(e, (c, e.shape[1])) for e in self.totals], axis=0)
        self.q_state = (self.q * jnp.exp(b)).astype(BF16)
        self.k_tail = (self.k * jnp.exp(self.b_end - b)).astype(BF16)
        self.decay = [jnp.exp(e) for e in self.totals]


def _split3(x):
    hi = x.astype(BF16)
    rest = x - hi.astype(F32)
    mid = rest.astype(BF16)
    return hi, mid, (rest - mid.astype(F32)).astype(BF16)


def _dot_01(m01, parts):
    return _dot(m01, parts[0]) + _dot(m01, parts[1]) + _dot(m01, parts[2])


def _gla_pair_scores(q_w, k_w, keep):
    c = GLA_CHUNK
    out = []
    for ci in range(q_w.shape[0] // c):
        rows = slice(ci * c, (ci + 1) * c)
        for p in range(B_HEADS // 2):
            ps = slice(p * LANES, (p + 1) * LANES)
            out.append(jnp.where(keep, _dot_nt(_split_head_pair(q_w[rows, ps]), k_w[rows, ps]), 0.0))
    return out


def _gla_scores_fast(prep):
    c = GLA_CHUNK
    qi = lax.broadcasted_iota(jnp.int32, (2 * c, c), 0) % c
    kj = lax.broadcasted_iota(jnp.int32, (2 * c, c), 1)
    keep = (kj >= qi) if prep.reverse else (kj <= qi)
    ref = 0.5 * prep.b_end
    q_in = (prep.q * jnp.exp(prep.b - ref)).astype(BF16)
    k_in = (prep.k * jnp.exp(ref - prep.b)).astype(BF16)
    return _gla_pair_scores(q_in, k_in, keep)


def _gla_scores_safe(prep):
    t = prep.q.shape[0]
    c = GLA_CHUNK
    rev = prep.reverse
    row = lax.broadcasted_iota(jnp.int32, (t, t), 0)
    col = lax.broadcasted_iota(jnp.int32, (t, t), 1)
    qi = lax.broadcasted_iota(jnp.int32, (2 * c, c), 0) % c
    kj = lax.broadcasted_iota(jnp.int32, (2 * c, c), 1)
    scores = _gla_pair_scores(prep.q.astype(BF16), prep.k.astype(BF16), qi == kj)
    s = c // 2
    while s >= 1:
        if rev:
            bnd = row - row % (2 * s) + s
            span = ((col >= bnd) & (col < row)) | ((col >= row) & (col < bnd))
        else:
            bnd = row - row % (2 * s) + s - 1
            span = ((col > bnd) & (col <= row)) | ((col > row) & (col <= bnd))
        e = jnp.exp(_dot_01(span.astype(BF16), prep.g_parts))
        q_side = ((qi % (2 * s)) < s) if rev else ((qi % (2 * s)) >= s)
        k_side = ((kj % (2 * s)) >= s) if rev else ((kj % (2 * s)) < s)
        keep = ((qi // (2 * s)) == (kj // (2 * s))) & q_side & k_side
        level = _gla_pair_scores((prep.q * e).astype(BF16), (prep.k * e).astype(BF16), keep)
        scores = [acc + a for acc, a in zip(scores, level)]
        s //= 2
    return scores


def _gla_local(scores, k_tail, v_ref):
    c = GLA_CHUNK
    n_pairs = B_HEADS // 2
    intra, upd = [], []
    for ci in range(k_tail.shape[0] // c):
        rows = slice(ci * c, (ci + 1) * c)
        intra_c, upd_c = [], []
        for p in range(n_pairs):
            a = scores[ci * n_pairs + p].astype(BF16)
            kt = _split_head_pair(k_tail[rows, p * LANES:(p + 1) * LANES])
            u = None
            for half in range(2):
                h = 2 * p + half
                vh = v_ref[rows, h * B_DV:(h + 1) * B_DV]
                intra_c.append(_dot(a[half * c:(half + 1) * c], vh))
                uh = _dot_tn(vh, kt[half * c:(half + 1) * c])
                u = uh if u is None else u + uh
            upd_c.append(u)
        intra.append(intra_c)
        upd.append(upd_c)
    return intra, upd


def _gla_states(s_ref, decay, upd, reverse):
    n = len(upd)
    order = range(n - 1, -1, -1) if reverse else range(n)
    starts = [None] * n
    s = [s_ref[p] for p in range(B_HEADS // 2)]
    for ci in order:
        starts[ci] = [sp.astype(BF16) for sp in s]
        s = [s[p] * decay[ci][:, p * LANES:(p + 1) * LANES] + upd[ci][p] for p in range(B_HEADS // 2)]
    for p in range(B_HEADS // 2):
        s_ref[p] = s[p]
    return starts


def _gla_finish(o_ref, q_state, starts, intra):
    c = GLA_CHUNK
    for ci in range(len(intra)):
        rows = slice(ci * c, (ci + 1) * c)
        for p in range(B_HEADS // 2):
            ps = slice(p * LANES, (p + 1) * LANES)
            inter = _dot_nt(_split_head_pair(q_state[rows, ps]), starts[ci][p])
            for half in range(2):
                h = 2 * p + half
                o_ref[rows, h * B_DV:(h + 1) * B_DV] = intra[ci][h] + inter[half * c:(half + 1) * c]


def _gla_kernel(qf_ref, kf_ref, vf_ref, gf_ref, qb_ref, kb_ref, vb_ref, gb_ref, s0f_ref, s0b_ref,
                of_ref, ob_ref, sf_out_ref, sb_out_ref, sf_ref, sb_ref):
    i = pl.program_id(1)

    @pl.when(i == 0)
    def _():
        sf_ref[...] = s0f_ref[...]
        sb_ref[...] = s0b_ref[...]

    fwd = _GlaPrep(qf_ref, kf_ref, gf_ref, False)
    bwd = _GlaPrep(qb_ref, kb_ref, gb_ref, True)

    def tile_body(score_fn):
        scores_f = score_fn(fwd)
        scores_b = score_fn(bwd)
        intra_f, upd_f = _gla_local(scores_f, fwd.k_tail, vf_ref)
        intra_b, upd_b = _gla_local(scores_b, bwd.k_tail, vb_ref)
        starts_f = _gla_states(sf_ref, fwd.decay, upd_f, False)
        starts_b = _gla_states(sb_ref, bwd.decay, upd_b, True)
        _gla_finish(of_ref, fwd.q_state, starts_f, intra_f)
        _gla_finish(ob_ref, bwd.q_state, starts_b, intra_b)

    in_fast_range = -jnp.min(jnp.concatenate(fwd.totals + bwd.totals, axis=0)) < GLA_FAST_RANGE

    @pl.when(in_fast_range)
    def _():
        tile_body(_gla_scores_fast)

    @pl.when(jnp.logical_not(in_fast_range))
    def _():
        tile_body(_gla_scores_safe)

    @pl.when(i == pl.num_programs(1) - 1)
    def _():
        sf_out_ref[...] = sf_ref[...]
        sb_out_ref[...] = sb_ref[...]


GLA_STATE_SHAPE = (B_HEADS // 2, B_DV, 2 * B_DK)


def _gla(q, k, v, gf, gb, s0f, s0b, *, tile):
    bsz, seq, _ = q.shape
    n = seq // tile
    fwd = lambda w: pl.BlockSpec((None, tile, w), lambda b, i: (b, i, 0))
    bwd = lambda w: pl.BlockSpec((None, tile, w), lambda b, i: (b, n - 1 - i, 0))
    st = pl.BlockSpec((None,) + GLA_STATE_SHAPE, lambda b, i: (b, 0, 0, 0))
    wk, wv = B_HEADS * B_DK, B_HEADS * B_DV
    st_shape = jax.ShapeDtypeStruct((bsz,) + GLA_STATE_SHAPE, F32)
    return pl.pallas_call(
        _gla_kernel,
        grid=(bsz, n),
        in_specs=[fwd(wk), fwd(wk), fwd(wv), fwd(wk), bwd(wk), bwd(wk), bwd(wv), bwd(wk), st, st],
        out_specs=[fwd(wv), bwd(wv), st, st],
        out_shape=[jax.ShapeDtypeStruct((bsz, seq, wv), F32), jax.ShapeDtypeStruct((bsz, seq, wv), F32),
                   st_shape, st_shape],
        scratch_shapes=[pltpu.VMEM(GLA_STATE_SHAPE, F32), pltpu.VMEM(GLA_STATE_SHAPE, F32)],
        compiler_params=_params(("parallel", "arbitrary")),
        name="gla_scan",
    )(q, k, v, gf, q, k, v, gb, s0f, s0b)


def _softmax_parts(scores, sink_tile):
    def lane_tiles(blocks):
        return [b[:, j:j + LANES] for b in blocks for j in range(0, b.shape[1], LANES)]

    tiles = lane_tiles(scores)
    if sink_tile is not None:
        tiles.append(sink_tile)
    m = functools.reduce(jnp.maximum, tiles).max(axis=-1, keepdims=True)
    ps = [jnp.exp2(s - m) for s in scores]
    acc = functools.reduce(jnp.add, lane_tiles(ps))
    if sink_tile is not None:
        lane = lax.broadcasted_iota(jnp.int32, sink_tile.shape, 1)
        acc = acc + jnp.where(lane == 0, jnp.exp2(sink_tile - m), 0.0)
    return ps, acc.sum(axis=-1, keepdims=True)


def _split_head_pair(t):
    lane = lax.broadcasted_iota(jnp.int32, t.shape, 1)
    zero = jnp.zeros_like(t)
    return jnp.concatenate([jnp.where(lane < HEAD_DIM, t, zero), jnp.where(lane >= HEAD_DIM, t, zero)], axis=0)


def _merge_head_pair(o):
    m = o.shape[0] // 2
    lane = lax.broadcasted_iota(jnp.int32, (m, LANES), 1)
    return jnp.where(lane < HEAD_DIM, o[:m], o[m:])


A_HEAD_ORDER = (0, 4, 1, 5, 2, 6, 3, 7)


def _win_kernel(sink_ref, q_ref, kp_ref, kc_ref, kn_ref, vp_ref, vc_ref, vn_ref, kx_ref, vx_ref, o_ref):
    n = pl.program_id(1)
    nb = pl.num_programs(1)
    n_tiles = A_HEADS // 2
    q = jnp.concatenate([_split_head_pair(q_ref[:, j * LANES:(j + 1) * LANES]) for j in range(n_tiles)], axis=0)
    sink_tile = jnp.concatenate([jnp.full((A_BLOCK, LANES), sink_ref[h], F32) for h in A_HEAD_ORDER], axis=0)
    qi = lax.broadcasted_iota(jnp.int32, (A_BLOCK, A_BLOCK), 0)
    kj = lax.broadcasted_iota(jnp.int32, (A_BLOCK, A_BLOCK), 1)
    neg = jnp.full((A_BLOCK, A_BLOCK), -jnp.inf, F32)
    zero = jnp.zeros((A_BLOCK, A_BLOCK), F32)
    prev_mask = jnp.where(n > 0, jnp.where(kj >= qi, zero, neg), neg)
    next_mask = jnp.where(n < nb - 1, jnp.where(kj <= qi, zero, neg), neg)
    s_p = _dot_nt(q, kp_ref[...]) + jnp.tile(prev_mask, (A_HEADS, 1))
    s_c = _dot_nt(q, kc_ref[...])
    s_n = _dot_nt(q, kn_ref[...]) + jnp.tile(next_mask, (A_HEADS, 1))
    s_x = _dot_nt(q, kx_ref[...])
    (p_p, p_c, p_n, p_x), denom = _softmax_parts([s_p, s_c, s_n, s_x], sink_tile)
    o = (_dot(p_p.astype(BF16), vp_ref[...]) + _dot(p_c.astype(BF16), vc_ref[...])
         + _dot(p_n.astype(BF16), vn_ref[...]) + _dot(p_x.astype(BF16), vx_ref[...]))
    o = o / denom
    for j in range(n_tiles):
        o_ref[:, j * LANES:(j + 1) * LANES] = _merge_head_pair(
            o[j * 2 * A_BLOCK:(j + 1) * 2 * A_BLOCK]).astype(o_ref.dtype)


def _window_attention(sink, q, k, v, k_ctx, v_ctx):
    bsz, seq, wq = q.shape
    nb = seq // A_BLOCK
    wkv = k.shape[-1]
    n_ctx = k_ctx.shape[1]
    kv_spec = lambda f: pl.BlockSpec((None, A_BLOCK, wkv), lambda b, n: (b, f(n), 0))
    prev = lambda n: jnp.maximum(n - 1, 0)
    cur = lambda n: n
    nxt = lambda n: jnp.minimum(n + 1, nb - 1)
    ctx_spec = pl.BlockSpec((None, n_ctx, wkv), lambda b, n: (b, 0, 0))
    return pl.pallas_call(
        _win_kernel,
        grid=(bsz, nb),
        in_specs=[
            pl.BlockSpec(memory_space=pltpu.SMEM),
            pl.BlockSpec((None, A_BLOCK, wq), lambda b, n: (b, n, 0)),
            kv_spec(prev), kv_spec(cur), kv_spec(nxt),
            kv_spec(prev), kv_spec(cur), kv_spec(nxt),
            ctx_spec, ctx_spec,
        ],
        out_specs=pl.BlockSpec((None, A_BLOCK, wq), lambda b, n: (b, n, 0)),
        out_shape=jax.ShapeDtypeStruct((bsz, seq, wq), BF16),
        compiler_params=_params(("parallel", "arbitrary")),
        name="window_attn",
    )(sink, q, k, k, k, v, v, v, k_ctx, v_ctx)


def _dense_kernel(sink_ref, q_ref, k_ref, v_ref, o_ref, *, head_order, shared_kv):
    n_q = q_ref.shape[0]
    for j in range(q_ref.shape[1] // LANES):
        qs = slice(j * LANES, (j + 1) * LANES)
        ks = slice(0, LANES) if shared_kv else qs
        s = _dot_nt(_split_head_pair(q_ref[:, qs]), k_ref[:, ks])
        sink_tile = None
        if head_order is not None:
            sink_tile = jnp.concatenate(
                [jnp.full((n_q, LANES), sink_ref[head_order[2 * j + half]], F32) for half in range(2)], axis=0)
        (p,), denom = _softmax_parts([s], sink_tile)
        o_ref[:, qs] = _merge_head_pair(_dot(p.astype(BF16), v_ref[:, ks]) / denom).astype(o_ref.dtype)


def _dense_attention(sink, q, k, v, *, head_order, shared_kv):
    bsz, n_q, wq = q.shape
    wkv = k.shape[-1]
    full = lambda w: pl.BlockSpec((None, n_q, w), lambda b: (b, 0, 0))
    return pl.pallas_call(
        functools.partial(_dense_kernel, head_order=head_order, shared_kv=shared_kv),
        grid=(bsz,),
        in_specs=[pl.BlockSpec(memory_space=pltpu.SMEM), full(wq), full(wkv), full(wkv)],
        out_specs=full(wq),
        out_shape=jax.ShapeDtypeStruct((bsz, n_q, wq), BF16),
        compiler_params=_params(("parallel",)),
        name="ctx_dense_attn_sink" if shared_kv else "ctx_dense_attn",
    )(sink, q, k, v)


NA_ROWS_PER_STEP = 8


N_NA_INPUTS = 6


def _na_kernel(*refs):
    q_ref, k_ref, v_ref, kx_ref, vx_ref, bias_ref = refs[:N_NA_INPUTS]
    n_side = (len(refs) - N_NA_INPUTS - 1) // 2
    o_ref = refs[N_NA_INPUTS + n_side]
    _run_side_casts(refs[N_NA_INPUTS:N_NA_INPUTS + n_side], refs[N_NA_INPUTS + n_side + 1:])
    step = pl.program_id(1)
    grid_rows = k_ref.shape[0] // GRID_W
    n_keys = C_WIN_ROWS * GRID_W
    n_pairs = C_HEADS // 2
    for rr in range(NA_ROWS_PER_STEP):
        r = step * NA_ROWS_PER_STEP + rr
        row0 = jnp.clip(r - C_WIN_ROWS // 2, 0, grid_rows - C_WIN_ROWS)
        d_row0 = row0 - r + (C_WIN_ROWS - 1)
        key0 = pl.multiple_of(row0 * GRID_W, GRID_W)
        qrows = slice(rr * GRID_W, (rr + 1) * GRID_W)
        s_l, s_x = [], []
        for p in range(n_pairs):
            ps = slice(p * LANES, (p + 1) * LANES)
            q2 = _split_head_pair(q_ref[qrows, ps])
            s_l.append(_dot_nt(q2, k_ref[pl.ds(key0, n_keys), ps]))
            s_x.append(_dot_nt(q2, kx_ref[:, ps]))
        bias = jnp.concatenate(
            [bias_ref[d_row0 + 2 * j].reshape(C_HEADS * GRID_W, LANES) for j in range(C_WIN_ROWS // 2)], axis=1)
        (p_l, p_x), denom = _softmax_parts([jnp.concatenate(s_l, axis=0) + bias, jnp.concatenate(s_x, axis=0)], None)
        p_l = p_l.astype(BF16)
        p_x = p_x.astype(BF16)
        for p in range(n_pairs):
            ps = slice(p * LANES, (p + 1) * LANES)
            pr = slice(p * 2 * GRID_W, (p + 1) * 2 * GRID_W)
            o = (_dot(p_l[pr], v_ref[pl.ds(key0, n_keys), ps]) + _dot(p_x[pr], vx_ref[:, ps])) / denom[pr]
            o_ref[qrows, ps] = _merge_head_pair(o).astype(o_ref.dtype)


def _neighbourhood_attention(q, k, v, k_ctx, v_ctx, bias_tbl, cast_weights=(), layer=0):
    bsz, seq, w = q.shape
    n_ctx = k_ctx.shape[1]
    tq = NA_ROWS_PER_STEP * GRID_W
    n_tiles = seq // tq
    whole = pl.BlockSpec((None, seq, w), lambda b, i: (b, 0, 0))
    ctx_spec = pl.BlockSpec((None, n_ctx, w), lambda b, i: (b, 0, 0))
    side = _SideCasts(cast_weights, layer, bsz * n_tiles, lambda b, i: b * n_tiles + i)
    outs = pl.pallas_call(
        _na_kernel,
        grid=(bsz, n_tiles),
        in_specs=[pl.BlockSpec((None, tq, w), lambda b, i: (b, i, 0)), whole, whole, ctx_spec, ctx_spec,
                  _const_spec(bias_tbl.shape)] + side.specs,
        out_specs=[pl.BlockSpec((None, tq, w), lambda b, i: (b, i, 0))] + side.out_specs,
        out_shape=[jax.ShapeDtypeStruct((bsz, seq, w), BF16)] + side.out_shapes,
        compiler_params=_params(("parallel", "arbitrary")),
        name="neighbourhood_attn",
    )(q, k, v, k_ctx, v_ctx, bias_tbl, *side.arrays)
    return outs[0], side.finish(outs[1:])


def _na_bias_table(rpb):
    qcol = jnp.arange(GRID_W)[:, None]
    kcol = jnp.arange(GRID_W)[None, :]
    wstart = jnp.clip(qcol - C_WIN_COLS // 2, 0, GRID_W - C_WIN_COLS)
    ok = (kcol >= wstart) & (kcol < wstart + C_WIN_COLS)
    n_off = C_WIN_COLS - 1
    period = 2 * GRID_W
    vec = jnp.concatenate([rpb[..., n_off:].astype(F32),
                           jnp.zeros(rpb.shape[:2] + (period - 2 * n_off - 1,), F32),
                           rpb[..., :n_off].astype(F32)], axis=-1)
    toep = jnp.tile(vec, (1, 1, GRID_W))[..., :GRID_W * (period - 1)]
    toep = toep.reshape(rpb.shape[:2] + (GRID_W, period - 1))[..., :GRID_W]
    tbl = jnp.where(ok[None, None], toep * LOG2E, -jnp.inf)
    tbl = jnp.moveaxis(tbl, 1, 0)
    return jnp.concatenate([tbl[:-1], tbl[1:]], axis=-1)


def _merge_kernel(x_ref, mod_ref, gain_ref, ya_ref, of_ref, ob_ref, gr_ref, yc_ref, gn_ref,
                  wm_ref, bm_ref, wa_ref, wb_ref, wc_ref, wo_ref, o_ref):
    d = D_MODEL
    gn = gn_ref[...]
    x = x_ref[...]
    h = _norm_mod(x, gain_ref[...], mod_ref[0:1, :], mod_ref[1:2, :]).astype(BF16)
    o_sum = of_ref[...] + ob_ref[...]
    parts = []
    for hh in range(B_HEADS):
        oh = o_sum[:, hh * B_DV:(hh + 1) * B_DV]
        parts.append((oh * lax.rsqrt(jnp.mean(oh * oh, axis=-1, keepdims=True) + EPS)) * gn)
    y_b = (jnp.concatenate(parts, axis=1) * _silu(gr_ref[...])).astype(BF16)
    mixed = None
    for j, (y, w_ref) in enumerate(((ya_ref[...], wa_ref), (y_b, wb_ref), (yc_ref[...], wc_ref))):
        gate = _sigmoid(_dot(h, wm_ref[:, j * d:(j + 1) * d]) + bm_ref[:, j * d:(j + 1) * d])
        term = gate * _dot(y, w_ref[...])
        mixed = term if mixed is None else mixed + term
    o_ref[...] = x + mod_ref[2:3, :] * _dot(mixed.astype(BF16), wo_ref[...])


def _merge(x, mod_rows, gain, y_a, o_f, o_b, g_r, y_c, gla_gain, w_merge, b_merge, w_a, w_b, w_c, w_out, *, tm):
    bsz, seq, d = x.shape
    per_batch_mod = mod_rows.shape[0] > 1
    mod_map = (lambda b, i: (b, 0, 0)) if per_batch_mod else (lambda b, i: (0, 0, 0))
    tile = lambda w: pl.BlockSpec((None, tm, w), lambda b, i: (b, i, 0))
    wy = y_a.shape[-1]
    return pl.pallas_call(
        _merge_kernel,
        grid=(bsz, seq // tm),
        in_specs=[tile(d), pl.BlockSpec((None, 6, d), mod_map), _const_spec((1, d)),
                  tile(wy), tile(wy), tile(wy), tile(wy), tile(wy), _const_spec((1, B_DV)),
                  _const_spec(w_merge.shape), _const_spec(b_merge.shape),
                  _const_spec(w_a.shape), _const_spec(w_b.shape), _const_spec(w_c.shape),
                  _const_spec(w_out.shape)],
        out_specs=tile(d),
        out_shape=jax.ShapeDtypeStruct((bsz, seq, d), F32),
        compiler_params=_params(("parallel", "arbitrary")),
        name="merge_out",
    )(x, mod_rows, gain, y_a, o_f, o_b, g_r, y_c, gla_gain, w_merge, b_merge, w_a, w_b, w_c, w_out)


FFN_CHUNKS = ((0, 1024), (1024, 2048), (2048, FFN_HIDDEN))
FFN_SUB_ROWS = 256


def _ffn_kernel(x_ref, mod_ref, gain_ref, w1_ref, w2_ref, fg_ref, o_ref, *, final):
    for r0 in range(0, x_ref.shape[0], FFN_SUB_ROWS):
        rows = slice(r0, r0 + FFN_SUB_ROWS)
        x = x_ref[rows, :]
        h = _norm_mod(x, gain_ref[...], mod_ref[3:4, :], mod_ref[4:5, :]).astype(BF16)
        acc = None
        for c0, c1 in FFN_CHUNKS:
            gate = _dot(h, w1_ref[:, c0:c1])
            up = _dot(h, w1_ref[:, FFN_HIDDEN + c0:FFN_HIDDEN + c1])
            part = _dot((_silu(gate) * up).astype(BF16), w2_ref[c0:c1, :])
            acc = part if acc is None else acc + part
        y = x + mod_ref[5:6, :] * acc
        if final:
            y = (y * lax.rsqrt(jnp.mean(y * y, axis=-1, keepdims=True) + EPS)) * fg_ref[...]
        o_ref[rows, :] = y


def _ffn(x, mod_rows, gain, w1, w2, final_gain, *, final, tm):
    bsz, seq, d = x.shape
    per_batch_mod = mod_rows.shape[0] > 1
    mod_map = (lambda b, i: (b, 0, 0)) if per_batch_mod else (lambda b, i: (0, 0, 0))
    tile = pl.BlockSpec((None, tm, d), lambda b, i: (b, i, 0))
    return pl.pallas_call(
        functools.partial(_ffn_kernel, final=final),
        grid=(bsz, seq // tm),
        in_specs=[tile, pl.BlockSpec((None, 6, d), mod_map), _const_spec((1, d)),
                  _const_spec(w1.shape), _const_spec(w2.shape), _const_spec((1, d))],
        out_specs=tile,
        out_shape=jax.ShapeDtypeStruct((bsz, seq, d), F32),
        compiler_params=_params(("parallel", "arbitrary")),
        name="ffn_final" if final else "ffn",
    )(x, mod_rows, gain, w1, w2, final_gain)


def _rope_tables(seq):
    t = jnp.arange(seq)
    row = (t // GRID_W).astype(F32)
    col = (t % GRID_W).astype(F32)
    n_freq = HEAD_DIM // 4
    inv = ROPE_BASE ** (-jnp.arange(n_freq, dtype=F32) / n_freq)
    ang = jnp.concatenate([row[:, None] * inv[None], col[:, None] * inv[None]], axis=-1)
    cos, sin = jnp.cos(ang), jnp.sin(ang)
    return jnp.tile(cos, (1, 4)), jnp.tile(jnp.concatenate([-sin, sin], axis=-1), (1, 2))


def _permute_w_branch_a(w):
    return w.reshape(A_HEADS, HEAD_DIM, w.shape[1])[jnp.array(A_HEAD_ORDER)].reshape(w.shape).astype(BF16)


def _gate_weights(w_fwd, b_fwd, w_bwd, b_bwd):
    n = B_HEADS * B_DK
    r = B_GATE_RANK
    w = jnp.zeros((LANES, 2 * n), F32)
    w = w.at[:r, :n].set(w_fwd).at[r:2 * r, n:].set(w_bwd)
    return w.astype(BF16), jnp.concatenate([b_fwd, b_bwd])[None, :]


def kernel(x, c, ctx, c_ctx, w_ada, b_ada, norm_mix, w_in, attn_sink, gla_gate_w_fwd, gla_gate_b_fwd,
           gla_gate_w_bwd, gla_gate_b_bwd, gla_norm, na_rpb, w_branch_a, w_branch_b, w_branch_c,
           w_merge, b_merge, w_out, norm_ffn, w_ffn_in, w_ffn_out, final_norm):
    bsz, seq, d = x.shape
    n_ctx = ctx.shape[1]
    cos, sin = _rope_tables(seq)

    cvec = jnp.zeros((8, d), F32).at[:bsz].set(c).at[bsz].set(c_ctx)
    mod, w_perm_all = _modulation(cvec, w_ada, b_ada, w_in)
    mod = mod.reshape(DEPTH, 8, 6, d)

    xc = ctx
    zeros_state = jnp.zeros((bsz,) + GLA_STATE_SHAPE, F32)
    for l in range(DEPTH):
        last = l == DEPTH - 1
        mod_x = mod[l, :bsz]
        mod_c = mod[l, bsz:bsz + 1]
        gain_m = norm_mix[l][None, :]
        w_perm = w_perm_all
        w_gate, b_gate = _gate_weights(gla_gate_w_fwd[l], gla_gate_b_fwd[l], gla_gate_w_bwd[l], gla_gate_b_bwd[l])

        (caq, cak, cav, cgq, cgk, cgv, cgr, cnq, cnk, cnv, cgf, cgb), _ = _project(
            xc, mod_c, gain_m, w_perm, w_gate, b_gate, cos, sin, rope=False, tm=n_ctx, layer=l)
        (aq, ak, av, gq, gk, gv, gr, nq, nk, nv, gf, gb), (wm_bf, wb_bf, wc_bf, wo_bf) = _project(
            x, mod_x, gain_m, w_perm, w_gate, b_gate, cos, sin, rope=True, tm=512,
            cast_weights=(w_merge, w_branch_b, w_branch_c, w_out), layer=l)

        co_f, co_b, s_f, s_b = _gla(cgq, cgk, cgv, cgf, cgb, zeros_state, zeros_state, tile=n_ctx)
        o_f, o_b, _, _ = _gla(gq, gk, gv, gf, gb, s_f, s_b, tile=256)

        sink = attn_sink[l] * LOG2E
        y_a = _window_attention(sink, aq, ak, av, cak, cav)
        y_c, (w1, w2) = _neighbourhood_attention(nq, nk, nv, cnk, cnv, _na_bias_table(na_rpb[l]),
                                                 cast_weights=(w_ffn_in, w_ffn_out), layer=l)

        merge_w = (gla_norm[l][None, :], wm_bf, b_merge[l][None, :],
                   _permute_w_branch_a(w_branch_a[l]), wb_bf, wc_bf, wo_bf)
        x = _merge(x, mod_x, gain_m, y_a, o_f, o_b, gr, y_c, *merge_w, tm=512)

        gain_f = norm_ffn[l][None, :]
        fg = final_norm[None, :]
        if not last:
            yc_a = _dense_attention(sink, caq, cak, cav, head_order=A_HEAD_ORDER, shared_kv=True)
            yc_c = _dense_attention(sink, cnq, cnk, cnv, head_order=None, shared_kv=False)
            xc = _merge(xc, mod_c, gain_m, yc_a, co_f, co_b, cgr, yc_c, *merge_w, tm=n_ctx)
            xc = _ffn(xc, mod_c, gain_f, w1, w2, fg, final=False, tm=n_ctx)
        x = _ffn(x, mod_x, gain_f, w1, w2, fg, final=last, tm=1024)
    return x
```

```python
import functools

import jax
import jax.numpy as jnp
from jax import lax
from jax.experimental import pallas as pl
from jax.experimental.pallas import tpu as pltpu

F32 = jnp.float32
BF16 = jnp.bfloat16

D_MODEL = 1024
DEPTH = 2
GRID_W = 64
HEAD_DIM = 64
EPS = 1e-6
ROPE_BASE = 10000.0
A_HEADS = 8
A_KV_HEADS = 2
A_BLOCK = 128
B_HEADS = 4
B_DK = 64
B_DV = 128
B_GATE_RANK = 16
B_GATE_NORM = 16.0
C_HEADS = 8
C_WIN_ROWS = 8
C_WIN_COLS = 16
FFN_HIDDEN = 2816

VMEM_LIMIT_BYTES = 56 * 1024 * 1024
LANES = 128

SEG_AQ = (0, 512)
SEG_AKV = (512, 768)
SEG_GQ = (768, 1024)
SEG_GK = (1024, 1280)
SEG_GV = (1280, 1792)
SEG_GR = (1792, 2304)
SEG_NQ = (2304, 2816)
SEG_NK = (2816, 3328)
SEG_NV = (3328, 3840)
SEG_GA = (3840, 3968)
IN_PADDED = 3968
W_IN_GATE_COLS = (2304, 2304 + 2 * B_GATE_RANK)

LOG2E = 1.4426950408889634
ATTN_Q_SCALE = HEAD_DIM ** -0.5 * LOG2E

GLA_CHUNK = 64
GLA_FAST_RANGE = 120.0


def _params(sem):
    return pltpu.CompilerParams(dimension_semantics=sem, vmem_limit_bytes=VMEM_LIMIT_BYTES)


def _const_spec(shape):
    nd = len(shape)
    return pl.BlockSpec(shape, lambda *_: (0,) * nd, pipeline_mode=pl.Buffered(1))


def _sigmoid(x):
    return 1.0 / (1.0 + jnp.exp(-x))


def _silu(x):
    return x * _sigmoid(x)


def _dot(a, b):
    return jnp.dot(a, b, preferred_element_type=F32)


def _dot_nt(a, b):
    return lax.dot_general(a, b, (((1,), (1,)), ((), ())), preferred_element_type=F32)


def _dot_tn(a, b):
    return lax.dot_general(a, b, (((0,), (0,)), ((), ())), preferred_element_type=F32)


def _norm_mod(x, gain, shift, scale):
    y = x * lax.rsqrt(jnp.mean(x * x, axis=-1, keepdims=True) + EPS)
    return (y * gain) * (1.0 + scale) + shift


def _relayout_w_in(wt_ref, o_ref):
    n_aq = SEG_AQ[1]
    ga0, ga1 = W_IN_GATE_COLS
    pieces = [wt_ref[h * HEAD_DIM:(h + 1) * HEAD_DIM, :] for h in A_HEAD_ORDER]
    pieces += [wt_ref[n_aq:ga0, :], wt_ref[ga1:, :], wt_ref[ga0:ga1, :],
               jnp.zeros((IN_PADDED - wt_ref.shape[0], wt_ref.shape[1]), F32)]
    o_ref[...] = jnp.concatenate(pieces, axis=0).T.astype(BF16)


def _mod_kernel(c_ref, w_ref, b_ref, w_in_ref, o_ref, w_perm_ref):
    s = _silu(c_ref[...])
    o_ref[...] = _dot(s.astype(BF16), w_ref[...].astype(BF16)) + b_ref[...]
    _relayout_w_in(w_in_ref, w_perm_ref)


def _modulation(cvec, w_ada, b_ada, w_in):
    tn = 1536
    n_out = w_ada.shape[-1]
    n_j = n_out // tn
    rows = w_in.shape[1] // n_j
    w_in_t = jnp.swapaxes(w_in, 1, 2)
    return pl.pallas_call(
        _mod_kernel,
        grid=(DEPTH, n_j),
        in_specs=[
            pl.BlockSpec((8, D_MODEL), lambda l, j: (0, 0)),
            pl.BlockSpec((None, D_MODEL, tn), lambda l, j: (l, 0, j)),
            pl.BlockSpec((None, 1, tn), lambda l, j: (l, 0, j)),
            pl.BlockSpec((None, w_in_t.shape[1], rows), lambda l, j: (l, 0, j)),
        ],
        out_specs=[pl.BlockSpec((None, 8, tn), lambda l, j: (l, 0, j)),
                   pl.BlockSpec((None, rows, IN_PADDED), lambda l, j: (l, j, 0))],
        out_shape=[jax.ShapeDtypeStruct((DEPTH, 8, n_out), F32),
                   jax.ShapeDtypeStruct((DEPTH, w_in.shape[1], IN_PADDED), BF16)],
        compiler_params=_params(("arbitrary", "arbitrary")),
        name="adaln_mod",
    )(cvec, w_ada, b_ada.reshape(DEPTH, 1, n_out), w_in_t)


def _rope(t, cos, sin):
    n = t.shape[1]
    lane = lax.broadcasted_iota(jnp.int32, t.shape, 1)
    first_half = (lane % HEAD_DIM) < (HEAD_DIM // 2)
    rot = jnp.where(first_half, pltpu.roll(t, n - HEAD_DIM // 2, 1), pltpu.roll(t, HEAD_DIM // 2, 1))
    reps = n // LANES
    return t * jnp.tile(cos, (1, reps)) + rot * jnp.tile(sin, (1, reps))


class _SideCasts:
    def __init__(self, stacked_weights, layer, n_steps, step_index):
        self.shapes2d = [w.shape[1:] for w in stacked_weights]
        self.arrays = [w.reshape(w.shape[0], n_steps, w.shape[1] // n_steps, w.shape[2]) for w in stacked_weights]
        self.specs = [pl.BlockSpec((None, None) + a.shape[2:], lambda *g: (layer, step_index(*g), 0, 0))
                      for a in self.arrays]
        self.out_specs = [pl.BlockSpec((None,) + a.shape[2:], lambda *g: (step_index(*g), 0, 0))
                          for a in self.arrays]
        self.out_shapes = [jax.ShapeDtypeStruct(a.shape[1:], BF16) for a in self.arrays]

    def finish(self, outs):
        return [o.reshape(s) for o, s in zip(outs, self.shapes2d)]


def _run_side_casts(in_refs, out_refs):
    for w_ref, o_ref in zip(in_refs, out_refs):
        o_ref[...] = w_ref[...].astype(BF16)


N_PROJ_INPUTS = 8
N_PROJ_OUTPUTS = 12


def _proj_kernel(*refs, rope):
    (x_ref, mod_ref, gain_ref, w_ref, wg_ref, bg_ref, cos_ref, sin_ref) = refs[:N_PROJ_INPUTS]
    n_side = (len(refs) - N_PROJ_INPUTS - N_PROJ_OUTPUTS) // 2
    outs = refs[N_PROJ_INPUTS + n_side:]
    (aq_ref, ak_ref, av_ref, gq_ref, gk_ref, gv_ref, gr_ref, nq_ref, nk_ref, nv_ref,
     gf_ref, gb_ref) = outs[:N_PROJ_OUTPUTS]
    _run_side_casts(refs[N_PROJ_INPUTS:N_PROJ_INPUTS + n_side], outs[N_PROJ_OUTPUTS:])
    scale = ATTN_Q_SCALE
    n_g = B_HEADS * B_DK
    h = _norm_mod(x_ref[...], gain_ref[...], mod_ref[0:1, :], mod_ref[1:2, :]).astype(BF16)

    def seg(s):
        return _dot(h, w_ref[:, s[0]:s[1]])

    aq = seg(SEG_AQ)
    akv = seg(SEG_AKV)
    z = _dot(seg(SEG_GA).astype(BF16), wg_ref[...]) + bg_ref[...]
    ak = akv[:, :LANES]
    if rope:
        cos = cos_ref[...]
        sin = sin_ref[...]
        aq = _rope(aq, cos, sin)
        ak = _rope(ak, cos, sin)
    aq_ref[...] = (aq * scale).astype(BF16)
    ak_ref[...] = ak.astype(BF16)
    av_ref[...] = akv[:, LANES:].astype(BF16)
    log_sig = jnp.minimum(z, 0.0) - jnp.log1p(jnp.exp(-jnp.abs(z)))
    g = log_sig / B_GATE_NORM
    gf_ref[...] = g[:, :n_g]
    gb_ref[...] = g[:, n_g:]
    gq_ref[...] = seg(SEG_GQ) * (B_DK ** -0.5)
    gk_ref[...] = seg(SEG_GK)
    gv_ref[...] = seg(SEG_GV).astype(BF16)
    gr_ref[...] = seg(SEG_GR)
    nq_ref[...] = (seg(SEG_NQ) * scale).astype(BF16)
    nk_ref[...] = seg(SEG_NK).astype(BF16)
    nv_ref[...] = seg(SEG_NV).astype(BF16)


def _project(x, mod_rows, gain, w_perm, w_gate, b_gate, cos, sin, *, rope, tm, cast_weights=(), layer=0):
    bsz, seq, _ = x.shape
    n_tiles = seq // tm
    per_batch_mod = mod_rows.shape[0] > 1
    mod_map = (lambda b, i: (b, 0, 0)) if per_batch_mod else (lambda b, i: (0, 0, 0))
    widths = [(512, BF16), (128, BF16), (128, BF16), (256, F32), (256, F32), (512, BF16), (512, F32),
              (512, BF16), (512, BF16), (512, BF16), (256, F32), (256, F32)]
    assert len(widths) == N_PROJ_OUTPUTS
    tile = lambda w: pl.BlockSpec((None, tm, w), lambda b, i: (b, i, 0))
    side = _SideCasts(cast_weights, layer, bsz * n_tiles, lambda b, i: b * n_tiles + i)
    outs = pl.pallas_call(
        functools.partial(_proj_kernel, rope=rope),
        grid=(bsz, n_tiles),
        in_specs=[
            tile(D_MODEL),
            pl.BlockSpec((None, 6, D_MODEL), mod_map),
            _const_spec((1, D_MODEL)),
            pl.BlockSpec((None, D_MODEL, IN_PADDED), lambda b, i: (layer, 0, 0), pipeline_mode=pl.Buffered(1)),
            _const_spec((LANES, 2 * B_HEADS * B_DK)),
            _const_spec((1, 2 * B_HEADS * B_DK)),
            pl.BlockSpec((tm, LANES), lambda b, i: (i, 0)),
            pl.BlockSpec((tm, LANES), lambda b, i: (i, 0)),
        ] + side.specs,
        out_specs=[tile(w) for w, _ in widths] + side.out_specs,
        out_shape=[jax.ShapeDtypeStruct((bsz, seq, w), dt) for w, dt in widths] + side.out_shapes,
        compiler_params=_params(("parallel", "arbitrary")),
        name="in_proj_rope" if rope else "in_proj_ctx",
    )(x, mod_rows, gain, w_perm, w_gate, b_gate, cos, sin, *side.arrays)
    return outs[:N_PROJ_OUTPUTS], side.finish(outs[N_PROJ_OUTPUTS:])


class _GlaPrep:
    def __init__(self, q_ref, k_ref, g_ref, reverse):
        t = q_ref.shape[0]
        c = GLA_CHUNK
        row = lax.broadcasted_iota(jnp.int32, (t, t), 0)
        col = lax.broadcasted_iota(jnp.int32, (t, t), 1)
        same_chunk = (row // c) == (col // c)
        tri = (same_chunk & ((col >= row) if reverse else (col <= row))).astype(BF16)
        self.reverse = reverse
        self.q = q_ref[...]
        self.k = k_ref[...]
        self.g_parts = _split3(g_ref[...])
        self.b = _dot_01(tri, self.g_parts)
        b = self.b
        self.totals = [b[ci * c:ci * c + 1, :] if reverse else b[ci * c + c - 1:ci * c + c, :]
                       for ci in range(t // c)]
        self.b_end = jnp.concatenate([jnp.broadcast_to(e, (c, e.shape[1])) for e in self.totals], axis=0)
        self.q_state = (self.q * jnp.exp(b)).astype(BF16)
        self.k_tail = (self.k * jnp.exp(self.b_end - b)).astype(BF16)
        self.decay = [jnp.exp(e) for e in self.totals]


def _split3(x):
    hi = x.astype(BF16)
    rest = x - hi.astype(F32)
    mid = rest.astype(BF16)
    return hi, mid, (rest - mid.astype(F32)).astype(BF16)


def _dot_01(m01, parts):
    return _dot(m01, parts[0]) + _dot(m01, parts[1]) + _dot(m01, parts[2])


def _gla_pair_scores(q_w, k_w, keep):
    c = GLA_CHUNK
    out = []
    for ci in range(q_w.shape[0] // c):
        rows = slice(ci * c, (ci + 1) * c)
        for p in range(B_HEADS // 2):
            ps = slice(p * LANES, (p + 1) * LANES)
            out.append(jnp.where(keep, _dot_nt(_split_head_pair(q_w[rows, ps]), k_w[rows, ps]), 0.0))
    return out


def _gla_scores_fast(prep):
    c = GLA_CHUNK
    qi = lax.broadcasted_iota(jnp.int32, (2 * c, c), 0) % c
    kj = lax.broadcasted_iota(jnp.int32, (2 * c, c), 1)
    keep = (kj >= qi) if prep.reverse else (kj <= qi)
    ref = 0.5 * prep.b_end
    q_in = (prep.q * jnp.exp(prep.b - ref)).astype(BF16)
    k_in = (prep.k * jnp.exp(ref - prep.b)).astype(BF16)
    return _gla_pair_scores(q_in, k_in, keep)


def _gla_scores_safe(prep):
    t = prep.q.shape[0]
    c = GLA_CHUNK
    rev = prep.reverse
    row = lax.broadcasted_iota(jnp.int32, (t, t), 0)
    col = lax.broadcasted_iota(jnp.int32, (t, t), 1)
    qi = lax.broadcasted_iota(jnp.int32, (2 * c, c), 0) % c
    kj = lax.broadcasted_iota(jnp.int32, (2 * c, c), 1)
    scores = _gla_pair_scores(prep.q.astype(BF16), prep.k.astype(BF16), qi == kj)
    s = c // 2
    while s >= 1:
        if rev:
            bnd = row - row % (2 * s) + s
            span = ((col >= bnd) & (col < row)) | ((col >= row) & (col < bnd))
        else:
            bnd = row - row % (2 * s) + s - 1
            span = ((col > bnd) & (col <= row)) | ((col > row) & (col <= bnd))
        e = jnp.exp(_dot_01(span.astype(BF16), prep.g_parts))
        q_side = ((qi % (2 * s)) < s) if rev else ((qi % (2 * s)) >= s)
        k_side = ((kj % (2 * s)) >= s) if rev else ((kj % (2 * s)) < s)
        keep = ((qi // (2 * s)) == (kj // (2 * s))) & q_side & k_side
        level = _gla_pair_scores((prep.q * e).astype(BF16), (prep.k * e).astype(BF16), keep)
        scores = [acc + a for acc, a in zip(scores, level)]
        s //= 2
    return scores


def _gla_local(scores, k_tail, v_ref):
    c = GLA_CHUNK
    n_pairs = B_HEADS // 2
    intra, upd = [], []
    for ci in range(k_tail.shape[0] // c):
        rows = slice(ci * c, (ci + 1) * c)
        intra_c, upd_c = [], []
        for p in range(n_pairs):
            a = scores[ci * n_pairs + p].astype(BF16)
            kt = _split_head_pair(k_tail[rows, p * LANES:(p + 1) * LANES])
            u = None
            for half in range(2):
                h = 2 * p + half
                vh = v_ref[rows, h * B_DV:(h + 1) * B_DV]
                intra_c.append(_dot(a[half * c:(half + 1) * c], vh))
                uh = _dot_tn(vh, kt[half * c:(half + 1) * c])
                u = uh if u is None else u + uh
            upd_c.append(u)
        intra.append(intra_c)
        upd.append(upd_c)
    return intra, upd


def _gla_states(s_ref, decay, upd, reverse):
    n = len(upd)
    order = range(n - 1, -1, -1) if reverse else range(n)
    starts = [None] * n
    s = [s_ref[p] for p in range(B_HEADS // 2)]
    for ci in order:
        starts[ci] = [sp.astype(BF16) for sp in s]
        s = [s[p] * decay[ci][:, p * LANES:(p + 1) * LANES] + upd[ci][p] for p in range(B_HEADS // 2)]
    for p in range(B_HEADS // 2):
        s_ref[p] = s[p]
    return starts


def _gla_finish(o_ref, q_state, starts, intra):
    c = GLA_CHUNK
    for ci in range(len(intra)):
        rows = slice(ci * c, (ci + 1) * c)
        for p in range(B_HEADS // 2):
            ps = slice(p * LANES, (p + 1) * LANES)
            inter = _dot_nt(_split_head_pair(q_state[rows, ps]), starts[ci][p])
            for half in range(2):
                h = 2 * p + half
                o_ref[rows, h * B_DV:(h + 1) * B_DV] = intra[ci][h] + inter[half * c:(half + 1) * c]


def _gla_kernel(qf_ref, kf_ref, vf_ref, gf_ref, qb_ref, kb_ref, vb_ref, gb_ref, s0f_ref, s0b_ref,
                of_ref, ob_ref, sf_out_ref, sb_out_ref, sf_ref, sb_ref):
    i = pl.program_id(1)

    @pl.when(i == 0)
    def _():
        sf_ref[...] = s0f_ref[...]
        sb_ref[...] = s0b_ref[...]

    fwd = _GlaPrep(qf_ref, kf_ref, gf_ref, False)
    bwd = _GlaPrep(qb_ref, kb_ref, gb_ref, True)

    def tile_body(score_fn):
        scores_f = score_fn(fwd)
        scores_b = score_fn(bwd)
        intra_f, upd_f = _gla_local(scores_f, fwd.k_tail, vf_ref)
        intra_b, upd_b = _gla_local(scores_b, bwd.k_tail, vb_ref)
        starts_f = _gla_states(sf_ref, fwd.decay, upd_f, False)
        starts_b = _gla_states(sb_ref, bwd.decay, upd_b, True)
        _gla_finish(of_ref, fwd.q_state, starts_f, intra_f)
        _gla_finish(ob_ref, bwd.q_state, starts_b, intra_b)

    in_fast_range = -jnp.min(jnp.concatenate(fwd.totals + bwd.totals, axis=0)) < GLA_FAST_RANGE

    @pl.when(in_fast_range)
    def _():
        tile_body(_gla_scores_fast)

    @pl.when(jnp.logical_not(in_fast_range))
    def _():
        tile_body(_gla_scores_safe)

    @pl.when(i == pl.num_programs(1) - 1)
    def _():
        sf_out_ref[...] = sf_ref[...]
        sb_out_ref[...] = sb_ref[...]


GLA_STATE_SHAPE = (B_HEADS // 2, B_DV, 2 * B_DK)


def _gla(q, k, v, gf, gb, s0f, s0b, *, tile):
    bsz, seq, _ = q.shape
    n = seq // tile
    fwd = lambda w: pl.BlockSpec((None, tile, w), lambda b, i: (b, i, 0))
    bwd = lambda w: pl.BlockSpec((None, tile, w), lambda b, i: (b, n - 1 - i, 0))
    st = pl.BlockSpec((None,) + GLA_STATE_SHAPE, lambda b, i: (b, 0, 0, 0))
    wk, wv = B_HEADS * B_DK, B_HEADS * B_DV
    st_shape = jax.ShapeDtypeStruct((bsz,) + GLA_STATE_SHAPE, F32)
    return pl.pallas_call(
        _gla_kernel,
        grid=(bsz, n),
        in_specs=[fwd(wk), fwd(wk), fwd(wv), fwd(wk), bwd(wk), bwd(wk), bwd(wv), bwd(wk), st, st],
        out_specs=[fwd(wv), bwd(wv), st, st],
        out_shape=[jax.ShapeDtypeStruct((bsz, seq, wv), F32), jax.ShapeDtypeStruct((bsz, seq, wv), F32),
                   st_shape, st_shape],
        scratch_shapes=[pltpu.VMEM(GLA_STATE_SHAPE, F32), pltpu.VMEM(GLA_STATE_SHAPE, F32)],
        compiler_params=_params(("parallel", "arbitrary")),
        name="gla_scan",
    )(q, k, v, gf, q, k, v, gb, s0f, s0b)


def _softmax_parts(scores, sink_tile):
    def lane_tiles(blocks):
        return [b[:, j:j + LANES] for b in blocks for j in range(0, b.shape[1], LANES)]

    tiles = lane_tiles(scores)
    if sink_tile is not None:
        tiles.append(sink_tile)
    m = functools.reduce(jnp.maximum, tiles).max(axis=-1, keepdims=True)
    ps = [jnp.exp2(s - m) for s in scores]
    acc = functools.reduce(jnp.add, lane_tiles(ps))
    if sink_tile is not None:
        lane = lax.broadcasted_iota(jnp.int32, sink_tile.shape, 1)
        acc = acc + jnp.where(lane == 0, jnp.exp2(sink_tile - m), 0.0)
    return ps, acc.sum(axis=-1, keepdims=True)


def _split_head_pair(t):
    lane = lax.broadcasted_iota(jnp.int32, t.shape, 1)
    zero = jnp.zeros_like(t)
    return jnp.concatenate([jnp.where(lane < HEAD_DIM, t, zero), jnp.where(lane >= HEAD_DIM, t, zero)], axis=0)


def _merge_head_pair(o):
    m = o.shape[0] // 2
    lane = lax.broadcasted_iota(jnp.int32, (m, LANES), 1)
    return jnp.where(lane < HEAD_DIM, o[:m], o[m:])


A_HEAD_ORDER = (0, 4, 1, 5, 2, 6, 3, 7)


def _win_kernel(sink_ref, q_ref, kp_ref, kc_ref, kn_ref, vp_ref, vc_ref, vn_ref, kx_ref, vx_ref, o_ref):
    n = pl.program_id(1)
    nb = pl.num_programs(1)
    n_tiles = A_HEADS // 2
    q = jnp.concatenate([_split_head_pair(q_ref[:, j * LANES:(j + 1) * LANES]) for j in range(n_tiles)], axis=0)
    sink_tile = jnp.concatenate([jnp.full((A_BLOCK, LANES), sink_ref[h], F32) for h in A_HEAD_ORDER], axis=0)
    qi = lax.broadcasted_iota(jnp.int32, (A_BLOCK, A_BLOCK), 0)
    kj = lax.broadcasted_iota(jnp.int32, (A_BLOCK, A_BLOCK), 1)
    neg = jnp.full((A_BLOCK, A_BLOCK), -jnp.inf, F32)
    zero = jnp.zeros((A_BLOCK, A_BLOCK), F32)
    prev_mask = jnp.where(n > 0, jnp.where(kj >= qi, zero, neg), neg)
    next_mask = jnp.where(n < nb - 1, jnp.where(kj <= qi, zero, neg), neg)
    s_p = _dot_nt(q, kp_ref[...]) + jnp.tile(prev_mask, (A_HEADS, 1))
    s_c = _dot_nt(q, kc_ref[...])
    s_n = _dot_nt(q, kn_ref[...]) + jnp.tile(next_mask, (A_HEADS, 1))
    s_x = _dot_nt(q, kx_ref[...])
    (p_p, p_c, p_n, p_x), denom = _softmax_parts([s_p, s_c, s_n, s_x], sink_tile)
    o = (_dot(p_p.astype(BF16), vp_ref[...]) + _dot(p_c.astype(BF16), vc_ref[...])
         + _dot(p_n.astype(BF16), vn_ref[...]) + _dot(p_x.astype(BF16), vx_ref[...]))
    o = o / denom
    for j in range(n_tiles):
        o_ref[:, j * LANES:(j + 1) * LANES] = _merge_head_pair(
            o[j * 2 * A_BLOCK:(j + 1) * 2 * A_BLOCK]).astype(o_ref.dtype)


def _window_attention(sink, q, k, v, k_ctx, v_ctx):
    bsz, seq, wq = q.shape
    nb = seq // A_BLOCK
    wkv = k.shape[-1]
    n_ctx = k_ctx.shape[1]
    kv_spec = lambda f: pl.BlockSpec((None, A_BLOCK, wkv), lambda b, n: (b, f(n), 0))
    prev = lambda n: jnp.maximum(n - 1, 0)
    cur = lambda n: n
    nxt = lambda n: jnp.minimum(n + 1, nb - 1)
    ctx_spec = pl.BlockSpec((None, n_ctx, wkv), lambda b, n: (b, 0, 0))
    return pl.pallas_call(
        _win_kernel,
        grid=(bsz, nb),
        in_specs=[
            pl.BlockSpec(memory_space=pltpu.SMEM),
            pl.BlockSpec((None, A_BLOCK, wq), lambda b, n: (b, n, 0)),
            kv_spec(prev), kv_spec(cur), kv_spec(nxt),
            kv_spec(prev), kv_spec(cur), kv_spec(nxt),
            ctx_spec, ctx_spec,
        ],
        out_specs=pl.BlockSpec((None, A_BLOCK, wq), lambda b, n: (b, n, 0)),
        out_shape=jax.ShapeDtypeStruct((bsz, seq, wq), BF16),
        compiler_params=_params(("parallel", "arbitrary")),
        name="window_attn",
    )(sink, q, k, k, k, v, v, v, k_ctx, v_ctx)


def _dense_kernel(sink_ref, q_ref, k_ref, v_ref, o_ref, *, head_order, shared_kv):
    n_q = q_ref.shape[0]
    for j in range(q_ref.shape[1] // LANES):
        qs = slice(j * LANES, (j + 1) * LANES)
        ks = slice(0, LANES) if shared_kv else qs
        s = _dot_nt(_split_head_pair(q_ref[:, qs]), k_ref[:, ks])
        sink_tile = None
        if head_order is not None:
            sink_tile = jnp.concatenate(
                [jnp.full((n_q, LANES), sink_ref[head_order[2 * j + half]], F32) for half in range(2)], axis=0)
        (p,), denom = _softmax_parts([s], sink_tile)
        o_ref[:, qs] = _merge_head_pair(_dot(p.astype(BF16), v_ref[:, ks]) / denom).astype(o_ref.dtype)


def _dense_attention(sink, q, k, v, *, head_order, shared_kv):
    bsz, n_q, wq = q.shape
    wkv = k.shape[-1]
    full = lambda w: pl.BlockSpec((None, n_q, w), lambda b: (b, 0, 0))
    return pl.pallas_call(
        functools.partial(_dense_kernel, head_order=head_order, shared_kv=shared_kv),
        grid=(bsz,),
        in_specs=[pl.BlockSpec(memory_space=pltpu.SMEM), full(wq), full(wkv), full(wkv)],
        out_specs=full(wq),
        out_shape=jax.ShapeDtypeStruct((bsz, n_q, wq), BF16),
        compiler_params=_params(("parallel",)),
        name="ctx_dense_attn_sink" if shared_kv else "ctx_dense_attn",
    )(sink, q, k, v)


NA_ROWS_PER_STEP = 8


N_NA_INPUTS = 6


def _na_kernel(*refs):
    q_ref, k_ref, v_ref, kx_ref, vx_ref, bias_ref = refs[:N_NA_INPUTS]
    n_side = (len(refs) - N_NA_INPUTS - 1) // 2
    o_ref = refs[N_NA_INPUTS + n_side]
    _run_side_casts(refs[N_NA_INPUTS:N_NA_INPUTS + n_side], refs[N_NA_INPUTS + n_side + 1:])
    step = pl.program_id(1)
    grid_rows = k_ref.shape[0] // GRID_W
    n_keys = C_WIN_ROWS * GRID_W
    n_pairs = C_HEADS // 2
    for rr in range(NA_ROWS_PER_STEP):
        r = step * NA_ROWS_PER_STEP + rr
        row0 = jnp.clip(r - C_WIN_ROWS // 2, 0, grid_rows - C_WIN_ROWS)
        d_row0 = row0 - r + (C_WIN_ROWS - 1)
        key0 = pl.multiple_of(row0 * GRID_W, GRID_W)
        qrows = slice(rr * GRID_W, (rr + 1) * GRID_W)
        s_l, s_x = [], []
        for p in range(n_pairs):
            ps = slice(p * LANES, (p + 1) * LANES)
            q2 = _split_head_pair(q_ref[qrows, ps])
            s_l.append(_dot_nt(q2, k_ref[pl.ds(key0, n_keys), ps]))
            s_x.append(_dot_nt(q2, kx_ref[:, ps]))
        bias = jnp.concatenate(
            [bias_ref[d_row0 + 2 * j].reshape(C_HEADS * GRID_W, LANES) for j in range(C_WIN_ROWS // 2)], axis=1)
        (p_l, p_x), denom = _softmax_parts([jnp.concatenate(s_l, axis=0) + bias, jnp.concatenate(s_x, axis=0)], None)
        p_l = p_l.astype(BF16)
        p_x = p_x.astype(BF16)
        for p in range(n_pairs):
            ps = slice(p * LANES, (p + 1) * LANES)
            pr = slice(p * 2 * GRID_W, (p + 1) * 2 * GRID_W)
            o = (_dot(p_l[pr], v_ref[pl.ds(key0, n_keys), ps]) + _dot(p_x[pr], vx_ref[:, ps])) / denom[pr]
            o_ref[qrows, ps] = _merge_head_pair(o).astype(o_ref.dtype)


def _neighbourhood_attention(q, k, v, k_ctx, v_ctx, bias_tbl, cast_weights=(), layer=0):
    bsz, seq, w = q.shape
    n_ctx = k_ctx.shape[1]
    tq = NA_ROWS_PER_STEP * GRID_W
    n_tiles = seq // tq
    whole = pl.BlockSpec((None, seq, w), lambda b, i: (b, 0, 0))
    ctx_spec = pl.BlockSpec((None, n_ctx, w), lambda b, i: (b, 0, 0))
    side = _SideCasts(cast_weights, layer, bsz * n_tiles, lambda b, i: b * n_tiles + i)
    outs = pl.pallas_call(
        _na_kernel,
        grid=(bsz, n_tiles),
        in_specs=[pl.BlockSpec((None, tq, w), lambda b, i: (b, i, 0)), whole, whole, ctx_spec, ctx_spec,
                  _const_spec(bias_tbl.shape)] + side.specs,
        out_specs=[pl.BlockSpec((None, tq, w), lambda b, i: (b, i, 0))] + side.out_specs,
        out_shape=[jax.ShapeDtypeStruct((bsz, seq, w), BF16)] + side.out_shapes,
        compiler_params=_params(("parallel", "arbitrary")),
        name="neighbourhood_attn",
    )(q, k, v, k_ctx, v_ctx, bias_tbl, *side.arrays)
    return outs[0], side.finish(outs[1:])


def _na_bias_table(rpb):
    qcol = jnp.arange(GRID_W)[:, None]
    kcol = jnp.arange(GRID_W)[None, :]
    wstart = jnp.clip(qcol - C_WIN_COLS // 2, 0, GRID_W - C_WIN_COLS)
    ok = (kcol >= wstart) & (kcol < wstart + C_WIN_COLS)
    n_off = C_WIN_COLS - 1
    period = 2 * GRID_W
    vec = jnp.concatenate([rpb[..., n_off:].astype(F32),
                           jnp.zeros(rpb.shape[:2] + (period - 2 * n_off - 1,), F32),
                           rpb[..., :n_off].astype(F32)], axis=-1)
    toep = jnp.tile(vec, (1, 1, GRID_W))[..., :GRID_W * (period - 1)]
    toep = toep.reshape(rpb.shape[:2] + (GRID_W, period - 1))[..., :GRID_W]
    tbl = jnp.where(ok[None, None], toep * LOG2E, -jnp.inf)
    tbl = jnp.moveaxis(tbl, 1, 0)
    return jnp.concatenate([tbl[:-1], tbl[1:]], axis=-1)


def _merge_kernel(x_ref, mod_ref, gain_ref, ya_ref, of_ref, ob_ref, gr_ref, yc_ref, gn_ref,
                  wm_ref, bm_ref, wa_ref, wb_ref, wc_ref, wo_ref, o_ref):
    d = D_MODEL
    gn = gn_ref[...]
    x = x_ref[...]
    h = _norm_mod(x, gain_ref[...], mod_ref[0:1, :], mod_ref[1:2, :]).astype(BF16)
    o_sum = of_ref[...] + ob_ref[...]
    parts = []
    for hh in range(B_HEADS):
        oh = o_sum[:, hh * B_DV:(hh + 1) * B_DV]
        parts.append((oh * lax.rsqrt(jnp.mean(oh * oh, axis=-1, keepdims=True) + EPS)) * gn)
    y_b = (jnp.concatenate(parts, axis=1) * _silu(gr_ref[...])).astype(BF16)
    mixed = None
    for j, (y, w_ref) in enumerate(((ya_ref[...], wa_ref), (y_b, wb_ref), (yc_ref[...], wc_ref))):
        gate = _sigmoid(_dot(h, wm_ref[:, j * d:(j + 1) * d]) + bm_ref[:, j * d:(j + 1) * d])
        term = gate * _dot(y, w_ref[...])
        mixed = term if mixed is None else mixed + term
    o_ref[...] = x + mod_ref[2:3, :] * _dot(mixed.astype(BF16), wo_ref[...])


def _merge(x, mod_rows, gain, y_a, o_f, o_b, g_r, y_c, gla_gain, w_merge, b_merge, w_a, w_b, w_c, w_out, *, tm):
    bsz, seq, d = x.shape
    per_batch_mod = mod_rows.shape[0] > 1
    mod_map = (lambda b, i: (b, 0, 0)) if per_batch_mod else (lambda b, i: (0, 0, 0))
    tile = lambda w: pl.BlockSpec((None, tm, w), lambda b, i: (b, i, 0))
    wy = y_a.shape[-1]
    return pl.pallas_call(
        _merge_kernel,
        grid=(bsz, seq // tm),
        in_specs=[tile(d), pl.BlockSpec((None, 6, d), mod_map), _const_spec((1, d)),
                  tile(wy), tile(wy), tile(wy), tile(wy), tile(wy), _const_spec((1, B_DV)),
                  _const_spec(w_merge.shape), _const_spec(b_merge.shape),
                  _const_spec(w_a.shape), _const_spec(w_b.shape), _const_spec(w_c.shape),
                  _const_spec(w_out.shape)],
        out_specs=tile(d),
        out_shape=jax.ShapeDtypeStruct((bsz, seq, d), F32),
        compiler_params=_params(("parallel", "arbitrary")),
        name="merge_out",
    )(x, mod_rows, gain, y_a, o_f, o_b, g_r, y_c, gla_gain, w_merge, b_merge, w_a, w_b, w_c, w_out)


FFN_CHUNKS = ((0, 1024), (1024, 2048), (2048, FFN_HIDDEN))
FFN_SUB_ROWS = 256


def _ffn_kernel(x_ref, mod_ref, gain_ref, w1_ref, w2_ref, fg_ref, o_ref, *, final):
    for r0 in range(0, x_ref.shape[0], FFN_SUB_ROWS):
        rows = slice(r0, r0 + FFN_SUB_ROWS)
        x = x_ref[rows, :]
        h = _norm_mod(x, gain_ref[...], mod_ref[3:4, :], mod_ref[4:5, :]).astype(BF16)
        acc = None
        for c0, c1 in FFN_CHUNKS:
            gate = _dot(h, w1_ref[:, c0:c1])
            up = _dot(h, w1_ref[:, FFN_HIDDEN + c0:FFN_HIDDEN + c1])
            part = _dot((_silu(gate) * up).astype(BF16), w2_ref[c0:c1, :])
            acc = part if acc is None else acc + part
        y = x + mod_ref[5:6, :] * acc
        if final:
            y = (y * lax.rsqrt(jnp.mean(y * y, axis=-1, keepdims=True) + EPS)) * fg_ref[...]
        o_ref[rows, :] = y


def _ffn(x, mod_rows, gain, w1, w2, final_gain, *, final, tm):
    bsz, seq, d = x.shape
    per_batch_mod = mod_rows.shape[0] > 1
    mod_map = (lambda b, i: (b, 0, 0)) if per_batch_mod else (lambda b, i: (0, 0, 0))
    tile = pl.BlockSpec((None, tm, d), lambda b, i: (b, i, 0))
    return pl.pallas_call(
        functools.partial(_ffn_kernel, final=final),
        grid=(bsz, seq // tm),
        in_specs=[tile, pl.BlockSpec((None, 6, d), mod_map), _const_spec((1, d)),
                  _const_spec(w1.shape), _const_spec(w2.shape), _const_spec((1, d))],
        out_specs=tile,
        out_shape=jax.ShapeDtypeStruct((bsz, seq, d), F32),
        compiler_params=_params(("parallel", "arbitrary")),
        name="ffn_final" if final else "ffn",
    )(x, mod_rows, gain, w1, w2, final_gain)


def _rope_tables(seq):
    t = jnp.arange(seq)
    row = (t // GRID_W).astype(F32)
    col = (t % GRID_W).astype(F32)
    n_freq = HEAD_DIM // 4
    inv = ROPE_BASE ** (-jnp.arange(n_freq, dtype=F32) / n_freq)
    ang = jnp.concatenate([row[:, None] * inv[None], col[:, None] * inv[None]], axis=-1)
    cos, sin = jnp.cos(ang), jnp.sin(ang)
    return jnp.tile(cos, (1, 4)), jnp.tile(jnp.concatenate([-sin, sin], axis=-1), (1, 2))


def _permute_w_branch_a(w):
    return w.reshape(A_HEADS, HEAD_DIM, w.shape[1])[jnp.array(A_HEAD_ORDER)].reshape(w.shape).astype(BF16)


def _gate_weights(w_fwd, b_fwd, w_bwd, b_bwd):
    n = B_HEADS * B_DK
    r = B_GATE_RANK
    w = jnp.zeros((LANES, 2 * n), F32)
    w = w.at[:r, :n].set(w_fwd).at[r:2 * r, n:].set(w_bwd)
    return w.astype(BF16), jnp.concatenate([b_fwd, b_bwd])[None, :]


def kernel(x, c, ctx, c_ctx, w_ada, b_ada, norm_mix, w_in, attn_sink, gla_gate_w_fwd, gla_gate_b_fwd,
           gla_gate_w_bwd, gla_gate_b_bwd, gla_norm, na_rpb, w_branch_a, w_branch_b, w_branch_c,
           w_merge, b_merge, w_out, norm_ffn, w_ffn_in, w_ffn_out, final_norm):
    bsz, seq, d = x.shape
    n_ctx = ctx.shape[1]
    cos, sin = _rope_tables(seq)

    cvec = jnp.zeros((8, d), F32).at[:bsz].set(c).at[bsz].set(c_ctx)
    mod, w_perm_all = _modulation(cvec, w_ada, b_ada, w_in)
    mod = mod.reshape(DEPTH, 8, 6, d)

    xc = ctx
    zeros_state = jnp.zeros((bsz,) + GLA_STATE_SHAPE, F32)
    for l in range(DEPTH):
        last = l == DEPTH - 1
        mod_x = mod[l, :bsz]
        mod_c = mod[l, bsz:bsz + 1]
        gain_m = norm_mix[l][None, :]
        w_perm = w_perm_all
        w_gate, b_gate = _gate_weights(gla_gate_w_fwd[l], gla_gate_b_fwd[l], gla_gate_w_bwd[l], gla_gate_b_bwd[l])

        (caq, cak, cav, cgq, cgk, cgv, cgr, cnq, cnk, cnv, cgf, cgb), _ = _project(
            xc, mod_c, gain_m, w_perm, w_gate, b_gate, cos, sin, rope=False, tm=n_ctx, layer=l)
        (aq, ak, av, gq, gk, gv, gr, nq, nk, nv, gf, gb), (wm_bf, wb_bf, wc_bf, wo_bf) = _project(
            x, mod_x, gain_m, w_perm, w_gate, b_gate, cos, sin, rope=True, tm=512,
            cast_weights=(w_merge, w_branch_b, w_branch_c, w_out), layer=l)

        co_f, co_b, s_f, s_b = _gla(cgq, cgk, cgv, cgf, cgb, zeros_state, zeros_state, tile=n_ctx)
        o_f, o_b, _, _ = _gla(gq, gk, gv, gf, gb, s_f, s_b, tile=256)

        sink = attn_sink[l] * LOG2E
        y_a = _window_attention(sink, aq, ak, av, cak, cav)
        y_c, (w1, w2) = _neighbourhood_attention(nq, nk, nv, cnk, cnv, _na_bias_table(na_rpb[l]),
                                                 cast_weights=(w_ffn_in, w_ffn_out), layer=l)

        merge_w = (gla_norm[l][None, :], wm_bf, b_merge[l][None, :],
                   _permute_w_branch_a(w_branch_a[l]), wb_bf, wc_bf, wo_bf)
        x = _merge(x, mod_x, gain_m, y_a, o_f, o_b, gr, y_c, *merge_w, tm=512)

        gain_f = norm_ffn[l][None, :]
        fg = final_norm[None, :]
        if not last:
            yc_a = _dense_attention(sink, caq, cak, cav, head_order=A_HEAD_ORDER, shared_kv=True)
            yc_c = _dense_attention(sink, cnq, cnk, cnv, head_order=None, shared_kv=False)
            xc = _merge(xc, mod_c, gain_m, yc_a, co_f, co_b, cgr, yc_c, *merge_w, tm=n_ctx)
            xc = _ffn(xc, mod_c, gain_f, w1, w2, fg, final=False, tm=n_ctx)
        x = _ffn(x, mod_x, gain_f, w1, w2, fg, final=last, tm=1024)
    return x
```

```python
import functools

import jax
import jax.numpy as jnp
from jax import lax
from jax.experimental import pallas as pl
from jax.experimental.pallas import tpu as pltpu

F32 = jnp.float32
BF16 = jnp.bfloat16

D_MODEL = 1024
DEPTH = 2
GRID_W = 64
HEAD_DIM = 64
EPS = 1e-6
ROPE_BASE = 10000.0
A_HEADS = 8
A_KV_HEADS = 2
A_BLOCK = 128
B_HEADS = 4
B_DK = 64
B_DV = 128
B_GATE_RANK = 16
B_GATE_NORM = 16.0
C_HEADS = 8
C_WIN_ROWS = 8
C_WIN_COLS = 16
FFN_HIDDEN = 2816

VMEM_LIMIT_BYTES = 56 * 1024 * 1024
LANES = 128

SEG_AQ = (0, 512)
SEG_AKV = (512, 768)
SEG_GQ = (768, 1024)
SEG_GK = (1024, 1280)
SEG_GV = (1280, 1792)
SEG_GR = (1792, 2304)
SEG_NQ = (2304, 2816)
SEG_NK = (2816, 3328)
SEG_NV = (3328, 3840)
SEG_GA = (3840, 3968)
IN_PADDED = 3968
W_IN_GATE_COLS = (2304, 2304 + 2 * B_GATE_RANK)

LOG2E = 1.4426950408889634
ATTN_Q_SCALE = HEAD_DIM ** -0.5 * LOG2E

GLA_CHUNK = 64
GLA_FAST_RANGE = 120.0


def _params(sem):
    return pltpu.CompilerParams(dimension_semantics=sem, vmem_limit_bytes=VMEM_LIMIT_BYTES)


def _const_spec(shape):
    nd = len(shape)
    return pl.BlockSpec(shape, lambda *_: (0,) * nd, pipeline_mode=pl.Buffered(1))


def _sigmoid(x):
    return 1.0 / (1.0 + jnp.exp(-x))


def _silu(x):
    return x * _sigmoid(x)


def _dot(a, b):
    return jnp.dot(a, b, preferred_element_type=F32)


def _dot_nt(a, b):
    return lax.dot_general(a, b, (((1,), (1,)), ((), ())), preferred_element_type=F32)


def _dot_tn(a, b):
    return lax.dot_general(a, b, (((0,), (0,)), ((), ())), preferred_element_type=F32)


def _norm_mod(x, gain, shift, scale):
    y = x * lax.rsqrt(jnp.mean(x * x, axis=-1, keepdims=True) + EPS)
    return (y * gain) * (1.0 + scale) + shift


def _relayout_w_in(wt_ref, o_ref):
    n_aq = SEG_AQ[1]
    ga0, ga1 = W_IN_GATE_COLS
    pieces = [wt_ref[h * HEAD_DIM:(h + 1) * HEAD_DIM, :] for h in A_HEAD_ORDER]
    pieces += [wt_ref[n_aq:ga0, :], wt_ref[ga1:, :], wt_ref[ga0:ga1, :],
               jnp.zeros((IN_PADDED - wt_ref.shape[0], wt_ref.shape[1]), F32)]
    o_ref[...] = jnp.concatenate(pieces, axis=0).T.astype(BF16)


def _mod_kernel(c_ref, w_ref, b_ref, w_in_ref, o_ref, w_perm_ref):
    s = _silu(c_ref[...])
    o_ref[...] = _dot(s.astype(BF16), w_ref[...].astype(BF16)) + b_ref[...]
    _relayout_w_in(w_in_ref, w_perm_ref)


def _modulation(cvec, w_ada, b_ada, w_in):
    tn = 1536
    n_out = w_ada.shape[-1]
    n_j = n_out // tn
    rows = w_in.shape[1] // n_j
    w_in_t = jnp.swapaxes(w_in, 1, 2)
    return pl.pallas_call(
        _mod_kernel,
        grid=(DEPTH, n_j),
        in_specs=[
            pl.BlockSpec((8, D_MODEL), lambda l, j: (0, 0)),
            pl.BlockSpec((None, D_MODEL, tn), lambda l, j: (l, 0, j)),
            pl.BlockSpec((None, 1, tn), lambda l, j: (l, 0, j)),
            pl.BlockSpec((None, w_in_t.shape[1], rows), lambda l, j: (l, 0, j)),
        ],
        out_specs=[pl.BlockSpec((None, 8, tn), lambda l, j: (l, 0, j)),
                   pl.BlockSpec((None, rows, IN_PADDED), lambda l, j: (l, j, 0))],
        out_shape=[jax.ShapeDtypeStruct((DEPTH, 8, n_out), F32),
                   jax.ShapeDtypeStruct((DEPTH, w_in.shape[1], IN_PADDED), BF16)],
        compiler_params=_params(("arbitrary", "arbitrary")),
        name="adaln_mod",
    )(cvec, w_ada, b_ada.reshape(DEPTH, 1, n_out), w_in_t)


def _rope(t, cos, sin):
    n = t.shape[1]
    lane = lax.broadcasted_iota(jnp.int32, t.shape, 1)
    first_half = (lane % HEAD_DIM) < (HEAD_DIM // 2)
    rot = jnp.where(first_half, pltpu.roll(t, n - HEAD_DIM // 2, 1), pltpu.roll(t, HEAD_DIM // 2, 1))
    reps = n // LANES
    return t * jnp.tile(cos, (1, reps)) + rot * jnp.tile(sin, (1, reps))


class _SideCasts:
    def __init__(self, stacked_weights, layer, n_steps, step_index):
        self.shapes2d = [w.shape[1:] for w in stacked_weights]
        self.arrays = [w.reshape(w.shape[0], n_steps, w.shape[1] // n_steps, w.shape[2]) for w in stacked_weights]
        self.specs = [pl.BlockSpec((None, None) + a.shape[2:], lambda *g: (layer, step_index(*g), 0, 0))
                      for a in self.arrays]
        self.out_specs = [pl.BlockSpec((None,) + a.shape[2:], lambda *g: (step_index(*g), 0, 0))
                          for a in self.arrays]
        self.out_shapes = [jax.ShapeDtypeStruct(a.shape[1:], BF16) for a in self.arrays]

    def finish(self, outs):
        return [o.reshape(s) for o, s in zip(outs, self.shapes2d)]


def _run_side_casts(in_refs, out_refs):
    for w_ref, o_ref in zip(in_refs, out_refs):
        o_ref[...] = w_ref[...].astype(BF16)


N_PROJ_INPUTS = 8
N_PROJ_OUTPUTS = 13


def _proj_kernel(*refs, rope):
    (x_ref, mod_ref, gain_ref, w_ref, wg_ref, bg_ref, cos_ref, sin_ref) = refs[:N_PROJ_INPUTS]
    n_side = (len(refs) - N_PROJ_INPUTS - N_PROJ_OUTPUTS) // 2
    outs = refs[N_PROJ_INPUTS + n_side:]
    (aq_ref, ak_ref, av_ref, gq_ref, gk_ref, gv_ref, gr_ref, nq_ref, nk_ref, nv_ref,
     gf_ref, gb_ref, h_ref) = outs[:N_PROJ_OUTPUTS]
    _run_side_casts(refs[N_PROJ_INPUTS:N_PROJ_INPUTS + n_side], outs[N_PROJ_OUTPUTS:])
    scale = ATTN_Q_SCALE
    n_g = B_HEADS * B_DK
    h = _norm_mod(x_ref[...], gain_ref[...], mod_ref[0:1, :], mod_ref[1:2, :]).astype(BF16)
    h_ref[...] = h

    def seg(s):
        return _dot(h, w_ref[:, s[0]:s[1]])

    aq = seg(SEG_AQ)
    akv = seg(SEG_AKV)
    z = _dot(seg(SEG_GA).astype(BF16), wg_ref[...]) + bg_ref[...]
    ak = akv[:, :LANES]
    if rope:
        cos = cos_ref[...]
        sin = sin_ref[...]
        aq = _rope(aq, cos, sin)
        ak = _rope(ak, cos, sin)
    aq_ref[...] = (aq * scale).astype(BF16)
    ak_ref[...] = ak.astype(BF16)
    av_ref[...] = akv[:, LANES:].astype(BF16)
    log_sig = jnp.minimum(z, 0.0) - jnp.log1p(jnp.exp(-jnp.abs(z)))
    g = log_sig / B_GATE_NORM
    gf_ref[...] = g[:, :n_g]
    gb_ref[...] = g[:, n_g:]
    gq_ref[...] = seg(SEG_GQ) * (B_DK ** -0.5)
    gk_ref[...] = seg(SEG_GK)
    gv_ref[...] = seg(SEG_GV).astype(BF16)
    gr_ref[...] = seg(SEG_GR)
    nq_ref[...] = (seg(SEG_NQ) * scale).astype(BF16)
    nk_ref[...] = seg(SEG_NK).astype(BF16)
    nv_ref[...] = seg(SEG_NV).astype(BF16)


def _project(x, mod_rows, gain, w_perm, w_gate, b_gate, cos, sin, *, rope, tm, cast_weights=(), layer=0):
    bsz, seq, _ = x.shape
    n_tiles = seq // tm
    per_batch_mod = mod_rows.shape[0] > 1
    mod_map = (lambda b, i: (b, 0, 0)) if per_batch_mod else (lambda b, i: (0, 0, 0))
    widths = [(512, BF16), (128, BF16), (128, BF16), (256, F32), (256, F32), (512, BF16), (512, F32),
              (512, BF16), (512, BF16), (512, BF16), (256, F32), (256, F32), (D_MODEL, BF16)]
    assert len(widths) == N_PROJ_OUTPUTS
    tile = lambda w: pl.BlockSpec((None, tm, w), lambda b, i: (b, i, 0))
    side = _SideCasts(cast_weights, layer, bsz * n_tiles, lambda b, i: b * n_tiles + i)
    outs = pl.pallas_call(
        functools.partial(_proj_kernel, rope=rope),
        grid=(bsz, n_tiles),
        in_specs=[
            tile(D_MODEL),
            pl.BlockSpec((None, 6, D_MODEL), mod_map),
            _const_spec((1, D_MODEL)),
            pl.BlockSpec((None, D_MODEL, IN_PADDED), lambda b, i: (layer, 0, 0), pipeline_mode=pl.Buffered(1)),
            _const_spec((LANES, 2 * B_HEADS * B_DK)),
            _const_spec((1, 2 * B_HEADS * B_DK)),
            pl.BlockSpec((tm, LANES), lambda b, i: (i, 0)),
            pl.BlockSpec((tm, LANES), lambda b, i: (i, 0)),
        ] + side.specs,
        out_specs=[tile(w) for w, _ in widths] + side.out_specs,
        out_shape=[jax.ShapeDtypeStruct((bsz, seq, w), dt) for w, dt in widths] + side.out_shapes,
        compiler_params=_params(("parallel", "arbitrary")),
        name="in_proj_rope" if rope else "in_proj_ctx",
    )(x, mod_rows, gain, w_perm, w_gate, b_gate, cos, sin, *side.arrays)
    return outs[:N_PROJ_OUTPUTS], side.finish(outs[N_PROJ_OUTPUTS:])


class _GlaPrep:
    def __init__(self, q_ref, k_ref, g_ref, reverse):
        t = q_ref.shape[0]
        c = GLA_CHUNK
        row = lax.broadcasted_iota(jnp.int32, (t, t), 0)
        col = lax.broadcasted_iota(jnp.int32, (t, t), 1)
        same_chunk = (row // c) == (col // c)
        tri = (same_chunk & ((col >= row) if reverse else (col <= row))).astype(BF16)
        self.reverse = reverse
        self.q = q_ref[...]
        self.k = k_ref[...]
        self.g_parts = _split3(g_ref[...])
        self.b = _dot_01(tri, self.g_parts)
        b = self.b
        self.totals = [b[ci * c:ci * c + 1, :] if reverse else b[ci * c + c - 1:ci * c + c, :]
                       for ci in range(t // c)]
        self.b_end = jnp.concatenate([jnp.broadcast_to(e, (c, e.shape[1])) for e in self.totals], axis=0)
        self.q_state = (self.q * jnp.exp(b)).astype(BF16)
        self.k_tail = (self.k * jnp.exp(self.b_end - b)).astype(BF16)
        self.decay = [jnp.exp(e) for e in self.totals]


def _split3(x):
    hi = x.astype(BF16)
    rest = x - hi.astype(F32)
    mid = rest.astype(BF16)
    return hi, mid, (rest - mid.astype(F32)).astype(BF16)


def _dot_01(m01, parts):
    return _dot(m01, parts[0]) + _dot(m01, parts[1]) + _dot(m01, parts[2])


def _gla_pair_scores(q_w, k_w, keep):
    c = GLA_CHUNK
    out = []
    for ci in range(q_w.shape[0] // c):
        rows = slice(ci * c, (ci + 1) * c)
        for p in range(B_HEADS // 2):
            ps = slice(p * LANES, (p + 1) * LANES)
            out.append(jnp.where(keep, _dot_nt(_split_head_pair(q_w[rows, ps]), k_w[rows, ps]), 0.0))
    return out


def _gla_scores_fast(prep):
    c = GLA_CHUNK
    qi = lax.broadcasted_iota(jnp.int32, (2 * c, c), 0) % c
    kj = lax.broadcasted_iota(jnp.int32, (2 * c, c), 1)
    keep = (kj >= qi) if prep.reverse else (kj <= qi)
    ref = 0.5 * prep.b_end
    q_in = (prep.q * jnp.exp(prep.b - ref)).astype(BF16)
    k_in = (prep.k * jnp.exp(ref - prep.b)).astype(BF16)
    return _gla_pair_scores(q_in, k_in, keep)


def _gla_scores_safe(prep):
    t = prep.q.shape[0]
    c = GLA_CHUNK
    rev = prep.reverse
    row = lax.broadcasted_iota(jnp.int32, (t, t), 0)
    col = lax.broadcasted_iota(jnp.int32, (t, t), 1)
    qi = lax.broadcasted_iota(jnp.int32, (2 * c, c), 0) % c
    kj = lax.broadcasted_iota(jnp.int32, (2 * c, c), 1)
    scores = _gla_pair_scores(prep.q.astype(BF16), prep.k.astype(BF16), qi == kj)
    s = c // 2
    while s >= 1:
        if rev:
            bnd = row - row % (2 * s) + s
            span = ((col >= bnd) & (col < row)) | ((col >= row) & (col < bnd))
        else:
            bnd = row - row % (2 * s) + s - 1
            span = ((col > bnd) & (col <= row)) | ((col > row) & (col <= bnd))
        e = jnp.exp(_dot_01(span.astype(BF16), prep.g_parts))
        q_side = ((qi % (2 * s)) < s) if rev else ((qi % (2 * s)) >= s)
        k_side = ((kj % (2 * s)) >= s) if rev else ((kj % (2 * s)) < s)
        keep = ((qi // (2 * s)) == (kj // (2 * s))) & q_side & k_side
        level = _gla_pair_scores((prep.q * e).astype(BF16), (prep.k * e).astype(BF16), keep)
        scores = [acc + a for acc, a in zip(scores, level)]
        s //= 2
    return scores


def _gla_local(scores, k_tail, v_ref):
    c = GLA_CHUNK
    n_pairs = B_HEADS // 2
    intra, upd = [], []
    for ci in range(k_tail.shape[0] // c):
        rows = slice(ci * c, (ci + 1) * c)
        intra_c, upd_c = [], []
        for p in range(n_pairs):
            a = scores[ci * n_pairs + p].astype(BF16)
            kt = _split_head_pair(k_tail[rows, p * LANES:(p + 1) * LANES])
            u = None
            for half in range(2):
                h = 2 * p + half
                vh = v_ref[rows, h * B_DV:(h + 1) * B_DV]
                intra_c.append(_dot(a[half * c:(half + 1) * c], vh))
                uh = _dot_tn(vh, kt[half * c:(half + 1) * c])
                u = uh if u is None else u + uh
            upd_c.append(u)
        intra.append(intra_c)
        upd.append(upd_c)
    return intra, upd


def _gla_states(s_ref, decay, upd, reverse):
    n = len(upd)
    order = range(n - 1, -1, -1) if reverse else range(n)
    starts = [None] * n
    s = [s_ref[p] for p in range(B_HEADS // 2)]
    for ci in order:
        starts[ci] = [sp.astype(BF16) for sp in s]
        s = [s[p] * decay[ci][:, p * LANES:(p + 1) * LANES] + upd[ci][p] for p in range(B_HEADS // 2)]
    for p in range(B_HEADS // 2):
        s_ref[p] = s[p]
    return starts


def _gla_finish(o_ref, q_state, starts, intra):
    c = GLA_CHUNK
    for ci in range(len(intra)):
        rows = slice(ci * c, (ci + 1) * c)
        for p in range(B_HEADS // 2):
            ps = slice(p * LANES, (p + 1) * LANES)
            inter = _dot_nt(_split_head_pair(q_state[rows, ps]), starts[ci][p])
            for half in range(2):
                h = 2 * p + half
                o_ref[rows, h * B_DV:(h + 1) * B_DV] = intra[ci][h] + inter[half * c:(half + 1) * c]


def _gla_sub_tile(qf_ref, kf_ref, vf_ref, gf_ref, qb_ref, kb_ref, vb_ref, gb_ref, of_ref, ob_ref, sf_ref, sb_ref):
    fwd = _GlaPrep(qf_ref, kf_ref, gf_ref, False)
    bwd = _GlaPrep(qb_ref, kb_ref, gb_ref, True)

    def tile_body(score_fn):
        scores_f = score_fn(fwd)
        scores_b = score_fn(bwd)
        intra_f, upd_f = _gla_local(scores_f, fwd.k_tail, vf_ref)
        intra_b, upd_b = _gla_local(scores_b, bwd.k_tail, vb_ref)
        starts_f = _gla_states(sf_ref, fwd.decay, upd_f, False)
        starts_b = _gla_states(sb_ref, bwd.decay, upd_b, True)
        _gla_finish(of_ref, fwd.q_state, starts_f, intra_f)
        _gla_finish(ob_ref, bwd.q_state, starts_b, intra_b)

    in_fast_range = -jnp.min(jnp.concatenate(fwd.totals + bwd.totals, axis=0)) < GLA_FAST_RANGE

    @pl.when(in_fast_range)
    def _():
        tile_body(_gla_scores_fast)

    @pl.when(jnp.logical_not(in_fast_range))
    def _():
        tile_body(_gla_scores_safe)


GLA_SUB_TILE = 256


def _gla_kernel(qf_ref, kf_ref, vf_ref, gf_ref, qb_ref, kb_ref, vb_ref, gb_ref, s0f_ref, s0b_ref,
                of_ref, ob_ref, sf_out_ref, sb_out_ref, sf_ref, sb_ref):
    i = pl.program_id(1)

    @pl.when(i == 0)
    def _():
        sf_ref[...] = s0f_ref[...]
        sb_ref[...] = s0b_ref[...]

    t = qf_ref.shape[0]
    sub = min(t, GLA_SUB_TILE)
    n_sub = t // sub
    for s in range(n_sub):
        f_rows = pl.ds(s * sub, sub)
        b_rows = pl.ds((n_sub - 1 - s) * sub, sub)
        _gla_sub_tile(qf_ref.at[f_rows], kf_ref.at[f_rows], vf_ref.at[f_rows], gf_ref.at[f_rows],
                      qb_ref.at[b_rows], kb_ref.at[b_rows], vb_ref.at[b_rows], gb_ref.at[b_rows],
                      of_ref.at[f_rows], ob_ref.at[b_rows], sf_ref, sb_ref)

    @pl.when(i == pl.num_programs(1) - 1)
    def _():
        sf_out_ref[...] = sf_ref[...]
        sb_out_ref[...] = sb_ref[...]


GLA_STATE_SHAPE = (B_HEADS // 2, B_DV, 2 * B_DK)


def _gla(q, k, v, gf, gb, s0f, s0b, *, tile):
    bsz, seq, _ = q.shape
    n = seq // tile
    fwd = lambda w: pl.BlockSpec((None, tile, w), lambda b, i: (b, i, 0))
    bwd = lambda w: pl.BlockSpec((None, tile, w), lambda b, i: (b, n - 1 - i, 0))
    st = pl.BlockSpec((None,) + GLA_STATE_SHAPE, lambda b, i: (b, 0, 0, 0))
    wk, wv = B_HEADS * B_DK, B_HEADS * B_DV
    st_shape = jax.ShapeDtypeStruct((bsz,) + GLA_STATE_SHAPE, F32)
    return pl.pallas_call(
        _gla_kernel,
        grid=(bsz, n),
        in_specs=[fwd(wk), fwd(wk), fwd(wv), fwd(wk), bwd(wk), bwd(wk), bwd(wv), bwd(wk), st, st],
        out_specs=[fwd(wv), bwd(wv), st, st],
        out_shape=[jax.ShapeDtypeStruct((bsz, seq, wv), F32), jax.ShapeDtypeStruct((bsz, seq, wv), F32),
                   st_shape, st_shape],
        scratch_shapes=[pltpu.VMEM(GLA_STATE_SHAPE, F32), pltpu.VMEM(GLA_STATE_SHAPE, F32)],
        compiler_params=_params(("parallel", "arbitrary")),
        name="gla_scan",
    )(q, k, v, gf, q, k, v, gb, s0f, s0b)


def _softmax_parts(scores, sink_tile):
    def lane_tiles(blocks):
        return [b[:, j:j + LANES] for b in blocks for j in range(0, b.shape[1], LANES)]

    tiles = lane_tiles(scores)
    if sink_tile is not None:
        tiles.append(sink_tile)
    m = functools.reduce(jnp.maximum, tiles).max(axis=-1, keepdims=True)
    ps = [jnp.exp2(s - m) for s in scores]
    acc = functools.reduce(jnp.add, lane_tiles(ps))
    if sink_tile is not None:
        lane = lax.broadcasted_iota(jnp.int32, sink_tile.shape, 1)
        acc = acc + jnp.where(lane == 0, jnp.exp2(sink_tile - m), 0.0)
    return ps, acc.sum(axis=-1, keepdims=True)


def _split_head_pair(t):
    lane = lax.broadcasted_iota(jnp.int32, t.shape, 1)
    zero = jnp.zeros_like(t)
    return jnp.concatenate([jnp.where(lane < HEAD_DIM, t, zero), jnp.where(lane >= HEAD_DIM, t, zero)], axis=0)


def _merge_head_pair(o):
    m = o.shape[0] // 2
    lane = lax.broadcasted_iota(jnp.int32, (m, LANES), 1)
    return jnp.where(lane < HEAD_DIM, o[:m], o[m:])


A_HEAD_ORDER = (0, 4, 1, 5, 2, 6, 3, 7)


def _win_block(q2, sink_tile, keys, values, prev_mask, next_mask):
    k_p, k_c, k_n, k_x = keys
    v_p, v_c, v_n, v_x = values
    s_p = _dot_nt(q2, k_p) + jnp.tile(prev_mask, (A_HEADS, 1))
    s_c = _dot_nt(q2, k_c)
    s_n = _dot_nt(q2, k_n) + jnp.tile(next_mask, (A_HEADS, 1))
    s_x = _dot_nt(q2, k_x)
    (p_p, p_c, p_n, p_x), denom = _softmax_parts([s_p, s_c, s_n, s_x], sink_tile)
    o = (_dot(p_p.astype(BF16), v_p) + _dot(p_c.astype(BF16), v_c)
         + _dot(p_n.astype(BF16), v_n) + _dot(p_x.astype(BF16), v_x))
    o = o / denom
    return jnp.concatenate([_merge_head_pair(o[j * 2 * A_BLOCK:(j + 1) * 2 * A_BLOCK])
                            for j in range(A_HEADS // 2)], axis=1)


WIN_BLOCKS_PER_STEP = 2


def _win_kernel(sink_ref, q_ref, kp_ref, kc_ref, kn_ref, vp_ref, vc_ref, vn_ref, kx_ref, vx_ref, o_ref):
    i = pl.program_id(1)
    n_steps = pl.num_programs(1)
    nbs = WIN_BLOCKS_PER_STEP
    sink_tile = jnp.concatenate([jnp.full((A_BLOCK, LANES), sink_ref[h], F32) for h in A_HEAD_ORDER], axis=0)
    qi = lax.broadcasted_iota(jnp.int32, (A_BLOCK, A_BLOCK), 0)
    kj = lax.broadcasted_iota(jnp.int32, (A_BLOCK, A_BLOCK), 1)
    neg = jnp.full((A_BLOCK, A_BLOCK), -jnp.inf, F32)
    zero = jnp.zeros((A_BLOCK, A_BLOCK), F32)
    prev_band = jnp.where(kj >= qi, zero, neg)
    next_band = jnp.where(kj <= qi, zero, neg)
    k_x, v_x = kx_ref[...], vx_ref[...]
    for a in range(nbs):
        rows = slice(a * A_BLOCK, (a + 1) * A_BLOCK)
        before = slice((a - 1) * A_BLOCK, a * A_BLOCK)
        after = slice((a + 1) * A_BLOCK, (a + 2) * A_BLOCK)
        prev_mask = prev_band if a > 0 else jnp.where(i > 0, prev_band, neg)
        next_mask = next_band if a < nbs - 1 else jnp.where(i < n_steps - 1, next_band, neg)
        keys = (kc_ref[before, :] if a > 0 else kp_ref[...], kc_ref[rows, :],
                kc_ref[after, :] if a < nbs - 1 else kn_ref[...], k_x)
        values = (vc_ref[before, :] if a > 0 else vp_ref[...], vc_ref[rows, :],
                  vc_ref[after, :] if a < nbs - 1 else vn_ref[...], v_x)
        q2 = jnp.concatenate([_split_head_pair(q_ref[rows, j * LANES:(j + 1) * LANES])
                              for j in range(A_HEADS // 2)], axis=0)
        o_ref[rows, :] = _win_block(q2, sink_tile, keys, values, prev_mask, next_mask).astype(o_ref.dtype)


def _window_attention(sink, q, k, v, k_ctx, v_ctx):
    bsz, seq, wq = q.shape
    nbs = WIN_BLOCKS_PER_STEP
    nb = seq // A_BLOCK
    wkv = k.shape[-1]
    n_ctx = k_ctx.shape[1]
    prev = pl.BlockSpec((None, A_BLOCK, wkv), lambda b, i: (b, jnp.maximum(nbs * i - 1, 0), 0))
    cur = pl.BlockSpec((None, nbs * A_BLOCK, wkv), lambda b, i: (b, i, 0))
    nxt = pl.BlockSpec((None, A_BLOCK, wkv), lambda b, i: (b, jnp.minimum(nbs * i + nbs, nb - 1), 0))
    ctx_spec = pl.BlockSpec((None, n_ctx, wkv), lambda b, i: (b, 0, 0))
    q_spec = pl.BlockSpec((None, nbs * A_BLOCK, wq), lambda b, i: (b, i, 0))
    return pl.pallas_call(
        _win_kernel,
        grid=(bsz, nb // nbs),
        in_specs=[pl.BlockSpec(memory_space=pltpu.SMEM), q_spec, prev, cur, nxt, prev, cur, nxt,
                  ctx_spec, ctx_spec],
        out_specs=q_spec,
        out_shape=jax.ShapeDtypeStruct((bsz, seq, wq), BF16),
        compiler_params=_params(("parallel", "arbitrary")),
        name="window_attn",
    )(sink, q, k, k, k, v, v, v, k_ctx, v_ctx)


def _dense_kernel(sink_ref, q_ref, k_ref, v_ref, o_ref, *, head_order, shared_kv):
    n_q = q_ref.shape[0]
    for j in range(q_ref.shape[1] // LANES):
        qs = slice(j * LANES, (j + 1) * LANES)
        ks = slice(0, LANES) if shared_kv else qs
        s = _dot_nt(_split_head_pair(q_ref[:, qs]), k_ref[:, ks])
        sink_tile = None
        if head_order is not None:
            sink_tile = jnp.concatenate(
                [jnp.full((n_q, LANES), sink_ref[head_order[2 * j + half]], F32) for half in range(2)], axis=0)
        (p,), denom = _softmax_parts([s], sink_tile)
        o_ref[:, qs] = _merge_head_pair(_dot(p.astype(BF16), v_ref[:, ks]) / denom).astype(o_ref.dtype)


def _dense_attention(sink, q, k, v, *, head_order, shared_kv):
    bsz, n_q, wq = q.shape
    wkv = k.shape[-1]
    full = lambda w: pl.BlockSpec((None, n_q, w), lambda b: (b, 0, 0))
    return pl.pallas_call(
        functools.partial(_dense_kernel, head_order=head_order, shared_kv=shared_kv),
        grid=(bsz,),
        in_specs=[pl.BlockSpec(memory_space=pltpu.SMEM), full(wq), full(wkv), full(wkv)],
        out_specs=full(wq),
        out_shape=jax.ShapeDtypeStruct((bsz, n_q, wq), BF16),
        compiler_params=_params(("parallel",)),
        name="ctx_dense_attn_sink" if shared_kv else "ctx_dense_attn",
    )(sink, q, k, v)


NA_ROWS_PER_STEP = 8


N_NA_INPUTS = 6


def _na_kernel(*refs):
    q_ref, k_ref, v_ref, kx_ref, vx_ref, bias_ref = refs[:N_NA_INPUTS]
    n_side = (len(refs) - N_NA_INPUTS - 1) // 2
    o_ref = refs[N_NA_INPUTS + n_side]
    _run_side_casts(refs[N_NA_INPUTS:N_NA_INPUTS + n_side], refs[N_NA_INPUTS + n_side + 1:])
    step = pl.program_id(1)
    grid_rows = k_ref.shape[0] // GRID_W
    n_keys = C_WIN_ROWS * GRID_W
    n_pairs = C_HEADS // 2
    for rr in range(NA_ROWS_PER_STEP):
        r = step * NA_ROWS_PER_STEP + rr
        row0 = jnp.clip(r - C_WIN_ROWS // 2, 0, grid_rows - C_WIN_ROWS)
        d_row0 = row0 - r + (C_WIN_ROWS - 1)
        key0 = pl.multiple_of(row0 * GRID_W, GRID_W)
        qrows = slice(rr * GRID_W, (rr + 1) * GRID_W)
        s_l, s_x = [], []
        for p in range(n_pairs):
            ps = slice(p * LANES, (p + 1) * LANES)
            q2 = _split_head_pair(q_ref[qrows, ps])
            s_l.append(_dot_nt(q2, k_ref[pl.ds(key0, n_keys), ps]))
            s_x.append(_dot_nt(q2, kx_ref[:, ps]))
        bias = jnp.concatenate(
            [bias_ref[d_row0 + 2 * j].reshape(C_HEADS * GRID_W, LANES) for j in range(C_WIN_ROWS // 2)], axis=1)
        (p_l, p_x), denom = _softmax_parts([jnp.concatenate(s_l, axis=0) + bias, jnp.concatenate(s_x, axis=0)], None)
        p_l = p_l.astype(BF16)
        p_x = p_x.astype(BF16)
        for p in range(n_pairs):
            ps = slice(p * LANES, (p + 1) * LANES)
            pr = slice(p * 2 * GRID_W, (p + 1) * 2 * GRID_W)
            o = (_dot(p_l[pr], v_ref[pl.ds(key0, n_keys), ps]) + _dot(p_x[pr], vx_ref[:, ps])) / denom[pr]
            o_ref[qrows, ps] = _merge_head_pair(o).astype(o_ref.dtype)


def _neighbourhood_attention(q, k, v, k_ctx, v_ctx, bias_tbl, cast_weights=(), layer=0):
    bsz, seq, w = q.shape
    n_ctx = k_ctx.shape[1]
    tq = NA_ROWS_PER_STEP * GRID_W
    n_tiles = seq // tq
    whole = pl.BlockSpec((None, seq, w), lambda b, i: (b, 0, 0))
    ctx_spec = pl.BlockSpec((None, n_ctx, w), lambda b, i: (b, 0, 0))
    side = _SideCasts(cast_weights, layer, bsz * n_tiles, lambda b, i: b * n_tiles + i)
    outs = pl.pallas_call(
        _na_kernel,
        grid=(bsz, n_tiles),
        in_specs=[pl.BlockSpec((None, tq, w), lambda b, i: (b, i, 0)), whole, whole, ctx_spec, ctx_spec,
                  _const_spec(bias_tbl.shape)] + side.specs,
        out_specs=[pl.BlockSpec((None, tq, w), lambda b, i: (b, i, 0))] + side.out_specs,
        out_shape=[jax.ShapeDtypeStruct((bsz, seq, w), BF16)] + side.out_shapes,
        compiler_params=_params(("parallel", "arbitrary")),
        name="neighbourhood_attn",
    )(q, k, v, k_ctx, v_ctx, bias_tbl, *side.arrays)
    return outs[0], side.finish(outs[1:])


def _na_bias_table(rpb):
    qcol = jnp.arange(GRID_W)[:, None]
    kcol = jnp.arange(GRID_W)[None, :]
    wstart = jnp.clip(qcol - C_WIN_COLS // 2, 0, GRID_W - C_WIN_COLS)
    ok = (kcol >= wstart) & (kcol < wstart + C_WIN_COLS)
    n_off = C_WIN_COLS - 1
    period = 2 * GRID_W
    vec = jnp.concatenate([rpb[..., n_off:].astype(F32),
                           jnp.zeros(rpb.shape[:2] + (period - 2 * n_off - 1,), F32),
                           rpb[..., :n_off].astype(F32)], axis=-1)
    toep = jnp.tile(vec, (1, 1, GRID_W))[..., :GRID_W * (period - 1)]
    toep = toep.reshape(rpb.shape[:2] + (GRID_W, period - 1))[..., :GRID_W]
    tbl = jnp.where(ok[None, None], toep * LOG2E, -jnp.inf)
    tbl = jnp.moveaxis(tbl, 1, 0)
    return jnp.concatenate([tbl[:-1], tbl[1:]], axis=-1)


def _merge_kernel(x_ref, h_ref, mod_ref, ya_ref, of_ref, ob_ref, gr_ref, yc_ref, gn_ref,
                  wm_ref, bm_ref, wa_ref, wb_ref, wc_ref, wo_ref, o_ref):
    d = D_MODEL
    gn = gn_ref[...]
    x = x_ref[...]
    h = h_ref[...]
    o_sum = of_ref[...] + ob_ref[...]
    parts = []
    for hh in range(B_HEADS):
        oh = o_sum[:, hh * B_DV:(hh + 1) * B_DV]
        parts.append((oh * lax.rsqrt(jnp.mean(oh * oh, axis=-1, keepdims=True) + EPS)) * gn)
    y_b = (jnp.concatenate(parts, axis=1) * _silu(gr_ref[...])).astype(BF16)
    mixed = None
    for j, (y, w_ref) in enumerate(((ya_ref[...], wa_ref), (y_b, wb_ref), (yc_ref[...], wc_ref))):
        gate = _sigmoid(_dot(h, wm_ref[:, j * d:(j + 1) * d]) + bm_ref[:, j * d:(j + 1) * d])
        term = gate * _dot(y, w_ref[...])
        mixed = term if mixed is None else mixed + term
    o_ref[...] = x + mod_ref[2:3, :] * _dot(mixed.astype(BF16), wo_ref[...])


def _merge(x, h, mod_rows, y_a, o_f, o_b, g_r, y_c, gla_gain, w_merge, b_merge, w_a, w_b, w_c, w_out, *, tm):
    bsz, seq, d = x.shape
    per_batch_mod = mod_rows.shape[0] > 1
    mod_map = (lambda b, i: (b, 0, 0)) if per_batch_mod else (lambda b, i: (0, 0, 0))
    tile = lambda w: pl.BlockSpec((None, tm, w), lambda b, i: (b, i, 0))
    wy = y_a.shape[-1]
    return pl.pallas_call(
        _merge_kernel,
        grid=(bsz, seq // tm),
        in_specs=[tile(d), tile(d), pl.BlockSpec((None, 6, d), mod_map),
                  tile(wy), tile(wy), tile(wy), tile(wy), tile(wy), _const_spec((1, B_DV)),
                  _const_spec(w_merge.shape), _const_spec(b_merge.shape),
                  _const_spec(w_a.shape), _const_spec(w_b.shape), _const_spec(w_c.shape),
                  _const_spec(w_out.shape)],
        out_specs=tile(d),
        out_shape=jax.ShapeDtypeStruct((bsz, seq, d), F32),
        compiler_params=_params(("parallel", "arbitrary")),
        name="merge_out",
    )(x, h, mod_rows, y_a, o_f, o_b, g_r, y_c, gla_gain, w_merge, b_merge, w_a, w_b, w_c, w_out)


FFN_CHUNKS = ((0, 1024), (1024, 2048), (2048, FFN_HIDDEN))
FFN_SUB_ROWS = 256


def _ffn_kernel(x_ref, mod_ref, gain_ref, w1_ref, w2_ref, fg_ref, o_ref, *, final):
    for r0 in range(0, x_ref.shape[0], FFN_SUB_ROWS):
        rows = slice(r0, r0 + FFN_SUB_ROWS)
        x = x_ref[rows, :]
        h = _norm_mod(x, gain_ref[...], mod_ref[3:4, :], mod_ref[4:5, :]).astype(BF16)
        acc = None
        for c0, c1 in FFN_CHUNKS:
            gate = _dot(h, w1_ref[:, c0:c1])
            up = _dot(h, w1_ref[:, FFN_HIDDEN + c0:FFN_HIDDEN + c1])
            part = _dot((_silu(gate) * up).astype(BF16), w2_ref[c0:c1, :])
            acc = part if acc is None else acc + part
        y = x + mod_ref[5:6, :] * acc
        if final:
            y = (y * lax.rsqrt(jnp.mean(y * y, axis=-1, keepdims=True) + EPS)) * fg_ref[...]
        o_ref[rows, :] = y


def _ffn(x, mod_rows, gain, w1, w2, final_gain, *, final, tm):
    bsz, seq, d = x.shape
    per_batch_mod = mod_rows.shape[0] > 1
    mod_map = (lambda b, i: (b, 0, 0)) if per_batch_mod else (lambda b, i: (0, 0, 0))
    tile = pl.BlockSpec((None, tm, d), lambda b, i: (b, i, 0))
    return pl.pallas_call(
        functools.partial(_ffn_kernel, final=final),
        grid=(bsz, seq // tm),
        in_specs=[tile, pl.BlockSpec((None, 6, d), mod_map), _const_spec((1, d)),
                  _const_spec(w1.shape), _const_spec(w2.shape), _const_spec((1, d))],
        out_specs=tile,
        out_shape=jax.ShapeDtypeStruct((bsz, seq, d), F32),
        compiler_params=_params(("parallel", "arbitrary")),
        name="ffn_final" if final else "ffn",
    )(x, mod_rows, gain, w1, w2, final_gain)


def _rope_tables(seq):
    t = jnp.arange(seq)
    row = (t // GRID_W).astype(F32)
    col = (t % GRID_W).astype(F32)
    n_freq = HEAD_DIM // 4
    inv = ROPE_BASE ** (-jnp.arange(n_freq, dtype=F32) / n_freq)
    ang = jnp.concatenate([row[:, None] * inv[None], col[:, None] * inv[None]], axis=-1)
    cos, sin = jnp.cos(ang), jnp.sin(ang)
    return jnp.tile(cos, (1, 4)), jnp.tile(jnp.concatenate([-sin, sin], axis=-1), (1, 2))


def _permute_w_branch_a(w):
    return w.reshape(A_HEADS, HEAD_DIM, w.shape[1])[jnp.array(A_HEAD_ORDER)].reshape(w.shape).astype(BF16)


def _gate_weights(w_fwd, b_fwd, w_bwd, b_bwd):
    n = B_HEADS * B_DK
    r = B_GATE_RANK
    w = jnp.zeros((LANES, 2 * n), F32)
    w = w.at[:r, :n].set(w_fwd).at[r:2 * r, n:].set(w_bwd)
    return w.astype(BF16), jnp.concatenate([b_fwd, b_bwd])[None, :]


def kernel(x, c, ctx, c_ctx, w_ada, b_ada, norm_mix, w_in, attn_sink, gla_gate_w_fwd, gla_gate_b_fwd,
           gla_gate_w_bwd, gla_gate_b_bwd, gla_norm, na_rpb, w_branch_a, w_branch_b, w_branch_c,
           w_merge, b_merge, w_out, norm_ffn, w_ffn_in, w_ffn_out, final_norm):
    bsz, seq, d = x.shape
    n_ctx = ctx.shape[1]
    cos, sin = _rope_tables(seq)

    cvec = jnp.zeros((8, d), F32).at[:bsz].set(c).at[bsz].set(c_ctx)
    mod, w_perm_all = _modulation(cvec, w_ada, b_ada, w_in)
    mod = mod.reshape(DEPTH, 8, 6, d)

    xc = ctx
    zeros_state = jnp.zeros((bsz,) + GLA_STATE_SHAPE, F32)
    for l in range(DEPTH):
        last = l == DEPTH - 1
        mod_x = mod[l, :bsz]
        mod_c = mod[l, bsz:bsz + 1]
        gain_m = norm_mix[l][None, :]
        w_perm = w_perm_all
        w_gate, b_gate = _gate_weights(gla_gate_w_fwd[l], gla_gate_b_fwd[l], gla_gate_w_bwd[l], gla_gate_b_bwd[l])

        (caq, cak, cav, cgq, cgk, cgv, cgr, cnq, cnk, cnv, cgf, cgb, hc), _ = _project(
            xc, mod_c, gain_m, w_perm, w_gate, b_gate, cos, sin, rope=False, tm=n_ctx, layer=l)
        (aq, ak, av, gq, gk, gv, gr, nq, nk, nv, gf, gb, h), (wm_bf, wb_bf, wc_bf, wo_bf) = _project(
            x, mod_x, gain_m, w_perm, w_gate, b_gate, cos, sin, rope=True, tm=512,
            cast_weights=(w_merge, w_branch_b, w_branch_c, w_out), layer=l)

        co_f, co_b, s_f, s_b = _gla(cgq, cgk, cgv, cgf, cgb, zeros_state, zeros_state, tile=n_ctx)
        o_f, o_b, _, _ = _gla(gq, gk, gv, gf, gb, s_f, s_b, tile=512)

        sink = attn_sink[l] * LOG2E
        y_a = _window_attention(sink, aq, ak, av, cak, cav)
        y_c, (w1, w2) = _neighbourhood_attention(nq, nk, nv, cnk, cnv, _na_bias_table(na_rpb[l]),
                                                 cast_weights=(w_ffn_in, w_ffn_out), layer=l)

        merge_w = (gla_norm[l][None, :], wm_bf, b_merge[l][None, :],
                   _permute_w_branch_a(w_branch_a[l]), wb_bf, wc_bf, wo_bf)
        x = _merge(x, h, mod_x, y_a, o_f, o_b, gr, y_c, *merge_w, tm=512)

        gain_f = norm_ffn[l][None, :]
        fg = final_norm[None, :]
        if not last:
            yc_a = _dense_attention(sink, caq, cak, cav, head_order=A_HEAD_ORDER, shared_kv=True)
            yc_c = _dense_attention(sink, cnq, cnk, cnv, head_order=None, shared_kv=False)
            xc = _merge(xc, hc, mod_c, yc_a, co_f, co_b, cgr, yc_c, *merge_w, tm=n_ctx)
            xc = _ffn(xc, mod_c, gain_f, w1, w2, fg, final=False, tm=n_ctx)
        x = _ffn(x, mod_x, gain_f, w1, w2, fg, final=last, tm=1024)
    return x
```

```python
import functools

import jax
import jax.numpy as jnp
from jax import lax
from jax.experimental import pallas as pl
from jax.experimental.pallas import tpu as pltpu

F32 = jnp.float32
BF16 = jnp.bfloat16

D_MODEL = 1024
DEPTH = 2
GRID_W = 64
HEAD_DIM = 64
EPS = 1e-6
ROPE_BASE = 10000.0
A_HEADS = 8
A_KV_HEADS = 2
A_BLOCK = 128
B_HEADS = 4
B_DK = 64
B_DV = 128
B_GATE_RANK = 16
B_GATE_NORM = 16.0
C_HEADS = 8
C_WIN_ROWS = 8
C_WIN_COLS = 16
FFN_HIDDEN = 2816

VMEM_LIMIT_BYTES = 56 * 1024 * 1024
LANES = 128

SEG_AQ = (0, 512)
SEG_AKV = (512, 768)
SEG_GQ = (768, 1024)
SEG_GK = (1024, 1280)
SEG_GV = (1280, 1792)
SEG_GR = (1792, 2304)
SEG_NQ = (2304, 2816)
SEG_NK = (2816, 3328)
SEG_NV = (3328, 3840)
SEG_GA = (3840, 3968)
IN_PADDED = 3968
W_IN_GATE_COLS = (2304, 2304 + 2 * B_GATE_RANK)

LOG2E = 1.4426950408889634
ATTN_Q_SCALE = HEAD_DIM ** -0.5 * LOG2E

GLA_CHUNK = 64
GLA_FAST_RANGE = 120.0


def _params(sem):
    return pltpu.CompilerParams(dimension_semantics=sem, vmem_limit_bytes=VMEM_LIMIT_BYTES)


def _const_spec(shape):
    nd = len(shape)
    return pl.BlockSpec(shape, lambda *_: (0,) * nd, pipeline_mode=pl.Buffered(1))


def _sigmoid(x):
    return 1.0 / (1.0 + jnp.exp(-x))


def _silu(x):
    return x * _sigmoid(x)


def _dot(a, b):
    return jnp.dot(a, b, preferred_element_type=F32)


def _dot_nt(a, b):
    return lax.dot_general(a, b, (((1,), (1,)), ((), ())), preferred_element_type=F32)


def _dot_tn(a, b):
    return lax.dot_general(a, b, (((0,), (0,)), ((), ())), preferred_element_type=F32)


def _norm_mod(x, gain, shift, scale):
    y = x * lax.rsqrt(jnp.mean(x * x, axis=-1, keepdims=True) + EPS)
    return (y * gain) * (1.0 + scale) + shift


def _relayout_w_in(wt_ref, o_ref):
    n_aq = SEG_AQ[1]
    ga0, ga1 = W_IN_GATE_COLS
    pieces = [wt_ref[h * HEAD_DIM:(h + 1) * HEAD_DIM, :] for h in A_HEAD_ORDER]
    pieces += [wt_ref[n_aq:ga0, :], wt_ref[ga1:, :], wt_ref[ga0:ga1, :],
               jnp.zeros((IN_PADDED - wt_ref.shape[0], wt_ref.shape[1]), F32)]
    o_ref[...] = jnp.concatenate(pieces, axis=0).T.astype(BF16)


def _mod_kernel(c_ref, w_ref, b_ref, w_in_ref, o_ref, w_perm_ref):
    s = _silu(c_ref[...])
    o_ref[...] = _dot(s.astype(BF16), w_ref[...].astype(BF16)) + b_ref[...]
    _relayout_w_in(w_in_ref, w_perm_ref)


def _modulation(cvec, w_ada, b_ada, w_in):
    tn = 1536
    n_out = w_ada.shape[-1]
    n_j = n_out // tn
    rows = w_in.shape[1] // n_j
    w_in_t = jnp.swapaxes(w_in, 1, 2)
    return pl.pallas_call(
        _mod_kernel,
        grid=(DEPTH, n_j),
        in_specs=[
            pl.BlockSpec((8, D_MODEL), lambda l, j: (0, 0)),
            pl.BlockSpec((None, D_MODEL, tn), lambda l, j: (l, 0, j)),
            pl.BlockSpec((None, 1, tn), lambda l, j: (l, 0, j)),
            pl.BlockSpec((None, w_in_t.shape[1], rows), lambda l, j: (l, 0, j)),
        ],
        out_specs=[pl.BlockSpec((None, 8, tn), lambda l, j: (l, 0, j)),
                   pl.BlockSpec((None, rows, IN_PADDED), lambda l, j: (l, j, 0))],
        out_shape=[jax.ShapeDtypeStruct((DEPTH, 8, n_out), F32),
                   jax.ShapeDtypeStruct((DEPTH, w_in.shape[1], IN_PADDED), BF16)],
        compiler_params=_params(("arbitrary", "arbitrary")),
        name="adaln_mod",
    )(cvec, w_ada, b_ada.reshape(DEPTH, 1, n_out), w_in_t)


def _rope(t, cos, sin):
    n = t.shape[1]
    lane = lax.broadcasted_iota(jnp.int32, t.shape, 1)
    first_half = (lane % HEAD_DIM) < (HEAD_DIM // 2)
    rot = jnp.where(first_half, pltpu.roll(t, n - HEAD_DIM // 2, 1), pltpu.roll(t, HEAD_DIM // 2, 1))
    reps = n // LANES
    return t * jnp.tile(cos, (1, reps)) + rot * jnp.tile(sin, (1, reps))


class _SideCasts:
    def __init__(self, stacked_weights, layer, n_steps, step_index):
        self.shapes2d = [w.shape[1:] for w in stacked_weights]
        self.arrays = [w.reshape(w.shape[0], n_steps, w.shape[1] // n_steps, w.shape[2]) for w in stacked_weights]
        self.specs = [pl.BlockSpec((None, None) + a.shape[2:], lambda *g: (layer, step_index(*g), 0, 0))
                      for a in self.arrays]
        self.out_specs = [pl.BlockSpec((None,) + a.shape[2:], lambda *g: (step_index(*g), 0, 0))
                          for a in self.arrays]
        self.out_shapes = [jax.ShapeDtypeStruct(a.shape[1:], BF16) for a in self.arrays]

    def finish(self, outs):
        return [o.reshape(s) for o, s in zip(outs, self.shapes2d)]


def _run_side_casts(in_refs, out_refs):
    for w_ref, o_ref in zip(in_refs, out_refs):
        o_ref[...] = w_ref[...].astype(BF16)


N_PROJ_INPUTS = 8
N_PROJ_OUTPUTS = 12


def _proj_kernel(*refs, rope):
    (x_ref, mod_ref, gain_ref, w_ref, wg_ref, bg_ref, cos_ref, sin_ref) = refs[:N_PROJ_INPUTS]
    n_side = (len(refs) - N_PROJ_INPUTS - N_PROJ_OUTPUTS) // 2
    outs = refs[N_PROJ_INPUTS + n_side:]
    (aq_ref, ak_ref, av_ref, gq_ref, gk_ref, gv_ref, gr_ref, nq_ref, nk_ref, nv_ref,
     gf_ref, gb_ref) = outs[:N_PROJ_OUTPUTS]
    _run_side_casts(refs[N_PROJ_INPUTS:N_PROJ_INPUTS + n_side], outs[N_PROJ_OUTPUTS:])
    scale = ATTN_Q_SCALE
    n_g = B_HEADS * B_DK
    h = _norm_mod(x_ref[...], gain_ref[...], mod_ref[0:1, :], mod_ref[1:2, :]).astype(BF16)

    def seg(s):
        return _dot(h, w_ref[:, s[0]:s[1]])

    aq = seg(SEG_AQ)
    akv = seg(SEG_AKV)
    z = _dot(seg(SEG_GA).astype(BF16), wg_ref[...]) + bg_ref[...]
    ak = akv[:, :LANES]
    if rope:
        cos = cos_ref[...]
        sin = sin_ref[...]
        aq = _rope(aq, cos, sin)
        ak = _rope(ak, cos, sin)
    aq_ref[...] = (aq * scale).astype(BF16)
    ak_ref[...] = ak.astype(BF16)
    av_ref[...] = akv[:, LANES:].astype(BF16)
    log_sig = jnp.minimum(z, 0.0) - jnp.log1p(jnp.exp(-jnp.abs(z)))
    g = log_sig / B_GATE_NORM
    gf_ref[...] = g[:, :n_g]
    gb_ref[...] = g[:, n_g:]
    gq_ref[...] = seg(SEG_GQ) * (B_DK ** -0.5)
    gk_ref[...] = seg(SEG_GK)
    gv_ref[...] = seg(SEG_GV).astype(BF16)
    gr_ref[...] = seg(SEG_GR)
    nq_ref[...] = (seg(SEG_NQ) * scale).astype(BF16)
    nk_ref[...] = seg(SEG_NK).astype(BF16)
    nv_ref[...] = seg(SEG_NV).astype(BF16)


def _project(x, mod_rows, gain, w_perm, w_gate, b_gate, cos, sin, *, rope, tm, cast_weights=(), layer=0):
    bsz, seq, _ = x.shape
    n_tiles = seq // tm
    per_batch_mod = mod_rows.shape[0] > 1
    mod_map = (lambda b, i: (b, 0, 0)) if per_batch_mod else (lambda b, i: (0, 0, 0))
    widths = [(512, BF16), (128, BF16), (128, BF16), (256, F32), (256, F32), (512, BF16), (512, F32),
              (512, BF16), (512, BF16), (512, BF16), (256, F32), (256, F32)]
    assert len(widths) == N_PROJ_OUTPUTS
    tile = lambda w: pl.BlockSpec((None, tm, w), lambda b, i: (b, i, 0))
    side = _SideCasts(cast_weights, layer, bsz * n_tiles, lambda b, i: b * n_tiles + i)
    outs = pl.pallas_call(
        functools.partial(_proj_kernel, rope=rope),
        grid=(bsz, n_tiles),
        in_specs=[
            tile(D_MODEL),
            pl.BlockSpec((None, 6, D_MODEL), mod_map),
            _const_spec((1, D_MODEL)),
            pl.BlockSpec((None, D_MODEL, IN_PADDED), lambda b, i: (layer, 0, 0), pipeline_mode=pl.Buffered(1)),
            _const_spec((LANES, 2 * B_HEADS * B_DK)),
            _const_spec((1, 2 * B_HEADS * B_DK)),
            pl.BlockSpec((tm, LANES), lambda b, i: (i, 0)),
            pl.BlockSpec((tm, LANES), lambda b, i: (i, 0)),
        ] + side.specs,
        out_specs=[tile(w) for w, _ in widths] + side.out_specs,
        out_shape=[jax.ShapeDtypeStruct((bsz, seq, w), dt) for w, dt in widths] + side.out_shapes,
        compiler_params=_params(("parallel", "arbitrary")),
        name="in_proj_rope" if rope else "in_proj_ctx",
    )(x, mod_rows, gain, w_perm, w_gate, b_gate, cos, sin, *side.arrays)
    return outs[:N_PROJ_OUTPUTS], side.finish(outs[N_PROJ_OUTPUTS:])


class _GlaPrep:
    def __init__(self, q_ref, k_ref, g_ref, reverse):
        t = q_ref.shape[0]
        c = GLA_CHUNK
        row = lax.broadcasted_iota(jnp.int32, (t, t), 0)
        col = lax.broadcasted_iota(jnp.int32, (t, t), 1)
        same_chunk = (row // c) == (col // c)
        tri = (same_chunk & ((col >= row) if reverse else (col <= row))).astype(BF16)
        self.reverse = reverse
        self.q = q_ref[...]
        self.k = k_ref[...]
        self.g_parts = _split3(g_ref[...])
        self.b = _dot_01(tri, self.g_parts)
        b = self.b
        self.totals = [b[ci * c:ci * c + 1, :] if reverse else b[ci * c + c - 1:ci * c + c, :]
                       for ci in range(t // c)]
        self.b_end = jnp.concatenate([jnp.broadcast_to(e, (c, e.shape[1])) for e in self.totals], axis=0)
        self.q_state = (self.q * jnp.exp(b)).astype(BF16)
        self.k_tail = (self.k * jnp.exp(self.b_end - b)).astype(BF16)
        self.decay = [jnp.exp(e) for e in self.totals]


def _split3(x):
    hi = x.astype(BF16)
    rest = x - hi.astype(F32)
    mid = rest.astype(BF16)
    return hi, mid, (rest - mid.astype(F32)).astype(BF16)


def _dot_01(m01, parts):
    return _dot(m01, parts[0]) + _dot(m01, parts[1]) + _dot(m01, parts[2])


def _gla_pair_scores(q_w, k_w, keep):
    c = GLA_CHUNK
    out = []
    for ci in range(q_w.shape[0] // c):
        rows = slice(ci * c, (ci + 1) * c)
        for p in range(B_HEADS // 2):
            ps = slice(p * LANES, (p + 1) * LANES)
            out.append(jnp.where(keep, _dot_nt(_split_head_pair(q_w[rows, ps]), k_w[rows, ps]), 0.0))
    return out


def _gla_scores_fast(prep):
    c = GLA_CHUNK
    qi = lax.broadcasted_iota(jnp.int32, (2 * c, c), 0) % c
    kj = lax.broadcasted_iota(jnp.int32, (2 * c, c), 1)
    keep = (kj >= qi) if prep.reverse else (kj <= qi)
    ref = 0.5 * prep.b_end
    q_in = (prep.q * jnp.exp(prep.b - ref)).astype(BF16)
    k_in = (prep.k * jnp.exp(ref - prep.b)).astype(BF16)
    return _gla_pair_scores(q_in, k_in, keep)


def _gla_scores_safe(prep):
    t = prep.q.shape[0]
    c = GLA_CHUNK
    rev = prep.reverse
    row = lax.broadcasted_iota(jnp.int32, (t, t), 0)
    col = lax.broadcasted_iota(jnp.int32, (t, t), 1)
    qi = lax.broadcasted_iota(jnp.int32, (2 * c, c), 0) % c
    kj = lax.broadcasted_iota(jnp.int32, (2 * c, c), 1)
    scores = _gla_pair_scores(prep.q.astype(BF16), prep.k.astype(BF16), qi == kj)
    s = c // 2
    while s >= 1:
        if rev:
            bnd = row - row % (2 * s) + s
            span = ((col >= bnd) & (col < row)) | ((col >= row) & (col < bnd))
        else:
            bnd = row - row % (2 * s) + s - 1
            span = ((col > bnd) & (col <= row)) | ((col > row) & (col <= bnd))
        e = jnp.exp(_dot_01(span.astype(BF16), prep.g_parts))
        q_side = ((qi % (2 * s)) < s) if rev else ((qi % (2 * s)) >= s)
        k_side = ((kj % (2 * s)) >= s) if rev else ((kj % (2 * s)) < s)
        keep = ((qi // (2 * s)) == (kj // (2 * s))) & q_side & k_side
        level = _gla_pair_scores((prep.q * e).astype(BF16), (prep.k * e).astype(BF16), keep)
        scores = [acc + a for acc, a in zip(scores, level)]
        s //= 2
    return scores


def _gla_local(scores, k_tail, v_ref):
    c = GLA_CHUNK
    n_pairs = B_HEADS // 2
    intra, upd = [], []
    for ci in range(k_tail.shape[0] // c):
        rows = slice(ci * c, (ci + 1) * c)
        intra_c, upd_c = [], []
        for p in range(n_pairs):
            a = scores[ci * n_pairs + p].astype(BF16)
            kt = _split_head_pair(k_tail[rows, p * LANES:(p + 1) * LANES])
            u = None
            for half in range(2):
                h = 2 * p + half
                vh = v_ref[rows, h * B_DV:(h + 1) * B_DV]
                intra_c.append(_dot(a[half * c:(half + 1) * c], vh))
                uh = _dot_tn(vh, kt[half * c:(half + 1) * c])
                u = uh if u is None else u + uh
            upd_c.append(u)
        intra.append(intra_c)
        upd.append(upd_c)
    return intra, upd


def _gla_states(s_ref, decay, upd, reverse):
    n = len(upd)
    order = range(n - 1, -1, -1) if reverse else range(n)
    starts = [None] * n
    s = [s_ref[p] for p in range(B_HEADS // 2)]
    for ci in order:
        starts[ci] = [sp.astype(BF16) for sp in s]
        s = [s[p] * decay[ci][:, p * LANES:(p + 1) * LANES] + upd[ci][p] for p in range(B_HEADS // 2)]
    for p in range(B_HEADS // 2):
        s_ref[p] = s[p]
    return starts


def _gla_finish(o_ref, q_state, starts, intra):
    c = GLA_CHUNK
    for ci in range(len(intra)):
        rows = slice(ci * c, (ci + 1) * c)
        for p in range(B_HEADS // 2):
            ps = slice(p * LANES, (p + 1) * LANES)
            inter = _dot_nt(_split_head_pair(q_state[rows, ps]), starts[ci][p])
            for half in range(2):
                h = 2 * p + half
                o_ref[rows, h * B_DV:(h + 1) * B_DV] = intra[ci][h] + inter[half * c:(half + 1) * c]


def _gla_sub_tile(qf_ref, kf_ref, vf_ref, gf_ref, qb_ref, kb_ref, vb_ref, gb_ref, of_ref, ob_ref, sf_ref, sb_ref):
    fwd = _GlaPrep(qf_ref, kf_ref, gf_ref, False)
    bwd = _GlaPrep(qb_ref, kb_ref, gb_ref, True)

    def tile_body(score_fn):
        scores_f = score_fn(fwd)
        scores_b = score_fn(bwd)
        intra_f, upd_f = _gla_local(scores_f, fwd.k_tail, vf_ref)
        intra_b, upd_b = _gla_local(scores_b, bwd.k_tail, vb_ref)
        starts_f = _gla_states(sf_ref, fwd.decay, upd_f, False)
        starts_b = _gla_states(sb_ref, bwd.decay, upd_b, True)
        _gla_finish(of_ref, fwd.q_state, starts_f, intra_f)
        _gla_finish(ob_ref, bwd.q_state, starts_b, intra_b)

    in_fast_range = -jnp.min(jnp.concatenate(fwd.totals + bwd.totals, axis=0)) < GLA_FAST_RANGE

    @pl.when(in_fast_range)
    def _():
        tile_body(_gla_scores_fast)

    @pl.when(jnp.logical_not(in_fast_range))
    def _():
        tile_body(_gla_scores_safe)


GLA_SUB_TILE = 256


def _gla_kernel(qf_ref, kf_ref, vf_ref, gf_ref, qb_ref, kb_ref, vb_ref, gb_ref, s0f_ref, s0b_ref,
                of_ref, ob_ref, sf_out_ref, sb_out_ref, sf_ref, sb_ref):
    i = pl.program_id(1)

    @pl.when(i == 0)
    def _():
        sf_ref[...] = s0f_ref[...]
        sb_ref[...] = s0b_ref[...]

    t = qf_ref.shape[0]
    sub = min(t, GLA_SUB_TILE)
    n_sub = t // sub
    for s in range(n_sub):
        f_rows = pl.ds(s * sub, sub)
        b_rows = pl.ds((n_sub - 1 - s) * sub, sub)
        _gla_sub_tile(qf_ref.at[f_rows], kf_ref.at[f_rows], vf_ref.at[f_rows], gf_ref.at[f_rows],
                      qb_ref.at[b_rows], kb_ref.at[b_rows], vb_ref.at[b_rows], gb_ref.at[b_rows],
                      of_ref.at[f_rows], ob_ref.at[b_rows], sf_ref, sb_ref)

    @pl.when(i == pl.num_programs(1) - 1)
    def _():
        sf_out_ref[...] = sf_ref[...]
        sb_out_ref[...] = sb_ref[...]


GLA_STATE_SHAPE = (B_HEADS // 2, B_DV, 2 * B_DK)


def _gla(q, k, v, gf, gb, s0f, s0b, *, tile):
    bsz, seq, _ = q.shape
    n = seq // tile
    fwd = lambda w: pl.BlockSpec((None, tile, w), lambda b, i: (b, i, 0))
    bwd = lambda w: pl.BlockSpec((None, tile, w), lambda b, i: (b, n - 1 - i, 0))
    st = pl.BlockSpec((None,) + GLA_STATE_SHAPE, lambda b, i: (b, 0, 0, 0))
    wk, wv = B_HEADS * B_DK, B_HEADS * B_DV
    st_shape = jax.ShapeDtypeStruct((bsz,) + GLA_STATE_SHAPE, F32)
    return pl.pallas_call(
        _gla_kernel,
        grid=(bsz, n),
        in_specs=[fwd(wk), fwd(wk), fwd(wv), fwd(wk), bwd(wk), bwd(wk), bwd(wv), bwd(wk), st, st],
        out_specs=[fwd(wv), bwd(wv), st, st],
        out_shape=[jax.ShapeDtypeStruct((bsz, seq, wv), F32), jax.ShapeDtypeStruct((bsz, seq, wv), F32),
                   st_shape, st_shape],
        scratch_shapes=[pltpu.VMEM(GLA_STATE_SHAPE, F32), pltpu.VMEM(GLA_STATE_SHAPE, F32)],
        compiler_params=_params(("parallel", "arbitrary")),
        name="gla_scan",
    )(q, k, v, gf, q, k, v, gb, s0f, s0b)


def _softmax_parts(scores, sink_tile):
    def lane_tiles(blocks):
        return [b[:, j:j + LANES] for b in blocks for j in range(0, b.shape[1], LANES)]

    tiles = lane_tiles(scores)
    if sink_tile is not None:
        tiles.append(sink_tile)
    m = functools.reduce(jnp.maximum, tiles).max(axis=-1, keepdims=True)
    ps = [jnp.exp2(s - m) for s in scores]
    acc = functools.reduce(jnp.add, lane_tiles(ps))
    if sink_tile is not None:
        lane = lax.broadcasted_iota(jnp.int32, sink_tile.shape, 1)
        acc = acc + jnp.where(lane == 0, jnp.exp2(sink_tile - m), 0.0)
    return ps, acc.sum(axis=-1, keepdims=True)


def _split_head_pair(t):
    lane = lax.broadcasted_iota(jnp.int32, t.shape, 1)
    zero = jnp.zeros_like(t)
    return jnp.concatenate([jnp.where(lane < HEAD_DIM, t, zero), jnp.where(lane >= HEAD_DIM, t, zero)], axis=0)


def _merge_head_pair(o):
    m = o.shape[0] // 2
    lane = lax.broadcasted_iota(jnp.int32, (m, LANES), 1)
    return jnp.where(lane < HEAD_DIM, o[:m], o[m:])


A_HEAD_ORDER = (0, 4, 1, 5, 2, 6, 3, 7)


def _win_block(q2, sink_tile, keys, values, prev_mask, next_mask):
    k_p, k_c, k_n, k_x = keys
    v_p, v_c, v_n, v_x = values
    s_p = _dot_nt(q2, k_p) + jnp.tile(prev_mask, (A_HEADS, 1))
    s_c = _dot_nt(q2, k_c)
    s_n = _dot_nt(q2, k_n) + jnp.tile(next_mask, (A_HEADS, 1))
    s_x = _dot_nt(q2, k_x)
    (p_p, p_c, p_n, p_x), denom = _softmax_parts([s_p, s_c, s_n, s_x], sink_tile)
    o = (_dot(p_p.astype(BF16), v_p) + _dot(p_c.astype(BF16), v_c)
         + _dot(p_n.astype(BF16), v_n) + _dot(p_x.astype(BF16), v_x))
    o = o / denom
    return jnp.concatenate([_merge_head_pair(o[j * 2 * A_BLOCK:(j + 1) * 2 * A_BLOCK])
                            for j in range(A_HEADS // 2)], axis=1)


WIN_BLOCKS_PER_STEP = 4


def _win_kernel(sink_ref, q_ref, kp_ref, kc_ref, kn_ref, vp_ref, vc_ref, vn_ref, kx_ref, vx_ref, o_ref):
    i = pl.program_id(1)
    n_steps = pl.num_programs(1)
    nbs = WIN_BLOCKS_PER_STEP
    sink_tile = jnp.concatenate([jnp.full((A_BLOCK, LANES), sink_ref[h], F32) for h in A_HEAD_ORDER], axis=0)
    qi = lax.broadcasted_iota(jnp.int32, (A_BLOCK, A_BLOCK), 0)
    kj = lax.broadcasted_iota(jnp.int32, (A_BLOCK, A_BLOCK), 1)
    neg = jnp.full((A_BLOCK, A_BLOCK), -jnp.inf, F32)
    zero = jnp.zeros((A_BLOCK, A_BLOCK), F32)
    prev_band = jnp.where(kj >= qi, zero, neg)
    next_band = jnp.where(kj <= qi, zero, neg)
    k_x, v_x = kx_ref[...], vx_ref[...]
    for a in range(nbs):
        rows = slice(a * A_BLOCK, (a + 1) * A_BLOCK)
        before = slice((a - 1) * A_BLOCK, a * A_BLOCK)
        after = slice((a + 1) * A_BLOCK, (a + 2) * A_BLOCK)
        prev_mask = prev_band if a > 0 else jnp.where(i > 0, prev_band, neg)
        next_mask = next_band if a < nbs - 1 else jnp.where(i < n_steps - 1, next_band, neg)
        keys = (kc_ref[before, :] if a > 0 else kp_ref[...], kc_ref[rows, :],
                kc_ref[after, :] if a < nbs - 1 else kn_ref[...], k_x)
        values = (vc_ref[before, :] if a > 0 else vp_ref[...], vc_ref[rows, :],
                  vc_ref[after, :] if a < nbs - 1 else vn_ref[...], v_x)
        q2 = jnp.concatenate([_split_head_pair(q_ref[rows, j * LANES:(j + 1) * LANES])
                              for j in range(A_HEADS // 2)], axis=0)
        o_ref[rows, :] = _win_block(q2, sink_tile, keys, values, prev_mask, next_mask).astype(o_ref.dtype)


def _window_attention(sink, q, k, v, k_ctx, v_ctx):
    bsz, seq, wq = q.shape
    nbs = WIN_BLOCKS_PER_STEP
    nb = seq // A_BLOCK
    wkv = k.shape[-1]
    n_ctx = k_ctx.shape[1]
    prev = pl.BlockSpec((None, A_BLOCK, wkv), lambda b, i: (b, jnp.maximum(nbs * i - 1, 0), 0))
    cur = pl.BlockSpec((None, nbs * A_BLOCK, wkv), lambda b, i: (b, i, 0))
    nxt = pl.BlockSpec((None, A_BLOCK, wkv), lambda b, i: (b, jnp.minimum(nbs * i + nbs, nb - 1), 0))
    ctx_spec = pl.BlockSpec((None, n_ctx, wkv), lambda b, i: (b, 0, 0))
    q_spec = pl.BlockSpec((None, nbs * A_BLOCK, wq), lambda b, i: (b, i, 0))
    return pl.pallas_call(
        _win_kernel,
        grid=(bsz, nb // nbs),
        in_specs=[pl.BlockSpec(memory_space=pltpu.SMEM), q_spec, prev, cur, nxt, prev, cur, nxt,
                  ctx_spec, ctx_spec],
        out_specs=q_spec,
        out_shape=jax.ShapeDtypeStruct((bsz, seq, wq), BF16),
        compiler_params=_params(("parallel", "arbitrary")),
        name="window_attn",
    )(sink, q, k, k, k, v, v, v, k_ctx, v_ctx)


def _dense_kernel(sink_ref, q_ref, k_ref, v_ref, o_ref, *, head_order, shared_kv):
    n_q = q_ref.shape[0]
    for j in range(q_ref.shape[1] // LANES):
        qs = slice(j * LANES, (j + 1) * LANES)
        ks = slice(0, LANES) if shared_kv else qs
        s = _dot_nt(_split_head_pair(q_ref[:, qs]), k_ref[:, ks])
        sink_tile = None
        if head_order is not None:
            sink_tile = jnp.concatenate(
                [jnp.full((n_q, LANES), sink_ref[head_order[2 * j + half]], F32) for half in range(2)], axis=0)
        (p,), denom = _softmax_parts([s], sink_tile)
        o_ref[:, qs] = _merge_head_pair(_dot(p.astype(BF16), v_ref[:, ks]) / denom).astype(o_ref.dtype)


def _dense_attention(sink, q, k, v, *, head_order, shared_kv):
    bsz, n_q, wq = q.shape
    wkv = k.shape[-1]
    full = lambda w: pl.BlockSpec((None, n_q, w), lambda b: (b, 0, 0))
    return pl.pallas_call(
        functools.partial(_dense_kernel, head_order=head_order, shared_kv=shared_kv),
        grid=(bsz,),
        in_specs=[pl.BlockSpec(memory_space=pltpu.SMEM), full(wq), full(wkv), full(wkv)],
        out_specs=full(wq),
        out_shape=jax.ShapeDtypeStruct((bsz, n_q, wq), BF16),
        compiler_params=_params(("parallel",)),
        name="ctx_dense_attn_sink" if shared_kv else "ctx_dense_attn",
    )(sink, q, k, v)


NA_ROWS_PER_STEP = 16


N_NA_INPUTS = 6


def _na_kernel(*refs):
    q_ref, k_ref, v_ref, kx_ref, vx_ref, bias_ref = refs[:N_NA_INPUTS]
    n_side = (len(refs) - N_NA_INPUTS - 1) // 2
    o_ref = refs[N_NA_INPUTS + n_side]
    _run_side_casts(refs[N_NA_INPUTS:N_NA_INPUTS + n_side], refs[N_NA_INPUTS + n_side + 1:])
    step = pl.program_id(1)
    grid_rows = k_ref.shape[0] // GRID_W
    n_keys = C_WIN_ROWS * GRID_W
    n_pairs = C_HEADS // 2
    for rr in range(NA_ROWS_PER_STEP):
        r = step * NA_ROWS_PER_STEP + rr
        row0 = jnp.clip(r - C_WIN_ROWS // 2, 0, grid_rows - C_WIN_ROWS)
        d_row0 = row0 - r + (C_WIN_ROWS - 1)
        key0 = pl.multiple_of(row0 * GRID_W, GRID_W)
        qrows = slice(rr * GRID_W, (rr + 1) * GRID_W)
        s_l, s_x = [], []
        for p in range(n_pairs):
            ps = slice(p * LANES, (p + 1) * LANES)
            q2 = _split_head_pair(q_ref[qrows, ps])
            s_l.append(_dot_nt(q2, k_ref[pl.ds(key0, n_keys), ps]))
            s_x.append(_dot_nt(q2, kx_ref[:, ps]))
        bias = jnp.concatenate(
            [bias_ref[d_row0 + 2 * j].reshape(C_HEADS * GRID_W, LANES) for j in range(C_WIN_ROWS // 2)], axis=1)
        (p_l, p_x), denom = _softmax_parts([jnp.concatenate(s_l, axis=0) + bias, jnp.concatenate(s_x, axis=0)], None)
        p_l = p_l.astype(BF16)
        p_x = p_x.astype(BF16)
        for p in range(n_pairs):
            ps = slice(p * LANES, (p + 1) * LANES)
            pr = slice(p * 2 * GRID_W, (p + 1) * 2 * GRID_W)
            o = (_dot(p_l[pr], v_ref[pl.ds(key0, n_keys), ps]) + _dot(p_x[pr], vx_ref[:, ps])) / denom[pr]
            o_ref[qrows, ps] = _merge_head_pair(o).astype(o_ref.dtype)


def _neighbourhood_attention(q, k, v, k_ctx, v_ctx, bias_tbl, cast_weights=(), layer=0):
    bsz, seq, w = q.shape
    n_ctx = k_ctx.shape[1]
    tq = NA_ROWS_PER_STEP * GRID_W
    n_tiles = seq // tq
    whole = pl.BlockSpec((None, seq, w), lambda b, i: (b, 0, 0))
    ctx_spec = pl.BlockSpec((None, n_ctx, w), lambda b, i: (b, 0, 0))
    side = _SideCasts(cast_weights, layer, bsz * n_tiles, lambda b, i: b * n_tiles + i)
    outs = pl.pallas_call(
        _na_kernel,
        grid=(bsz, n_tiles),
        in_specs=[pl.BlockSpec((None, tq, w), lambda b, i: (b, i, 0)), whole, whole, ctx_spec, ctx_spec,
                  _const_spec(bias_tbl.shape)] + side.specs,
        out_specs=[pl.BlockSpec((None, tq, w), lambda b, i: (b, i, 0))] + side.out_specs,
        out_shape=[jax.ShapeDtypeStruct((bsz, seq, w), BF16)] + side.out_shapes,
        compiler_params=_params(("parallel", "arbitrary")),
        name="neighbourhood_attn",
    )(q, k, v, k_ctx, v_ctx, bias_tbl, *side.arrays)
    return outs[0], side.finish(outs[1:])


def _na_bias_table(rpb):
    qcol = jnp.arange(GRID_W)[:, None]
    kcol = jnp.arange(GRID_W)[None, :]
    wstart = jnp.clip(qcol - C_WIN_COLS // 2, 0, GRID_W - C_WIN_COLS)
    ok = (kcol >= wstart) & (kcol < wstart + C_WIN_COLS)
    n_off = C_WIN_COLS - 1
    period = 2 * GRID_W
    vec = jnp.concatenate([rpb[..., n_off:].astype(F32),
                           jnp.zeros(rpb.shape[:2] + (period - 2 * n_off - 1,), F32),
                           rpb[..., :n_off].astype(F32)], axis=-1)
    toep = jnp.tile(vec, (1, 1, GRID_W))[..., :GRID_W * (period - 1)]
    toep = toep.reshape(rpb.shape[:2] + (GRID_W, period - 1))[..., :GRID_W]
    tbl = jnp.where(ok[None, None], toep * LOG2E, -jnp.inf)
    tbl = jnp.moveaxis(tbl, 1, 0)
    return jnp.concatenate([tbl[:-1], tbl[1:]], axis=-1)


def _merge_kernel(x_ref, mod_ref, gain_ref, ya_ref, of_ref, ob_ref, gr_ref, yc_ref, gn_ref,
                  wm_ref, bm_ref, wa_ref, wb_ref, wc_ref, wo_ref, o_ref):
    d = D_MODEL
    gn = gn_ref[...]
    x = x_ref[...]
    h = _norm_mod(x, gain_ref[...], mod_ref[0:1, :], mod_ref[1:2, :]).astype(BF16)
    o_sum = of_ref[...] + ob_ref[...]
    parts = []
    for hh in range(B_HEADS):
        oh = o_sum[:, hh * B_DV:(hh + 1) * B_DV]
        parts.append((oh * lax.rsqrt(jnp.mean(oh * oh, axis=-1, keepdims=True) + EPS)) * gn)
    y_b = (jnp.concatenate(parts, axis=1) * _silu(gr_ref[...])).astype(BF16)
    mixed = None
    for j, (y, w_ref) in enumerate(((ya_ref[...], wa_ref), (y_b, wb_ref), (yc_ref[...], wc_ref))):
        gate = _sigmoid(_dot(h, wm_ref[:, j * d:(j + 1) * d]) + bm_ref[:, j * d:(j + 1) * d])
        term = gate * _dot(y, w_ref[...])
        mixed = term if mixed is None else mixed + term
    o_ref[...] = x + mod_ref[2:3, :] * _dot(mixed.astype(BF16), wo_ref[...])


def _merge(x, mod_rows, gain, y_a, o_f, o_b, g_r, y_c, gla_gain, w_merge, b_merge, w_a, w_b, w_c, w_out, *, tm):
    bsz, seq, d = x.shape
    per_batch_mod = mod_rows.shape[0] > 1
    mod_map = (lambda b, i: (b, 0, 0)) if per_batch_mod else (lambda b, i: (0, 0, 0))
    tile = lambda w: pl.BlockSpec((None, tm, w), lambda b, i: (b, i, 0))
    wy = y_a.shape[-1]
    return pl.pallas_call(
        _merge_kernel,
        grid=(bsz, seq // tm),
        in_specs=[tile(d), pl.BlockSpec((None, 6, d), mod_map), _const_spec((1, d)),
                  tile(wy), tile(wy), tile(wy), tile(wy), tile(wy), _const_spec((1, B_DV)),
                  _const_spec(w_merge.shape), _const_spec(b_merge.shape),
                  _const_spec(w_a.shape), _const_spec(w_b.shape), _const_spec(w_c.shape),
                  _const_spec(w_out.shape)],
        out_specs=tile(d),
        out_shape=jax.ShapeDtypeStruct((bsz, seq, d), F32),
        compiler_params=_params(("parallel", "arbitrary")),
        name="merge_out",
    )(x, mod_rows, gain, y_a, o_f, o_b, g_r, y_c, gla_gain, w_merge, b_merge, w_a, w_b, w_c, w_out)


FFN_CHUNKS = ((0, 1024), (1024, 2048), (2048, FFN_HIDDEN))
FFN_SUB_ROWS = 256


def _ffn_kernel(x_ref, mod_ref, gain_ref, w1_ref, w2_ref, fg_ref, o_ref, *, final):
    for r0 in range(0, x_ref.shape[0], FFN_SUB_ROWS):
        rows = slice(r0, r0 + FFN_SUB_ROWS)
        x = x_ref[rows, :]
        h = _norm_mod(x, gain_ref[...], mod_ref[3:4, :], mod_ref[4:5, :]).astype(BF16)
        acc = None
        for c0, c1 in FFN_CHUNKS:
            gate = _dot(h, w1_ref[:, c0:c1])
            up = _dot(h, w1_ref[:, FFN_HIDDEN + c0:FFN_HIDDEN + c1])
            part = _dot((_silu(gate) * up).astype(BF16), w2_ref[c0:c1, :])
            acc = part if acc is None else acc + part
        y = x + mod_ref[5:6, :] * acc
        if final:
            y = (y * lax.rsqrt(jnp.mean(y * y, axis=-1, keepdims=True) + EPS)) * fg_ref[...]
        o_ref[rows, :] = y


def _ffn(x, mod_rows, gain, w1, w2, final_gain, *, final, tm):
    bsz, seq, d = x.shape
    per_batch_mod = mod_rows.shape[0] > 1
    mod_map = (lambda b, i: (b, 0, 0)) if per_batch_mod else (lambda b, i: (0, 0, 0))
    tile = pl.BlockSpec((None, tm, d), lambda b, i: (b, i, 0))
    return pl.pallas_call(
        functools.partial(_ffn_kernel, final=final),
        grid=(bsz, seq // tm),
        in_specs=[tile, pl.BlockSpec((None, 6, d), mod_map), _const_spec((1, d)),
                  _const_spec(w1.shape), _const_spec(w2.shape), _const_spec((1, d))],
        out_specs=tile,
        out_shape=jax.ShapeDtypeStruct((bsz, seq, d), F32),
        compiler_params=_params(("parallel", "arbitrary")),
        name="ffn_final" if final else "ffn",
    )(x, mod_rows, gain, w1, w2, final_gain)


def _rope_tables(seq):
    t = jnp.arange(seq)
    row = (t // GRID_W).astype(F32)
    col = (t % GRID_W).astype(F32)
    n_freq = HEAD_DIM // 4
    inv = ROPE_BASE ** (-jnp.arange(n_freq, dtype=F32) / n_freq)
    ang = jnp.concatenate([row[:, None] * inv[None], col[:, None] * inv[None]], axis=-1)
    cos, sin = jnp.cos(ang), jnp.sin(ang)
    return jnp.tile(cos, (1, 4)), jnp.tile(jnp.concatenate([-sin, sin], axis=-1), (1, 2))


def _permute_w_branch_a(w):
    return w.reshape(A_HEADS, HEAD_DIM, w.shape[1])[jnp.array(A_HEAD_ORDER)].reshape(w.shape).astype(BF16)


def _gate_weights(w_fwd, b_fwd, w_bwd, b_bwd):
    n = B_HEADS * B_DK
    r = B_GATE_RANK
    w = jnp.zeros((LANES, 2 * n), F32)
    w = w.at[:r, :n].set(w_fwd).at[r:2 * r, n:].set(w_bwd)
    return w.astype(BF16), jnp.concatenate([b_fwd, b_bwd])[None, :]


def kernel(x, c, ctx, c_ctx, w_ada, b_ada, norm_mix, w_in, attn_sink, gla_gate_w_fwd, gla_gate_b_fwd,
           gla_gate_w_bwd, gla_gate_b_bwd, gla_norm, na_rpb, w_branch_a, w_branch_b, w_branch_c,
           w_merge, b_merge, w_out, norm_ffn, w_ffn_in, w_ffn_out, final_norm):
    bsz, seq, d = x.shape
    n_ctx = ctx.shape[1]
    cos, sin = _rope_tables(seq)

    cvec = jnp.zeros((8, d), F32).at[:bsz].set(c).at[bsz].set(c_ctx)
    mod, w_perm_all = _modulation(cvec, w_ada, b_ada, w_in)
    mod = mod.reshape(DEPTH, 8, 6, d)

    xc = ctx
    zeros_state = jnp.zeros((bsz,) + GLA_STATE_SHAPE, F32)
    for l in range(DEPTH):
        last = l == DEPTH - 1
        mod_x = mod[l, :bsz]
        mod_c = mod[l, bsz:bsz + 1]
        gain_m = norm_mix[l][None, :]
        w_perm = w_perm_all
        w_gate, b_gate = _gate_weights(gla_gate_w_fwd[l], gla_gate_b_fwd[l], gla_gate_w_bwd[l], gla_gate_b_bwd[l])

        (caq, cak, cav, cgq, cgk, cgv, cgr, cnq, cnk, cnv, cgf, cgb), _ = _project(
            xc, mod_c, gain_m, w_perm, w_gate, b_gate, cos, sin, rope=False, tm=n_ctx, layer=l)
        (aq, ak, av, gq, gk, gv, gr, nq, nk, nv, gf, gb), (wm_bf, wb_bf, wc_bf, wo_bf) = _project(
            x, mod_x, gain_m, w_perm, w_gate, b_gate, cos, sin, rope=True, tm=512,
            cast_weights=(w_merge, w_branch_b, w_branch_c, w_out), layer=l)

        co_f, co_b, s_f, s_b = _gla(cgq, cgk, cgv, cgf, cgb, zeros_state, zeros_state, tile=n_ctx)
        o_f, o_b, _, _ = _gla(gq, gk, gv, gf, gb, s_f, s_b, tile=1024)

        sink = attn_sink[l] * LOG2E
        y_a = _window_attention(sink, aq, ak, av, cak, cav)
        y_c, (w1, w2) = _neighbourhood_attention(nq, nk, nv, cnk, cnv, _na_bias_table(na_rpb[l]),
                                                 cast_weights=(w_ffn_in, w_ffn_out), layer=l)

        merge_w = (gla_norm[l][None, :], wm_bf, b_merge[l][None, :],
                   _permute_w_branch_a(w_branch_a[l]), wb_bf, wc_bf, wo_bf)
        x = _merge(x, mod_x, gain_m, y_a, o_f, o_b, gr, y_c, *merge_w, tm=512)

        gain_f = norm_ffn[l][None, :]
        fg = final_norm[None, :]
        if not last:
            yc_a = _dense_attention(sink, caq, cak, cav, head_order=A_HEAD_ORDER, shared_kv=True)
            yc_c = _dense_attention(sink, cnq, cnk, cnv, head_order=None, shared_kv=False)
            xc = _merge(xc, mod_c, gain_m, yc_a, co_f, co_b, cgr, yc_c, *merge_w, tm=n_ctx)
            xc = _ffn(xc, mod_c, gain_f, w1, w2, fg, final=False, tm=n_ctx)
        x = _ffn(x, mod_x, gain_f, w1, w2, fg, final=last, tm=1024)
    return x
```

```python
import functools

import jax
import jax.numpy as jnp
from jax import lax
from jax.experimental import pallas as pl
from jax.experimental.pallas import tpu as pltpu

F32 = jnp.float32
BF16 = jnp.bfloat16

D_MODEL = 1024
DEPTH = 2
GRID_W = 64
HEAD_DIM = 64
EPS = 1e-6
ROPE_BASE = 10000.0
A_HEADS = 8
A_KV_HEADS = 2
A_BLOCK = 128
B_HEADS = 4
B_DK = 64
B_DV = 128
B_GATE_RANK = 16
B_GATE_NORM = 16.0
C_HEADS = 8
C_WIN_ROWS = 8
C_WIN_COLS = 16
FFN_HIDDEN = 2816

VMEM_LIMIT_BYTES = 56 * 1024 * 1024
LANES = 128

SEG_AQ = (0, 512)
SEG_AKV = (512, 768)
SEG_GQ = (768, 1024)
SEG_GK = (1024, 1280)
SEG_GV = (1280, 1792)
SEG_GR = (1792, 2304)
SEG_NQ = (2304, 2816)
SEG_NK = (2816, 3328)
SEG_NV = (3328, 3840)
SEG_GA = (3840, 3968)
IN_PADDED = 3968
W_IN_GATE_COLS = (2304, 2304 + 2 * B_GATE_RANK)

LOG2E = 1.4426950408889634
ATTN_Q_SCALE = HEAD_DIM ** -0.5 * LOG2E

GLA_CHUNK = 64
GLA_FAST_RANGE = 120.0


def _params(sem):
    return pltpu.CompilerParams(dimension_semantics=sem, vmem_limit_bytes=VMEM_LIMIT_BYTES)


def _const_spec(shape):
    nd = len(shape)
    return pl.BlockSpec(shape, lambda *_: (0,) * nd, pipeline_mode=pl.Buffered(1))


def _sigmoid(x):
    return 1.0 / (1.0 + jnp.exp(-x))


def _silu(x):
    return x * _sigmoid(x)


def _dot(a, b):
    return jnp.dot(a, b, preferred_element_type=F32)


def _dot_nt(a, b):
    return lax.dot_general(a, b, (((1,), (1,)), ((), ())), preferred_element_type=F32)


def _dot_tn(a, b):
    return lax.dot_general(a, b, (((0,), (0,)), ((), ())), preferred_element_type=F32)


def _norm_mod(x, gain, shift, scale):
    y = x * lax.rsqrt(jnp.mean(x * x, axis=-1, keepdims=True) + EPS)
    return (y * gain) * (1.0 + scale) + shift


def _relayout_w_in(wt_ref, o_ref):
    n_aq = SEG_AQ[1]
    ga0, ga1 = W_IN_GATE_COLS
    pieces = [wt_ref[h * HEAD_DIM:(h + 1) * HEAD_DIM, :] for h in A_HEAD_ORDER]
    pieces += [wt_ref[n_aq:ga0, :], wt_ref[ga1:, :], wt_ref[ga0:ga1, :],
               jnp.zeros((IN_PADDED - wt_ref.shape[0], wt_ref.shape[1]), F32)]
    o_ref[...] = jnp.concatenate(pieces, axis=0).T.astype(BF16)


def _mod_kernel(c_ref, w_ref, b_ref, w_in_ref, o_ref, w_perm_ref):
    s = _silu(c_ref[...])
    o_ref[...] = _dot(s.astype(BF16), w_ref[...].astype(BF16)) + b_ref[...]
    _relayout_w_in(w_in_ref, w_perm_ref)


def _modulation(cvec, w_ada, b_ada, w_in):
    tn = 1536
    n_out = w_ada.shape[-1]
    n_j = n_out // tn
    rows = w_in.shape[1] // n_j
    w_in_t = jnp.swapaxes(w_in, 1, 2)
    return pl.pallas_call(
        _mod_kernel,
        grid=(DEPTH, n_j),
        in_specs=[
            pl.BlockSpec((8, D_MODEL), lambda l, j: (0, 0)),
            pl.BlockSpec((None, D_MODEL, tn), lambda l, j: (l, 0, j)),
            pl.BlockSpec((None, 1, tn), lambda l, j: (l, 0, j)),
            pl.BlockSpec((None, w_in_t.shape[1], rows), lambda l, j: (l, 0, j)),
        ],
        out_specs=[pl.BlockSpec((None, 8, tn), lambda l, j: (l, 0, j)),
                   pl.BlockSpec((None, rows, IN_PADDED), lambda l, j: (l, j, 0))],
        out_shape=[jax.ShapeDtypeStruct((DEPTH, 8, n_out), F32),
                   jax.ShapeDtypeStruct((DEPTH, w_in.shape[1], IN_PADDED), BF16)],
        compiler_params=_params(("arbitrary", "arbitrary")),
        name="adaln_mod",
    )(cvec, w_ada, b_ada.reshape(DEPTH, 1, n_out), w_in_t)


def _rope(t, cos, sin):
    n = t.shape[1]
    lane = lax.broadcasted_iota(jnp.int32, t.shape, 1)
    first_half = (lane % HEAD_DIM) < (HEAD_DIM // 2)
    rot = jnp.where(first_half, pltpu.roll(t, n - HEAD_DIM // 2, 1), pltpu.roll(t, HEAD_DIM // 2, 1))
    reps = n // LANES
    return t * jnp.tile(cos, (1, reps)) + rot * jnp.tile(sin, (1, reps))


class _SideCasts:
    def __init__(self, stacked_weights, layer, n_steps, step_index):
        self.shapes2d = [w.shape[1:] for w in stacked_weights]
        self.arrays = [w.reshape(w.shape[0], n_steps, w.shape[1] // n_steps, w.shape[2]) for w in stacked_weights]
        self.specs = [pl.BlockSpec((None, None) + a.shape[2:], lambda *g: (layer, step_index(*g), 0, 0))
                      for a in self.arrays]
        self.out_specs = [pl.BlockSpec((None,) + a.shape[2:], lambda *g: (step_index(*g), 0, 0))
                          for a in self.arrays]
        self.out_shapes = [jax.ShapeDtypeStruct(a.shape[1:], BF16) for a in self.arrays]

    def finish(self, outs):
        return [o.reshape(s) for o, s in zip(outs, self.shapes2d)]


def _run_side_casts(in_refs, out_refs):
    for w_ref, o_ref in zip(in_refs, out_refs):
        o_ref[...] = w_ref[...].astype(BF16)


N_PROJ_INPUTS = 8
N_PROJ_OUTPUTS = 12


def _proj_kernel(*refs, rope):
    (x_ref, mod_ref, gain_ref, w_ref, wg_ref, bg_ref, cos_ref, sin_ref) = refs[:N_PROJ_INPUTS]
    n_side = (len(refs) - N_PROJ_INPUTS - N_PROJ_OUTPUTS) // 2
    outs = refs[N_PROJ_INPUTS + n_side:]
    (aq_ref, ak_ref, av_ref, gq_ref, gk_ref, gv_ref, gr_ref, nq_ref, nk_ref, nv_ref,
     gf_ref, gb_ref) = outs[:N_PROJ_OUTPUTS]
    _run_side_casts(refs[N_PROJ_INPUTS:N_PROJ_INPUTS + n_side], outs[N_PROJ_OUTPUTS:])
    scale = ATTN_Q_SCALE
    n_g = B_HEADS * B_DK
    h = _norm_mod(x_ref[...], gain_ref[...], mod_ref[0:1, :], mod_ref[1:2, :]).astype(BF16)

    def seg(s):
        return _dot(h, w_ref[:, s[0]:s[1]])

    aq = seg(SEG_AQ)
    akv = seg(SEG_AKV)
    z = _dot(seg(SEG_GA).astype(BF16), wg_ref[...]) + bg_ref[...]
    ak = akv[:, :LANES]
    if rope:
        cos = cos_ref[...]
        sin = sin_ref[...]
        aq = _rope(aq, cos, sin)
        ak = _rope(ak, cos, sin)
    aq_ref[...] = (aq * scale).astype(BF16)
    ak_ref[...] = ak.astype(BF16)
    av_ref[...] = akv[:, LANES:].astype(BF16)
    log_sig = jnp.minimum(z, 0.0) - jnp.log1p(jnp.exp(-jnp.abs(z)))
    g = log_sig / B_GATE_NORM
    gf_ref[...] = g[:, :n_g]
    gb_ref[...] = g[:, n_g:]
    gq_ref[...] = seg(SEG_GQ) * (B_DK ** -0.5)
    gk_ref[...] = seg(SEG_GK)
    gv_ref[...] = seg(SEG_GV).astype(BF16)
    gr_ref[...] = seg(SEG_GR)
    nq_ref[...] = (seg(SEG_NQ) * scale).astype(BF16)
    nk_ref[...] = seg(SEG_NK).astype(BF16)
    nv_ref[...] = seg(SEG_NV).astype(BF16)


def _project(x, mod_rows, gain, w_perm, w_gate, b_gate, cos, sin, *, rope, tm, cast_weights=(), layer=0):
    bsz, seq, _ = x.shape
    n_tiles = seq // tm
    per_batch_mod = mod_rows.shape[0] > 1
    mod_map = (lambda b, i: (b, 0, 0)) if per_batch_mod else (lambda b, i: (0, 0, 0))
    widths = [(512, BF16), (128, BF16), (128, BF16), (256, F32), (256, F32), (512, BF16), (512, F32),
              (512, BF16), (512, BF16), (512, BF16), (256, F32), (256, F32)]
    assert len(widths) == N_PROJ_OUTPUTS
    tile = lambda w: pl.BlockSpec((None, tm, w), lambda b, i: (b, i, 0))
    side = _SideCasts(cast_weights, layer, bsz * n_tiles, lambda b, i: b * n_tiles + i)
    outs = pl.pallas_call(
        functools.partial(_proj_kernel, rope=rope),
        grid=(bsz, n_tiles),
        in_specs=[
            tile(D_MODEL),
            pl.BlockSpec((None, 6, D_MODEL), mod_map),
            _const_spec((1, D_MODEL)),
            pl.BlockSpec((None, D_MODEL, IN_PADDED), lambda b, i: (layer, 0, 0), pipeline_mode=pl.Buffered(1)),
            _const_spec((LANES, 2 * B_HEADS * B_DK)),
            _const_spec((1, 2 * B_HEADS * B_DK)),
            pl.BlockSpec((tm, LANES), lambda b, i: (i, 0)),
            pl.BlockSpec((tm, LANES), lambda b, i: (i, 0)),
        ] + side.specs,
        out_specs=[tile(w) for w, _ in widths] + side.out_specs,
        out_shape=[jax.ShapeDtypeStruct((bsz, seq, w), dt) for w, dt in widths] + side.out_shapes,
        compiler_params=_params(("parallel", "arbitrary")),
        name="in_proj_rope" if rope else "in_proj_ctx",
    )(x, mod_rows, gain, w_perm, w_gate, b_gate, cos, sin, *side.arrays)
    return outs[:N_PROJ_OUTPUTS], side.finish(outs[N_PROJ_OUTPUTS:])


class _GlaPrep:
    def __init__(self, q_ref, k_ref, g_ref, reverse):
        t = q_ref.shape[0]
        c = GLA_CHUNK
        row = lax.broadcasted_iota(jnp.int32, (t, t), 0)
        col = lax.broadcasted_iota(jnp.int32, (t, t), 1)
        same_chunk = (row // c) == (col // c)
        tri = (same_chunk & ((col >= row) if reverse else (col <= row))).astype(BF16)
        self.reverse = reverse
        self.q = q_ref[...]
        self.k = k_ref[...]
        self.g_parts = _split3(g_ref[...])
        self.b = _dot_01(tri, self.g_parts)
        b = self.b
        self.totals = [b[ci * c:ci * c + 1, :] if reverse else b[ci * c + c - 1:ci * c + c, :]
                       for ci in range(t // c)]
        self.b_end = jnp.concatenate([jnp.broadcast_to(e, (c, e.shape[1])) for e in self.totals], axis=0)
        self.q_state = (self.q * jnp.exp(b)).astype(BF16)
        self.k_tail = (self.k * jnp.exp(self.b_end - b)).astype(BF16)
        self.decay = [jnp.exp(e) for e in self.totals]


def _split3(x):
    hi = x.astype(BF16)
    rest = x - hi.astype(F32)
    mid = rest.astype(BF16)
    return hi, mid, (rest - mid.astype(F32)).astype(BF16)


def _dot_01(m01, parts):
    return _dot(m01, parts[0]) + _dot(m01, parts[1]) + _dot(m01, parts[2])


def _gla_pair_scores(q_w, k_w, keep):
    c = GLA_CHUNK
    out = []
    for ci in range(q_w.shape[0] // c):
        rows = slice(ci * c, (ci + 1) * c)
        for p in range(B_HEADS // 2):
            ps = slice(p * LANES, (p + 1) * LANES)
            out.append(jnp.where(keep, _dot_nt(_split_head_pair(q_w[rows, ps]), k_w[rows, ps]), 0.0))
    return out


def _gla_scores_fast(prep):
    c = GLA_CHUNK
    qi = lax.broadcasted_iota(jnp.int32, (2 * c, c), 0) % c
    kj = lax.broadcasted_iota(jnp.int32, (2 * c, c), 1)
    keep = (kj >= qi) if prep.reverse else (kj <= qi)
    ref = 0.5 * prep.b_end
    q_in = (prep.q * jnp.exp(prep.b - ref)).astype(BF16)
    k_in = (prep.k * jnp.exp(ref - prep.b)).astype(BF16)
    return _gla_pair_scores(q_in, k_in, keep)


def _gla_scores_safe(prep):
    t = prep.q.shape[0]
    c = GLA_CHUNK
    rev = prep.reverse
    row = lax.broadcasted_iota(jnp.int32, (t, t), 0)
    col = lax.broadcasted_iota(jnp.int32, (t, t), 1)
    qi = lax.broadcasted_iota(jnp.int32, (2 * c, c), 0) % c
    kj = lax.broadcasted_iota(jnp.int32, (2 * c, c), 1)
    scores = _gla_pair_scores(prep.q.astype(BF16), prep.k.astype(BF16), qi == kj)
    s = c // 2
    while s >= 1:
        if rev:
            bnd = row - row % (2 * s) + s
            span = ((col >= bnd) & (col < row)) | ((col >= row) & (col < bnd))
        else:
            bnd = row - row % (2 * s) + s - 1
            span = ((col > bnd) & (col <= row)) | ((col > row) & (col <= bnd))
        e = jnp.exp(_dot_01(span.astype(BF16), prep.g_parts))
        q_side = ((qi % (2 * s)) < s) if rev else ((qi % (2 * s)) >= s)
        k_side = ((kj % (2 * s)) >= s) if rev else ((kj % (2 * s)) < s)
        keep = ((qi // (2 * s)) == (kj // (2 * s))) & q_side & k_side
        level = _gla_pair_scores((prep.q * e).astype(BF16), (prep.k * e).astype(BF16), keep)
        scores = [acc + a for acc, a in zip(scores, level)]
        s //= 2
    return scores


def _gla_local(scores, k_tail, v_ref):
    c = GLA_CHUNK
    n_pairs = B_HEADS // 2
    intra, upd = [], []
    for ci in range(k_tail.shape[0] // c):
        rows = slice(ci * c, (ci + 1) * c)
        intra_c, upd_c = [], []
        for p in range(n_pairs):
            a = scores[ci * n_pairs + p].astype(BF16)
            kt = _split_head_pair(k_tail[rows, p * LANES:(p + 1) * LANES])
            u = None
            for half in range(2):
                h = 2 * p + half
                vh = v_ref[rows, h * B_DV:(h + 1) * B_DV]
                intra_c.append(_dot(a[half * c:(half + 1) * c], vh))
                uh = _dot_tn(vh, kt[half * c:(half + 1) * c])
                u = uh if u is None else u + uh
            upd_c.append(u)
        intra.append(intra_c)
        upd.append(upd_c)
    return intra, upd


def _gla_states(s_ref, decay, upd, reverse):
    n = len(upd)
    order = range(n - 1, -1, -1) if reverse else range(n)
    starts = [None] * n
    s = [s_ref[p] for p in range(B_HEADS // 2)]
    for ci in order:
        starts[ci] = [sp.astype(BF16) for sp in s]
        s = [s[p] * decay[ci][:, p * LANES:(p + 1) * LANES] + upd[ci][p] for p in range(B_HEADS // 2)]
    for p in range(B_HEADS // 2):
        s_ref[p] = s[p]
    return starts


def _gla_finish(o_ref, q_state, starts, intra):
    c = GLA_CHUNK
    for ci in range(len(intra)):
        rows = slice(ci * c, (ci + 1) * c)
        for p in range(B_HEADS // 2):
            ps = slice(p * LANES, (p + 1) * LANES)
            inter = _dot_nt(_split_head_pair(q_state[rows, ps]), starts[ci][p])
            for half in range(2):
                h = 2 * p + half
                o_ref[rows, h * B_DV:(h + 1) * B_DV] = intra[ci][h] + inter[half * c:(half + 1) * c]


def _gla_sub_tile(qf_ref, kf_ref, vf_ref, gf_ref, qb_ref, kb_ref, vb_ref, gb_ref, of_ref, ob_ref, sf_ref, sb_ref):
    fwd = _GlaPrep(qf_ref, kf_ref, gf_ref, False)
    bwd = _GlaPrep(qb_ref, kb_ref, gb_ref, True)

    def tile_body(score_fn):
        scores_f = score_fn(fwd)
        scores_b = score_fn(bwd)
        intra_f, upd_f = _gla_local(scores_f, fwd.k_tail, vf_ref)
        intra_b, upd_b = _gla_local(scores_b, bwd.k_tail, vb_ref)
        starts_f = _gla_states(sf_ref, fwd.decay, upd_f, False)
        starts_b = _gla_states(sb_ref, bwd.decay, upd_b, True)
        _gla_finish(of_ref, fwd.q_state, starts_f, intra_f)
        _gla_finish(ob_ref, bwd.q_state, starts_b, intra_b)

    in_fast_range = -jnp.min(jnp.concatenate(fwd.totals + bwd.totals, axis=0)) < GLA_FAST_RANGE

    @pl.when(in_fast_range)
    def _():
        tile_body(_gla_scores_fast)

    @pl.when(jnp.logical_not(in_fast_range))
    def _():
        tile_body(_gla_scores_safe)


GLA_SUB_TILE = 256


def _gla_kernel(qf_ref, kf_ref, vf_ref, gf_ref, qb_ref, kb_ref, vb_ref, gb_ref, s0f_ref, s0b_ref,
                of_ref, ob_ref, sf_out_ref, sb_out_ref, sf_ref, sb_ref):
    i = pl.program_id(1)

    @pl.when(i == 0)
    def _():
        sf_ref[...] = s0f_ref[...]
        sb_ref[...] = s0b_ref[...]

    t = qf_ref.shape[0]
    sub = min(t, GLA_SUB_TILE)
    n_sub = t // sub
    for s in range(n_sub):
        f_rows = pl.ds(s * sub, sub)
        b_rows = pl.ds((n_sub - 1 - s) * sub, sub)
        _gla_sub_tile(qf_ref.at[f_rows], kf_ref.at[f_rows], vf_ref.at[f_rows], gf_ref.at[f_rows],
                      qb_ref.at[b_rows], kb_ref.at[b_rows], vb_ref.at[b_rows], gb_ref.at[b_rows],
                      of_ref.at[f_rows], ob_ref.at[b_rows], sf_ref, sb_ref)

    @pl.when(i == pl.num_programs(1) - 1)
    def _():
        sf_out_ref[...] = sf_ref[...]
        sb_out_ref[...] = sb_ref[...]


GLA_STATE_SHAPE = (B_HEADS // 2, B_DV, 2 * B_DK)


def _gla(q, k, v, gf, gb, s0f, s0b, *, tile):
    bsz, seq, _ = q.shape
    n = seq // tile
    fwd = lambda w: pl.BlockSpec((None, tile, w), lambda b, i: (b, i, 0))
    bwd = lambda w: pl.BlockSpec((None, tile, w), lambda b, i: (b, n - 1 - i, 0))
    st = pl.BlockSpec((None,) + GLA_STATE_SHAPE, lambda b, i: (b, 0, 0, 0))
    wk, wv = B_HEADS * B_DK, B_HEADS * B_DV
    st_shape = jax.ShapeDtypeStruct((bsz,) + GLA_STATE_SHAPE, F32)
    return pl.pallas_call(
        _gla_kernel,
        grid=(bsz, n),
        in_specs=[fwd(wk), fwd(wk), fwd(wv), fwd(wk), bwd(wk), bwd(wk), bwd(wv), bwd(wk), st, st],
        out_specs=[fwd(wv), bwd(wv), st, st],
        out_shape=[jax.ShapeDtypeStruct((bsz, seq, wv), F32), jax.ShapeDtypeStruct((bsz, seq, wv), F32),
                   st_shape, st_shape],
        scratch_shapes=[pltpu.VMEM(GLA_STATE_SHAPE, F32), pltpu.VMEM(GLA_STATE_SHAPE, F32)],
        compiler_params=_params(("parallel", "arbitrary")),
        name="gla_scan",
    )(q, k, v, gf, q, k, v, gb, s0f, s0b)


def _softmax_parts(scores, sink_tile):
    def lane_tiles(blocks):
        return [b[:, j:j + LANES] for b in blocks for j in range(0, b.shape[1], LANES)]

    tiles = lane_tiles(scores)
    if sink_tile is not None:
        tiles.append(sink_tile)
    m = functools.reduce(jnp.maximum, tiles).max(axis=-1, keepdims=True)
    ps = [jnp.exp2(s - m) for s in scores]
    acc = functools.reduce(jnp.add, lane_tiles(ps))
    if sink_tile is not None:
        lane = lax.broadcasted_iota(jnp.int32, sink_tile.shape, 1)
        acc = acc + jnp.where(lane == 0, jnp.exp2(sink_tile - m), 0.0)
    return ps, acc.sum(axis=-1, keepdims=True)


def _split_head_pair(t):
    lane = lax.broadcasted_iota(jnp.int32, t.shape, 1)
    zero = jnp.zeros_like(t)
    return jnp.concatenate([jnp.where(lane < HEAD_DIM, t, zero), jnp.where(lane >= HEAD_DIM, t, zero)], axis=0)


def _merge_head_pair(o):
    m = o.shape[0] // 2
    lane = lax.broadcasted_iota(jnp.int32, (m, LANES), 1)
    return jnp.where(lane < HEAD_DIM, o[:m], o[m:])


A_HEAD_ORDER = (0, 4, 1, 5, 2, 6, 3, 7)


def _win_block(q2, sink_tile, keys, values, prev_mask, next_mask):
    k_p, k_c, k_n, k_x = keys
    v_p, v_c, v_n, v_x = values
    s_p = _dot_nt(q2, k_p) + jnp.tile(prev_mask, (A_HEADS, 1))
    s_c = _dot_nt(q2, k_c)
    s_n = _dot_nt(q2, k_n) + jnp.tile(next_mask, (A_HEADS, 1))
    s_x = _dot_nt(q2, k_x)
    (p_p, p_c, p_n, p_x), denom = _softmax_parts([s_p, s_c, s_n, s_x], sink_tile)
    o = (_dot(p_p.astype(BF16), v_p) + _dot(p_c.astype(BF16), v_c)
         + _dot(p_n.astype(BF16), v_n) + _dot(p_x.astype(BF16), v_x))
    o = o / denom
    return jnp.concatenate([_merge_head_pair(o[j * 2 * A_BLOCK:(j + 1) * 2 * A_BLOCK])
                            for j in range(A_HEADS // 2)], axis=1)


WIN_BLOCKS_PER_STEP = 8


def _win_kernel(sink_ref, q_ref, kp_ref, kc_ref, kn_ref, vp_ref, vc_ref, vn_ref, kx_ref, vx_ref, o_ref):
    i = pl.program_id(1)
    n_steps = pl.num_programs(1)
    nbs = WIN_BLOCKS_PER_STEP
    sink_tile = jnp.concatenate([jnp.full((A_BLOCK, LANES), sink_ref[h], F32) for h in A_HEAD_ORDER], axis=0)
    qi = lax.broadcasted_iota(jnp.int32, (A_BLOCK, A_BLOCK), 0)
    kj = lax.broadcasted_iota(jnp.int32, (A_BLOCK, A_BLOCK), 1)
    neg = jnp.full((A_BLOCK, A_BLOCK), -jnp.inf, F32)
    zero = jnp.zeros((A_BLOCK, A_BLOCK), F32)
    prev_band = jnp.where(kj >= qi, zero, neg)
    next_band = jnp.where(kj <= qi, zero, neg)
    k_x, v_x = kx_ref[...], vx_ref[...]
    for a in range(nbs):
        rows = slice(a * A_BLOCK, (a + 1) * A_BLOCK)
        before = slice((a - 1) * A_BLOCK, a * A_BLOCK)
        after = slice((a + 1) * A_BLOCK, (a + 2) * A_BLOCK)
        prev_mask = prev_band if a > 0 else jnp.where(i > 0, prev_band, neg)
        next_mask = next_band if a < nbs - 1 else jnp.where(i < n_steps - 1, next_band, neg)
        keys = (kc_ref[before, :] if a > 0 else kp_ref[...], kc_ref[rows, :],
                kc_ref[after, :] if a < nbs - 1 else kn_ref[...], k_x)
        values = (vc_ref[before, :] if a > 0 else vp_ref[...], vc_ref[rows, :],
                  vc_ref[after, :] if a < nbs - 1 else vn_ref[...], v_x)
        q2 = jnp.concatenate([_split_head_pair(q_ref[rows, j * LANES:(j + 1) * LANES])
                              for j in range(A_HEADS // 2)], axis=0)
        o_ref[rows, :] = _win_block(q2, sink_tile, keys, values, prev_mask, next_mask).astype(o_ref.dtype)


def _window_attention(sink, q, k, v, k_ctx, v_ctx):
    bsz, seq, wq = q.shape
    nbs = WIN_BLOCKS_PER_STEP
    nb = seq // A_BLOCK
    wkv = k.shape[-1]
    n_ctx = k_ctx.shape[1]
    prev = pl.BlockSpec((None, A_BLOCK, wkv), lambda b, i: (b, jnp.maximum(nbs * i - 1, 0), 0))
    cur = pl.BlockSpec((None, nbs * A_BLOCK, wkv), lambda b, i: (b, i, 0))
    nxt = pl.BlockSpec((None, A_BLOCK, wkv), lambda b, i: (b, jnp.minimum(nbs * i + nbs, nb - 1), 0))
    ctx_spec = pl.BlockSpec((None, n_ctx, wkv), lambda b, i: (b, 0, 0))
    q_spec = pl.BlockSpec((None, nbs * A_BLOCK, wq), lambda b, i: (b, i, 0))
    return pl.pallas_call(
        _win_kernel,
        grid=(bsz, nb // nbs),
        in_specs=[pl.BlockSpec(memory_space=pltpu.SMEM), q_spec, prev, cur, nxt, prev, cur, nxt,
                  ctx_spec, ctx_spec],
        out_specs=q_spec,
        out_shape=jax.ShapeDtypeStruct((bsz, seq, wq), BF16),
        compiler_params=_params(("parallel", "arbitrary")),
        name="window_attn",
    )(sink, q, k, k, k, v, v, v, k_ctx, v_ctx)


def _dense_kernel(sink_ref, q_ref, k_ref, v_ref, o_ref, *, head_order, shared_kv):
    n_q = q_ref.shape[0]
    for j in range(q_ref.shape[1] // LANES):
        qs = slice(j * LANES, (j + 1) * LANES)
        ks = slice(0, LANES) if shared_kv else qs
        s = _dot_nt(_split_head_pair(q_ref[:, qs]), k_ref[:, ks])
        sink_tile = None
        if head_order is not None:
            sink_tile = jnp.concatenate(
                [jnp.full((n_q, LANES), sink_ref[head_order[2 * j + half]], F32) for half in range(2)], axis=0)
        (p,), denom = _softmax_parts([s], sink_tile)
        o_ref[:, qs] = _merge_head_pair(_dot(p.astype(BF16), v_ref[:, ks]) / denom).astype(o_ref.dtype)


def _dense_attention(sink, q, k, v, *, head_order, shared_kv):
    bsz, n_q, wq = q.shape
    wkv = k.shape[-1]
    full = lambda w: pl.BlockSpec((None, n_q, w), lambda b: (b, 0, 0))
    return pl.pallas_call(
        functools.partial(_dense_kernel, head_order=head_order, shared_kv=shared_kv),
        grid=(bsz,),
        in_specs=[pl.BlockSpec(memory_space=pltpu.SMEM), full(wq), full(wkv), full(wkv)],
        out_specs=full(wq),
        out_shape=jax.ShapeDtypeStruct((bsz, n_q, wq), BF16),
        compiler_params=_params(("parallel",)),
        name="ctx_dense_attn_sink" if shared_kv else "ctx_dense_attn",
    )(sink, q, k, v)


NA_ROWS_PER_STEP = 16


N_NA_INPUTS = 6


def _na_kernel(*refs):
    q_ref, k_ref, v_ref, kx_ref, vx_ref, bias_ref = refs[:N_NA_INPUTS]
    n_side = (len(refs) - N_NA_INPUTS - 1) // 2
    o_ref = refs[N_NA_INPUTS + n_side]
    _run_side_casts(refs[N_NA_INPUTS:N_NA_INPUTS + n_side], refs[N_NA_INPUTS + n_side + 1:])
    step = pl.program_id(1)
    grid_rows = k_ref.shape[0] // GRID_W
    n_keys = C_WIN_ROWS * GRID_W
    n_pairs = C_HEADS // 2
    for rr in range(NA_ROWS_PER_STEP):
        r = step * NA_ROWS_PER_STEP + rr
        row0 = jnp.clip(r - C_WIN_ROWS // 2, 0, grid_rows - C_WIN_ROWS)
        d_row0 = row0 - r + (C_WIN_ROWS - 1)
        key0 = pl.multiple_of(row0 * GRID_W, GRID_W)
        qrows = slice(rr * GRID_W, (rr + 1) * GRID_W)
        s_l, s_x = [], []
        for p in range(n_pairs):
            ps = slice(p * LANES, (p + 1) * LANES)
            q2 = _split_head_pair(q_ref[qrows, ps])
            s_l.append(_dot_nt(q2, k_ref[pl.ds(key0, n_keys), ps]))
            s_x.append(_dot_nt(q2, kx_ref[:, ps]))
        bias = jnp.concatenate(
            [bias_ref[d_row0 + 2 * j].reshape(C_HEADS * GRID_W, LANES) for j in range(C_WIN_ROWS // 2)], axis=1)
        (p_l, p_x), denom = _softmax_parts([jnp.concatenate(s_l, axis=0) + bias, jnp.concatenate(s_x, axis=0)], None)
        p_l = p_l.astype(BF16)
        p_x = p_x.astype(BF16)
        for p in range(n_pairs):
            ps = slice(p * LANES, (p + 1) * LANES)
            pr = slice(p * 2 * GRID_W, (p + 1) * 2 * GRID_W)
            o = (_dot(p_l[pr], v_ref[pl.ds(key0, n_keys), ps]) + _dot(p_x[pr], vx_ref[:, ps])) / denom[pr]
            o_ref[qrows, ps] = _merge_head_pair(o).astype(o_ref.dtype)


def _neighbourhood_attention(q, k, v, k_ctx, v_ctx, bias_tbl, cast_weights=(), layer=0):
    bsz, seq, w = q.shape
    n_ctx = k_ctx.shape[1]
    tq = NA_ROWS_PER_STEP * GRID_W
    n_tiles = seq // tq
    whole = pl.BlockSpec((None, seq, w), lambda b, i: (b, 0, 0))
    ctx_spec = pl.BlockSpec((None, n_ctx, w), lambda b, i: (b, 0, 0))
    side = _SideCasts(cast_weights, layer, bsz * n_tiles, lambda b, i: b * n_tiles + i)
    outs = pl.pallas_call(
        _na_kernel,
        grid=(bsz, n_tiles),
        in_specs=[pl.BlockSpec((None, tq, w), lambda b, i: (b, i, 0)), whole, whole, ctx_spec, ctx_spec,
                  _const_spec(bias_tbl.shape)] + side.specs,
        out_specs=[pl.BlockSpec((None, tq, w), lambda b, i: (b, i, 0))] + side.out_specs,
        out_shape=[jax.ShapeDtypeStruct((bsz, seq, w), BF16)] + side.out_shapes,
        compiler_params=_params(("parallel", "arbitrary")),
        name="neighbourhood_attn",
    )(q, k, v, k_ctx, v_ctx, bias_tbl, *side.arrays)
    return outs[0], side.finish(outs[1:])


def _na_bias_kernel(vec_ref, o_ref):
    q = lax.broadcasted_iota(jnp.int32, (GRID_W, LANES), 0)
    c = lax.broadcasted_iota(jnp.int32, (GRID_W, LANES), 1)
    k = c % GRID_W
    wstart = jnp.clip(q - C_WIN_COLS // 2, 0, GRID_W - C_WIN_COLS)
    ok = (k >= wstart) & (k < wstart + C_WIN_COLS)
    n_dr = vec_ref.shape[1]
    for h in range(vec_ref.shape[0]):
        toep = [pltpu.roll(jnp.broadcast_to(vec_ref[h, dr:dr + 1, :], (GRID_W, LANES)), 0, 1,
                           stride=1, stride_axis=0) for dr in range(n_dr)]
        for dr in range(n_dr - 1):
            pair = jnp.where(c < GRID_W, toep[dr], pltpu.roll(toep[dr + 1], GRID_W, 1))
            o_ref[dr, h] = jnp.where(ok, pair * LOG2E, -jnp.inf)


def _na_bias_table(rpb):
    n_off = C_WIN_COLS - 1
    vec = jnp.concatenate([rpb[..., n_off:].astype(F32),
                           jnp.zeros(rpb.shape[:2] + (LANES - 2 * n_off - 1,), F32),
                           rpb[..., :n_off].astype(F32)], axis=-1)
    return pl.pallas_call(
        _na_bias_kernel,
        out_shape=jax.ShapeDtypeStruct((rpb.shape[1] - 1, rpb.shape[0], GRID_W, LANES), F32),
        name="na_bias_table",
    )(vec)


def _merge_kernel(x_ref, mod_ref, gain_ref, ya_ref, of_ref, ob_ref, gr_ref, yc_ref, gn_ref,
                  wm_ref, bm_ref, wa_ref, wb_ref, wc_ref, wo_ref, o_ref):
    d = D_MODEL
    gn = gn_ref[...]
    x = x_ref[...]
    h = _norm_mod(x, gain_ref[...], mod_ref[0:1, :], mod_ref[1:2, :]).astype(BF16)
    o_sum = of_ref[...] + ob_ref[...]
    parts = []
    for hh in range(B_HEADS):
        oh = o_sum[:, hh * B_DV:(hh + 1) * B_DV]
        parts.append((oh * lax.rsqrt(jnp.mean(oh * oh, axis=-1, keepdims=True) + EPS)) * gn)
    y_b = (jnp.concatenate(parts, axis=1) * _silu(gr_ref[...])).astype(BF16)
    mixed = None
    for j, (y, w_ref) in enumerate(((ya_ref[...], wa_ref), (y_b, wb_ref), (yc_ref[...], wc_ref))):
        gate = _sigmoid(_dot(h, wm_ref[:, j * d:(j + 1) * d]) + bm_ref[:, j * d:(j + 1) * d])
        term = gate * _dot(y, w_ref[...])
        mixed = term if mixed is None else mixed + term
    o_ref[...] = x + mod_ref[2:3, :] * _dot(mixed.astype(BF16), wo_ref[...])


def _merge(x, mod_rows, gain, y_a, o_f, o_b, g_r, y_c, gla_gain, w_merge, b_merge, w_a, w_b, w_c, w_out, *, tm):
    bsz, seq, d = x.shape
    per_batch_mod = mod_rows.shape[0] > 1
    mod_map = (lambda b, i: (b, 0, 0)) if per_batch_mod else (lambda b, i: (0, 0, 0))
    tile = lambda w: pl.BlockSpec((None, tm, w), lambda b, i: (b, i, 0))
    wy = y_a.shape[-1]
    return pl.pallas_call(
        _merge_kernel,
        grid=(bsz, seq // tm),
        in_specs=[tile(d), pl.BlockSpec((None, 6, d), mod_map), _const_spec((1, d)),
                  tile(wy), tile(wy), tile(wy), tile(wy), tile(wy), _const_spec((1, B_DV)),
                  _const_spec(w_merge.shape), _const_spec(b_merge.shape),
                  _const_spec(w_a.shape), _const_spec(w_b.shape), _const_spec(w_c.shape),
                  _const_spec(w_out.shape)],
        out_specs=tile(d),
        out_shape=jax.ShapeDtypeStruct((bsz, seq, d), F32),
        compiler_params=_params(("parallel", "arbitrary")),
        name="merge_out",
    )(x, mod_rows, gain, y_a, o_f, o_b, g_r, y_c, gla_gain, w_merge, b_merge, w_a, w_b, w_c, w_out)


FFN_CHUNKS = ((0, 1024), (1024, 2048), (2048, FFN_HIDDEN))
FFN_SUB_ROWS = 256


def _ffn_kernel(x_ref, mod_ref, gain_ref, w1_ref, w2_ref, fg_ref, o_ref, *, final):
    for r0 in range(0, x_ref.shape[0], FFN_SUB_ROWS):
        rows = slice(r0, r0 + FFN_SUB_ROWS)
        x = x_ref[rows, :]
        h = _norm_mod(x, gain_ref[...], mod_ref[3:4, :], mod_ref[4:5, :]).astype(BF16)
        acc = None
        for c0, c1 in FFN_CHUNKS:
            gate = _dot(h, w1_ref[:, c0:c1])
            up = _dot(h, w1_ref[:, FFN_HIDDEN + c0:FFN_HIDDEN + c1])
            part = _dot((_silu(gate) * up).astype(BF16), w2_ref[c0:c1, :])
            acc = part if acc is None else acc + part
        y = x + mod_ref[5:6, :] * acc
        if final:
            y = (y * lax.rsqrt(jnp.mean(y * y, axis=-1, keepdims=True) + EPS)) * fg_ref[...]
        o_ref[rows, :] = y


def _ffn(x, mod_rows, gain, w1, w2, final_gain, *, final, tm):
    bsz, seq, d = x.shape
    per_batch_mod = mod_rows.shape[0] > 1
    mod_map = (lambda b, i: (b, 0, 0)) if per_batch_mod else (lambda b, i: (0, 0, 0))
    tile = pl.BlockSpec((None, tm, d), lambda b, i: (b, i, 0))
    return pl.pallas_call(
        functools.partial(_ffn_kernel, final=final),
        grid=(bsz, seq // tm),
        in_specs=[tile, pl.BlockSpec((None, 6, d), mod_map), _const_spec((1, d)),
                  _const_spec(w1.shape), _const_spec(w2.shape), _const_spec((1, d))],
        out_specs=tile,
        out_shape=jax.ShapeDtypeStruct((bsz, seq, d), F32),
        compiler_params=_params(("parallel", "arbitrary")),
        name="ffn_final" if final else "ffn",
    )(x, mod_rows, gain, w1, w2, final_gain)


def _rope_tables(seq):
    t = jnp.arange(seq)
    row = (t // GRID_W).astype(F32)
    col = (t % GRID_W).astype(F32)
    n_freq = HEAD_DIM // 4
    inv = ROPE_BASE ** (-jnp.arange(n_freq, dtype=F32) / n_freq)
    ang = jnp.concatenate([row[:, None] * inv[None], col[:, None] * inv[None]], axis=-1)
    cos, sin = jnp.cos(ang), jnp.sin(ang)
    return jnp.tile(cos, (1, 4)), jnp.tile(jnp.concatenate([-sin, sin], axis=-1), (1, 2))


def _permute_w_branch_a(w):
    return w.reshape(A_HEADS, HEAD_DIM, w.shape[1])[jnp.array(A_HEAD_ORDER)].reshape(w.shape).astype(BF16)


def _gate_weights(w_fwd, b_fwd, w_bwd, b_bwd):
    n = B_HEADS * B_DK
    r = B_GATE_RANK
    w = jnp.zeros((LANES, 2 * n), F32)
    w = w.at[:r, :n].set(w_fwd).at[r:2 * r, n:].set(w_bwd)
    return w.astype(BF16), jnp.concatenate([b_fwd, b_bwd])[None, :]


def kernel(x, c, ctx, c_ctx, w_ada, b_ada, norm_mix, w_in, attn_sink, gla_gate_w_fwd, gla_gate_b_fwd,
           gla_gate_w_bwd, gla_gate_b_bwd, gla_norm, na_rpb, w_branch_a, w_branch_b, w_branch_c,
           w_merge, b_merge, w_out, norm_ffn, w_ffn_in, w_ffn_out, final_norm):
    bsz, seq, d = x.shape
    n_ctx = ctx.shape[1]
    cos, sin = _rope_tables(seq)

    cvec = jnp.zeros((8, d), F32).at[:bsz].set(c).at[bsz].set(c_ctx)
    mod, w_perm_all = _modulation(cvec, w_ada, b_ada, w_in)
    mod = mod.reshape(DEPTH, 8, 6, d)

    xc = ctx
    zeros_state = jnp.zeros((bsz,) + GLA_STATE_SHAPE, F32)
    for l in range(DEPTH):
        last = l == DEPTH - 1
        mod_x = mod[l, :bsz]
        mod_c = mod[l, bsz:bsz + 1]
        gain_m = norm_mix[l][None, :]
        w_perm = w_perm_all
        w_gate, b_gate = _gate_weights(gla_gate_w_fwd[l], gla_gate_b_fwd[l], gla_gate_w_bwd[l], gla_gate_b_bwd[l])

        (caq, cak, cav, cgq, cgk, cgv, cgr, cnq, cnk, cnv, cgf, cgb), _ = _project(
            xc, mod_c, gain_m, w_perm, w_gate, b_gate, cos, sin, rope=False, tm=n_ctx, layer=l)
        (aq, ak, av, gq, gk, gv, gr, nq, nk, nv, gf, gb), (wm_bf, wb_bf, wc_bf, wo_bf) = _project(
            x, mod_x, gain_m, w_perm, w_gate, b_gate, cos, sin, rope=True, tm=512,
            cast_weights=(w_merge, w_branch_b, w_branch_c, w_out), layer=l)

        co_f, co_b, s_f, s_b = _gla(cgq, cgk, cgv, cgf, cgb, zeros_state, zeros_state, tile=n_ctx)
        o_f, o_b, _, _ = _gla(gq, gk, gv, gf, gb, s_f, s_b, tile=1024)

        sink = attn_sink[l] * LOG2E
        y_a = _window_attention(sink, aq, ak, av, cak, cav)
        y_c, (w1, w2) = _neighbourhood_attention(nq, nk, nv, cnk, cnv, _na_bias_table(na_rpb[l]),
                                                 cast_weights=(w_ffn_in, w_ffn_out), layer=l)

        merge_w = (gla_norm[l][None, :], wm_bf, b_merge[l][None, :],
                   _permute_w_branch_a(w_branch_a[l]), wb_bf, wc_bf, wo_bf)
        x = _merge(x, mod_x, gain_m, y_a, o_f, o_b, gr, y_c, *merge_w, tm=512)

        gain_f = norm_ffn[l][None, :]
        fg = final_norm[None, :]
        if not last:
            yc_a = _dense_attention(sink, caq, cak, cav, head_order=A_HEAD_ORDER, shared_kv=True)
            yc_c = _dense_attention(sink, cnq, cnk, cnv, head_order=None, shared_kv=False)
            xc = _merge(xc, mod_c, gain_m, yc_a, co_f, co_b, cgr, yc_c, *merge_w, tm=n_ctx)
            xc = _ffn(xc, mod_c, gain_f, w1, w2, fg, final=False, tm=n_ctx)
        x = _ffn(x, mod_x, gain_f, w1, w2, fg, final=last, tm=1024)
    return x
```

```python
import functools

import jax
import jax.numpy as jnp
from jax import lax
from jax.experimental import pallas as pl
from jax.experimental.pallas import tpu as pltpu

F32 = jnp.float32
BF16 = jnp.bfloat16

D_MODEL = 1024
DEPTH = 2
GRID_W = 64
HEAD_DIM = 64
EPS = 1e-6
ROPE_BASE = 10000.0
A_HEADS = 8
A_KV_HEADS = 2
A_BLOCK = 128
B_HEADS = 4
B_DK = 64
B_DV = 128
B_GATE_RANK = 16
B_GATE_NORM = 16.0
C_HEADS = 8
C_WIN_ROWS = 8
C_WIN_COLS = 16
FFN_HIDDEN = 2816

VMEM_LIMIT_BYTES = 56 * 1024 * 1024
LANES = 128

SEG_AQ = (0, 512)
SEG_AKV = (512, 768)
SEG_GQ = (768, 1024)
SEG_GK = (1024, 1280)
SEG_GV = (1280, 1792)
SEG_GR = (1792, 2304)
SEG_NQ = (2304, 2816)
SEG_NK = (2816, 3328)
SEG_NV = (3328, 3840)
SEG_GA = (3840, 3968)
IN_PADDED = 3968
W_IN_GATE_COLS = (2304, 2304 + 2 * B_GATE_RANK)

LOG2E = 1.4426950408889634
ATTN_Q_SCALE = HEAD_DIM ** -0.5 * LOG2E

GLA_CHUNK = 128
GLA_FAST_RANGE = 120.0


def _params(sem):
    return pltpu.CompilerParams(dimension_semantics=sem, vmem_limit_bytes=VMEM_LIMIT_BYTES)


def _const_spec(shape):
    nd = len(shape)
    return pl.BlockSpec(shape, lambda *_: (0,) * nd, pipeline_mode=pl.Buffered(1))


def _sigmoid(x):
    return 1.0 / (1.0 + jnp.exp(-x))


def _silu(x):
    return x * _sigmoid(x)


def _dot(a, b):
    return jnp.dot(a, b, preferred_element_type=F32)


def _dot_nt(a, b):
    return lax.dot_general(a, b, (((1,), (1,)), ((), ())), preferred_element_type=F32)


def _dot_tn(a, b):
    return lax.dot_general(a, b, (((0,), (0,)), ((), ())), preferred_element_type=F32)


def _norm_mod(x, gain, shift, scale):
    y = x * lax.rsqrt(jnp.mean(x * x, axis=-1, keepdims=True) + EPS)
    return (y * gain) * (1.0 + scale) + shift


def _relayout_w_in(wt_ref, o_ref):
    n_aq = SEG_AQ[1]
    ga0, ga1 = W_IN_GATE_COLS
    pieces = [wt_ref[h * HEAD_DIM:(h + 1) * HEAD_DIM, :] for h in A_HEAD_ORDER]
    pieces += [wt_ref[n_aq:ga0, :], wt_ref[ga1:, :], wt_ref[ga0:ga1, :],
               jnp.zeros((IN_PADDED - wt_ref.shape[0], wt_ref.shape[1]), F32)]
    o_ref[...] = jnp.concatenate(pieces, axis=0).T.astype(BF16)


def _mod_kernel(c_ref, w_ref, b_ref, w_in_ref, o_ref, w_perm_ref):
    s = _silu(c_ref[...])
    o_ref[...] = _dot(s.astype(BF16), w_ref[...].astype(BF16)) + b_ref[...]
    _relayout_w_in(w_in_ref, w_perm_ref)


def _modulation(cvec, w_ada, b_ada, w_in):
    tn = 1536
    n_out = w_ada.shape[-1]
    n_j = n_out // tn
    rows = w_in.shape[1] // n_j
    w_in_t = jnp.swapaxes(w_in, 1, 2)
    return pl.pallas_call(
        _mod_kernel,
        grid=(DEPTH, n_j),
        in_specs=[
            pl.BlockSpec((8, D_MODEL), lambda l, j: (0, 0)),
            pl.BlockSpec((None, D_MODEL, tn), lambda l, j: (l, 0, j)),
            pl.BlockSpec((None, 1, tn), lambda l, j: (l, 0, j)),
            pl.BlockSpec((None, w_in_t.shape[1], rows), lambda l, j: (l, 0, j)),
        ],
        out_specs=[pl.BlockSpec((None, 8, tn), lambda l, j: (l, 0, j)),
                   pl.BlockSpec((None, rows, IN_PADDED), lambda l, j: (l, j, 0))],
        out_shape=[jax.ShapeDtypeStruct((DEPTH, 8, n_out), F32),
                   jax.ShapeDtypeStruct((DEPTH, w_in.shape[1], IN_PADDED), BF16)],
        compiler_params=_params(("arbitrary", "arbitrary")),
        name="adaln_mod",
    )(cvec, w_ada, b_ada.reshape(DEPTH, 1, n_out), w_in_t)


def _rope(t, cos, sin):
    n = t.shape[1]
    lane = lax.broadcasted_iota(jnp.int32, t.shape, 1)
    first_half = (lane % HEAD_DIM) < (HEAD_DIM // 2)
    rot = jnp.where(first_half, pltpu.roll(t, n - HEAD_DIM // 2, 1), pltpu.roll(t, HEAD_DIM // 2, 1))
    reps = n // LANES
    return t * jnp.tile(cos, (1, reps)) + rot * jnp.tile(sin, (1, reps))


class _SideCasts:
    def __init__(self, stacked_weights, layer, n_steps, step_index):
        self.shapes2d = [w.shape[1:] for w in stacked_weights]
        self.arrays = [w.reshape(w.shape[0], n_steps, w.shape[1] // n_steps, w.shape[2]) for w in stacked_weights]
        self.specs = [pl.BlockSpec((None, None) + a.shape[2:], lambda *g: (layer, step_index(*g), 0, 0))
                      for a in self.arrays]
        self.out_specs = [pl.BlockSpec((None,) + a.shape[2:], lambda *g: (step_index(*g), 0, 0))
                          for a in self.arrays]
        self.out_shapes = [jax.ShapeDtypeStruct(a.shape[1:], BF16) for a in self.arrays]

    def finish(self, outs):
        return [o.reshape(s) for o, s in zip(outs, self.shapes2d)]


def _run_side_casts(in_refs, out_refs):
    for w_ref, o_ref in zip(in_refs, out_refs):
        o_ref[...] = w_ref[...].astype(BF16)


N_PROJ_INPUTS = 8
N_PROJ_OUTPUTS = 12


def _proj_kernel(*refs, rope):
    (x_ref, mod_ref, gain_ref, w_ref, wg_ref, bg_ref, cos_ref, sin_ref) = refs[:N_PROJ_INPUTS]
    n_side = (len(refs) - N_PROJ_INPUTS - N_PROJ_OUTPUTS) // 2
    outs = refs[N_PROJ_INPUTS + n_side:]
    (aq_ref, ak_ref, av_ref, gq_ref, gk_ref, gv_ref, gr_ref, nq_ref, nk_ref, nv_ref,
     gf_ref, gb_ref) = outs[:N_PROJ_OUTPUTS]
    _run_side_casts(refs[N_PROJ_INPUTS:N_PROJ_INPUTS + n_side], outs[N_PROJ_OUTPUTS:])
    scale = ATTN_Q_SCALE
    n_g = B_HEADS * B_DK
    h = _norm_mod(x_ref[...], gain_ref[...], mod_ref[0:1, :], mod_ref[1:2, :]).astype(BF16)

    def seg(s):
        return _dot(h, w_ref[:, s[0]:s[1]])

    aq = seg(SEG_AQ)
    akv = seg(SEG_AKV)
    z = _dot(seg(SEG_GA).astype(BF16), wg_ref[...]) + bg_ref[...]
    ak = akv[:, :LANES]
    if rope:
        cos = cos_ref[...]
        sin = sin_ref[...]
        aq = _rope(aq, cos, sin)
        ak = _rope(ak, cos, sin)
    aq_ref[...] = (aq * scale).astype(BF16)
    ak_ref[...] = ak.astype(BF16)
    av_ref[...] = akv[:, LANES:].astype(BF16)
    log_sig = jnp.minimum(z, 0.0) - jnp.log1p(jnp.exp(-jnp.abs(z)))
    g = log_sig / B_GATE_NORM
    gf_ref[...] = g[:, :n_g]
    gb_ref[...] = g[:, n_g:]
    gq_ref[...] = seg(SEG_GQ) * (B_DK ** -0.5)
    gk_ref[...] = seg(SEG_GK)
    gv_ref[...] = seg(SEG_GV).astype(BF16)
    gr_ref[...] = seg(SEG_GR)
    nq_ref[...] = (seg(SEG_NQ) * scale).astype(BF16)
    nk_ref[...] = seg(SEG_NK).astype(BF16)
    nv_ref[...] = seg(SEG_NV).astype(BF16)


def _project(x, mod_rows, gain, w_perm, w_gate, b_gate, cos, sin, *, rope, tm, cast_weights=(), layer=0):
    bsz, seq, _ = x.shape
    n_tiles = seq // tm
    per_batch_mod = mod_rows.shape[0] > 1
    mod_map = (lambda b, i: (b, 0, 0)) if per_batch_mod else (lambda b, i: (0, 0, 0))
    widths = [(512, BF16), (128, BF16), (128, BF16), (256, F32), (256, F32), (512, BF16), (512, F32),
              (512, BF16), (512, BF16), (512, BF16), (256, F32), (256, F32)]
    assert len(widths) == N_PROJ_OUTPUTS
    tile = lambda w: pl.BlockSpec((None, tm, w), lambda b, i: (b, i, 0))
    side = _SideCasts(cast_weights, layer, bsz * n_tiles, lambda b, i: b * n_tiles + i)
    outs = pl.pallas_call(
        functools.partial(_proj_kernel, rope=rope),
        grid=(bsz, n_tiles),
        in_specs=[
            tile(D_MODEL),
            pl.BlockSpec((None, 6, D_MODEL), mod_map),
            _const_spec((1, D_MODEL)),
            pl.BlockSpec((None, D_MODEL, IN_PADDED), lambda b, i: (layer, 0, 0), pipeline_mode=pl.Buffered(1)),
            _const_spec((LANES, 2 * B_HEADS * B_DK)),
            _const_spec((1, 2 * B_HEADS * B_DK)),
            pl.BlockSpec((tm, LANES), lambda b, i: (i, 0)),
            pl.BlockSpec((tm, LANES), lambda b, i: (i, 0)),
        ] + side.specs,
        out_specs=[tile(w) for w, _ in widths] + side.out_specs,
        out_shape=[jax.ShapeDtypeStruct((bsz, seq, w), dt) for w, dt in widths] + side.out_shapes,
        compiler_params=_params(("parallel", "arbitrary")),
        name="in_proj_rope" if rope else "in_proj_ctx",
    )(x, mod_rows, gain, w_perm, w_gate, b_gate, cos, sin, *side.arrays)
    return outs[:N_PROJ_OUTPUTS], side.finish(outs[N_PROJ_OUTPUTS:])


class _GlaPrep:
    def __init__(self, q_ref, k_ref, g_ref, reverse):
        t = q_ref.shape[0]
        c = GLA_CHUNK
        row = lax.broadcasted_iota(jnp.int32, (t, t), 0)
        col = lax.broadcasted_iota(jnp.int32, (t, t), 1)
        same_chunk = (row // c) == (col // c)
        tri = (same_chunk & ((col >= row) if reverse else (col <= row))).astype(BF16)
        self.reverse = reverse
        self.q = q_ref[...]
        self.k = k_ref[...]
        self.g_parts = _split3(g_ref[...])
        self.b = _dot_01(tri, self.g_parts)
        b = self.b
        self.totals = [b[ci * c:ci * c + 1, :] if reverse else b[ci * c + c - 1:ci * c + c, :]
                       for ci in range(t // c)]
        self.b_end = jnp.concatenate([jnp.broadcast_to(e, (c, e.shape[1])) for e in self.totals], axis=0)
        self.q_state = (self.q * jnp.exp(b)).astype(BF16)
        self.k_tail = (self.k * jnp.exp(self.b_end - b)).astype(BF16)
        self.decay = [jnp.exp(e) for e in self.totals]


def _split3(x):
    hi = x.astype(BF16)
    rest = x - hi.astype(F32)
    mid = rest.astype(BF16)
    return hi, mid, (rest - mid.astype(F32)).astype(BF16)


def _dot_01(m01, parts):
    return _dot(m01, parts[0]) + _dot(m01, parts[1]) + _dot(m01, parts[2])


def _gla_pair_scores(q_w, k_w, keep):
    c = GLA_CHUNK
    out = []
    for ci in range(q_w.shape[0] // c):
        rows = slice(ci * c, (ci + 1) * c)
        for p in range(B_HEADS // 2):
            ps = slice(p * LANES, (p + 1) * LANES)
            out.append(jnp.where(keep, _dot_nt(_split_head_pair(q_w[rows, ps]), k_w[rows, ps]), 0.0))
    return out


def _gla_scores_fast(prep):
    c = GLA_CHUNK
    qi = lax.broadcasted_iota(jnp.int32, (2 * c, c), 0) % c
    kj = lax.broadcasted_iota(jnp.int32, (2 * c, c), 1)
    keep = (kj >= qi) if prep.reverse else (kj <= qi)
    ref = 0.5 * prep.b_end
    q_in = (prep.q * jnp.exp(prep.b - ref)).astype(BF16)
    k_in = (prep.k * jnp.exp(ref - prep.b)).astype(BF16)
    return _gla_pair_scores(q_in, k_in, keep)


def _gla_scores_safe(prep):
    t = prep.q.shape[0]
    c = GLA_CHUNK
    rev = prep.reverse
    row = lax.broadcasted_iota(jnp.int32, (t, t), 0)
    col = lax.broadcasted_iota(jnp.int32, (t, t), 1)
    qi = lax.broadcasted_iota(jnp.int32, (2 * c, c), 0) % c
    kj = lax.broadcasted_iota(jnp.int32, (2 * c, c), 1)
    scores = _gla_pair_scores(prep.q.astype(BF16), prep.k.astype(BF16), qi == kj)
    s = c // 2
    while s >= 1:
        if rev:
            bnd = row - row % (2 * s) + s
            span = ((col >= bnd) & (col < row)) | ((col >= row) & (col < bnd))
        else:
            bnd = row - row % (2 * s) + s - 1
            span = ((col > bnd) & (col <= row)) | ((col > row) & (col <= bnd))
        e = jnp.exp(_dot_01(span.astype(BF16), prep.g_parts))
        q_side = ((qi % (2 * s)) < s) if rev else ((qi % (2 * s)) >= s)
        k_side = ((kj % (2 * s)) >= s) if rev else ((kj % (2 * s)) < s)
        keep = ((qi // (2 * s)) == (kj // (2 * s))) & q_side & k_side
        level = _gla_pair_scores((prep.q * e).astype(BF16), (prep.k * e).astype(BF16), keep)
        scores = [acc + a for acc, a in zip(scores, level)]
        s //= 2
    return scores


def _gla_local(scores, k_tail, v_ref):
    c = GLA_CHUNK
    n_pairs = B_HEADS // 2
    intra, upd = [], []
    for ci in range(k_tail.shape[0] // c):
        rows = slice(ci * c, (ci + 1) * c)
        intra_c, upd_c = [], []
        for p in range(n_pairs):
            a = scores[ci * n_pairs + p].astype(BF16)
            kt = _split_head_pair(k_tail[rows, p * LANES:(p + 1) * LANES])
            u = None
            for half in range(2):
                h = 2 * p + half
                vh = v_ref[rows, h * B_DV:(h + 1) * B_DV]
                intra_c.append(_dot(a[half * c:(half + 1) * c], vh))
                uh = _dot_tn(vh, kt[half * c:(half + 1) * c])
                u = uh if u is None else u + uh
            upd_c.append(u)
        intra.append(intra_c)
        upd.append(upd_c)
    return intra, upd


def _gla_states(s_ref, decay, upd, reverse):
    n = len(upd)
    order = range(n - 1, -1, -1) if reverse else range(n)
    starts = [None] * n
    s = [s_ref[p] for p in range(B_HEADS // 2)]
    for ci in order:
        starts[ci] = [sp.astype(BF16) for sp in s]
        s = [s[p] * decay[ci][:, p * LANES:(p + 1) * LANES] + upd[ci][p] for p in range(B_HEADS // 2)]
    for p in range(B_HEADS // 2):
        s_ref[p] = s[p]
    return starts


def _gla_finish(o_ref, q_state, starts, intra):
    c = GLA_CHUNK
    for ci in range(len(intra)):
        rows = slice(ci * c, (ci + 1) * c)
        for p in range(B_HEADS // 2):
            ps = slice(p * LANES, (p + 1) * LANES)
            inter = _dot_nt(_split_head_pair(q_state[rows, ps]), starts[ci][p])
            for half in range(2):
                h = 2 * p + half
                o_ref[rows, h * B_DV:(h + 1) * B_DV] = intra[ci][h] + inter[half * c:(half + 1) * c]


def _gla_sub_tile(qf_ref, kf_ref, vf_ref, gf_ref, qb_ref, kb_ref, vb_ref, gb_ref, of_ref, ob_ref, sf_ref, sb_ref):
    fwd = _GlaPrep(qf_ref, kf_ref, gf_ref, False)
    bwd = _GlaPrep(qb_ref, kb_ref, gb_ref, True)

    def tile_body(score_fn):
        scores_f = score_fn(fwd)
        scores_b = score_fn(bwd)
        intra_f, upd_f = _gla_local(scores_f, fwd.k_tail, vf_ref)
        intra_b, upd_b = _gla_local(scores_b, bwd.k_tail, vb_ref)
        starts_f = _gla_states(sf_ref, fwd.decay, upd_f, False)
        starts_b = _gla_states(sb_ref, bwd.decay, upd_b, True)
        _gla_finish(of_ref, fwd.q_state, starts_f, intra_f)
        _gla_finish(ob_ref, bwd.q_state, starts_b, intra_b)

    in_fast_range = -jnp.min(jnp.concatenate(fwd.totals + bwd.totals, axis=0)) < GLA_FAST_RANGE

    @pl.when(in_fast_range)
    def _():
        tile_body(_gla_scores_fast)

    @pl.when(jnp.logical_not(in_fast_range))
    def _():
        tile_body(_gla_scores_safe)


GLA_SUB_TILE = 256


def _gla_kernel(qf_ref, kf_ref, vf_ref, gf_ref, qb_ref, kb_ref, vb_ref, gb_ref, s0f_ref, s0b_ref,
                of_ref, ob_ref, sf_out_ref, sb_out_ref, sf_ref, sb_ref):
    i = pl.program_id(1)

    @pl.when(i == 0)
    def _():
        sf_ref[...] = s0f_ref[...]
        sb_ref[...] = s0b_ref[...]

    t = qf_ref.shape[0]
    sub = min(t, GLA_SUB_TILE)
    n_sub = t // sub
    for s in range(n_sub):
        f_rows = pl.ds(s * sub, sub)
        b_rows = pl.ds((n_sub - 1 - s) * sub, sub)
        _gla_sub_tile(qf_ref.at[f_rows], kf_ref.at[f_rows], vf_ref.at[f_rows], gf_ref.at[f_rows],
                      qb_ref.at[b_rows], kb_ref.at[b_rows], vb_ref.at[b_rows], gb_ref.at[b_rows],
                      of_ref.at[f_rows], ob_ref.at[b_rows], sf_ref, sb_ref)

    @pl.when(i == pl.num_programs(1) - 1)
    def _():
        sf_out_ref[...] = sf_ref[...]
        sb_out_ref[...] = sb_ref[...]


GLA_STATE_SHAPE = (B_HEADS // 2, B_DV, 2 * B_DK)


def _gla(q, k, v, gf, gb, s0f, s0b, *, tile):
    bsz, seq, _ = q.shape
    n = seq // tile
    fwd = lambda w: pl.BlockSpec((None, tile, w), lambda b, i: (b, i, 0))
    bwd = lambda w: pl.BlockSpec((None, tile, w), lambda b, i: (b, n - 1 - i, 0))
    st = pl.BlockSpec((None,) + GLA_STATE_SHAPE, lambda b, i: (b, 0, 0, 0))
    wk, wv = B_HEADS * B_DK, B_HEADS * B_DV
    st_shape = jax.ShapeDtypeStruct((bsz,) + GLA_STATE_SHAPE, F32)
    return pl.pallas_call(
        _gla_kernel,
        grid=(bsz, n),
        in_specs=[fwd(wk), fwd(wk), fwd(wv), fwd(wk), bwd(wk), bwd(wk), bwd(wv), bwd(wk), st, st],
        out_specs=[fwd(wv), bwd(wv), st, st],
        out_shape=[jax.ShapeDtypeStruct((bsz, seq, wv), F32), jax.ShapeDtypeStruct((bsz, seq, wv), F32),
                   st_shape, st_shape],
        scratch_shapes=[pltpu.VMEM(GLA_STATE_SHAPE, F32), pltpu.VMEM(GLA_STATE_SHAPE, F32)],
        compiler_params=_params(("parallel", "arbitrary")),
        name="gla_scan",
    )(q, k, v, gf, q, k, v, gb, s0f, s0b)


def _softmax_parts(scores, sink_tile):
    def lane_tiles(blocks):
        return [b[:, j:j + LANES] for b in blocks for j in range(0, b.shape[1], LANES)]

    tiles = lane_tiles(scores)
    if sink_tile is not None:
        tiles.append(sink_tile)
    m = functools.reduce(jnp.maximum, tiles).max(axis=-1, keepdims=True)
    ps = [jnp.exp2(s - m) for s in scores]
    acc = functools.reduce(jnp.add, lane_tiles(ps))
    if sink_tile is not None:
        lane = lax.broadcasted_iota(jnp.int32, sink_tile.shape, 1)
        acc = acc + jnp.where(lane == 0, jnp.exp2(sink_tile - m), 0.0)
    return ps, acc.sum(axis=-1, keepdims=True)


def _split_head_pair(t):
    lane = lax.broadcasted_iota(jnp.int32, t.shape, 1)
    zero = jnp.zeros_like(t)
    return jnp.concatenate([jnp.where(lane < HEAD_DIM, t, zero), jnp.where(lane >= HEAD_DIM, t, zero)], axis=0)


def _merge_head_pair(o):
    m = o.shape[0] // 2
    lane = lax.broadcasted_iota(jnp.int32, (m, LANES), 1)
    return jnp.where(lane < HEAD_DIM, o[:m], o[m:])


A_HEAD_ORDER = (0, 4, 1, 5, 2, 6, 3, 7)


def _win_block(q2, sink_tile, keys, values, prev_mask, next_mask):
    k_p, k_c, k_n, k_x = keys
    v_p, v_c, v_n, v_x = values
    s_p = _dot_nt(q2, k_p) + jnp.tile(prev_mask, (A_HEADS, 1))
    s_c = _dot_nt(q2, k_c)
    s_n = _dot_nt(q2, k_n) + jnp.tile(next_mask, (A_HEADS, 1))
    s_x = _dot_nt(q2, k_x)
    (p_p, p_c, p_n, p_x), denom = _softmax_parts([s_p, s_c, s_n, s_x], sink_tile)
    o = (_dot(p_p.astype(BF16), v_p) + _dot(p_c.astype(BF16), v_c)
         + _dot(p_n.astype(BF16), v_n) + _dot(p_x.astype(BF16), v_x))
    o = o / denom
    return jnp.concatenate([_merge_head_pair(o[j * 2 * A_BLOCK:(j + 1) * 2 * A_BLOCK])
                            for j in range(A_HEADS // 2)], axis=1)


WIN_BLOCKS_PER_STEP = 8


def _win_kernel(sink_ref, q_ref, kp_ref, kc_ref, kn_ref, vp_ref, vc_ref, vn_ref, kx_ref, vx_ref, o_ref):
    i = pl.program_id(1)
    n_steps = pl.num_programs(1)
    nbs = WIN_BLOCKS_PER_STEP
    sink_tile = jnp.concatenate([jnp.full((A_BLOCK, LANES), sink_ref[h], F32) for h in A_HEAD_ORDER], axis=0)
    qi = lax.broadcasted_iota(jnp.int32, (A_BLOCK, A_BLOCK), 0)
    kj = lax.broadcasted_iota(jnp.int32, (A_BLOCK, A_BLOCK), 1)
    neg = jnp.full((A_BLOCK, A_BLOCK), -jnp.inf, F32)
    zero = jnp.zeros((A_BLOCK, A_BLOCK), F32)
    prev_band = jnp.where(kj >= qi, zero, neg)
    next_band = jnp.where(kj <= qi, zero, neg)
    k_x, v_x = kx_ref[...], vx_ref[...]
    for a in range(nbs):
        rows = slice(a * A_BLOCK, (a + 1) * A_BLOCK)
        before = slice((a - 1) * A_BLOCK, a * A_BLOCK)
        after = slice((a + 1) * A_BLOCK, (a + 2) * A_BLOCK)
        prev_mask = prev_band if a > 0 else jnp.where(i > 0, prev_band, neg)
        next_mask = next_band if a < nbs - 1 else jnp.where(i < n_steps - 1, next_band, neg)
        keys = (kc_ref[before, :] if a > 0 else kp_ref[...], kc_ref[rows, :],
                kc_ref[after, :] if a < nbs - 1 else kn_ref[...], k_x)
        values = (vc_ref[before, :] if a > 0 else vp_ref[...], vc_ref[rows, :],
                  vc_ref[after, :] if a < nbs - 1 else vn_ref[...], v_x)
        q2 = jnp.concatenate([_split_head_pair(q_ref[rows, j * LANES:(j + 1) * LANES])
                              for j in range(A_HEADS // 2)], axis=0)
        o_ref[rows, :] = _win_block(q2, sink_tile, keys, values, prev_mask, next_mask).astype(o_ref.dtype)


def _window_attention(sink, q, k, v, k_ctx, v_ctx):
    bsz, seq, wq = q.shape
    nbs = WIN_BLOCKS_PER_STEP
    nb = seq // A_BLOCK
    wkv = k.shape[-1]
    n_ctx = k_ctx.shape[1]
    prev = pl.BlockSpec((None, A_BLOCK, wkv), lambda b, i: (b, jnp.maximum(nbs * i - 1, 0), 0))
    cur = pl.BlockSpec((None, nbs * A_BLOCK, wkv), lambda b, i: (b, i, 0))
    nxt = pl.BlockSpec((None, A_BLOCK, wkv), lambda b, i: (b, jnp.minimum(nbs * i + nbs, nb - 1), 0))
    ctx_spec = pl.BlockSpec((None, n_ctx, wkv), lambda b, i: (b, 0, 0))
    q_spec = pl.BlockSpec((None, nbs * A_BLOCK, wq), lambda b, i: (b, i, 0))
    return pl.pallas_call(
        _win_kernel,
        grid=(bsz, nb // nbs),
        in_specs=[pl.BlockSpec(memory_space=pltpu.SMEM), q_spec, prev, cur, nxt, prev, cur, nxt,
                  ctx_spec, ctx_spec],
        out_specs=q_spec,
        out_shape=jax.ShapeDtypeStruct((bsz, seq, wq), BF16),
        compiler_params=_params(("parallel", "arbitrary")),
        name="window_attn",
    )(sink, q, k, k, k, v, v, v, k_ctx, v_ctx)


def _dense_kernel(sink_ref, q_ref, k_ref, v_ref, o_ref, *, head_order, shared_kv):
    n_q = q_ref.shape[0]
    for j in range(q_ref.shape[1] // LANES):
        qs = slice(j * LANES, (j + 1) * LANES)
        ks = slice(0, LANES) if shared_kv else qs
        s = _dot_nt(_split_head_pair(q_ref[:, qs]), k_ref[:, ks])
        sink_tile = None
        if head_order is not None:
            sink_tile = jnp.concatenate(
                [jnp.full((n_q, LANES), sink_ref[head_order[2 * j + half]], F32) for half in range(2)], axis=0)
        (p,), denom = _softmax_parts([s], sink_tile)
        o_ref[:, qs] = _merge_head_pair(_dot(p.astype(BF16), v_ref[:, ks]) / denom).astype(o_ref.dtype)


def _dense_attention(sink, q, k, v, *, head_order, shared_kv):
    bsz, n_q, wq = q.shape
    wkv = k.shape[-1]
    full = lambda w: pl.BlockSpec((None, n_q, w), lambda b: (b, 0, 0))
    return pl.pallas_call(
        functools.partial(_dense_kernel, head_order=head_order, shared_kv=shared_kv),
        grid=(bsz,),
        in_specs=[pl.BlockSpec(memory_space=pltpu.SMEM), full(wq), full(wkv), full(wkv)],
        out_specs=full(wq),
        out_shape=jax.ShapeDtypeStruct((bsz, n_q, wq), BF16),
        compiler_params=_params(("parallel",)),
        name="ctx_dense_attn_sink" if shared_kv else "ctx_dense_attn",
    )(sink, q, k, v)


NA_ROWS_PER_STEP = 16


N_NA_INPUTS = 6


def _na_kernel(*refs):
    q_ref, k_ref, v_ref, kx_ref, vx_ref, bias_ref = refs[:N_NA_INPUTS]
    n_side = (len(refs) - N_NA_INPUTS - 1) // 2
    o_ref = refs[N_NA_INPUTS + n_side]
    _run_side_casts(refs[N_NA_INPUTS:N_NA_INPUTS + n_side], refs[N_NA_INPUTS + n_side + 1:])
    step = pl.program_id(1)
    grid_rows = k_ref.shape[0] // GRID_W
    n_keys = C_WIN_ROWS * GRID_W
    n_pairs = C_HEADS // 2
    for rr in range(NA_ROWS_PER_STEP):
        r = step * NA_ROWS_PER_STEP + rr
        row0 = jnp.clip(r - C_WIN_ROWS // 2, 0, grid_rows - C_WIN_ROWS)
        d_row0 = row0 - r + (C_WIN_ROWS - 1)
        key0 = pl.multiple_of(row0 * GRID_W, GRID_W)
        qrows = slice(rr * GRID_W, (rr + 1) * GRID_W)
        s_l, s_x = [], []
        for p in range(n_pairs):
            ps = slice(p * LANES, (p + 1) * LANES)
            q2 = _split_head_pair(q_ref[qrows, ps])
            s_l.append(_dot_nt(q2, k_ref[pl.ds(key0, n_keys), ps]))
            s_x.append(_dot_nt(q2, kx_ref[:, ps]))
        bias = jnp.concatenate(
            [bias_ref[d_row0 + 2 * j].reshape(C_HEADS * GRID_W, LANES) for j in range(C_WIN_ROWS // 2)], axis=1)
        (p_l, p_x), denom = _softmax_parts([jnp.concatenate(s_l, axis=0) + bias, jnp.concatenate(s_x, axis=0)], None)
        p_l = p_l.astype(BF16)
        p_x = p_x.astype(BF16)
        for p in range(n_pairs):
            ps = slice(p * LANES, (p + 1) * LANES)
            pr = slice(p * 2 * GRID_W, (p + 1) * 2 * GRID_W)
            o = (_dot(p_l[pr], v_ref[pl.ds(key0, n_keys), ps]) + _dot(p_x[pr], vx_ref[:, ps])) / denom[pr]
            o_ref[qrows, ps] = _merge_head_pair(o).astype(o_ref.dtype)


def _neighbourhood_attention(q, k, v, k_ctx, v_ctx, bias_tbl, cast_weights=(), layer=0):
    bsz, seq, w = q.shape
    n_ctx = k_ctx.shape[1]
    tq = NA_ROWS_PER_STEP * GRID_W
    n_tiles = seq // tq
    whole = pl.BlockSpec((None, seq, w), lambda b, i: (b, 0, 0))
    ctx_spec = pl.BlockSpec((None, n_ctx, w), lambda b, i: (b, 0, 0))
    side = _SideCasts(cast_weights, layer, bsz * n_tiles, lambda b, i: b * n_tiles + i)
    outs = pl.pallas_call(
        _na_kernel,
        grid=(bsz, n_tiles),
        in_specs=[pl.BlockSpec((None, tq, w), lambda b, i: (b, i, 0)), whole, whole, ctx_spec, ctx_spec,
                  _const_spec(bias_tbl.shape)] + side.specs,
        out_specs=[pl.BlockSpec((None, tq, w), lambda b, i: (b, i, 0))] + side.out_specs,
        out_shape=[jax.ShapeDtypeStruct((bsz, seq, w), BF16)] + side.out_shapes,
        compiler_params=_params(("parallel", "arbitrary")),
        name="neighbourhood_attn",
    )(q, k, v, k_ctx, v_ctx, bias_tbl, *side.arrays)
    return outs[0], side.finish(outs[1:])


def _na_bias_kernel(vec_ref, o_ref):
    q = lax.broadcasted_iota(jnp.int32, (GRID_W, LANES), 0)
    c = lax.broadcasted_iota(jnp.int32, (GRID_W, LANES), 1)
    k = c % GRID_W
    wstart = jnp.clip(q - C_WIN_COLS // 2, 0, GRID_W - C_WIN_COLS)
    ok = (k >= wstart) & (k < wstart + C_WIN_COLS)
    n_dr = vec_ref.shape[1]
    for h in range(vec_ref.shape[0]):
        toep = [pltpu.roll(jnp.broadcast_to(vec_ref[h, dr:dr + 1, :], (GRID_W, LANES)), 0, 1,
                           stride=1, stride_axis=0) for dr in range(n_dr)]
        for dr in range(n_dr - 1):
            pair = jnp.where(c < GRID_W, toep[dr], pltpu.roll(toep[dr + 1], GRID_W, 1))
            o_ref[dr, h] = jnp.where(ok, pair * LOG2E, -jnp.inf)


def _na_bias_table(rpb):
    n_off = C_WIN_COLS - 1
    vec = jnp.concatenate([rpb[..., n_off:].astype(F32),
                           jnp.zeros(rpb.shape[:2] + (LANES - 2 * n_off - 1,), F32),
                           rpb[..., :n_off].astype(F32)], axis=-1)
    return pl.pallas_call(
        _na_bias_kernel,
        out_shape=jax.ShapeDtypeStruct((rpb.shape[1] - 1, rpb.shape[0], GRID_W, LANES), F32),
        name="na_bias_table",
    )(vec)


def _merge_kernel(x_ref, mod_ref, gain_ref, ya_ref, of_ref, ob_ref, gr_ref, yc_ref, gn_ref,
                  wm_ref, bm_ref, wa_ref, wb_ref, wc_ref, wo_ref, o_ref):
    d = D_MODEL
    gn = gn_ref[...]
    x = x_ref[...]
    h = _norm_mod(x, gain_ref[...], mod_ref[0:1, :], mod_ref[1:2, :]).astype(BF16)
    o_sum = of_ref[...] + ob_ref[...]
    parts = []
    for hh in range(B_HEADS):
        oh = o_sum[:, hh * B_DV:(hh + 1) * B_DV]
        parts.append((oh * lax.rsqrt(jnp.mean(oh * oh, axis=-1, keepdims=True) + EPS)) * gn)
    y_b = (jnp.concatenate(parts, axis=1) * _silu(gr_ref[...])).astype(BF16)
    mixed = None
    for j, (y, w_ref) in enumerate(((ya_ref[...], wa_ref), (y_b, wb_ref), (yc_ref[...], wc_ref))):
        gate = _sigmoid(_dot(h, wm_ref[:, j * d:(j + 1) * d]) + bm_ref[:, j * d:(j + 1) * d])
        term = gate * _dot(y, w_ref[...])
        mixed = term if mixed is None else mixed + term
    o_ref[...] = x + mod_ref[2:3, :] * _dot(mixed.astype(BF16), wo_ref[...])


def _merge(x, mod_rows, gain, y_a, o_f, o_b, g_r, y_c, gla_gain, w_merge, b_merge, w_a, w_b, w_c, w_out, *, tm):
    bsz, seq, d = x.shape
    per_batch_mod = mod_rows.shape[0] > 1
    mod_map = (lambda b, i: (b, 0, 0)) if per_batch_mod else (lambda b, i: (0, 0, 0))
    tile = lambda w: pl.BlockSpec((None, tm, w), lambda b, i: (b, i, 0))
    wy = y_a.shape[-1]
    return pl.pallas_call(
        _merge_kernel,
        grid=(bsz, seq // tm),
        in_specs=[tile(d), pl.BlockSpec((None, 6, d), mod_map), _const_spec((1, d)),
                  tile(wy), tile(wy), tile(wy), tile(wy), tile(wy), _const_spec((1, B_DV)),
                  _const_spec(w_merge.shape), _const_spec(b_merge.shape),
                  _const_spec(w_a.shape), _const_spec(w_b.shape), _const_spec(w_c.shape),
                  _const_spec(w_out.shape)],
        out_specs=tile(d),
        out_shape=jax.ShapeDtypeStruct((bsz, seq, d), F32),
        compiler_params=_params(("parallel", "arbitrary")),
        name="merge_out",
    )(x, mod_rows, gain, y_a, o_f, o_b, g_r, y_c, gla_gain, w_merge, b_merge, w_a, w_b, w_c, w_out)


FFN_CHUNKS = ((0, 1024), (1024, 2048), (2048, FFN_HIDDEN))
FFN_SUB_ROWS = 256


def _ffn_kernel(x_ref, mod_ref, gain_ref, w1_ref, w2_ref, fg_ref, o_ref, *, final):
    for r0 in range(0, x_ref.shape[0], FFN_SUB_ROWS):
        rows = slice(r0, r0 + FFN_SUB_ROWS)
        x = x_ref[rows, :]
        h = _norm_mod(x, gain_ref[...], mod_ref[3:4, :], mod_ref[4:5, :]).astype(BF16)
        acc = None
        for c0, c1 in FFN_CHUNKS:
            gate = _dot(h, w1_ref[:, c0:c1])
            up = _dot(h, w1_ref[:, FFN_HIDDEN + c0:FFN_HIDDEN + c1])
            part = _dot((_silu(gate) * up).astype(BF16), w2_ref[c0:c1, :])
            acc = part if acc is None else acc + part
        y = x + mod_ref[5:6, :] * acc
        if final:
            y = (y * lax.rsqrt(jnp.mean(y * y, axis=-1, keepdims=True) + EPS)) * fg_ref[...]
        o_ref[rows, :] = y


def _ffn(x, mod_rows, gain, w1, w2, final_gain, *, final, tm):
    bsz, seq, d = x.shape
    per_batch_mod = mod_rows.shape[0] > 1
    mod_map = (lambda b, i: (b, 0, 0)) if per_batch_mod else (lambda b, i: (0, 0, 0))
    tile = pl.BlockSpec((None, tm, d), lambda b, i: (b, i, 0))
    return pl.pallas_call(
        functools.partial(_ffn_kernel, final=final),
        grid=(bsz, seq // tm),
        in_specs=[tile, pl.BlockSpec((None, 6, d), mod_map), _const_spec((1, d)),
                  _const_spec(w1.shape), _const_spec(w2.shape), _const_spec((1, d))],
        out_specs=tile,
        out_shape=jax.ShapeDtypeStruct((bsz, seq, d), F32),
        compiler_params=_params(("parallel", "arbitrary")),
        name="ffn_final" if final else "ffn",
    )(x, mod_rows, gain, w1, w2, final_gain)


def _rope_tables(seq):
    t = jnp.arange(seq)
    row = (t // GRID_W).astype(F32)
    col = (t % GRID_W).astype(F32)
    n_freq = HEAD_DIM // 4
    inv = ROPE_BASE ** (-jnp.arange(n_freq, dtype=F32) / n_freq)
    ang = jnp.concatenate([row[:, None] * inv[None], col[:, None] * inv[None]], axis=-1)
    cos, sin = jnp.cos(ang), jnp.sin(ang)
    return jnp.tile(cos, (1, 4)), jnp.tile(jnp.concatenate([-sin, sin], axis=-1), (1, 2))


def _permute_w_branch_a(w):
    return w.reshape(A_HEADS, HEAD_DIM, w.shape[1])[jnp.array(A_HEAD_ORDER)].reshape(w.shape).astype(BF16)


def _gate_weights(w_fwd, b_fwd, w_bwd, b_bwd):
    n = B_HEADS * B_DK
    r = B_GATE_RANK
    w = jnp.zeros((LANES, 2 * n), F32)
    w = w.at[:r, :n].set(w_fwd).at[r:2 * r, n:].set(w_bwd)
    return w.astype(BF16), jnp.concatenate([b_fwd, b_bwd])[None, :]


def kernel(x, c, ctx, c_ctx, w_ada, b_ada, norm_mix, w_in, attn_sink, gla_gate_w_fwd, gla_gate_b_fwd,
           gla_gate_w_bwd, gla_gate_b_bwd, gla_norm, na_rpb, w_branch_a, w_branch_b, w_branch_c,
           w_merge, b_merge, w_out, norm_ffn, w_ffn_in, w_ffn_out, final_norm):
    bsz, seq, d = x.shape
    n_ctx = ctx.shape[1]
    cos, sin = _rope_tables(seq)

    cvec = jnp.zeros((8, d), F32).at[:bsz].set(c).at[bsz].set(c_ctx)
    mod, w_perm_all = _modulation(cvec, w_ada, b_ada, w_in)
    mod = mod.reshape(DEPTH, 8, 6, d)

    xc = ctx
    zeros_state = jnp.zeros((bsz,) + GLA_STATE_SHAPE, F32)
    for l in range(DEPTH):
        last = l == DEPTH - 1
        mod_x = mod[l, :bsz]
        mod_c = mod[l, bsz:bsz + 1]
        gain_m = norm_mix[l][None, :]
        w_perm = w_perm_all
        w_gate, b_gate = _gate_weights(gla_gate_w_fwd[l], gla_gate_b_fwd[l], gla_gate_w_bwd[l], gla_gate_b_bwd[l])

        (caq, cak, cav, cgq, cgk, cgv, cgr, cnq, cnk, cnv, cgf, cgb), _ = _project(
            xc, mod_c, gain_m, w_perm, w_gate, b_gate, cos, sin, rope=False, tm=n_ctx, layer=l)
        (aq, ak, av, gq, gk, gv, gr, nq, nk, nv, gf, gb), (wm_bf, wb_bf, wc_bf, wo_bf) = _project(
            x, mod_x, gain_m, w_perm, w_gate, b_gate, cos, sin, rope=True, tm=512,
            cast_weights=(w_merge, w_branch_b, w_branch_c, w_out), layer=l)

        co_f, co_b, s_f, s_b = _gla(cgq, cgk, cgv, cgf, cgb, zeros_state, zeros_state, tile=n_ctx)
        o_f, o_b, _, _ = _gla(gq, gk, gv, gf, gb, s_f, s_b, tile=1024)

        sink = attn_sink[l] * LOG2E
        y_a = _window_attention(sink, aq, ak, av, cak, cav)
        y_c, (w1, w2) = _neighbourhood_attention(nq, nk, nv, cnk, cnv, _na_bias_table(na_rpb[l]),
                                                 cast_weights=(w_ffn_in, w_ffn_out), layer=l)

        merge_w = (gla_norm[l][None, :], wm_bf, b_merge[l][None, :],
                   _permute_w_branch_a(w_branch_a[l]), wb_bf, wc_bf, wo_bf)
        x = _merge(x, mod_x, gain_m, y_a, o_f, o_b, gr, y_c, *merge_w, tm=512)

        gain_f = norm_ffn[l][None, :]
        fg = final_norm[None, :]
        if not last:
            yc_a = _dense_attention(sink, caq, cak, cav, head_order=A_HEAD_ORDER, shared_kv=True)
            yc_c = _dense_attention(sink, cnq, cnk, cnv, head_order=None, shared_kv=False)
            xc = _merge(xc, mod_c, gain_m, yc_a, co_f, co_b, cgr, yc_c, *merge_w, tm=n_ctx)
            xc = _ffn(xc, mod_c, gain_f, w1, w2, fg, final=False, tm=n_ctx)
        x = _ffn(x, mod_x, gain_f, w1, w2, fg, final=last, tm=1024)
    return x
```

```python
import functools

import jax
import jax.numpy as jnp
from jax import lax
from jax.experimental import pallas as pl
from jax.experimental.pallas import tpu as pltpu

F32 = jnp.float32
BF16 = jnp.bfloat16

D_MODEL = 1024
DEPTH = 2
GRID_W = 64
HEAD_DIM = 64
EPS = 1e-6
ROPE_BASE = 10000.0
A_HEADS = 8
A_KV_HEADS = 2
A_BLOCK = 128
B_HEADS = 4
B_DK = 64
B_DV = 128
B_GATE_RANK = 16
B_GATE_NORM = 16.0
C_HEADS = 8
C_WIN_ROWS = 8
C_WIN_COLS = 16
FFN_HIDDEN = 2816

VMEM_LIMIT_BYTES = 56 * 1024 * 1024
LANES = 128

SEG_AQ = (0, 512)
SEG_AKV = (512, 768)
SEG_GQ = (768, 1024)
SEG_GK = (1024, 1280)
SEG_GV = (1280, 1792)
SEG_GR = (1792, 2304)
SEG_NQ = (2304, 2816)
SEG_NK = (2816, 3328)
SEG_NV = (3328, 3840)
SEG_GA = (3840, 3968)
IN_PADDED = 3968
W_IN_GATE_COLS = (2304, 2304 + 2 * B_GATE_RANK)

LOG2E = 1.4426950408889634
ATTN_Q_SCALE = HEAD_DIM ** -0.5 * LOG2E

GLA_CHUNK = 128
GLA_FAST_RANGE = 120.0


def _params(sem):
    return pltpu.CompilerParams(dimension_semantics=sem, vmem_limit_bytes=VMEM_LIMIT_BYTES)


def _const_spec(shape):
    nd = len(shape)
    return pl.BlockSpec(shape, lambda *_: (0,) * nd, pipeline_mode=pl.Buffered(1))


def _sigmoid(x):
    return 1.0 / (1.0 + jnp.exp(-x))


def _silu(x):
    return x * _sigmoid(x)


def _dot(a, b):
    return jnp.dot(a, b, preferred_element_type=F32)


def _dot_nt(a, b):
    return lax.dot_general(a, b, (((1,), (1,)), ((), ())), preferred_element_type=F32)


def _dot_tn(a, b):
    return lax.dot_general(a, b, (((0,), (0,)), ((), ())), preferred_element_type=F32)


def _norm_mod(x, gain, shift, scale):
    y = x * lax.rsqrt(jnp.mean(x * x, axis=-1, keepdims=True) + EPS)
    return (y * gain) * (1.0 + scale) + shift


def _relayout_w_in(wt_ref, o_ref):
    n_aq = SEG_AQ[1]
    ga0, ga1 = W_IN_GATE_COLS
    pieces = [wt_ref[h * HEAD_DIM:(h + 1) * HEAD_DIM, :] for h in A_HEAD_ORDER]
    pieces += [wt_ref[n_aq:ga0, :], wt_ref[ga1:, :], wt_ref[ga0:ga1, :],
               jnp.zeros((IN_PADDED - wt_ref.shape[0], wt_ref.shape[1]), F32)]
    o_ref[...] = jnp.concatenate(pieces, axis=0).T.astype(BF16)


def _mod_kernel(c_ref, w_ref, b_ref, w_in_ref, o_ref, w_perm_ref):
    s = _silu(c_ref[...])
    o_ref[...] = _dot(s.astype(BF16), w_ref[...].astype(BF16)) + b_ref[...]
    _relayout_w_in(w_in_ref, w_perm_ref)


def _modulation(cvec, w_ada, b_ada, w_in):
    tn = 1536
    n_out = w_ada.shape[-1]
    n_j = n_out // tn
    rows = w_in.shape[1] // n_j
    w_in_t = jnp.swapaxes(w_in, 1, 2)
    return pl.pallas_call(
        _mod_kernel,
        grid=(DEPTH, n_j),
        in_specs=[
            pl.BlockSpec((8, D_MODEL), lambda l, j: (0, 0)),
            pl.BlockSpec((None, D_MODEL, tn), lambda l, j: (l, 0, j)),
            pl.BlockSpec((None, 1, tn), lambda l, j: (l, 0, j)),
            pl.BlockSpec((None, w_in_t.shape[1], rows), lambda l, j: (l, 0, j)),
        ],
        out_specs=[pl.BlockSpec((None, 8, tn), lambda l, j: (l, 0, j)),
                   pl.BlockSpec((None, rows, IN_PADDED), lambda l, j: (l, j, 0))],
        out_shape=[jax.ShapeDtypeStruct((DEPTH, 8, n_out), F32),
                   jax.ShapeDtypeStruct((DEPTH, w_in.shape[1], IN_PADDED), BF16)],
        compiler_params=_params(("arbitrary", "arbitrary")),
        name="adaln_mod",
    )(cvec, w_ada, b_ada.reshape(DEPTH, 1, n_out), w_in_t)


def _rope(t, cos, sin):
    n = t.shape[1]
    lane = lax.broadcasted_iota(jnp.int32, t.shape, 1)
    first_half = (lane % HEAD_DIM) < (HEAD_DIM // 2)
    rot = jnp.where(first_half, pltpu.roll(t, n - HEAD_DIM // 2, 1), pltpu.roll(t, HEAD_DIM // 2, 1))
    reps = n // LANES
    return t * jnp.tile(cos, (1, reps)) + rot * jnp.tile(sin, (1, reps))


class _SideCasts:
    def __init__(self, stacked_weights, layer, n_steps, step_index):
        self.shapes2d = [w.shape[1:] for w in stacked_weights]
        self.arrays = [w.reshape(w.shape[0], n_steps, w.shape[1] // n_steps, w.shape[2]) for w in stacked_weights]
        self.specs = [pl.BlockSpec((None, None) + a.shape[2:], lambda *g: (layer, step_index(*g), 0, 0))
                      for a in self.arrays]
        self.out_specs = [pl.BlockSpec((None,) + a.shape[2:], lambda *g: (step_index(*g), 0, 0))
                          for a in self.arrays]
        self.out_shapes = [jax.ShapeDtypeStruct(a.shape[1:], BF16) for a in self.arrays]

    def finish(self, outs):
        return [o.reshape(s) for o, s in zip(outs, self.shapes2d)]


def _run_side_casts(in_refs, out_refs):
    for w_ref, o_ref in zip(in_refs, out_refs):
        o_ref[...] = w_ref[...].astype(BF16)


N_PROJ_INPUTS = 8
N_PROJ_OUTPUTS = 12


def _proj_kernel(*refs, rope):
    (x_ref, mod_ref, gain_ref, w_ref, wg_ref, bg_ref, cos_ref, sin_ref) = refs[:N_PROJ_INPUTS]
    n_side = (len(refs) - N_PROJ_INPUTS - N_PROJ_OUTPUTS) // 2
    outs = refs[N_PROJ_INPUTS + n_side:]
    (aq_ref, ak_ref, av_ref, gq_ref, gk_ref, gv_ref, gr_ref, nq_ref, nk_ref, nv_ref,
     gf_ref, gb_ref) = outs[:N_PROJ_OUTPUTS]
    _run_side_casts(refs[N_PROJ_INPUTS:N_PROJ_INPUTS + n_side], outs[N_PROJ_OUTPUTS:])
    scale = ATTN_Q_SCALE
    n_g = B_HEADS * B_DK
    h = _norm_mod(x_ref[...], gain_ref[...], mod_ref[0:1, :], mod_ref[1:2, :]).astype(BF16)

    def seg(s):
        return _dot(h, w_ref[:, s[0]:s[1]])

    aq = seg(SEG_AQ)
    akv = seg(SEG_AKV)
    z = _dot(seg(SEG_GA).astype(BF16), wg_ref[...]) + bg_ref[...]
    ak = akv[:, :LANES]
    if rope:
        cos = cos_ref[...]
        sin = sin_ref[...]
        aq = _rope(aq, cos, sin)
        ak = _rope(ak, cos, sin)
    aq_ref[...] = (aq * scale).astype(BF16)
    ak_ref[...] = ak.astype(BF16)
    av_ref[...] = akv[:, LANES:].astype(BF16)
    log_sig = jnp.minimum(z, 0.0) - jnp.log1p(jnp.exp(-jnp.abs(z)))
    g = log_sig / B_GATE_NORM
    gf_ref[...] = g[:, :n_g]
    gb_ref[...] = g[:, n_g:]
    gq_ref[...] = seg(SEG_GQ) * (B_DK ** -0.5)
    gk_ref[...] = seg(SEG_GK)
    gv_ref[...] = seg(SEG_GV).astype(BF16)
    gr_ref[...] = seg(SEG_GR)
    nq_ref[...] = (seg(SEG_NQ) * scale).astype(BF16)
    nk_ref[...] = seg(SEG_NK).astype(BF16)
    nv_ref[...] = seg(SEG_NV).astype(BF16)


def _project(x, mod_rows, gain, w_perm, w_gate, b_gate, cos, sin, *, rope, tm, cast_weights=(), layer=0):
    bsz, seq, _ = x.shape
    n_tiles = seq // tm
    per_batch_mod = mod_rows.shape[0] > 1
    mod_map = (lambda b, i: (b, 0, 0)) if per_batch_mod else (lambda b, i: (0, 0, 0))
    widths = [(512, BF16), (128, BF16), (128, BF16), (256, F32), (256, F32), (512, BF16), (512, F32),
              (512, BF16), (512, BF16), (512, BF16), (256, F32), (256, F32)]
    assert len(widths) == N_PROJ_OUTPUTS
    tile = lambda w: pl.BlockSpec((None, tm, w), lambda b, i: (b, i, 0))
    side = _SideCasts(cast_weights, layer, bsz * n_tiles, lambda b, i: b * n_tiles + i)
    outs = pl.pallas_call(
        functools.partial(_proj_kernel, rope=rope),
        grid=(bsz, n_tiles),
        in_specs=[
            tile(D_MODEL),
            pl.BlockSpec((None, 6, D_MODEL), mod_map),
            _const_spec((1, D_MODEL)),
            pl.BlockSpec((None, D_MODEL, IN_PADDED), lambda b, i: (layer, 0, 0), pipeline_mode=pl.Buffered(1)),
            _const_spec((LANES, 2 * B_HEADS * B_DK)),
            _const_spec((1, 2 * B_HEADS * B_DK)),
            pl.BlockSpec((tm, LANES), lambda b, i: (i, 0)),
            pl.BlockSpec((tm, LANES), lambda b, i: (i, 0)),
        ] + side.specs,
        out_specs=[tile(w) for w, _ in widths] + side.out_specs,
        out_shape=[jax.ShapeDtypeStruct((bsz, seq, w), dt) for w, dt in widths] + side.out_shapes,
        compiler_params=_params(("parallel", "arbitrary")),
        name="in_proj_rope" if rope else "in_proj_ctx",
    )(x, mod_rows, gain, w_perm, w_gate, b_gate, cos, sin, *side.arrays)
    return outs[:N_PROJ_OUTPUTS], side.finish(outs[N_PROJ_OUTPUTS:])


class _GlaPrep:
    def __init__(self, q_ref, k_ref, g_ref, reverse):
        t = q_ref.shape[0]
        c = GLA_CHUNK
        row = lax.broadcasted_iota(jnp.int32, (t, t), 0)
        col = lax.broadcasted_iota(jnp.int32, (t, t), 1)
        same_chunk = (row // c) == (col // c)
        tri = (same_chunk & ((col >= row) if reverse else (col <= row))).astype(BF16)
        self.reverse = reverse
        self.q = q_ref[...]
        self.k = k_ref[...]
        self.g_parts = _split3(g_ref[...])
        self.b = _dot_01(tri, self.g_parts)
        b = self.b
        self.totals = [b[ci * c:ci * c + 1, :] if reverse else b[ci * c + c - 1:ci * c + c, :]
                       for ci in range(t // c)]
        self.b_end = jnp.concatenate([jnp.broadcast_to(e, (c, e.shape[1])) for e in self.totals], axis=0)
        self.q_state = (self.q * jnp.exp(b)).astype(BF16)
        self.k_tail = (self.k * jnp.exp(self.b_end - b)).astype(BF16)
        self.decay = [jnp.exp(e) for e in self.totals]


def _split3(x):
    hi = x.astype(BF16)
    rest = x - hi.astype(F32)
    mid = rest.astype(BF16)
    return hi, mid, (rest - mid.astype(F32)).astype(BF16)


def _dot_01(m01, parts):
    return _dot(m01, parts[0]) + _dot(m01, parts[1]) + _dot(m01, parts[2])


def _gla_pair_scores(q_w, k_w, keep):
    c = GLA_CHUNK
    out = []
    for ci in range(q_w.shape[0] // c):
        rows = slice(ci * c, (ci + 1) * c)
        for p in range(B_HEADS // 2):
            ps = slice(p * LANES, (p + 1) * LANES)
            out.append(jnp.where(keep, _dot_nt(_split_head_pair(q_w[rows, ps]), k_w[rows, ps]), 0.0))
    return out


def _gla_scores_fast(prep):
    c = GLA_CHUNK
    qi = lax.broadcasted_iota(jnp.int32, (2 * c, c), 0) % c
    kj = lax.broadcasted_iota(jnp.int32, (2 * c, c), 1)
    keep = (kj >= qi) if prep.reverse else (kj <= qi)
    ref = 0.5 * prep.b_end
    q_in = (prep.q * jnp.exp(prep.b - ref)).astype(BF16)
    k_in = (prep.k * jnp.exp(ref - prep.b)).astype(BF16)
    return _gla_pair_scores(q_in, k_in, keep)


def _gla_scores_safe(prep):
    t = prep.q.shape[0]
    c = GLA_CHUNK
    rev = prep.reverse
    row = lax.broadcasted_iota(jnp.int32, (t, t), 0)
    col = lax.broadcasted_iota(jnp.int32, (t, t), 1)
    qi = lax.broadcasted_iota(jnp.int32, (2 * c, c), 0) % c
    kj = lax.broadcasted_iota(jnp.int32, (2 * c, c), 1)
    scores = _gla_pair_scores(prep.q.astype(BF16), prep.k.astype(BF16), qi == kj)
    s = c // 2
    while s >= 1:
        if rev:
            bnd = row - row % (2 * s) + s
            span = ((col >= bnd) & (col < row)) | ((col >= row) & (col < bnd))
        else:
            bnd = row - row % (2 * s) + s - 1
            span = ((col > bnd) & (col <= row)) | ((col > row) & (col <= bnd))
        e = jnp.exp(_dot_01(span.astype(BF16), prep.g_parts))
        q_side = ((qi % (2 * s)) < s) if rev else ((qi % (2 * s)) >= s)
        k_side = ((kj % (2 * s)) >= s) if rev else ((kj % (2 * s)) < s)
        keep = ((qi // (2 * s)) == (kj // (2 * s))) & q_side & k_side
        level = _gla_pair_scores((prep.q * e).astype(BF16), (prep.k * e).astype(BF16), keep)
        scores = [acc + a for acc, a in zip(scores, level)]
        s //= 2
    return scores


def _gla_local(scores, k_tail, v_ref):
    c = GLA_CHUNK
    n_pairs = B_HEADS // 2
    intra, upd = [], []
    for ci in range(k_tail.shape[0] // c):
        rows = slice(ci * c, (ci + 1) * c)
        intra_c, upd_c = [], []
        for p in range(n_pairs):
            a = scores[ci * n_pairs + p].astype(BF16)
            kt = _split_head_pair(k_tail[rows, p * LANES:(p + 1) * LANES])
            u = None
            for half in range(2):
                h = 2 * p + half
                vh = v_ref[rows, h * B_DV:(h + 1) * B_DV]
                intra_c.append(_dot(a[half * c:(half + 1) * c], vh))
                uh = _dot_tn(vh, kt[half * c:(half + 1) * c])
                u = uh if u is None else u + uh
            upd_c.append(u)
        intra.append(intra_c)
        upd.append(upd_c)
    return intra, upd


def _gla_states(s_ref, decay, upd, reverse):
    n = len(upd)
    order = range(n - 1, -1, -1) if reverse else range(n)
    starts = [None] * n
    s = [s_ref[p] for p in range(B_HEADS // 2)]
    for ci in order:
        starts[ci] = [sp.astype(BF16) for sp in s]
        s = [s[p] * decay[ci][:, p * LANES:(p + 1) * LANES] + upd[ci][p] for p in range(B_HEADS // 2)]
    for p in range(B_HEADS // 2):
        s_ref[p] = s[p]
    return starts


def _gla_finish(o_ref, q_state, starts, intra):
    c = GLA_CHUNK
    for ci in range(len(intra)):
        rows = slice(ci * c, (ci + 1) * c)
        for p in range(B_HEADS // 2):
            ps = slice(p * LANES, (p + 1) * LANES)
            inter = _dot_nt(_split_head_pair(q_state[rows, ps]), starts[ci][p])
            for half in range(2):
                h = 2 * p + half
                o_ref[rows, h * B_DV:(h + 1) * B_DV] = intra[ci][h] + inter[half * c:(half + 1) * c]


def _gla_sub_tile(qf_ref, kf_ref, vf_ref, gf_ref, qb_ref, kb_ref, vb_ref, gb_ref, of_ref, ob_ref, sf_ref, sb_ref):
    fwd = _GlaPrep(qf_ref, kf_ref, gf_ref, False)
    bwd = _GlaPrep(qb_ref, kb_ref, gb_ref, True)

    def tile_body(score_fn):
        scores_f = score_fn(fwd)
        scores_b = score_fn(bwd)
        intra_f, upd_f = _gla_local(scores_f, fwd.k_tail, vf_ref)
        intra_b, upd_b = _gla_local(scores_b, bwd.k_tail, vb_ref)
        starts_f = _gla_states(sf_ref, fwd.decay, upd_f, False)
        starts_b = _gla_states(sb_ref, bwd.decay, upd_b, True)
        _gla_finish(of_ref, fwd.q_state, starts_f, intra_f)
        _gla_finish(ob_ref, bwd.q_state, starts_b, intra_b)

    in_fast_range = -jnp.min(jnp.concatenate(fwd.totals + bwd.totals, axis=0)) < GLA_FAST_RANGE

    @pl.when(in_fast_range)
    def _():
        tile_body(_gla_scores_fast)

    @pl.when(jnp.logical_not(in_fast_range))
    def _():
        tile_body(_gla_scores_safe)


GLA_SUB_TILE = 256


def _gla_kernel(qf_ref, kf_ref, vf_ref, gf_ref, qb_ref, kb_ref, vb_ref, gb_ref, s0f_ref, s0b_ref,
                of_ref, ob_ref, sf_out_ref, sb_out_ref, sf_ref, sb_ref):
    i = pl.program_id(1)

    @pl.when(i == 0)
    def _():
        sf_ref[...] = s0f_ref[...]
        sb_ref[...] = s0b_ref[...]

    t = qf_ref.shape[0]
    sub = min(t, GLA_SUB_TILE)
    n_sub = t // sub
    for s in range(n_sub):
        f_rows = pl.ds(s * sub, sub)
        b_rows = pl.ds((n_sub - 1 - s) * sub, sub)
        _gla_sub_tile(qf_ref.at[f_rows], kf_ref.at[f_rows], vf_ref.at[f_rows], gf_ref.at[f_rows],
                      qb_ref.at[b_rows], kb_ref.at[b_rows], vb_ref.at[b_rows], gb_ref.at[b_rows],
                      of_ref.at[f_rows], ob_ref.at[b_rows], sf_ref, sb_ref)

    @pl.when(i == pl.num_programs(1) - 1)
    def _():
        sf_out_ref[...] = sf_ref[...]
        sb_out_ref[...] = sb_ref[...]


GLA_STATE_SHAPE = (B_HEADS // 2, B_DV, 2 * B_DK)


def _gla(q, k, v, gf, gb, s0f, s0b, *, tile):
    bsz, seq, _ = q.shape
    n = seq // tile
    fwd = lambda w: pl.BlockSpec((None, tile, w), lambda b, i: (b, i, 0))
    bwd = lambda w: pl.BlockSpec((None, tile, w), lambda b, i: (b, n - 1 - i, 0))
    st = pl.BlockSpec((None,) + GLA_STATE_SHAPE, lambda b, i: (b, 0, 0, 0))
    wk, wv = B_HEADS * B_DK, B_HEADS * B_DV
    st_shape = jax.ShapeDtypeStruct((bsz,) + GLA_STATE_SHAPE, F32)
    return pl.pallas_call(
        _gla_kernel,
        grid=(bsz, n),
        in_specs=[fwd(wk), fwd(wk), fwd(wv), fwd(wk), bwd(wk), bwd(wk), bwd(wv), bwd(wk), st, st],
        out_specs=[fwd(wv), bwd(wv), st, st],
        out_shape=[jax.ShapeDtypeStruct((bsz, seq, wv), F32), jax.ShapeDtypeStruct((bsz, seq, wv), F32),
                   st_shape, st_shape],
        scratch_shapes=[pltpu.VMEM(GLA_STATE_SHAPE, F32), pltpu.VMEM(GLA_STATE_SHAPE, F32)],
        compiler_params=_params(("parallel", "arbitrary")),
        name="gla_scan",
    )(q, k, v, gf, q, k, v, gb, s0f, s0b)


def _softmax_parts(scores, sink_tile):
    def lane_tiles(blocks):
        return [b[:, j:j + LANES] for b in blocks for j in range(0, b.shape[1], LANES)]

    tiles = lane_tiles(scores)
    if sink_tile is not None:
        tiles.append(sink_tile)
    m = functools.reduce(jnp.maximum, tiles).max(axis=-1, keepdims=True)
    ps = [jnp.exp2(s - m) for s in scores]
    acc = functools.reduce(jnp.add, lane_tiles(ps))
    if sink_tile is not None:
        lane = lax.broadcasted_iota(jnp.int32, sink_tile.shape, 1)
        acc = acc + jnp.where(lane == 0, jnp.exp2(sink_tile - m), 0.0)
    return ps, acc.sum(axis=-1, keepdims=True)


def _split_head_pair(t):
    lane = lax.broadcasted_iota(jnp.int32, t.shape, 1)
    zero = jnp.zeros_like(t)
    return jnp.concatenate([jnp.where(lane < HEAD_DIM, t, zero), jnp.where(lane >= HEAD_DIM, t, zero)], axis=0)


def _merge_head_pair(o):
    m = o.shape[0] // 2
    lane = lax.broadcasted_iota(jnp.int32, (m, LANES), 1)
    return jnp.where(lane < HEAD_DIM, o[:m], o[m:])


A_HEAD_ORDER = (0, 4, 1, 5, 2, 6, 3, 7)


def _win_block(q2, sink_tile, keys, values, prev_mask, next_mask):
    k_p, k_c, k_n, k_x = keys
    v_p, v_c, v_n, v_x = values
    s_p = _dot_nt(q2, k_p) + jnp.tile(prev_mask, (A_HEADS, 1))
    s_c = _dot_nt(q2, k_c)
    s_n = _dot_nt(q2, k_n) + jnp.tile(next_mask, (A_HEADS, 1))
    s_x = _dot_nt(q2, k_x)
    (p_p, p_c, p_n, p_x), denom = _softmax_parts([s_p, s_c, s_n, s_x], sink_tile)
    o = (_dot(p_p.astype(BF16), v_p) + _dot(p_c.astype(BF16), v_c)
         + _dot(p_n.astype(BF16), v_n) + _dot(p_x.astype(BF16), v_x))
    o = o / denom
    return jnp.concatenate([_merge_head_pair(o[j * 2 * A_BLOCK:(j + 1) * 2 * A_BLOCK])
                            for j in range(A_HEADS // 2)], axis=1)


WIN_BLOCKS_PER_STEP = 8


def _win_kernel(sink_ref, q_ref, kp_ref, kc_ref, kn_ref, vp_ref, vc_ref, vn_ref, kx_ref, vx_ref, o_ref):
    i = pl.program_id(1)
    n_steps = pl.num_programs(1)
    nbs = WIN_BLOCKS_PER_STEP
    sink_tile = jnp.concatenate([jnp.full((A_BLOCK, LANES), sink_ref[h], F32) for h in A_HEAD_ORDER], axis=0)
    qi = lax.broadcasted_iota(jnp.int32, (A_BLOCK, A_BLOCK), 0)
    kj = lax.broadcasted_iota(jnp.int32, (A_BLOCK, A_BLOCK), 1)
    neg = jnp.full((A_BLOCK, A_BLOCK), -jnp.inf, F32)
    zero = jnp.zeros((A_BLOCK, A_BLOCK), F32)
    prev_band = jnp.where(kj >= qi, zero, neg)
    next_band = jnp.where(kj <= qi, zero, neg)
    k_x, v_x = kx_ref[...], vx_ref[...]
    for a in range(nbs):
        rows = slice(a * A_BLOCK, (a + 1) * A_BLOCK)
        before = slice((a - 1) * A_BLOCK, a * A_BLOCK)
        after = slice((a + 1) * A_BLOCK, (a + 2) * A_BLOCK)
        prev_mask = prev_band if a > 0 else jnp.where(i > 0, prev_band, neg)
        next_mask = next_band if a < nbs - 1 else jnp.where(i < n_steps - 1, next_band, neg)
        keys = (kc_ref[before, :] if a > 0 else kp_ref[...], kc_ref[rows, :],
                kc_ref[after, :] if a < nbs - 1 else kn_ref[...], k_x)
        values = (vc_ref[before, :] if a > 0 else vp_ref[...], vc_ref[rows, :],
                  vc_ref[after, :] if a < nbs - 1 else vn_ref[...], v_x)
        q2 = jnp.concatenate([_split_head_pair(q_ref[rows, j * LANES:(j + 1) * LANES])
                              for j in range(A_HEADS // 2)], axis=0)
        o_ref[rows, :] = _win_block(q2, sink_tile, keys, values, prev_mask, next_mask).astype(o_ref.dtype)


def _window_attention(sink, q, k, v, k_ctx, v_ctx):
    bsz, seq, wq = q.shape
    nbs = WIN_BLOCKS_PER_STEP
    nb = seq // A_BLOCK
    wkv = k.shape[-1]
    n_ctx = k_ctx.shape[1]
    prev = pl.BlockSpec((None, A_BLOCK, wkv), lambda b, i: (b, jnp.maximum(nbs * i - 1, 0), 0))
    cur = pl.BlockSpec((None, nbs * A_BLOCK, wkv), lambda b, i: (b, i, 0))
    nxt = pl.BlockSpec((None, A_BLOCK, wkv), lambda b, i: (b, jnp.minimum(nbs * i + nbs, nb - 1), 0))
    ctx_spec = pl.BlockSpec((None, n_ctx, wkv), lambda b, i: (b, 0, 0))
    q_spec = pl.BlockSpec((None, nbs * A_BLOCK, wq), lambda b, i: (b, i, 0))
    return pl.pallas_call(
        _win_kernel,
        grid=(bsz, nb // nbs),
        in_specs=[pl.BlockSpec(memory_space=pltpu.SMEM), q_spec, prev, cur, nxt, prev, cur, nxt,
                  ctx_spec, ctx_spec],
        out_specs=q_spec,
        out_shape=jax.ShapeDtypeStruct((bsz, seq, wq), BF16),
        compiler_params=_params(("parallel", "arbitrary")),
        name="window_attn",
    )(sink, q, k, k, k, v, v, v, k_ctx, v_ctx)


def _dense_kernel(sink_ref, q_ref, k_ref, v_ref, o_ref, *, head_order, shared_kv):
    n_q = q_ref.shape[0]
    for j in range(q_ref.shape[1] // LANES):
        qs = slice(j * LANES, (j + 1) * LANES)
        ks = slice(0, LANES) if shared_kv else qs
        s = _dot_nt(_split_head_pair(q_ref[:, qs]), k_ref[:, ks])
        sink_tile = None
        if head_order is not None:
            sink_tile = jnp.concatenate(
                [jnp.full((n_q, LANES), sink_ref[head_order[2 * j + half]], F32) for half in range(2)], axis=0)
        (p,), denom = _softmax_parts([s], sink_tile)
        o_ref[:, qs] = _merge_head_pair(_dot(p.astype(BF16), v_ref[:, ks]) / denom).astype(o_ref.dtype)


def _dense_attention(sink, q, k, v, *, head_order, shared_kv):
    bsz, n_q, wq = q.shape
    wkv = k.shape[-1]
    full = lambda w: pl.BlockSpec((None, n_q, w), lambda b: (b, 0, 0))
    return pl.pallas_call(
        functools.partial(_dense_kernel, head_order=head_order, shared_kv=shared_kv),
        grid=(bsz,),
        in_specs=[pl.BlockSpec(memory_space=pltpu.SMEM), full(wq), full(wkv), full(wkv)],
        out_specs=full(wq),
        out_shape=jax.ShapeDtypeStruct((bsz, n_q, wq), BF16),
        compiler_params=_params(("parallel",)),
        name="ctx_dense_attn_sink" if shared_kv else "ctx_dense_attn",
    )(sink, q, k, v)


NA_ROWS_PER_STEP = 16


N_NA_INPUTS = 6


def _na_kernel(*refs):
    q_ref, k_ref, v_ref, kx_ref, vx_ref, bias_ref = refs[:N_NA_INPUTS]
    n_side = (len(refs) - N_NA_INPUTS - 1) // 2
    o_ref = refs[N_NA_INPUTS + n_side]
    _run_side_casts(refs[N_NA_INPUTS:N_NA_INPUTS + n_side], refs[N_NA_INPUTS + n_side + 1:])
    step = pl.program_id(1)
    grid_rows = k_ref.shape[0] // GRID_W
    n_keys = C_WIN_ROWS * GRID_W
    n_pairs = C_HEADS // 2
    for rr in range(NA_ROWS_PER_STEP):
        r = step * NA_ROWS_PER_STEP + rr
        row0 = jnp.clip(r - C_WIN_ROWS // 2, 0, grid_rows - C_WIN_ROWS)
        d_row0 = row0 - r + (C_WIN_ROWS - 1)
        key0 = pl.multiple_of(row0 * GRID_W, GRID_W)
        qrows = slice(rr * GRID_W, (rr + 1) * GRID_W)
        s_l, s_x = [], []
        for p in range(n_pairs):
            ps = slice(p * LANES, (p + 1) * LANES)
            q2 = _split_head_pair(q_ref[qrows, ps])
            s_l.append(_dot_nt(q2, k_ref[pl.ds(key0, n_keys), ps]))
            s_x.append(_dot_nt(q2, kx_ref[:, ps]))
        bias = jnp.concatenate(
            [bias_ref[d_row0 + 2 * j].reshape(C_HEADS * GRID_W, LANES) for j in range(C_WIN_ROWS // 2)], axis=1)
        (p_l, p_x), denom = _softmax_parts([jnp.concatenate(s_l, axis=0) + bias, jnp.concatenate(s_x, axis=0)], None)
        p_l = p_l.astype(BF16)
        p_x = p_x.astype(BF16)
        for p in range(n_pairs):
            ps = slice(p * LANES, (p + 1) * LANES)
            pr = slice(p * 2 * GRID_W, (p + 1) * 2 * GRID_W)
            o = (_dot(p_l[pr], v_ref[pl.ds(key0, n_keys), ps]) + _dot(p_x[pr], vx_ref[:, ps])) / denom[pr]
            o_ref[qrows, ps] = _merge_head_pair(o).astype(o_ref.dtype)


def _neighbourhood_attention(q, k, v, k_ctx, v_ctx, bias_tbl, cast_weights=(), layer=0):
    bsz, seq, w = q.shape
    n_ctx = k_ctx.shape[1]
    tq = NA_ROWS_PER_STEP * GRID_W
    n_tiles = seq // tq
    whole = pl.BlockSpec((None, seq, w), lambda b, i: (b, 0, 0))
    ctx_spec = pl.BlockSpec((None, n_ctx, w), lambda b, i: (b, 0, 0))
    side = _SideCasts(cast_weights, layer, bsz * n_tiles, lambda b, i: b * n_tiles + i)
    outs = pl.pallas_call(
        _na_kernel,
        grid=(bsz, n_tiles),
        in_specs=[pl.BlockSpec((None, tq, w), lambda b, i: (b, i, 0)), whole, whole, ctx_spec, ctx_spec,
                  _const_spec(bias_tbl.shape)] + side.specs,
        out_specs=[pl.BlockSpec((None, tq, w), lambda b, i: (b, i, 0))] + side.out_specs,
        out_shape=[jax.ShapeDtypeStruct((bsz, seq, w), BF16)] + side.out_shapes,
        compiler_params=_params(("parallel", "arbitrary")),
        name="neighbourhood_attn",
    )(q, k, v, k_ctx, v_ctx, bias_tbl, *side.arrays)
    return outs[0], side.finish(outs[1:])


def _na_bias_kernel(vec_ref, o_ref):
    q = lax.broadcasted_iota(jnp.int32, (GRID_W, LANES), 0)
    c = lax.broadcasted_iota(jnp.int32, (GRID_W, LANES), 1)
    k = c % GRID_W
    wstart = jnp.clip(q - C_WIN_COLS // 2, 0, GRID_W - C_WIN_COLS)
    ok = (k >= wstart) & (k < wstart + C_WIN_COLS)
    n_dr = vec_ref.shape[1]
    for h in range(vec_ref.shape[0]):
        toep = [pltpu.roll(jnp.broadcast_to(vec_ref[h, dr:dr + 1, :], (GRID_W, LANES)), 0, 1,
                           stride=1, stride_axis=0) for dr in range(n_dr)]
        for dr in range(n_dr - 1):
            pair = jnp.where(c < GRID_W, toep[dr], pltpu.roll(toep[dr + 1], GRID_W, 1))
            o_ref[dr, h] = jnp.where(ok, pair * LOG2E, -jnp.inf)


def _na_bias_table(rpb):
    n_off = C_WIN_COLS - 1
    vec = jnp.concatenate([rpb[..., n_off:].astype(F32),
                           jnp.zeros(rpb.shape[:2] + (LANES - 2 * n_off - 1,), F32),
                           rpb[..., :n_off].astype(F32)], axis=-1)
    return pl.pallas_call(
        _na_bias_kernel,
        out_shape=jax.ShapeDtypeStruct((rpb.shape[1] - 1, rpb.shape[0], GRID_W, LANES), F32),
        name="na_bias_table",
    )(vec)


MERGE_SUB_ROWS = 512


def _merge_kernel(x_ref, mod_ref, gain_ref, ya_ref, of_ref, ob_ref, gr_ref, yc_ref, gn_ref,
                  wm_ref, bm_ref, wa_ref, wb_ref, wc_ref, wo_ref, o_ref):
    d = D_MODEL
    gn = gn_ref[...]
    for r0 in range(0, x_ref.shape[0], MERGE_SUB_ROWS):
        rows = slice(r0, r0 + MERGE_SUB_ROWS)
        x = x_ref[rows, :]
        h = _norm_mod(x, gain_ref[...], mod_ref[0:1, :], mod_ref[1:2, :]).astype(BF16)
        o_sum = of_ref[rows, :] + ob_ref[rows, :]
        parts = []
        for hh in range(B_HEADS):
            oh = o_sum[:, hh * B_DV:(hh + 1) * B_DV]
            parts.append((oh * lax.rsqrt(jnp.mean(oh * oh, axis=-1, keepdims=True) + EPS)) * gn)
        y_b = (jnp.concatenate(parts, axis=1) * _silu(gr_ref[rows, :])).astype(BF16)
        mixed = None
        for j, (y, w_ref) in enumerate(((ya_ref[rows, :], wa_ref), (y_b, wb_ref), (yc_ref[rows, :], wc_ref))):
            gate = _sigmoid(_dot(h, wm_ref[:, j * d:(j + 1) * d]) + bm_ref[:, j * d:(j + 1) * d])
            term = gate * _dot(y, w_ref[...])
            mixed = term if mixed is None else mixed + term
        o_ref[rows, :] = x + mod_ref[2:3, :] * _dot(mixed.astype(BF16), wo_ref[...])


def _merge(x, mod_rows, gain, y_a, o_f, o_b, g_r, y_c, gla_gain, w_merge, b_merge, w_a, w_b, w_c, w_out, *, tm):
    bsz, seq, d = x.shape
    per_batch_mod = mod_rows.shape[0] > 1
    mod_map = (lambda b, i: (b, 0, 0)) if per_batch_mod else (lambda b, i: (0, 0, 0))
    tile = lambda w: pl.BlockSpec((None, tm, w), lambda b, i: (b, i, 0))
    wy = y_a.shape[-1]
    return pl.pallas_call(
        _merge_kernel,
        grid=(bsz, seq // tm),
        in_specs=[tile(d), pl.BlockSpec((None, 6, d), mod_map), _const_spec((1, d)),
                  tile(wy), tile(wy), tile(wy), tile(wy), tile(wy), _const_spec((1, B_DV)),
                  _const_spec(w_merge.shape), _const_spec(b_merge.shape),
                  _const_spec(w_a.shape), _const_spec(w_b.shape), _const_spec(w_c.shape),
                  _const_spec(w_out.shape)],
        out_specs=tile(d),
        out_shape=jax.ShapeDtypeStruct((bsz, seq, d), F32),
        compiler_params=_params(("parallel", "arbitrary")),
        name="merge_out",
    )(x, mod_rows, gain, y_a, o_f, o_b, g_r, y_c, gla_gain, w_merge, b_merge, w_a, w_b, w_c, w_out)


FFN_CHUNKS = ((0, 1024), (1024, 2048), (2048, FFN_HIDDEN))
FFN_SUB_ROWS = 256


def _ffn_kernel(x_ref, mod_ref, gain_ref, w1_ref, w2_ref, fg_ref, o_ref, *, final):
    for r0 in range(0, x_ref.shape[0], FFN_SUB_ROWS):
        rows = slice(r0, r0 + FFN_SUB_ROWS)
        x = x_ref[rows, :]
        h = _norm_mod(x, gain_ref[...], mod_ref[3:4, :], mod_ref[4:5, :]).astype(BF16)
        acc = None
        for c0, c1 in FFN_CHUNKS:
            gate = _dot(h, w1_ref[:, c0:c1])
            up = _dot(h, w1_ref[:, FFN_HIDDEN + c0:FFN_HIDDEN + c1])
            part = _dot((_silu(gate) * up).astype(BF16), w2_ref[c0:c1, :])
            acc = part if acc is None else acc + part
        y = x + mod_ref[5:6, :] * acc
        if final:
            y = (y * lax.rsqrt(jnp.mean(y * y, axis=-1, keepdims=True) + EPS)) * fg_ref[...]
        o_ref[rows, :] = y


def _ffn(x, mod_rows, gain, w1, w2, final_gain, *, final, tm):
    bsz, seq, d = x.shape
    per_batch_mod = mod_rows.shape[0] > 1
    mod_map = (lambda b, i: (b, 0, 0)) if per_batch_mod else (lambda b, i: (0, 0, 0))
    tile = pl.BlockSpec((None, tm, d), lambda b, i: (b, i, 0))
    return pl.pallas_call(
        functools.partial(_ffn_kernel, final=final),
        grid=(bsz, seq // tm),
        in_specs=[tile, pl.BlockSpec((None, 6, d), mod_map), _const_spec((1, d)),
                  _const_spec(w1.shape), _const_spec(w2.shape), _const_spec((1, d))],
        out_specs=tile,
        out_shape=jax.ShapeDtypeStruct((bsz, seq, d), F32),
        compiler_params=_params(("parallel", "arbitrary")),
        name="ffn_final" if final else "ffn",
    )(x, mod_rows, gain, w1, w2, final_gain)


def _rope_tables(seq):
    t = jnp.arange(seq)
    row = (t // GRID_W).astype(F32)
    col = (t % GRID_W).astype(F32)
    n_freq = HEAD_DIM // 4
    inv = ROPE_BASE ** (-jnp.arange(n_freq, dtype=F32) / n_freq)
    ang = jnp.concatenate([row[:, None] * inv[None], col[:, None] * inv[None]], axis=-1)
    cos, sin = jnp.cos(ang), jnp.sin(ang)
    return jnp.tile(cos, (1, 4)), jnp.tile(jnp.concatenate([-sin, sin], axis=-1), (1, 2))


def _permute_w_branch_a(w):
    return w.reshape(A_HEADS, HEAD_DIM, w.shape[1])[jnp.array(A_HEAD_ORDER)].reshape(w.shape).astype(BF16)


def _gate_weights(w_fwd, b_fwd, w_bwd, b_bwd):
    n = B_HEADS * B_DK
    r = B_GATE_RANK
    w = jnp.zeros((LANES, 2 * n), F32)
    w = w.at[:r, :n].set(w_fwd).at[r:2 * r, n:].set(w_bwd)
    return w.astype(BF16), jnp.concatenate([b_fwd, b_bwd])[None, :]


def kernel(x, c, ctx, c_ctx, w_ada, b_ada, norm_mix, w_in, attn_sink, gla_gate_w_fwd, gla_gate_b_fwd,
           gla_gate_w_bwd, gla_gate_b_bwd, gla_norm, na_rpb, w_branch_a, w_branch_b, w_branch_c,
           w_merge, b_merge, w_out, norm_ffn, w_ffn_in, w_ffn_out, final_norm):
    bsz, seq, d = x.shape
    n_ctx = ctx.shape[1]
    cos, sin = _rope_tables(seq)

    cvec = jnp.zeros((8, d), F32).at[:bsz].set(c).at[bsz].set(c_ctx)
    mod, w_perm_all = _modulation(cvec, w_ada, b_ada, w_in)
    mod = mod.reshape(DEPTH, 8, 6, d)

    xc = ctx
    zeros_state = jnp.zeros((bsz,) + GLA_STATE_SHAPE, F32)
    for l in range(DEPTH):
        last = l == DEPTH - 1
        mod_x = mod[l, :bsz]
        mod_c = mod[l, bsz:bsz + 1]
        gain_m = norm_mix[l][None, :]
        w_perm = w_perm_all
        w_gate, b_gate = _gate_weights(gla_gate_w_fwd[l], gla_gate_b_fwd[l], gla_gate_w_bwd[l], gla_gate_b_bwd[l])

        (caq, cak, cav, cgq, cgk, cgv, cgr, cnq, cnk, cnv, cgf, cgb), _ = _project(
            xc, mod_c, gain_m, w_perm, w_gate, b_gate, cos, sin, rope=False, tm=n_ctx, layer=l)
        (aq, ak, av, gq, gk, gv, gr, nq, nk, nv, gf, gb), (wm_bf, wb_bf, wc_bf, wo_bf) = _project(
            x, mod_x, gain_m, w_perm, w_gate, b_gate, cos, sin, rope=True, tm=512,
            cast_weights=(w_merge, w_branch_b, w_branch_c, w_out), layer=l)

        co_f, co_b, s_f, s_b = _gla(cgq, cgk, cgv, cgf, cgb, zeros_state, zeros_state, tile=n_ctx)
        o_f, o_b, _, _ = _gla(gq, gk, gv, gf, gb, s_f, s_b, tile=1024)

        sink = attn_sink[l] * LOG2E
        y_a = _window_attention(sink, aq, ak, av, cak, cav)
        y_c, (w1, w2) = _neighbourhood_attention(nq, nk, nv, cnk, cnv, _na_bias_table(na_rpb[l]),
                                                 cast_weights=(w_ffn_in, w_ffn_out), layer=l)

        merge_w = (gla_norm[l][None, :], wm_bf, b_merge[l][None, :],
                   _permute_w_branch_a(w_branch_a[l]), wb_bf, wc_bf, wo_bf)
        x = _merge(x, mod_x, gain_m, y_a, o_f, o_b, gr, y_c, *merge_w, tm=1024)

        gain_f = norm_ffn[l][None, :]
        fg = final_norm[None, :]
        if not last:
            yc_a = _dense_attention(sink, caq, cak, cav, head_order=A_HEAD_ORDER, shared_kv=True)
            yc_c = _dense_attention(sink, cnq, cnk, cnv, head_order=None, shared_kv=False)
            xc = _merge(xc, mod_c, gain_m, yc_a, co_f, co_b, cgr, yc_c, *merge_w, tm=n_ctx)
            xc = _ffn(xc, mod_c, gain_f, w1, w2, fg, final=False, tm=n_ctx)
        x = _ffn(x, mod_x, gain_f, w1, w2, fg, final=last, tm=1024)
    return x
```
